```python
import math
import jax, jax.numpy as jnp
from jax import lax
import numpy as np

D_MODEL = 1024
BATCH = 8
SEQ = 8192
DEPTH = 2

MEM_LEN = 256
N_MIXERS = 2
N_LAYERS_A = (DEPTH + 1) // 2
N_LAYERS_B = DEPTH // 2

MEM_HEADS = 4
MEM_HEAD_DIM = 128
MEM_W = MEM_HEADS * MEM_HEAD_DIM

CHUNK = 128
A_WIDTH = D_MODEL
A_GROUPS = 8
A_GROUP_DIM = A_WIDTH // A_GROUPS

MLA_HEADS = 8
Q_LORA = 512
KV_LORA = 256
NOPE_DIM = 128
ROPE_DIM = 64
V_DIM = 128
QK_DIM = NOPE_DIM + ROPE_DIM
ROPE_BASE = 10000.0
Q_BLOCK = 128

N_GROUPS = 4
EXPERTS_PER_GROUP = 8
N_EXPERTS = N_GROUPS * EXPERTS_PER_GROUP
TOP_K = 2
EXPERT_FF = 256
TOKEN_BLOCK = 2048

EPS = 1e-6

kernel_name = "hybrid_gmlp_mla_memory_hmoe"


def rms_norm(x, g):
    xf = x.astype(jnp.float32)
    y = xf * lax.rsqrt(jnp.mean(xf * xf, axis=-1, keepdims=True) + EPS)
    return (y * g.astype(jnp.float32)).astype(x.dtype)


def layer_norm(x, g, b):
    xf = x.astype(jnp.float32)
    mu = jnp.mean(xf, axis=-1, keepdims=True)
    var = jnp.mean(jnp.square(xf - mu), axis=-1, keepdims=True)
    y = (xf - mu) * lax.rsqrt(var + EPS)
    return (y * g.astype(jnp.float32) + b.astype(jnp.float32)).astype(x.dtype)


def rope(x, pos):
    r = x.shape[-1]
    half = r // 2
    inv = ROPE_BASE ** (-(jnp.arange(half, dtype=jnp.float32) * 2.0 / r))
    ang = pos.astype(jnp.float32)[:, :, None, None] * inv
    cos, sin = jnp.cos(ang), jnp.sin(ang)
    xf = x.astype(jnp.float32)
    x1, x2 = xf[..., :half], xf[..., half:]
    return jnp.concatenate([x1 * cos - x2 * sin, x1 * sin + x2 * cos], axis=-1).astype(x.dtype)


def causal_block_attention(q, k, v):
    b, s, h, dk = q.shape
    nb = s // Q_BLOCK
    scale = dk ** -0.5
    qb = q.reshape(b, nb, Q_BLOCK, h, dk).transpose(1, 0, 2, 3, 4)
    kpos = jnp.arange(s)
    neg = jnp.finfo(jnp.float32).min

    def one_block(args):
        i, qi = args
        sc = jnp.einsum('bqhd,bkhd->bhqk', qi, k, preferred_element_type=jnp.float32) * scale
        qpos = i * Q_BLOCK + jnp.arange(Q_BLOCK)
        sc = jnp.where(kpos[None, :] <= qpos[:, None], sc, neg)
        p = jax.nn.softmax(sc, axis=-1)
        return jnp.einsum('bhqk,bkhd->bqhd', p.astype(v.dtype), v)

    out = lax.map(one_block, (jnp.arange(nb), qb))
    return out.transpose(1, 0, 2, 3, 4).reshape(b, s, h, v.shape[-1])


def memory_attention(q, mem_k, mem_v, qn_g, kn_g):
    q = rms_norm(q, qn_g)
    k = rms_norm(mem_k, kn_g)
    sc = jnp.einsum('bshd,bmhd->bhsm', q, k, preferred_element_type=jnp.float32) * (MEM_HEAD_DIM ** -0.5)
    p = jax.nn.softmax(sc, axis=-1)
    o = jnp.einsum('bhsm,bmhd->bshd', p.astype(mem_v.dtype), mem_v)
    return o.reshape(q.shape[0], q.shape[1], MEM_W)


def gmlp_mixer(h, w_in, ln_g, ln_b, w_s, b_s):
    b, s, _ = h.shape
    z = h @ w_in
    uv = jax.nn.gelu(z[..., :2 * A_WIDTH], approximate=False)
    u, v = uv[..., :A_WIDTH], uv[..., A_WIDTH:]
    q_mem = z[..., 2 * A_WIDTH:].reshape(b, s, MEM_HEADS, MEM_HEAD_DIM)
    v = layer_norm(v, ln_g, ln_b)
    vc = v.reshape(b, s // CHUNK, CHUNK, A_GROUPS, A_GROUP_DIM)
    causal = jnp.tril(jnp.ones((CHUNK, CHUNK), dtype=bool))
    w = jnp.where(causal[None], w_s, jnp.zeros((), w_s.dtype))
    mixed = jnp.einsum('gts,bnsgc->bntgc', w, vc) + b_s.T[None, None, :, :, None]
    return u * mixed.reshape(b, s, A_WIDTH), q_mem


def mla_mixer(h, positions, w_in, q_norm_g, kv_norm_g, w_q_up, w_kv_up, qn_g, kn_g):
    b, s, _ = h.shape
    z = h @ w_in
    o1 = Q_LORA
    o2 = o1 + KV_LORA
    o3 = o2 + ROPE_DIM
    cq = rms_norm(z[..., :o1], q_norm_g)
    ckv = rms_norm(z[..., o1:o2], kv_norm_g)
    k_rope = z[..., o2:o3]
    q_mem = z[..., o3:].reshape(b, s, MEM_HEADS, MEM_HEAD_DIM)
    q = (cq @ w_q_up).reshape(b, s, MLA_HEADS, QK_DIM)
    kv = (ckv @ w_kv_up).reshape(b, s, MLA_HEADS, NOPE_DIM + V_DIM)
    k_nope, v = kv[..., :NOPE_DIM], kv[..., NOPE_DIM:]
    k_rope = jnp.broadcast_to(k_rope[:, :, None, :], (b, s, MLA_HEADS, ROPE_DIM))
    k = jnp.concatenate([k_nope, k_rope], axis=-1)
    q = rms_norm(q, qn_g)
    k = rms_norm(k, kn_g)
    q = jnp.concatenate([q[..., :NOPE_DIM], rope(q[..., NOPE_DIM:], positions)], axis=-1)
    k = jnp.concatenate([k[..., :NOPE_DIM], rope(k[..., NOPE_DIM:], positions)], axis=-1)
    o = causal_block_attention(q, k, v)
    return o.reshape(b, s, MLA_HEADS * V_DIM), q_mem


def hierarchical_moe(h, w_group, b_group, w_expert, b_expert, w_gate, w_up, w_down):
    b, s, d = h.shape
    t = h.reshape(-1, d)
    n = t.shape[0]
    g_prob = jax.nn.softmax((t @ w_group).astype(jnp.float32) + b_group.astype(jnp.float32), axis=-1)
    g_idx = jnp.argmax(g_prob, axis=-1)
    g_w = jnp.take_along_axis(g_prob, g_idx[:, None], axis=-1)
    e_logits = ((t @ w_expert).astype(jnp.float32) + b_expert.astype(jnp.float32))
    e_logits = e_logits.reshape(n, N_GROUPS, EXPERTS_PER_GROUP)
    e_sel = jnp.take_along_axis(e_logits, g_idx[:, None, None], axis=1)[:, 0]
    e_prob = jax.nn.softmax(e_sel, axis=-1)
    top_w, top_i = lax.top_k(e_prob, TOP_K)
    top_w = top_w / jnp.sum(top_w, axis=-1, keepdims=True)
    expert_id = g_idx[:, None] * EXPERTS_PER_GROUP + top_i
    weights = g_w * top_w
    combine = jnp.sum(jax.nn.one_hot(expert_id, N_EXPERTS, dtype=jnp.float32) * weights[..., None], axis=1)
    tb = math.gcd(n, TOKEN_BLOCK)
    nb = n // tb

    def expert_block(args):
        xb, cb = args
        gate = jnp.einsum('td,edf->tef', xb, w_gate)
        up = jnp.einsum('td,edf->tef', xb, w_up)
        act = jax.nn.silu(gate) * up * cb[..., None].astype(xb.dtype)
        return jnp.einsum('tef,efd->td', act, w_down)

    y = lax.map(expert_block, (t.reshape(nb, tb, d), combine.reshape(nb, tb, N_EXPERTS)))
    return y.reshape(b, s, d)


def setup_inputs(seed: int = 0) -> dict:
    key = jax.random.key(seed)
    ks = iter(jax.random.split(key, 40))
    f32 = jnp.float32

    def nrm(shape, scale):
        return jax.random.normal(next(ks), shape, f32) * scale

    def gain(shape):
        return 1.0 + 0.05 * jax.random.normal(next(ks), shape, f32)

    D = D_MODEL
    a_in = 2 * A_WIDTH + MEM_W
    a_out = A_WIDTH + MEM_W
    b_in = Q_LORA + KV_LORA + ROPE_DIM + MEM_W
    b_out = MLA_HEADS * V_DIM + MEM_W
    x = jax.random.normal(next(ks), (BATCH, SEQ, D), f32)
    mem = jax.random.normal(next(ks), (BATCH, MEM_LEN, D), f32)
    offsets = jax.random.randint(next(ks), (BATCH, 1), 0, 4096, dtype=jnp.int32)
    positions = offsets + jnp.arange(SEQ, dtype=jnp.int32)[None, :]
    return {
        "x": x,
        "mem": mem,
        "positions": positions,
        "mem_norm_g": gain((D,)),
        "w_mem_kv": nrm((D, 2 * MEM_W), D ** -0.5),
        "mem_qn_g": gain((DEPTH, MEM_HEAD_DIM)),
        "mem_kn_g": gain((DEPTH, MEM_HEAD_DIM)),
        "norm1_g": gain((DEPTH, D)),
        "norm2_g": gain((DEPTH, D)),
        "a_w_in": nrm((N_LAYERS_A, D, a_in), D ** -0.5),
        "a_ln_g": gain((N_LAYERS_A, A_WIDTH)),
        "a_ln_b": nrm((N_LAYERS_A, A_WIDTH), 0.02),
        "a_w_s": nrm((N_LAYERS_A, A_GROUPS, CHUNK, CHUNK), CHUNK ** -0.5),
        "a_b_s": gain((N_LAYERS_A, A_GROUPS, CHUNK)),
        "a_w_out": nrm((N_LAYERS_A, a_out, D), a_out ** -0.5),
        "b_w_in": nrm((N_LAYERS_B, D, b_in), D ** -0.5),
        "b_q_norm_g": gain((N_LAYERS_B, Q_LORA)),
        "b_kv_norm_g": gain((N_LAYERS_B, KV_LORA)),
        "b_w_q_up": nrm((N_LAYERS_B, Q_LORA, MLA_HEADS * QK_DIM), Q_LORA ** -0.5),
        "b_w_kv_up": nrm((N_LAYERS_B, KV_LORA, MLA_HEADS * (NOPE_DIM + V_DIM)), KV_LORA ** -0.5),
        "b_qn_g": gain((N_LAYERS_B, QK_DIM)),
        "b_kn_g": gain((N_LAYERS_B, QK_DIM)),
        "b_w_out": nrm((N_LAYERS_B, b_out, D), b_out ** -0.5),
        "moe_w_group": nrm((DEPTH, D, N_GROUPS), D ** -0.5),
        "moe_b_group": nrm((DEPTH, N_GROUPS), 0.01),
        "moe_w_expert": nrm((DEPTH, D, N_EXPERTS), D ** -0.5),
        "moe_b_expert": nrm((DEPTH, N_EXPERTS), 0.01),
        "moe_w_gate": nrm((DEPTH, N_EXPERTS, D, EXPERT_FF), D ** -0.5),
        "moe_w_up": nrm((DEPTH, N_EXPERTS, D, EXPERT_FF), D ** -0.5),
        "moe_w_down": nrm((DEPTH, N_EXPERTS, EXPERT_FF, D), EXPERT_FF ** -0.5),
    }


def reference(x, mem, positions, mem_norm_g, w_mem_kv, mem_qn_g, mem_kn_g, norm1_g, norm2_g,
              a_w_in, a_ln_g, a_ln_b, a_w_s, a_b_s, a_w_out,
              b_w_in, b_q_norm_g, b_kv_norm_g, b_w_q_up, b_w_kv_up, b_qn_g, b_kn_g, b_w_out,
              moe_w_group, moe_b_group, moe_w_expert, moe_b_expert, moe_w_gate, moe_w_up, moe_w_down):
    b = x.shape[0]
    m = mem.shape[1]
    mem_kv = rms_norm(mem, mem_norm_g) @ w_mem_kv
    mem_k = mem_kv[..., :MEM_W].reshape(b, m, MEM_HEADS, MEM_HEAD_DIM)
    mem_v = mem_kv[..., MEM_W:].reshape(b, m, MEM_HEADS, MEM_HEAD_DIM)
    for i in range(DEPTH):
        h = rms_norm(x, norm1_g[i])
        j = i // N_MIXERS
        if i % N_MIXERS == 0:
            mix, q_mem = gmlp_mixer(h, a_w_in[j], a_ln_g[j], a_ln_b[j], a_w_s[j], a_b_s[j])
            w_out = a_w_out[j]
        else:
            mix, q_mem = mla_mixer(h, positions, b_w_in[j], b_q_norm_g[j], b_kv_norm_g[j],
                                   b_w_q_up[j], b_w_kv_up[j], b_qn_g[j], b_kn_g[j])
            w_out = b_w_out[j]
        mem_out = memory_attention(q_mem, mem_k, mem_v, mem_qn_g[i], mem_kn_g[i])
        x = x + jnp.concatenate([mix, mem_out], axis=-1) @ w_out
        x = x + hierarchical_moe(rms_norm(x, norm2_g[i]), moe_w_group[i], moe_b_group[i],
                                 moe_w_expert[i], moe_b_expert[i], moe_w_gate[i],
                                 moe_w_up[i], moe_w_down[i])
    return x
```

```python
import functools
import math

import jax
import jax.numpy as jnp
from jax import lax
from jax.experimental import pallas as pl
from jax.experimental.pallas import tpu as pltpu

EPS = 1e-6
LANE = 128
MEM_HEADS = 4
MEM_HEAD_DIM = 128
MEM_W = MEM_HEADS * MEM_HEAD_DIM
CHUNK = 128
A_GROUPS = 8
MLA_HEADS = 8
Q_LORA = 512
KV_LORA = 256
NOPE_DIM = 128
ROPE_DIM = 64
ROPE_HALF = ROPE_DIM // 2
V_DIM = 128
QK_DIM = NOPE_DIM + ROPE_DIM
QK_PAD = 2 * LANE
ROPE_BASE = 10000.0
N_GROUPS = 4
EXPERTS_PER_GROUP = 8
N_EXPERTS = N_GROUPS * EXPERTS_PER_GROUP
EXPERT_FF = 256
LOG2E = 1.4426950408889634

TOKEN_TILE = 256
ROW_TILE = 256
ATTN_TILE = 512
VMEM_LIMIT = 56 * 1024 * 1024
NEG_BIG = -1e30

_NT = (((1,), (1,)), ((), ()))


def _const_spec(shape):
    nd = len(shape)
    return pl.BlockSpec(shape, lambda *_: (0,) * nd, pipeline_mode=pl.Buffered(1))


def _params(sem):
    return pltpu.CompilerParams(dimension_semantics=sem, vmem_limit_bytes=VMEM_LIMIT)


def _rms(x, g):
    return x * lax.rsqrt(jnp.mean(x * x, axis=-1, keepdims=True) + EPS) * g


def _gelu(x):
    return 0.5 * x * (1.0 + lax.erf(x * (2.0 ** -0.5)))


def _bf(x):
    return x.astype(jnp.bfloat16)


def _dot(a, b):
    return jnp.dot(a, b, preferred_element_type=jnp.float32)


def _memkv_kernel(mem_ref, g_ref, w_ref, kng_ref, kn_ref, v_ref):
    h = _bf(_rms(mem_ref[...], g_ref[...]))
    kv = _dot(h, w_ref[...])
    v_ref[...] = _bf(kv[:, MEM_W:])
    for layer in range(kn_ref.shape[0]):
        g = kng_ref[layer]
        for hh in range(MEM_HEADS):
            k = kv[:, hh * LANE:(hh + 1) * LANE]
            kn_ref[layer, :, hh * LANE:(hh + 1) * LANE] = _bf(_rms(k, g))


def _mem_attention(qm, kn_ref, v_ref, qg):
    outs = []
    for hh in range(MEM_HEADS):
        sl = slice(hh * LANE, (hh + 1) * LANE)
        q = _rms(qm[:, sl], qg) * (MEM_HEAD_DIM ** -0.5)
        s = lax.dot_general(_bf(q), kn_ref[:, sl], _NT, preferred_element_type=jnp.float32)
        p = jnp.exp(s - jnp.max(s, axis=-1, keepdims=True))
        l = jnp.sum(p, axis=-1, keepdims=True)
        outs.append(_dot(_bf(p), v_ref[:, sl]) / l)
    return jnp.concatenate(outs, axis=-1)


def _route(x, g2_ref, wr_ref, br_ref, carry_ref, xn_ref, slab_ref, cnt_ref):
    t = x.shape[0]

    @pl.when(pl.program_id(0) == 0)
    def _():
        carry_ref[...] = jnp.zeros_like(carry_ref)

    xn = _rms(x, g2_ref[...])
    xn_ref[...] = xn
    logits = _dot(_bf(xn), wr_ref[...]) + br_ref[...]
    lane = lax.broadcasted_iota(jnp.int32, (t, LANE), 1)

    def first_max(v):
        m = jnp.max(v, axis=-1, keepdims=True)
        idx = jnp.min(jnp.where(v == m, lane, LANE), axis=-1, keepdims=True)
        return m, idx

    lg = jnp.where(lane < N_GROUPS, logits, NEG_BIG)
    gmax, gidx = first_max(lg)
    g_w = 1.0 / jnp.sum(jnp.exp(lg - gmax), axis=-1, keepdims=True)

    eid = lane - N_GROUPS
    in_grp = (eid >= 0) & (eid < N_EXPERTS) & ((eid >> 3) == gidx)
    le = jnp.where(in_grp, logits, NEG_BIG)
    m1, i1 = first_max(le)
    m2, i2 = first_max(jnp.where(lane == i1, NEG_BIG, le))
    r = jnp.exp(m2 - m1)
    w1 = g_w / (1.0 + r)
    w2 = w1 * r
    e1 = i1 - N_GROUPS
    e2 = i2 - N_GROUPS

    oh1 = lane == e1
    oh2 = lane == e2
    oh = jnp.where(oh1 | oh2, 1.0, 0.0)
    row = lax.broadcasted_iota(jnp.int32, (t, t), 0)
    col = lax.broadcasted_iota(jnp.int32, (t, t), 1)
    earlier = jnp.where(row > col, 1.0, 0.0).astype(jnp.bfloat16)
    before = _dot(earlier, _bf(oh)) + carry_ref[...]
    r1 = jnp.sum(jnp.where(oh1, before, 0.0), axis=-1, keepdims=True)
    r2 = jnp.sum(jnp.where(oh2, before, 0.0), axis=-1, keepdims=True)
    carry_ref[...] += jnp.sum(oh, axis=0, keepdims=True)
    cnt_ref[...] = carry_ref[...]

    vals = (e1.astype(jnp.float32), e2.astype(jnp.float32), w1, w2, r1, r2)
    slab = jnp.zeros((t, LANE), jnp.float32)
    for i, v in enumerate(vals):
        slab = jnp.where(lane == i, v, slab)
    slab_ref[...] = slab


def _layer0_kernel(x_ref, g1_ref, win_ref, lng_ref, lnb_ref, ws_ref, bs_ref, kn_ref, v_ref, qg_ref,
                   wout_ref, g2_ref, wr_ref, br_ref,
                   xo_ref, xn_ref, slab_ref, cnt_ref, carry_ref):
    x = x_ref[...]
    t, d = x.shape
    h = _bf(_rms(x, g1_ref[...]))
    z = _dot(h, win_ref[...])
    u = _gelu(z[:, :d])
    v = _gelu(z[:, d:2 * d])
    mu = jnp.mean(v, axis=-1, keepdims=True)
    vc = v - mu
    var = jnp.mean(vc * vc, axis=-1, keepdims=True)
    v = _bf(vc * lax.rsqrt(var + EPS) * lng_ref[...] + lnb_ref[...])

    row = lax.broadcasted_iota(jnp.int32, (CHUNK, CHUNK), 0)
    col = lax.broadcasted_iota(jnp.int32, (CHUNK, CHUNK), 1)
    causal = row >= col
    rows = []
    for c in range(t // CHUNK):
        cols = []
        for g in range(A_GROUPS):
            w = jnp.where(causal, ws_ref[g], jnp.zeros((), ws_ref.dtype))
            cols.append(_dot(w, v[c * CHUNK:(c + 1) * CHUNK, g * LANE:(g + 1) * LANE]))
        rows.append(jnp.concatenate(cols, axis=-1) + bs_ref[...])
    mixed = jnp.concatenate(rows, axis=0)
    mix = _bf(u * mixed)

    mem = _bf(_mem_attention(z[:, 2 * d:], kn_ref, v_ref, qg_ref[...]))
    xo = x + _dot(mix, wout_ref[:d, :]) + _dot(mem, wout_ref[d:, :])
    xo_ref[...] = xo
    _route(xo, g2_ref, wr_ref, br_ref, carry_ref, xn_ref, slab_ref, cnt_ref)


def _layer1_proj_kernel(x_ref, pos_ref, g1_ref, win_ref, qng_ref, kvng_ref, wq_ref, wkv_ref,
                        qg_ref, kg_ref, inv_ref, sgn_ref, kn_ref, mv_ref, mqg_ref,
                        q_ref, k_ref, v_ref, mem_ref):
    x = x_ref[...]
    h = _bf(_rms(x, g1_ref[...]))
    z = _dot(h, win_ref[...])
    o1 = Q_LORA
    o2 = o1 + KV_LORA
    o3 = o2 + MEM_W
    cq = _bf(_rms(z[:, :o1], qng_ref[...]))
    ckv = _bf(_rms(z[:, o1:o2], kvng_ref[...]))
    k_rope = z[:, o3:o3 + LANE]

    ang = pos_ref[...].astype(jnp.float32) * inv_ref[...]
    cos = jnp.cos(ang)
    sin = jnp.sin(ang) * sgn_ref[...]

    def rope(r):
        return r * cos + pltpu.roll(r, LANE // 2, 1) * sin

    q = _dot(cq, wq_ref[...])
    kv = _dot(ckv, wkv_ref[...])
    qg = qg_ref[...]
    kg = kg_ref[...]
    kr_ss = jnp.sum(k_rope * k_rope, axis=-1, keepdims=True)
    kr = rope(k_rope * kg[:, LANE:])
    q_scale = (QK_DIM ** -0.5) * LOG2E
    for hh in range(MLA_HEADS):
        qh = q[:, hh * QK_PAD:(hh + 1) * QK_PAD]
        rq = lax.rsqrt(jnp.sum(qh * qh, axis=-1, keepdims=True) * (1.0 / QK_DIM) + EPS) * q_scale
        qh = qh * rq * qg
        q_ref[hh, :, :LANE] = _bf(qh[:, :LANE])
        q_ref[hh, :, LANE:] = _bf(rope(qh[:, LANE:]))
        kn = kv[:, hh * LANE:(hh + 1) * LANE]
        rk = lax.rsqrt((jnp.sum(kn * kn, axis=-1, keepdims=True) + kr_ss) * (1.0 / QK_DIM) + EPS)
        k_ref[hh, :, :LANE] = _bf(kn * rk * kg[:, :LANE])
        k_ref[hh, :, LANE:] = _bf(kr * rk)
        v_ref[hh] = _bf(kv[:, (MLA_HEADS + hh) * LANE:(MLA_HEADS + hh + 1) * LANE])

    mem_ref[...] = _bf(_mem_attention(z[:, o2:o3], kn_ref, mv_ref, mqg_ref[...]))


def _attn_kernel(q_ref, k_ref, v_ref, o_ref):
    i = pl.program_id(2)
    tq = q_ref.shape[0]
    q = q_ref[...]

    def block(j, carry, masked):
        m, l, acc = carry
        start = pl.multiple_of(j * tq, tq)
        s = lax.dot_general(q, k_ref[pl.ds(start, tq), :], _NT, preferred_element_type=jnp.float32)
        if masked:
            row = lax.broadcasted_iota(jnp.int32, (tq, tq), 0)
            col = lax.broadcasted_iota(jnp.int32, (tq, tq), 1)
            s = jnp.where(col <= row, s, NEG_BIG)
        m_new = jnp.maximum(m, jnp.max(s, axis=-1, keepdims=True))
        alpha = jnp.exp2(m - m_new)
        p = jnp.exp2(s - m_new)
        l = alpha * l + jnp.sum(p, axis=-1, keepdims=True)
        acc = alpha * acc + _dot(_bf(p), v_ref[pl.ds(start, tq), :])
        return m_new, l, acc

    init = (jnp.full((tq, 1), NEG_BIG, jnp.float32), jnp.zeros((tq, 1), jnp.float32),
            jnp.zeros((tq, v_ref.shape[1]), jnp.float32))
    carry = lax.fori_loop(0, i, lambda j, c: block(j, c, False), init)
    m, l, acc = block(i, carry, True)
    o_ref[...] = _bf(acc / l)


def _layer1_out_kernel(x_ref, o_ref, mem_ref, wout_ref, g2_ref, wr_ref, br_ref,
                       xo_ref, xn_ref, slab_ref, cnt_ref, carry_ref):
    d = o_ref.shape[1]
    xo = x_ref[...] + _dot(o_ref[...], wout_ref[:d, :]) + _dot(mem_ref[...], wout_ref[d:, :])
    xo_ref[...] = xo
    _route(xo, g2_ref, wr_ref, br_ref, carry_ref, xn_ref, slab_ref, cnt_ref)


def _row_copy(src_ref, src_row, dst_ref, dst_row, sem):
    return pltpu.make_async_copy(src_ref.at[pl.ds(src_row, 1)], dst_ref.at[pl.ds(dst_row, 1)], sem)


def _dispatch_kernel(off_ref, pos_ref, xn_ref, xs_ref, zero_ref, sem, zsem):
    t = xn_ref.shape[0]
    rt = zero_ref.shape[0]

    @pl.when(pl.program_id(0) == 0)
    def _():
        zero_ref[...] = jnp.zeros_like(zero_ref)

        def tail_copy(e):
            end = pl.multiple_of(off_ref[e + 1], rt)
            return pltpu.make_async_copy(zero_ref, xs_ref.at[pl.ds(end - rt, rt)], zsem)

        for e in range(N_EXPERTS):
            @pl.when(off_ref[e + 1] > off_ref[e])
            def _():
                tail_copy(e).start()
        for e in range(N_EXPERTS):
            @pl.when(off_ref[e + 1] > off_ref[e])
            def _():
                tail_copy(e).wait()

    def issue(i, c):
        _row_copy(xn_ref, i, xs_ref, pos_ref[0, 0, 2 * i], sem).start()
        _row_copy(xn_ref, i, xs_ref, pos_ref[0, 0, 2 * i + 1], sem).start()
        return c

    lax.fori_loop(0, t, issue, 0)

    def drain(i, c):
        _row_copy(xn_ref, 0, xs_ref, 0, sem).wait()
        return c

    lax.fori_loop(0, 2 * t, drain, 0)


def _ffn_kernel(blk_ref, exp_ref, nvalid_ref, xs_ref, wg_ref, wu_ref, wd_ref, ys_ref):
    @pl.when(pl.program_id(0) < nvalid_ref[0])
    def _():
        x = _bf(xs_ref[...])
        g = _dot(x, wg_ref[...])
        u = _dot(x, wu_ref[...])
        act = _bf(g * jax.nn.sigmoid(g) * u)
        ys_ref[...] = _dot(act, wd_ref[...])


def _combine_kernel(pos_ref, x_ref, slab_ref, ys_ref, out_ref, y0_ref, y1_ref, sem):
    t = x_ref.shape[0]

    def issue(i, c):
        _row_copy(ys_ref, pos_ref[0, 0, 2 * i], y0_ref, i, sem).start()
        _row_copy(ys_ref, pos_ref[0, 0, 2 * i + 1], y1_ref, i, sem).start()
        return c

    lax.fori_loop(0, t, issue, 0)

    def drain(i, c):
        _row_copy(ys_ref, 0, y0_ref, 0, sem).wait()
        return c

    lax.fori_loop(0, 2 * t, drain, 0)
    slab = slab_ref[...]
    out_ref[...] = x_ref[...] + slab[:, 2:3] * y0_ref[...] + slab[:, 3:4] * y1_ref[...]


def _moe(x, xn, slab, counts, w_gate, w_up, w_down):
    n, d = x.shape
    tt = TOKEN_TILE
    rt = ROW_TILE
    p_max = 2 * n + N_EXPERTS * rt
    n_tiles = p_max // rt

    cnt = counts[0, :N_EXPERTS].astype(jnp.int32)
    padded = ((cnt + rt - 1) // rt) * rt
    off = jnp.concatenate([jnp.zeros((1,), jnp.int32), jnp.cumsum(padded)]).astype(jnp.int32)
    eid = slab[:, 0:2].astype(jnp.int32)
    rank = slab[:, 4:6].astype(jnp.int32)
    pos = (off[eid] + rank).reshape(n // tt, 1, 2 * tt)
    n_valid = off[N_EXPERTS] // rt
    tile_blk = jnp.minimum(jnp.arange(n_tiles, dtype=jnp.int32), n_valid - 1)
    tile_exp = jnp.minimum(jnp.searchsorted(off[1:], tile_blk * rt, side="right"),
                           N_EXPERTS - 1).astype(jnp.int32)

    pos_spec = pl.BlockSpec((1, 1, 2 * tt), lambda i, *_: (i, 0, 0), memory_space=pltpu.SMEM)
    tok_spec = pl.BlockSpec((tt, d), lambda i, *_: (i, 0))

    xs = pl.pallas_call(
        _dispatch_kernel,
        grid_spec=pltpu.PrefetchScalarGridSpec(
            num_scalar_prefetch=1, grid=(n // tt,),
            in_specs=[pos_spec, tok_spec],
            out_specs=pl.BlockSpec(memory_space=pl.ANY),
            scratch_shapes=[pltpu.VMEM((rt, d), jnp.float32), pltpu.SemaphoreType.DMA,
                            pltpu.SemaphoreType.DMA]),
        out_shape=jax.ShapeDtypeStruct((p_max, d), jnp.float32),
        compiler_params=_params(("arbitrary",)),
        name="moe_dispatch",
    )(off, pos, xn)

    f = w_gate.shape[-1]
    ys = pl.pallas_call(
        _ffn_kernel,
        grid_spec=pltpu.PrefetchScalarGridSpec(
            num_scalar_prefetch=3, grid=(n_tiles,),
            in_specs=[pl.BlockSpec((rt, d), lambda j, blk, ex, nv: (blk[j], 0)),
                      pl.BlockSpec((None, d, f), lambda j, blk, ex, nv: (ex[j], 0, 0)),
                      pl.BlockSpec((None, d, f), lambda j, blk, ex, nv: (ex[j], 0, 0)),
                      pl.BlockSpec((None, f, d), lambda j, blk, ex, nv: (ex[j], 0, 0))],
            out_specs=pl.BlockSpec((rt, d), lambda j, blk, ex, nv: (blk[j], 0))),
        out_shape=jax.ShapeDtypeStruct((p_max, d), jnp.float32),
        compiler_params=_params(("arbitrary",)),
        name="moe_ffn",
    )(tile_blk, tile_exp, n_valid.reshape(1), xs, w_gate, w_up, w_down)

    return pl.pallas_call(
        _combine_kernel,
        grid=(n // tt,),
        in_specs=[pos_spec, tok_spec, pl.BlockSpec((tt, LANE), lambda i: (i, 0)),
                  pl.BlockSpec(memory_space=pl.ANY)],
        out_specs=tok_spec,
        out_shape=jax.ShapeDtypeStruct((n, d), jnp.float32),
        scratch_shapes=[pltpu.VMEM((tt, d), jnp.float32), pltpu.VMEM((tt, d), jnp.float32),
                        pltpu.SemaphoreType.DMA],
        compiler_params=_params(("arbitrary",)),
        name="moe_combine",
    )(pos, x, slab, ys)


def _router_weights(w_group, b_group, w_expert, b_expert):
    d = w_group.shape[0]
    pad = LANE - N_GROUPS - N_EXPERTS
    wr = jnp.concatenate([w_group, w_expert, jnp.zeros((d, pad), w_group.dtype)], axis=1)
    br = jnp.concatenate([b_group, b_expert, jnp.zeros((pad,), b_group.dtype)])
    return _bf(wr), br.reshape(1, LANE).astype(jnp.float32)


def _rope_lanes(vec_half):
    z = jnp.zeros_like(vec_half)
    return jnp.concatenate([vec_half, z, vec_half, z], axis=-1)


def _pad_rope_cols(w):
    z = jnp.zeros(w.shape[:-1] + (ROPE_HALF,), w.dtype)
    return jnp.concatenate([w[..., :ROPE_HALF], z, w[..., ROPE_HALF:], z], axis=-1)


def kernel(x, mem, positions, mem_norm_g, w_mem_kv, mem_qn_g, mem_kn_g, norm1_g, norm2_g, a_w_in, a_ln_g, a_ln_b, a_w_s, a_b_s, a_w_out, b_w_in, b_q_norm_g, b_kv_norm_g, b_w_q_up, b_w_kv_up, b_qn_g, b_kn_g, b_w_out, moe_w_group, moe_b_group, moe_w_expert, moe_b_expert, moe_w_gate, moe_w_up, moe_w_down):
    b, s, d = x.shape
    m = mem.shape[1]
    n = b * s
    depth = norm1_g.shape[0]
    tt = TOKEN_TILE
    tiles_per_batch = s // tt
    assert depth == 2 and s % tt == 0 and s % ATTN_TILE == 0 and d == A_GROUPS * LANE
    f32 = jnp.float32
    row = lambda v: v.reshape(1, -1).astype(f32)

    kn_all, mem_v = pl.pallas_call(
        _memkv_kernel,
        grid=(b,),
        in_specs=[pl.BlockSpec((m, d), lambda i: (i, 0)), _const_spec((1, d)),
                  _const_spec((d, 2 * MEM_W)), _const_spec((depth, 1, MEM_HEAD_DIM))],
        out_specs=[pl.BlockSpec((depth, m, MEM_W), lambda i: (0, i, 0)),
                   pl.BlockSpec((m, MEM_W), lambda i: (i, 0))],
        out_shape=[jax.ShapeDtypeStruct((depth, b * m, MEM_W), jnp.bfloat16),
                   jax.ShapeDtypeStruct((b * m, MEM_W), jnp.bfloat16)],
        compiler_params=_params(("arbitrary",)),
        name="mem_kv",
    )(mem.reshape(b * m, d), row(mem_norm_g), _bf(w_mem_kv), mem_kn_g.reshape(depth, 1, MEM_HEAD_DIM))

    tok = lambda width: pl.BlockSpec((tt, width), lambda i: (i, 0))
    kn_spec = lambda layer: pl.BlockSpec((None, m, MEM_W), lambda i: (layer, i // tiles_per_batch, 0))
    mv_spec = pl.BlockSpec((m, MEM_W), lambda i: (i // tiles_per_batch, 0))
    route_out_specs = [tok(d), tok(d), tok(LANE), pl.BlockSpec((1, LANE), lambda i: (0, 0))]
    route_out_shape = [jax.ShapeDtypeStruct((n, d), f32), jax.ShapeDtypeStruct((n, d), f32),
                       jax.ShapeDtypeStruct((n, LANE), f32), jax.ShapeDtypeStruct((1, LANE), f32)]
    route_scratch = [pltpu.VMEM((1, LANE), f32)]

    x2 = x.reshape(n, d)

    wr0, br0 = _router_weights(moe_w_group[0], moe_b_group[0], moe_w_expert[0], moe_b_expert[0])
    a_in = a_w_in.shape[-1]
    bias_s = jnp.repeat(a_b_s[0].T, LANE, axis=1).astype(f32)
    x2, xn, slab, counts = pl.pallas_call(
        _layer0_kernel,
        grid=(n // tt,),
        in_specs=[tok(d), _const_spec((1, d)), _const_spec((d, a_in)), _const_spec((1, d)),
                  _const_spec((1, d)), _const_spec((A_GROUPS, CHUNK, CHUNK)), _const_spec((CHUNK, d)),
                  kn_spec(0), mv_spec, _const_spec((1, MEM_HEAD_DIM)),
                  _const_spec((d + MEM_W, d)), _const_spec((1, d)), _const_spec((d, LANE)),
                  _const_spec((1, LANE))],
        out_specs=route_out_specs,
        out_shape=route_out_shape,
        scratch_shapes=route_scratch,
        compiler_params=_params(("arbitrary",)),
        name="layer0_mixer",
    )(x2, row(norm1_g[0]), _bf(a_w_in[0]), row(a_ln_g[0]), row(a_ln_b[0]), _bf(a_w_s[0]), bias_s,
      kn_all, mem_v, row(mem_qn_g[0]), _bf(a_w_out[0]), row(norm2_g[0]), wr0, br0)
    x2 = _moe(x2, xn, slab, counts, _bf(moe_w_gate[0]), _bf(moe_w_up[0]), _bf(moe_w_down[0]))

    hq = MLA_HEADS
    o1, o2, o3 = Q_LORA, Q_LORA + KV_LORA, Q_LORA + KV_LORA + ROPE_DIM
    w_in = b_w_in[0]
    w_in_p = jnp.concatenate([w_in[:, :o2], w_in[:, o3:], _pad_rope_cols(w_in[:, o2:o3])], axis=1)
    wq = b_w_q_up[0].reshape(Q_LORA, hq, QK_DIM)
    wq_p = jnp.concatenate([wq[..., :NOPE_DIM], _pad_rope_cols(wq[..., NOPE_DIM:])], axis=-1)
    wq_p = wq_p.reshape(Q_LORA, hq * QK_PAD)
    wkv = b_w_kv_up[0].reshape(KV_LORA, hq, NOPE_DIM + V_DIM)
    wkv_p = jnp.concatenate([wkv[..., :NOPE_DIM].reshape(KV_LORA, hq * NOPE_DIM),
                             wkv[..., NOPE_DIM:].reshape(KV_LORA, hq * V_DIM)], axis=1)
    pad_gain = lambda g: jnp.concatenate([g[:NOPE_DIM], _pad_rope_cols(g[NOPE_DIM:])]).reshape(1, QK_PAD)
    half = jnp.arange(ROPE_HALF, dtype=f32)
    inv = ROPE_BASE ** (-(half * 2.0 / ROPE_DIM))
    inv_l = _rope_lanes(inv).reshape(1, LANE)
    sgn_l = jnp.concatenate([-jnp.ones((2 * ROPE_HALF,), f32), jnp.ones((2 * ROPE_HALF,), f32)]).reshape(1, LANE)

    in_w = w_in_p.shape[1]
    head_spec = lambda width: pl.BlockSpec((None, hq, tt, width),
                                           lambda i: (i // tiles_per_batch, 0, i % tiles_per_batch, 0))
    q, k, v, mem_o = pl.pallas_call(
        _layer1_proj_kernel,
        grid=(n // tt,),
        in_specs=[tok(d), tok(1), _const_spec((1, d)), _const_spec((d, in_w)), _const_spec((1, Q_LORA)),
                  _const_spec((1, KV_LORA)), _const_spec((Q_LORA, hq * QK_PAD)),
                  _const_spec((KV_LORA, hq * (NOPE_DIM + V_DIM))), _const_spec((1, QK_PAD)),
                  _const_spec((1, QK_PAD)), _const_spec((1, LANE)), _const_spec((1, LANE)),
                  kn_spec(1), mv_spec, _const_spec((1, MEM_HEAD_DIM))],
        out_specs=[head_spec(QK_PAD), head_spec(QK_PAD), head_spec(V_DIM), tok(MEM_W)],
        out_shape=[jax.ShapeDtypeStruct((b, hq, s, QK_PAD), jnp.bfloat16),
                   jax.ShapeDtypeStruct((b, hq, s, QK_PAD), jnp.bfloat16),
                   jax.ShapeDtypeStruct((b, hq, s, V_DIM), jnp.bfloat16),
                   jax.ShapeDtypeStruct((n, MEM_W), jnp.bfloat16)],
        compiler_params=_params(("arbitrary",)),
        name="layer1_proj",
    )(x2, positions.reshape(n, 1), row(norm1_g[1]), _bf(w_in_p), row(b_q_norm_g[0]), row(b_kv_norm_g[0]),
      _bf(wq_p), _bf(wkv_p), pad_gain(b_qn_g[0]).astype(f32), pad_gain(b_kn_g[0]).astype(f32),
      inv_l, sgn_l, kn_all, mem_v, row(mem_qn_g[1]))

    ta = ATTN_TILE
    qb = s // ta
    attn = pl.pallas_call(
        _attn_kernel,
        grid=(b, hq, qb),
        in_specs=[pl.BlockSpec((None, None, ta, QK_PAD), lambda bi, hi, i: (bi, hi, i, 0)),
                  pl.BlockSpec((None, None, s, QK_PAD), lambda bi, hi, i: (bi, hi, 0, 0)),
                  pl.BlockSpec((None, None, s, V_DIM), lambda bi, hi, i: (bi, hi, 0, 0))],
        out_specs=pl.BlockSpec((ta, V_DIM), lambda bi, hi, i: (bi * qb + i, hi)),
        out_shape=jax.ShapeDtypeStruct((n, hq * V_DIM), jnp.bfloat16),
        compiler_params=_params(("arbitrary", "arbitrary", "arbitrary")),
        name="causal_attention",
    )(q, k, v)

    wr1, br1 = _router_weights(moe_w_group[1], moe_b_group[1], moe_w_expert[1], moe_b_expert[1])
    x2, xn, slab, counts = pl.pallas_call(
        _layer1_out_kernel,
        grid=(n // tt,),
        in_specs=[tok(d), tok(hq * V_DIM), tok(MEM_W), _const_spec((hq * V_DIM + MEM_W, d)),
                  _const_spec((1, d)), _const_spec((d, LANE)), _const_spec((1, LANE))],
        out_specs=route_out_specs,
        out_shape=route_out_shape,
        scratch_shapes=route_scratch,
        compiler_params=_params(("arbitrary",)),
        name="layer1_out",
    )(x2, attn, mem_o, _bf(b_w_out[0]), row(norm2_g[1]), wr1, br1)
    x2 = _moe(x2, xn, slab, counts, _bf(moe_w_gate[1]), _bf(moe_w_up[1]), _bf(moe_w_down[1]))
    return x2.reshape(b, s, d)
```

```python
import jax
import jax.numpy as jnp
from jax import lax
from jax.experimental import pallas as pl
from jax.experimental.pallas import tpu as pltpu
from jax.experimental.pallas import tpu_sc as plsc

EPS = 1e-6
LANE = 128
MEM_HEADS = 4
MEM_HEAD_DIM = 128
MEM_W = MEM_HEADS * MEM_HEAD_DIM
CHUNK = 128
A_GROUPS = 8
MLA_HEADS = 8
Q_LORA = 512
KV_LORA = 256
NOPE_DIM = 128
ROPE_DIM = 64
ROPE_HALF = ROPE_DIM // 2
V_DIM = 128
QK_DIM = NOPE_DIM + ROPE_DIM
QK_PAD = 2 * LANE
ROPE_BASE = 10000.0
N_GROUPS = 4
EXPERTS_PER_GROUP = 8
N_EXPERTS = N_GROUPS * EXPERTS_PER_GROUP
EXPERT_FF = 256
LOG2E = 1.4426950408889634

TOKEN_TILE = 256
ROW_TILE = 256
ATTN_TILE = 512
VMEM_LIMIT = 56 * 1024 * 1024
NEG_BIG = -1e30

SC_CORES = 2
SC_SUBCORES = 16
SC_WORKERS = SC_CORES * SC_SUBCORES
SC_INDEX_GROUP = 128
SC_CHUNK = 64

_NT = (((1,), (1,)), ((), ()))


def _const_spec(shape):
    nd = len(shape)
    return pl.BlockSpec(shape, lambda *_: (0,) * nd, pipeline_mode=pl.Buffered(1))


def _params(sem):
    return pltpu.CompilerParams(dimension_semantics=sem, vmem_limit_bytes=VMEM_LIMIT)


def _rms(x, g):
    return x * lax.rsqrt(jnp.mean(x * x, axis=-1, keepdims=True) + EPS) * g


def _gelu(x):
    return 0.5 * x * (1.0 + lax.erf(x * (2.0 ** -0.5)))


def _bf(x):
    return x.astype(jnp.bfloat16)


def _dot(a, b):
    return jnp.dot(a, b, preferred_element_type=jnp.float32)


def _memkv_kernel(mem_ref, g_ref, w_ref, kng_ref, kn_ref, v_ref):
    h = _bf(_rms(mem_ref[...], g_ref[...]))
    kv = _dot(h, w_ref[...])
    v_ref[...] = _bf(kv[:, MEM_W:])
    for layer in range(kn_ref.shape[0]):
        g = kng_ref[layer]
        for hh in range(MEM_HEADS):
            k = kv[:, hh * LANE:(hh + 1) * LANE]
            kn_ref[layer, :, hh * LANE:(hh + 1) * LANE] = _bf(_rms(k, g))


def _mem_attention(qm, kn_ref, v_ref, qg):
    outs = []
    for hh in range(MEM_HEADS):
        sl = slice(hh * LANE, (hh + 1) * LANE)
        q = _rms(qm[:, sl], qg) * (MEM_HEAD_DIM ** -0.5)
        s = lax.dot_general(_bf(q), kn_ref[:, sl], _NT, preferred_element_type=jnp.float32)
        p = jnp.exp(s - jnp.max(s, axis=-1, keepdims=True))
        l = jnp.sum(p, axis=-1, keepdims=True)
        outs.append(_dot(_bf(p), v_ref[:, sl]) / l)
    return jnp.concatenate(outs, axis=-1)


def _pack_rows(x):
    w = x.shape[1] // 2
    bits = lambda v: lax.bitcast_convert_type(_bf(v).astype(jnp.float32), jnp.uint32)
    return (bits(x[:, :w]) >> 16) | (bits(x[:, w:]) & jnp.uint32(0xFFFF0000))


def _unpack_rows(p):
    lo = lax.bitcast_convert_type(p << 16, jnp.float32)
    hi = lax.bitcast_convert_type(p & jnp.uint32(0xFFFF0000), jnp.float32)
    return jnp.concatenate([lo, hi], axis=-1)


def _route(x, g2_ref, wr_ref, br_ref, carry_ref, xn_ref, slab_ref, cnt_ref):
    t = x.shape[0]

    @pl.when(pl.program_id(0) == 0)
    def _():
        carry_ref[...] = jnp.zeros_like(carry_ref)

    xn = _rms(x, g2_ref[...])
    xn_ref[...] = _pack_rows(xn)
    logits = _dot(_bf(xn), wr_ref[...]) + br_ref[...]
    lane = lax.broadcasted_iota(jnp.int32, (t, LANE), 1)

    def first_max(v):
        m = jnp.max(v, axis=-1, keepdims=True)
        idx = jnp.min(jnp.where(v == m, lane, LANE), axis=-1, keepdims=True)
        return m, idx

    lg = jnp.where(lane < N_GROUPS, logits, NEG_BIG)
    gmax, gidx = first_max(lg)
    g_w = 1.0 / jnp.sum(jnp.exp(lg - gmax), axis=-1, keepdims=True)

    eid = lane - N_GROUPS
    in_grp = (eid >= 0) & (eid < N_EXPERTS) & ((eid >> 3) == gidx)
    le = jnp.where(in_grp, logits, NEG_BIG)
    m1, i1 = first_max(le)
    m2, i2 = first_max(jnp.where(lane == i1, NEG_BIG, le))
    r = jnp.exp(m2 - m1)
    w1 = g_w / (1.0 + r)
    w2 = w1 * r
    e1 = i1 - N_GROUPS
    e2 = i2 - N_GROUPS

    oh1 = lane == e1
    oh2 = lane == e2
    oh = jnp.where(oh1 | oh2, 1.0, 0.0)
    row = lax.broadcasted_iota(jnp.int32, (t, t), 0)
    col = lax.broadcasted_iota(jnp.int32, (t, t), 1)
    earlier = jnp.where(row > col, 1.0, 0.0).astype(jnp.bfloat16)
    before = _dot(earlier, _bf(oh)) + carry_ref[...]
    r1 = jnp.sum(jnp.where(oh1, before, 0.0), axis=-1, keepdims=True)
    r2 = jnp.sum(jnp.where(oh2, before, 0.0), axis=-1, keepdims=True)
    carry_ref[...] += jnp.sum(oh, axis=0, keepdims=True)
    cnt_ref[...] = carry_ref[...]

    vals = (e1.astype(jnp.float32), e2.astype(jnp.float32), w1, w2, r1, r2)
    slab = jnp.zeros((t, LANE), jnp.float32)
    for i, v in enumerate(vals):
        slab = jnp.where(lane == i, v, slab)
    slab_ref[...] = slab


def _layer0_kernel(x_ref, g1_ref, win_ref, lng_ref, lnb_ref, ws_ref, bs_ref, kn_ref, v_ref, qg_ref,
                   wout_ref, g2_ref, wr_ref, br_ref,
                   xo_ref, xn_ref, slab_ref, cnt_ref, carry_ref):
    x = x_ref[...]
    t, d = x.shape
    h = _bf(_rms(x, g1_ref[...]))
    z = _dot(h, win_ref[...])
    u = _gelu(z[:, :d])
    v = _gelu(z[:, d:2 * d])
    mu = jnp.mean(v, axis=-1, keepdims=True)
    vc = v - mu
    var = jnp.mean(vc * vc, axis=-1, keepdims=True)
    v = _bf(vc * lax.rsqrt(var + EPS) * lng_ref[...] + lnb_ref[...])

    row = lax.broadcasted_iota(jnp.int32, (CHUNK, CHUNK), 0)
    col = lax.broadcasted_iota(jnp.int32, (CHUNK, CHUNK), 1)
    causal = row >= col
    rows = []
    for c in range(t // CHUNK):
        cols = []
        for g in range(A_GROUPS):
            w = jnp.where(causal, ws_ref[g], jnp.zeros((), ws_ref.dtype))
            cols.append(_dot(w, v[c * CHUNK:(c + 1) * CHUNK, g * LANE:(g + 1) * LANE]))
        rows.append(jnp.concatenate(cols, axis=-1) + bs_ref[...])
    mixed = jnp.concatenate(rows, axis=0)
    mix = _bf(u * mixed)

    mem = _bf(_mem_attention(z[:, 2 * d:], kn_ref, v_ref, qg_ref[...]))
    xo = x + _dot(mix, wout_ref[:d, :]) + _dot(mem, wout_ref[d:, :])
    xo_ref[...] = xo
    _route(xo, g2_ref, wr_ref, br_ref, carry_ref, xn_ref, slab_ref, cnt_ref)


def _layer1_proj_kernel(x_ref, pos_ref, g1_ref, win_ref, qng_ref, kvng_ref, wq_ref, wkv_ref,
                        qg_ref, kg_ref, inv_ref, sgn_ref, kn_ref, mv_ref, mqg_ref,
                        q_ref, k_ref, v_ref, mem_ref):
    x = x_ref[...]
    h = _bf(_rms(x, g1_ref[...]))
    z = _dot(h, win_ref[...])
    o1 = Q_LORA
    o2 = o1 + KV_LORA
    o3 = o2 + MEM_W
    cq = _bf(_rms(z[:, :o1], qng_ref[...]))
    ckv = _bf(_rms(z[:, o1:o2], kvng_ref[...]))
    k_rope = z[:, o3:o3 + LANE]

    ang = pos_ref[...].astype(jnp.float32) * inv_ref[...]
    cos = jnp.cos(ang)
    sin = jnp.sin(ang) * sgn_ref[...]

    def rope(r):
        return r * cos + pltpu.roll(r, LANE // 2, 1) * sin

    q = _dot(cq, wq_ref[...])
    kv = _dot(ckv, wkv_ref[...])
    qg = qg_ref[...]
    kg = kg_ref[...]
    kr_ss = jnp.sum(k_rope * k_rope, axis=-1, keepdims=True)
    kr = rope(k_rope * kg[:, LANE:])
    q_scale = (QK_DIM ** -0.5) * LOG2E
    for hh in range(MLA_HEADS):
        qh = q[:, hh * QK_PAD:(hh + 1) * QK_PAD]
        rq = lax.rsqrt(jnp.sum(qh * qh, axis=-1, keepdims=True) * (1.0 / QK_DIM) + EPS) * q_scale
        qh = qh * rq * qg
        q_ref[hh, :, :LANE] = _bf(qh[:, :LANE])
        q_ref[hh, :, LANE:] = _bf(rope(qh[:, LANE:]))
        kn = kv[:, hh * LANE:(hh + 1) * LANE]
        rk = lax.rsqrt((jnp.sum(kn * kn, axis=-1, keepdims=True) + kr_ss) * (1.0 / QK_DIM) + EPS)
        k_ref[hh, :, :LANE] = _bf(kn * rk * kg[:, :LANE])
        k_ref[hh, :, LANE:] = _bf(kr * rk)
        v_ref[hh] = _bf(kv[:, (MLA_HEADS + hh) * LANE:(MLA_HEADS + hh + 1) * LANE])

    mem_ref[...] = _bf(_mem_attention(z[:, o2:o3], kn_ref, mv_ref, mqg_ref[...]))


def _attn_kernel(q_ref, k_ref, v_ref, o_ref):
    i = pl.program_id(2)
    tq = q_ref.shape[0]
    q = q_ref[...]

    def block(j, carry, masked):
        m, l, acc = carry
        start = pl.multiple_of(j * tq, tq)
        s = lax.dot_general(q, k_ref[pl.ds(start, tq), :], _NT, preferred_element_type=jnp.float32)
        if masked:
            row = lax.broadcasted_iota(jnp.int32, (tq, tq), 0)
            col = lax.broadcasted_iota(jnp.int32, (tq, tq), 1)
            s = jnp.where(col <= row, s, NEG_BIG)
        m_new = jnp.maximum(m, jnp.max(s, axis=-1, keepdims=True))
        alpha = jnp.exp2(m - m_new)
        p = jnp.exp2(s - m_new)
        l = alpha * l + jnp.sum(p, axis=-1, keepdims=True)
        acc = alpha * acc + _dot(_bf(p), v_ref[pl.ds(start, tq), :])
        return m_new, l, acc

    init = (jnp.full((tq, 1), NEG_BIG, jnp.float32), jnp.zeros((tq, 1), jnp.float32),
            jnp.zeros((tq, v_ref.shape[1]), jnp.float32))
    carry = lax.fori_loop(0, i, lambda j, c: block(j, c, False), init)
    m, l, acc = block(i, carry, True)
    o_ref[...] = _bf(acc / l)


def _layer1_out_kernel(x_ref, o_ref, mem_ref, wout_ref, g2_ref, wr_ref, br_ref,
                       xo_ref, xn_ref, slab_ref, cnt_ref, carry_ref):
    d = o_ref.shape[1]
    xo = x_ref[...] + _dot(o_ref[...], wout_ref[:d, :]) + _dot(mem_ref[...], wout_ref[d:, :])
    xo_ref[...] = xo
    _route(xo, g2_ref, wr_ref, br_ref, carry_ref, xn_ref, slab_ref, cnt_ref)


def _sc_mesh():
    return plsc.VectorSubcoreMesh(core_axis_name="c", subcore_axis_name="s")


def _sc_worker_base(rows_per_worker):
    return (lax.axis_index("c") * SC_SUBCORES + lax.axis_index("s")) * rows_per_worker


def _sc_scatter_rows(x, idx0, idx1, p_rows):
    n, w = x.shape
    per = n // SC_WORKERS
    chunks = SC_INDEX_GROUP // SC_CHUNK
    assert n % SC_WORKERS == 0 and per % SC_INDEX_GROUP == 0

    @pl.kernel(out_type=jax.ShapeDtypeStruct((p_rows, w), x.dtype), mesh=_sc_mesh(),
               scratch_types=[pltpu.VMEM((1, SC_INDEX_GROUP), jnp.int32),
                              pltpu.VMEM((1, SC_INDEX_GROUP), jnp.int32),
                              pltpu.VMEM((SC_CHUNK, w), x.dtype), pltpu.VMEM((SC_CHUNK, w), x.dtype),
                              pltpu.SemaphoreType.DMA, pltpu.SemaphoreType.DMA],
               name="moe_dispatch_sc")
    def scatter(x_hbm, i0_hbm, i1_hbm, o_hbm, i0_v, i1_v, buf_a, buf_b, sem_a, sem_b):
        base = _sc_worker_base(per)

        @pl.loop(0, per // SC_INDEX_GROUP)
        def _(g):
            off = pl.multiple_of(base + g * SC_INDEX_GROUP, SC_INDEX_GROUP)
            pltpu.sync_copy(i0_hbm.at[:, pl.ds(off, SC_INDEX_GROUP)], i0_v)
            pltpu.sync_copy(i1_hbm.at[:, pl.ds(off, SC_INDEX_GROUP)], i1_v)
            pending = []
            for c in range(chunks):
                buf, sem = ((buf_a, sem_a), (buf_b, sem_b))[c % 2]
                if c >= 2:
                    for cp in pending[c - 2]:
                        cp.wait()
                pltpu.sync_copy(x_hbm.at[pl.ds(off + c * SC_CHUNK, SC_CHUNK)], buf)
                sl = pl.ds(c * SC_CHUNK, SC_CHUNK)
                pending.append((pltpu.async_copy(buf, o_hbm.at[i0_v.at[0, sl]], sem),
                                pltpu.async_copy(buf, o_hbm.at[i1_v.at[0, sl]], sem)))
            for cps in pending[max(chunks - 2, 0):]:
                for cp in cps:
                    cp.wait()

    return scatter(x, idx0.reshape(1, n), idx1.reshape(1, n))


def _sc_gather_rows(table, idx):
    m = idx.shape[0]
    w = table.shape[1]
    per = m // SC_WORKERS
    chunks = SC_INDEX_GROUP // SC_CHUNK
    assert m % SC_WORKERS == 0 and per % SC_INDEX_GROUP == 0

    @pl.kernel(out_type=jax.ShapeDtypeStruct((m, w), table.dtype), mesh=_sc_mesh(),
               scratch_types=[pltpu.VMEM((1, SC_INDEX_GROUP), jnp.int32),
                              pltpu.VMEM((SC_CHUNK, w), table.dtype), pltpu.VMEM((SC_CHUNK, w), table.dtype),
                              pltpu.SemaphoreType.DMA, pltpu.SemaphoreType.DMA],
               name="moe_combine_sc")
    def gather(t_hbm, i_hbm, o_hbm, i_v, buf_a, buf_b, sem_a, sem_b):
        base = _sc_worker_base(per)

        @pl.loop(0, per // SC_INDEX_GROUP)
        def _(g):
            off = pl.multiple_of(base + g * SC_INDEX_GROUP, SC_INDEX_GROUP)
            pltpu.sync_copy(i_hbm.at[:, pl.ds(off, SC_INDEX_GROUP)], i_v)
            pending = []
            for c in range(chunks):
                buf, sem = ((buf_a, sem_a), (buf_b, sem_b))[c % 2]
                if c >= 2:
                    pending[c - 2].wait()
                pltpu.sync_copy(t_hbm.at[i_v.at[0, pl.ds(c * SC_CHUNK, SC_CHUNK)]], buf)
                pending.append(pltpu.async_copy(buf, o_hbm.at[pl.ds(off + c * SC_CHUNK, SC_CHUNK)], sem))
            for cp in pending[max(chunks - 2, 0):]:
                cp.wait()

    return gather(table, idx.reshape(1, m))


def _ffn_kernel(blk_ref, exp_ref, rows_ref, nvalid_ref, xs_ref, wg_ref, wu_ref, wd_ref, ys_ref,
                wg_bf, wu_bf, wd_bf):
    j = pl.program_id(0)

    @pl.when((j == 0) | (exp_ref[j] != exp_ref[jnp.maximum(j - 1, 0)]))
    def _():
        wg_bf[...] = _bf(wg_ref[...])
        wu_bf[...] = _bf(wu_ref[...])
        wd_bf[...] = _bf(wd_ref[...])

    @pl.when(j < nvalid_ref[0])
    def _():
        x = _unpack_rows(xs_ref[...])
        live = lax.broadcasted_iota(jnp.int32, x.shape, 0) < rows_ref[j]
        x = _bf(jnp.where(live, x, 0.0))
        g = _dot(x, wg_bf[...])
        u = _dot(x, wu_bf[...])
        act = _bf(g * jax.nn.sigmoid(g) * u)
        ys_ref[...] = _pack_rows(_dot(act, wd_bf[...]))

    @pl.when(j >= nvalid_ref[0])
    def _():
        ys_ref[...] = jnp.zeros_like(ys_ref)


def _combine_kernel(x_ref, slab_ref, y0_ref, y1_ref, out_ref):
    slab = slab_ref[...]
    out_ref[...] = (x_ref[...] + slab[:, 2:3] * _unpack_rows(y0_ref[...])
                    + slab[:, 3:4] * _unpack_rows(y1_ref[...]))


def _moe(x, xn, slab, counts, w_gate, w_up, w_down):
    n, d = x.shape
    wp = xn.shape[1]
    tt = TOKEN_TILE
    rt = ROW_TILE
    p_max = 2 * n + N_EXPERTS * rt
    n_tiles = p_max // rt

    cnt = counts[0, :N_EXPERTS].astype(jnp.int32)
    padded = ((cnt + rt - 1) // rt) * rt
    off = jnp.concatenate([jnp.zeros((1,), jnp.int32), jnp.cumsum(padded)]).astype(jnp.int32)
    eid = slab[:, 0:2].astype(jnp.int32)
    rank = slab[:, 4:6].astype(jnp.int32)
    pos = off[eid] + rank
    n_valid = off[N_EXPERTS] // rt
    tile_blk = jnp.minimum(jnp.arange(n_tiles, dtype=jnp.int32), n_valid - 1)
    tile_start = tile_blk * rt
    tile_exp = jnp.minimum(jnp.sum(off[1:][None, :] <= tile_start[:, None], axis=1),
                           N_EXPERTS - 1).astype(jnp.int32)
    tile_rows = jnp.clip(off[tile_exp] + cnt[tile_exp] - tile_start, 0, rt).astype(jnp.int32)

    xs = _sc_scatter_rows(xn, pos[:, 0], pos[:, 1], p_max)

    f = w_gate.shape[-1]
    ys = pl.pallas_call(
        _ffn_kernel,
        grid_spec=pltpu.PrefetchScalarGridSpec(
            num_scalar_prefetch=4, grid=(n_tiles,),
            in_specs=[pl.BlockSpec((rt, wp), lambda j, blk, ex, rw, nv: (blk[j], 0)),
                      pl.BlockSpec((None, d, f), lambda j, blk, ex, rw, nv: (ex[j], 0, 0)),
                      pl.BlockSpec((None, d, f), lambda j, blk, ex, rw, nv: (ex[j], 0, 0)),
                      pl.BlockSpec((None, f, d), lambda j, blk, ex, rw, nv: (ex[j], 0, 0))],
            out_specs=pl.BlockSpec((rt, wp), lambda j, blk, ex, rw, nv: (j, 0)),
            scratch_shapes=[pltpu.VMEM((d, f), jnp.bfloat16), pltpu.VMEM((d, f), jnp.bfloat16),
                            pltpu.VMEM((f, d), jnp.bfloat16)]),
        out_shape=jax.ShapeDtypeStruct((p_max, wp), jnp.uint32),
        compiler_params=_params(("arbitrary",)),
        name="moe_ffn",
    )(tile_blk, tile_exp, tile_rows, n_valid.reshape(1), xs, w_gate, w_up, w_down)

    picked = _sc_gather_rows(ys, jnp.concatenate([pos[:, 0], pos[:, 1]]))

    tok_spec = pl.BlockSpec((tt, d), lambda i: (i, 0))
    nb = n // tt
    return pl.pallas_call(
        _combine_kernel,
        grid=(nb,),
        in_specs=[tok_spec, pl.BlockSpec((tt, LANE), lambda i: (i, 0)),
                  pl.BlockSpec((tt, wp), lambda i: (i, 0)),
                  pl.BlockSpec((tt, wp), lambda i: (i + nb, 0))],
        out_specs=tok_spec,
        out_shape=jax.ShapeDtypeStruct((n, d), jnp.float32),
        compiler_params=_params(("arbitrary",)),
        name="moe_combine",
    )(x, slab, picked, picked)


def _router_weights(w_group, b_group, w_expert, b_expert):
    d = w_group.shape[0]
    pad = LANE - N_GROUPS - N_EXPERTS
    wr = jnp.concatenate([w_group, w_expert, jnp.zeros((d, pad), w_group.dtype)], axis=1)
    br = jnp.concatenate([b_group, b_expert, jnp.zeros((pad,), b_group.dtype)])
    return _bf(wr), br.reshape(1, LANE).astype(jnp.float32)


def _rope_lanes(vec_half):
    z = jnp.zeros_like(vec_half)
    return jnp.concatenate([vec_half, z, vec_half, z], axis=-1)


def _pad_rope_cols(w):
    z = jnp.zeros(w.shape[:-1] + (ROPE_HALF,), w.dtype)
    return jnp.concatenate([w[..., :ROPE_HALF], z, w[..., ROPE_HALF:], z], axis=-1)


def kernel(x, mem, positions, mem_norm_g, w_mem_kv, mem_qn_g, mem_kn_g, norm1_g, norm2_g, a_w_in, a_ln_g, a_ln_b, a_w_s, a_b_s, a_w_out, b_w_in, b_q_norm_g, b_kv_norm_g, b_w_q_up, b_w_kv_up, b_qn_g, b_kn_g, b_w_out, moe_w_group, moe_b_group, moe_w_expert, moe_b_expert, moe_w_gate, moe_w_up, moe_w_down):
    b, s, d = x.shape
    m = mem.shape[1]
    n = b * s
    depth = norm1_g.shape[0]
    tt = TOKEN_TILE
    tiles_per_batch = s // tt
    assert depth == 2 and s % tt == 0 and s % ATTN_TILE == 0 and d == A_GROUPS * LANE
    f32 = jnp.float32
    row = lambda v: v.reshape(1, -1).astype(f32)

    kn_all, mem_v = pl.pallas_call(
        _memkv_kernel,
        grid=(b,),
        in_specs=[pl.BlockSpec((m, d), lambda i: (i, 0)), _const_spec((1, d)),
                  _const_spec((d, 2 * MEM_W)), _const_spec((depth, 1, MEM_HEAD_DIM))],
        out_specs=[pl.BlockSpec((depth, m, MEM_W), lambda i: (0, i, 0)),
                   pl.BlockSpec((m, MEM_W), lambda i: (i, 0))],
        out_shape=[jax.ShapeDtypeStruct((depth, b * m, MEM_W), jnp.bfloat16),
                   jax.ShapeDtypeStruct((b * m, MEM_W), jnp.bfloat16)],
        compiler_params=_params(("arbitrary",)),
        name="mem_kv",
    )(mem.reshape(b * m, d), row(mem_norm_g), _bf(w_mem_kv), mem_kn_g.reshape(depth, 1, MEM_HEAD_DIM))

    tok = lambda width: pl.BlockSpec((tt, width), lambda i: (i, 0))
    kn_spec = lambda layer: pl.BlockSpec((None, m, MEM_W), lambda i: (layer, i // tiles_per_batch, 0))
    mv_spec = pl.BlockSpec((m, MEM_W), lambda i: (i // tiles_per_batch, 0))
    route_out_specs = [tok(d), tok(d // 2), tok(LANE), pl.BlockSpec((1, LANE), lambda i: (0, 0))]
    route_out_shape = [jax.ShapeDtypeStruct((n, d), f32), jax.ShapeDtypeStruct((n, d // 2), jnp.uint32),
                       jax.ShapeDtypeStruct((n, LANE), f32), jax.ShapeDtypeStruct((1, LANE), f32)]
    route_scratch = [pltpu.VMEM((1, LANE), f32)]

    x2 = x.reshape(n, d)

    wr0, br0 = _router_weights(moe_w_group[0], moe_b_group[0], moe_w_expert[0], moe_b_expert[0])
    a_in = a_w_in.shape[-1]
    bias_s = jnp.repeat(a_b_s[0].T, LANE, axis=1).astype(f32)
    x2, xn, slab, counts = pl.pallas_call(
        _layer0_kernel,
        grid=(n // tt,),
        in_specs=[tok(d), _const_spec((1, d)), _const_spec((d, a_in)), _const_spec((1, d)),
                  _const_spec((1, d)), _const_spec((A_GROUPS, CHUNK, CHUNK)), _const_spec((CHUNK, d)),
                  kn_spec(0), mv_spec, _const_spec((1, MEM_HEAD_DIM)),
                  _const_spec((d + MEM_W, d)), _const_spec((1, d)), _const_spec((d, LANE)),
                  _const_spec((1, LANE))],
        out_specs=route_out_specs,
        out_shape=route_out_shape,
        scratch_shapes=route_scratch,
        compiler_params=_params(("arbitrary",)),
        name="layer0_mixer",
    )(x2, row(norm1_g[0]), _bf(a_w_in[0]), row(a_ln_g[0]), row(a_ln_b[0]), _bf(a_w_s[0]), bias_s,
      kn_all, mem_v, row(mem_qn_g[0]), _bf(a_w_out[0]), row(norm2_g[0]), wr0, br0)
    x2 = _moe(x2, xn, slab, counts, moe_w_gate[0], moe_w_up[0], moe_w_down[0])

    hq = MLA_HEADS
    o1, o2, o3 = Q_LORA, Q_LORA + KV_LORA, Q_LORA + KV_LORA + ROPE_DIM
    w_in = b_w_in[0]
    w_in_p = jnp.concatenate([w_in[:, :o2], w_in[:, o3:], _pad_rope_cols(w_in[:, o2:o3])], axis=1)
    wq = b_w_q_up[0].reshape(Q_LORA, hq, QK_DIM)
    wq_p = jnp.concatenate([wq[..., :NOPE_DIM], _pad_rope_cols(wq[..., NOPE_DIM:])], axis=-1)
    wq_p = wq_p.reshape(Q_LORA, hq * QK_PAD)
    wkv = b_w_kv_up[0].reshape(KV_LORA, hq, NOPE_DIM + V_DIM)
    wkv_p = jnp.concatenate([wkv[..., :NOPE_DIM].reshape(KV_LORA, hq * NOPE_DIM),
                             wkv[..., NOPE_DIM:].reshape(KV_LORA, hq * V_DIM)], axis=1)
    pad_gain = lambda g: jnp.concatenate([g[:NOPE_DIM], _pad_rope_cols(g[NOPE_DIM:])]).reshape(1, QK_PAD)
    half = jnp.arange(ROPE_HALF, dtype=f32)
    inv = ROPE_BASE ** (-(half * 2.0 / ROPE_DIM))
    inv_l = _rope_lanes(inv).reshape(1, LANE)
    sgn_l = jnp.concatenate([-jnp.ones((2 * ROPE_HALF,), f32), jnp.ones((2 * ROPE_HALF,), f32)]).reshape(1, LANE)

    in_w = w_in_p.shape[1]
    head_spec = lambda width: pl.BlockSpec((None, hq, tt, width),
                                           lambda i: (i // tiles_per_batch, 0, i % tiles_per_batch, 0))
    q, k, v, mem_o = pl.pallas_call(
        _layer1_proj_kernel,
        grid=(n // tt,),
        in_specs=[tok(d), tok(1), _const_spec((1, d)), _const_spec((d, in_w)), _const_spec((1, Q_LORA)),
                  _const_spec((1, KV_LORA)), _const_spec((Q_LORA, hq * QK_PAD)),
                  _const_spec((KV_LORA, hq * (NOPE_DIM + V_DIM))), _const_spec((1, QK_PAD)),
                  _const_spec((1, QK_PAD)), _const_spec((1, LANE)), _const_spec((1, LANE)),
                  kn_spec(1), mv_spec, _const_spec((1, MEM_HEAD_DIM))],
        out_specs=[head_spec(QK_PAD), head_spec(QK_PAD), head_spec(V_DIM), tok(MEM_W)],
        out_shape=[jax.ShapeDtypeStruct((b, hq, s, QK_PAD), jnp.bfloat16),
                   jax.ShapeDtypeStruct((b, hq, s, QK_PAD), jnp.bfloat16),
                   jax.ShapeDtypeStruct((b, hq, s, V_DIM), jnp.bfloat16),
                   jax.ShapeDtypeStruct((n, MEM_W), jnp.bfloat16)],
        compiler_params=_params(("arbitrary",)),
        name="layer1_proj",
    )(x2, positions.reshape(n, 1), row(norm1_g[1]), _bf(w_in_p), row(b_q_norm_g[0]), row(b_kv_norm_g[0]),
      _bf(wq_p), _bf(wkv_p), pad_gain(b_qn_g[0]).astype(f32), pad_gain(b_kn_g[0]).astype(f32),
      inv_l, sgn_l, kn_all, mem_v, row(mem_qn_g[1]))

    ta = ATTN_TILE
    qb = s // ta
    attn = pl.pallas_call(
        _attn_kernel,
        grid=(b, hq, qb),
        in_specs=[pl.BlockSpec((None, None, ta, QK_PAD), lambda bi, hi, i: (bi, hi, i, 0)),
                  pl.BlockSpec((None, None, s, QK_PAD), lambda bi, hi, i: (bi, hi, 0, 0)),
                  pl.BlockSpec((None, None, s, V_DIM), lambda bi, hi, i: (bi, hi, 0, 0))],
        out_specs=pl.BlockSpec((ta, V_DIM), lambda bi, hi, i: (bi * qb + i, hi)),
        out_shape=jax.ShapeDtypeStruct((n, hq * V_DIM), jnp.bfloat16),
        compiler_params=_params(("arbitrary", "arbitrary", "arbitrary")),
        name="causal_attention",
    )(q, k, v)

    wr1, br1 = _router_weights(moe_w_group[1], moe_b_group[1], moe_w_expert[1], moe_b_expert[1])
    x2, xn, slab, counts = pl.pallas_call(
        _layer1_out_kernel,
        grid=(n // tt,),
        in_specs=[tok(d), tok(hq * V_DIM), tok(MEM_W), _const_spec((hq * V_DIM + MEM_W, d)),
                  _const_spec((1, d)), _const_spec((d, LANE)), _const_spec((1, LANE))],
        out_specs=route_out_specs,
        out_shape=route_out_shape,
        scratch_shapes=route_scratch,
        compiler_params=_params(("arbitrary",)),
        name="layer1_out",
    )(x2, attn, mem_o, _bf(b_w_out[0]), row(norm2_g[1]), wr1, br1)
    x2 = _moe(x2, xn, slab, counts, moe_w_gate[1], moe_w_up[1], moe_w_down[1])
    return x2.reshape(b, s, d)
```

```python
import jax
import jax.numpy as jnp
from jax import lax
from jax.experimental import pallas as pl
from jax.experimental.pallas import tpu as pltpu
from jax.experimental.pallas import tpu_sc as plsc

EPS = 1e-6
LANE = 128
MEM_HEADS = 4
MEM_HEAD_DIM = 128
MEM_W = MEM_HEADS * MEM_HEAD_DIM
CHUNK = 128
A_GROUPS = 8
MLA_HEADS = 8
Q_LORA = 512
KV_LORA = 256
NOPE_DIM = 128
ROPE_DIM = 64
ROPE_HALF = ROPE_DIM // 2
V_DIM = 128
QK_DIM = NOPE_DIM + ROPE_DIM
QK_PAD = 2 * LANE
ROPE_BASE = 10000.0
N_GROUPS = 4
EXPERTS_PER_GROUP = 8
N_EXPERTS = N_GROUPS * EXPERTS_PER_GROUP
EXPERT_FF = 256
LOG2E = 1.4426950408889634

TOKEN_TILE = 256
ROW_TILE = 256
ATTN_TILE = 512
ATTN_HEADS_PER_STEP = 2
ATTN_SUM_ROWS = 16
VMEM_LIMIT = 56 * 1024 * 1024
NEG_BIG = -1e30

SC_CORES = 2
SC_SUBCORES = 16
SC_WORKERS = SC_CORES * SC_SUBCORES
SC_INDEX_GROUP = 128
SC_CHUNK = 64

_NT = (((1,), (1,)), ((), ()))


def _const_spec(shape):
    nd = len(shape)
    return pl.BlockSpec(shape, lambda *_: (0,) * nd, pipeline_mode=pl.Buffered(1))


def _params(sem):
    return pltpu.CompilerParams(dimension_semantics=sem, vmem_limit_bytes=VMEM_LIMIT)


def _rms(x, g):
    return x * lax.rsqrt(jnp.mean(x * x, axis=-1, keepdims=True) + EPS) * g


def _gelu(x):
    return 0.5 * x * (1.0 + lax.erf(x * (2.0 ** -0.5)))


def _bf(x):
    return x.astype(jnp.bfloat16)


def _dot(a, b):
    return jnp.dot(a, b, preferred_element_type=jnp.float32)


def _memkv_kernel(mem_ref, g_ref, w_ref, kng_ref, kn_ref, v_ref):
    h = _bf(_rms(mem_ref[...], g_ref[...]))
    kv = _dot(h, w_ref[...])
    v_ref[...] = _bf(kv[:, MEM_W:])
    for layer in range(kn_ref.shape[0]):
        g = kng_ref[layer]
        for hh in range(MEM_HEADS):
            k = kv[:, hh * LANE:(hh + 1) * LANE]
            kn_ref[layer, :, hh * LANE:(hh + 1) * LANE] = _bf(_rms(k, g))


def _mem_attention(qm, kn_ref, v_ref, qg):
    outs = []
    for hh in range(MEM_HEADS):
        sl = slice(hh * LANE, (hh + 1) * LANE)
        q = _rms(qm[:, sl], qg) * (MEM_HEAD_DIM ** -0.5)
        s = lax.dot_general(_bf(q), kn_ref[:, sl], _NT, preferred_element_type=jnp.float32)
        p = jnp.exp(s - jnp.max(s, axis=-1, keepdims=True))
        l = jnp.sum(p, axis=-1, keepdims=True)
        outs.append(_dot(_bf(p), v_ref[:, sl]) / l)
    return jnp.concatenate(outs, axis=-1)


def _pack_rows(x):
    w = x.shape[1] // 2
    bits = lambda v: lax.bitcast_convert_type(_bf(v).astype(jnp.float32), jnp.uint32)
    return (bits(x[:, :w]) >> 16) | (bits(x[:, w:]) & jnp.uint32(0xFFFF0000))


def _unpack_rows(p):
    lo = lax.bitcast_convert_type(p << 16, jnp.float32)
    hi = lax.bitcast_convert_type(p & jnp.uint32(0xFFFF0000), jnp.float32)
    return jnp.concatenate([lo, hi], axis=-1)


def _route(x, g2_ref, wr_ref, br_ref, carry_ref, xn_ref, slab_ref, cnt_ref):
    t = x.shape[0]

    @pl.when(pl.program_id(0) == 0)
    def _():
        carry_ref[...] = jnp.zeros_like(carry_ref)

    xn = _rms(x, g2_ref[...])
    xn_ref[...] = _pack_rows(xn)
    logits = _dot(_bf(xn), wr_ref[...]) + br_ref[...]
    lane = lax.broadcasted_iota(jnp.int32, (t, LANE), 1)

    def first_max(v):
        m = jnp.max(v, axis=-1, keepdims=True)
        idx = jnp.min(jnp.where(v == m, lane, LANE), axis=-1, keepdims=True)
        return m, idx

    lg = jnp.where(lane < N_GROUPS, logits, NEG_BIG)
    gmax, gidx = first_max(lg)
    g_w = 1.0 / jnp.sum(jnp.exp(lg - gmax), axis=-1, keepdims=True)

    eid = lane - N_GROUPS
    in_grp = (eid >= 0) & (eid < N_EXPERTS) & ((eid >> 3) == gidx)
    le = jnp.where(in_grp, logits, NEG_BIG)
    m1, i1 = first_max(le)
    m2, i2 = first_max(jnp.where(lane == i1, NEG_BIG, le))
    r = jnp.exp(m2 - m1)
    w1 = g_w / (1.0 + r)
    w2 = w1 * r
    e1 = i1 - N_GROUPS
    e2 = i2 - N_GROUPS

    oh1 = lane == e1
    oh2 = lane == e2
    oh = jnp.where(oh1 | oh2, 1.0, 0.0)
    row = lax.broadcasted_iota(jnp.int32, (t, t), 0)
    col = lax.broadcasted_iota(jnp.int32, (t, t), 1)
    earlier = jnp.where(row > col, 1.0, 0.0).astype(jnp.bfloat16)
    before = _dot(earlier, _bf(oh)) + carry_ref[...]
    r1 = jnp.sum(jnp.where(oh1, before, 0.0), axis=-1, keepdims=True)
    r2 = jnp.sum(jnp.where(oh2, before, 0.0), axis=-1, keepdims=True)
    carry_ref[...] += jnp.sum(oh, axis=0, keepdims=True)
    cnt_ref[...] = carry_ref[...]

    vals = (e1.astype(jnp.float32), e2.astype(jnp.float32), w1, w2, r1, r2)
    slab = jnp.zeros((t, LANE), jnp.float32)
    for i, v in enumerate(vals):
        slab = jnp.where(lane == i, v, slab)
    slab_ref[...] = slab


def _layer0_kernel(x_ref, g1_ref, win_ref, lng_ref, lnb_ref, ws_ref, bs_ref, kn_ref, v_ref, qg_ref,
                   wout_ref, g2_ref, wr_ref, br_ref,
                   xo_ref, xn_ref, slab_ref, cnt_ref, carry_ref):
    x = x_ref[...]
    t, d = x.shape
    h = _bf(_rms(x, g1_ref[...]))
    z = _dot(h, win_ref[...])
    u = _gelu(z[:, :d])
    v = _gelu(z[:, d:2 * d])
    mu = jnp.mean(v, axis=-1, keepdims=True)
    vc = v - mu
    var = jnp.mean(vc * vc, axis=-1, keepdims=True)
    v = _bf(vc * lax.rsqrt(var + EPS) * lng_ref[...] + lnb_ref[...])

    row = lax.broadcasted_iota(jnp.int32, (CHUNK, CHUNK), 0)
    col = lax.broadcasted_iota(jnp.int32, (CHUNK, CHUNK), 1)
    causal = row >= col
    rows = []
    for c in range(t // CHUNK):
        cols = []
        for g in range(A_GROUPS):
            w = jnp.where(causal, ws_ref[g], jnp.zeros((), ws_ref.dtype))
            cols.append(_dot(w, v[c * CHUNK:(c + 1) * CHUNK, g * LANE:(g + 1) * LANE]))
        rows.append(jnp.concatenate(cols, axis=-1) + bs_ref[...])
    mixed = jnp.concatenate(rows, axis=0)
    mix = _bf(u * mixed)

    mem = _bf(_mem_attention(z[:, 2 * d:], kn_ref, v_ref, qg_ref[...]))
    xo = x + _dot(mix, wout_ref[:d, :]) + _dot(mem, wout_ref[d:, :])
    xo_ref[...] = xo
    _route(xo, g2_ref, wr_ref, br_ref, carry_ref, xn_ref, slab_ref, cnt_ref)


def _layer1_proj_kernel(x_ref, pos_ref, g1_ref, win_ref, qng_ref, kvng_ref, wq_ref, wkv_ref,
                        qg_ref, kg_ref, inv_ref, sgn_ref, kn_ref, mv_ref, mqg_ref,
                        q_ref, k_ref, v_ref, mem_ref):
    x = x_ref[...]
    h = _bf(_rms(x, g1_ref[...]))
    z = _dot(h, win_ref[...])
    o1 = Q_LORA
    o2 = o1 + KV_LORA
    o3 = o2 + MEM_W
    cq = _bf(_rms(z[:, :o1], qng_ref[...]))
    ckv = _bf(_rms(z[:, o1:o2], kvng_ref[...]))
    k_rope = z[:, o3:o3 + LANE]

    ang = pos_ref[...].astype(jnp.float32) * inv_ref[...]
    cos = jnp.cos(ang)
    sin = jnp.sin(ang) * sgn_ref[...]

    def rope(r):
        return r * cos + pltpu.roll(r, LANE // 2, 1) * sin

    q = _dot(cq, wq_ref[...])
    kv = _dot(ckv, wkv_ref[...])
    qg = qg_ref[...]
    kg = kg_ref[...]
    kr_ss = jnp.sum(k_rope * k_rope, axis=-1, keepdims=True)
    kr = rope(k_rope * kg[:, LANE:])
    q_scale = (QK_DIM ** -0.5) * LOG2E
    for hh in range(MLA_HEADS):
        qh = q[:, hh * QK_PAD:(hh + 1) * QK_PAD]
        rq = lax.rsqrt(jnp.sum(qh * qh, axis=-1, keepdims=True) * (1.0 / QK_DIM) + EPS) * q_scale
        qh = qh * rq * qg
        q_ref[hh, :, :LANE] = _bf(qh[:, :LANE])
        q_ref[hh, :, LANE:] = _bf(rope(qh[:, LANE:]))
        kn = kv[:, hh * LANE:(hh + 1) * LANE]
        rk = lax.rsqrt((jnp.sum(kn * kn, axis=-1, keepdims=True) + kr_ss) * (1.0 / QK_DIM) + EPS)
        k_ref[hh, :, :LANE] = _bf(kn * rk * kg[:, :LANE])
        k_ref[hh, :, LANE:] = _bf(kr * rk)
        v_ref[hh] = _bf(kv[:, (MLA_HEADS + hh) * LANE:(MLA_HEADS + hh + 1) * LANE].T)

    mem_ref[...] = _bf(_mem_attention(z[:, o2:o3], kn_ref, mv_ref, mqg_ref[...]))


def _attn_kernel(q_ref, k_ref, vt_ref, o_ref, m_ref, acc_ref):
    i = pl.program_id(2)
    heads, tq = q_ref.shape[0], q_ref.shape[1]
    m_ref[...] = jnp.full_like(m_ref, NEG_BIG)
    acc_ref[...] = jnp.zeros_like(acc_ref)

    def scores(hh, j, masked):
        start = pl.multiple_of(j * tq, tq)
        s = lax.dot_general(k_ref[hh, pl.ds(start, tq), :], q_ref[hh], _NT,
                            preferred_element_type=jnp.float32)
        if masked:
            key = lax.broadcasted_iota(jnp.int32, (tq, tq), 0)
            qry = lax.broadcasted_iota(jnp.int32, (tq, tq), 1)
            s = jnp.where(key <= qry, s, NEG_BIG)
        return s

    ones_rows = jnp.ones((ATTN_SUM_ROWS, tq), jnp.bfloat16)

    def update(hh, j, s):
        m = m_ref[hh]
        m_new = jnp.maximum(m, jnp.max(s, axis=0, keepdims=True))
        alpha = jnp.exp2(m - m_new)
        p = _bf(jnp.exp2(s - m_new))
        m_ref[hh] = m_new
        vt1 = jnp.concatenate([vt_ref[hh, j], ones_rows], axis=0)
        acc_ref[hh] = alpha * acc_ref[hh] + _dot(vt1, p)

    def run(items, masked):
        ss = {}
        for t in range(len(items) + 2):
            if t < len(items):
                ss[t] = scores(*items[t], masked)
            if t >= 2:
                update(*items[t - 2], ss.pop(t - 2))

    def body(jj, c):
        run([(hh, 2 * jj + u) for u in range(2) for hh in range(heads)], False)
        return c

    lax.fori_loop(0, i // 2, body, 0)

    @pl.when(i % 2 == 1)
    def _():
        run([(hh, i - 1) for hh in range(heads)], False)

    run([(hh, i) for hh in range(heads)], True)
    for hh in range(heads):
        acc = acc_ref[hh]
        o_ref[:, hh * V_DIM:(hh + 1) * V_DIM] = _bf((acc[:V_DIM] / acc[V_DIM:V_DIM + 1]).T)


def _layer1_out_kernel(x_ref, o_ref, mem_ref, wout_ref, g2_ref, wr_ref, br_ref,
                       xo_ref, xn_ref, slab_ref, cnt_ref, carry_ref):
    d = o_ref.shape[1]
    xo = x_ref[...] + _dot(o_ref[...], wout_ref[:d, :]) + _dot(mem_ref[...], wout_ref[d:, :])
    xo_ref[...] = xo
    _route(xo, g2_ref, wr_ref, br_ref, carry_ref, xn_ref, slab_ref, cnt_ref)


def _sc_mesh():
    return plsc.VectorSubcoreMesh(core_axis_name="c", subcore_axis_name="s")


def _sc_worker_base(rows_per_worker):
    return (lax.axis_index("c") * SC_SUBCORES + lax.axis_index("s")) * rows_per_worker


def _sc_scatter_rows(x, idx0, idx1, p_rows):
    n, w = x.shape
    per = n // SC_WORKERS
    chunks = SC_INDEX_GROUP // SC_CHUNK
    assert n % SC_WORKERS == 0 and per % SC_INDEX_GROUP == 0

    @pl.kernel(out_type=jax.ShapeDtypeStruct((p_rows, w), x.dtype), mesh=_sc_mesh(),
               scratch_types=[pltpu.VMEM((1, SC_INDEX_GROUP), jnp.int32),
                              pltpu.VMEM((1, SC_INDEX_GROUP), jnp.int32),
                              pltpu.VMEM((SC_CHUNK, w), x.dtype), pltpu.VMEM((SC_CHUNK, w), x.dtype),
                              pltpu.SemaphoreType.DMA, pltpu.SemaphoreType.DMA],
               name="moe_dispatch_sc")
    def scatter(x_hbm, i0_hbm, i1_hbm, o_hbm, i0_v, i1_v, buf_a, buf_b, sem_a, sem_b):
        base = _sc_worker_base(per)

        @pl.loop(0, per // SC_INDEX_GROUP)
        def _(g):
            off = pl.multiple_of(base + g * SC_INDEX_GROUP, SC_INDEX_GROUP)
            pltpu.sync_copy(i0_hbm.at[:, pl.ds(off, SC_INDEX_GROUP)], i0_v)
            pltpu.sync_copy(i1_hbm.at[:, pl.ds(off, SC_INDEX_GROUP)], i1_v)
            pending = []
            for c in range(chunks):
                buf, sem = ((buf_a, sem_a), (buf_b, sem_b))[c % 2]
                if c >= 2:
                    for cp in pending[c - 2]:
                        cp.wait()
                pltpu.sync_copy(x_hbm.at[pl.ds(off + c * SC_CHUNK, SC_CHUNK)], buf)
                sl = pl.ds(c * SC_CHUNK, SC_CHUNK)
                pending.append((pltpu.async_copy(buf, o_hbm.at[i0_v.at[0, sl]], sem),
                                pltpu.async_copy(buf, o_hbm.at[i1_v.at[0, sl]], sem)))
            for cps in pending[max(chunks - 2, 0):]:
                for cp in cps:
                    cp.wait()

    return scatter(x, idx0.reshape(1, n), idx1.reshape(1, n))


def _sc_gather_rows(table, idx):
    m = idx.shape[0]
    w = table.shape[1]
    per = m // SC_WORKERS
    chunks = SC_INDEX_GROUP // SC_CHUNK
    assert m % SC_WORKERS == 0 and per % SC_INDEX_GROUP == 0

    @pl.kernel(out_type=jax.ShapeDtypeStruct((m, w), table.dtype), mesh=_sc_mesh(),
               scratch_types=[pltpu.VMEM((1, SC_INDEX_GROUP), jnp.int32),
                              pltpu.VMEM((SC_CHUNK, w), table.dtype), pltpu.VMEM((SC_CHUNK, w), table.dtype),
                              pltpu.SemaphoreType.DMA, pltpu.SemaphoreType.DMA],
               name="moe_combine_sc")
    def gather(t_hbm, i_hbm, o_hbm, i_v, buf_a, buf_b, sem_a, sem_b):
        base = _sc_worker_base(per)

        @pl.loop(0, per // SC_INDEX_GROUP)
        def _(g):
            off = pl.multiple_of(base + g * SC_INDEX_GROUP, SC_INDEX_GROUP)
            pltpu.sync_copy(i_hbm.at[:, pl.ds(off, SC_INDEX_GROUP)], i_v)
            pending = []
            for c in range(chunks):
                buf, sem = ((buf_a, sem_a), (buf_b, sem_b))[c % 2]
                if c >= 2:
                    pending[c - 2].wait()
                pltpu.sync_copy(t_hbm.at[i_v.at[0, pl.ds(c * SC_CHUNK, SC_CHUNK)]], buf)
                pending.append(pltpu.async_copy(buf, o_hbm.at[pl.ds(off + c * SC_CHUNK, SC_CHUNK)], sem))
            for cp in pending[max(chunks - 2, 0):]:
                cp.wait()

    return gather(table, idx.reshape(1, m))


def _ffn_kernel(blk_ref, exp_ref, rows_ref, nvalid_ref, xs_ref, wg_ref, wu_ref, wd_ref, ys_ref,
                wg_bf, wu_bf, wd_bf):
    j = pl.program_id(0)

    @pl.when((j == 0) | (exp_ref[j] != exp_ref[jnp.maximum(j - 1, 0)]))
    def _():
        wg_bf[...] = _bf(wg_ref[...])
        wu_bf[...] = _bf(wu_ref[...])
        wd_bf[...] = _bf(wd_ref[...])

    @pl.when(j < nvalid_ref[0])
    def _():
        x = _unpack_rows(xs_ref[...])
        live = lax.broadcasted_iota(jnp.int32, x.shape, 0) < rows_ref[j]
        x = _bf(jnp.where(live, x, 0.0))
        g = _dot(x, wg_bf[...])
        u = _dot(x, wu_bf[...])
        act = _bf(g * jax.nn.sigmoid(g) * u)
        ys_ref[...] = _pack_rows(_dot(act, wd_bf[...]))

    @pl.when(j >= nvalid_ref[0])
    def _():
        ys_ref[...] = jnp.zeros_like(ys_ref)


def _combine_kernel(x_ref, slab_ref, y0_ref, y1_ref, out_ref):
    slab = slab_ref[...]
    out_ref[...] = (x_ref[...] + slab[:, 2:3] * _unpack_rows(y0_ref[...])
                    + slab[:, 3:4] * _unpack_rows(y1_ref[...]))


def _moe(x, xn, slab, counts, w_gate, w_up, w_down):
    n, d = x.shape
    wp = xn.shape[1]
    tt = TOKEN_TILE
    rt = ROW_TILE
    p_max = 2 * n + N_EXPERTS * rt
    n_tiles = p_max // rt

    cnt = counts[0, :N_EXPERTS].astype(jnp.int32)
    padded = ((cnt + rt - 1) // rt) * rt
    off = jnp.concatenate([jnp.zeros((1,), jnp.int32), jnp.cumsum(padded)]).astype(jnp.int32)
    eid = slab[:, 0:2].astype(jnp.int32)
    rank = slab[:, 4:6].astype(jnp.int32)
    pos = off[eid] + rank
    n_valid = off[N_EXPERTS] // rt
    tile_blk = jnp.minimum(jnp.arange(n_tiles, dtype=jnp.int32), n_valid - 1)
    tile_start = tile_blk * rt
    tile_exp = jnp.minimum(jnp.sum(off[1:][None, :] <= tile_start[:, None], axis=1),
                           N_EXPERTS - 1).astype(jnp.int32)
    tile_rows = jnp.clip(off[tile_exp] + cnt[tile_exp] - tile_start, 0, rt).astype(jnp.int32)

    xs = _sc_scatter_rows(xn, pos[:, 0], pos[:, 1], p_max)

    f = w_gate.shape[-1]
    ys = pl.pallas_call(
        _ffn_kernel,
        grid_spec=pltpu.PrefetchScalarGridSpec(
            num_scalar_prefetch=4, grid=(n_tiles,),
            in_specs=[pl.BlockSpec((rt, wp), lambda j, blk, ex, rw, nv: (blk[j], 0)),
                      pl.BlockSpec((None, d, f), lambda j, blk, ex, rw, nv: (ex[j], 0, 0)),
                      pl.BlockSpec((None, d, f), lambda j, blk, ex, rw, nv: (ex[j], 0, 0)),
                      pl.BlockSpec((None, f, d), lambda j, blk, ex, rw, nv: (ex[j], 0, 0))],
            out_specs=pl.BlockSpec((rt, wp), lambda j, blk, ex, rw, nv: (j, 0)),
            scratch_shapes=[pltpu.VMEM((d, f), jnp.bfloat16), pltpu.VMEM((d, f), jnp.bfloat16),
                            pltpu.VMEM((f, d), jnp.bfloat16)]),
        out_shape=jax.ShapeDtypeStruct((p_max, wp), jnp.uint32),
        compiler_params=_params(("arbitrary",)),
        name="moe_ffn",
    )(tile_blk, tile_exp, tile_rows, n_valid.reshape(1), xs, w_gate, w_up, w_down)

    picked = _sc_gather_rows(ys, jnp.concatenate([pos[:, 0], pos[:, 1]]))

    tok_spec = pl.BlockSpec((tt, d), lambda i: (i, 0))
    nb = n // tt
    return pl.pallas_call(
        _combine_kernel,
        grid=(nb,),
        in_specs=[tok_spec, pl.BlockSpec((tt, LANE), lambda i: (i, 0)),
                  pl.BlockSpec((tt, wp), lambda i: (i, 0)),
                  pl.BlockSpec((tt, wp), lambda i: (i + nb, 0))],
        out_specs=tok_spec,
        out_shape=jax.ShapeDtypeStruct((n, d), jnp.float32),
        compiler_params=_params(("arbitrary",)),
        name="moe_combine",
    )(x, slab, picked, picked)


def _router_weights(w_group, b_group, w_expert, b_expert):
    d = w_group.shape[0]
    pad = LANE - N_GROUPS - N_EXPERTS
    wr = jnp.concatenate([w_group, w_expert, jnp.zeros((d, pad), w_group.dtype)], axis=1)
    br = jnp.concatenate([b_group, b_expert, jnp.zeros((pad,), b_group.dtype)])
    return _bf(wr), br.reshape(1, LANE).astype(jnp.float32)


def _rope_lanes(vec_half):
    z = jnp.zeros_like(vec_half)
    return jnp.concatenate([vec_half, z, vec_half, z], axis=-1)


def _pad_rope_cols(w):
    z = jnp.zeros(w.shape[:-1] + (ROPE_HALF,), w.dtype)
    return jnp.concatenate([w[..., :ROPE_HALF], z, w[..., ROPE_HALF:], z], axis=-1)


def kernel(x, mem, positions, mem_norm_g, w_mem_kv, mem_qn_g, mem_kn_g, norm1_g, norm2_g, a_w_in, a_ln_g, a_ln_b, a_w_s, a_b_s, a_w_out, b_w_in, b_q_norm_g, b_kv_norm_g, b_w_q_up, b_w_kv_up, b_qn_g, b_kn_g, b_w_out, moe_w_group, moe_b_group, moe_w_expert, moe_b_expert, moe_w_gate, moe_w_up, moe_w_down):
    b, s, d = x.shape
    m = mem.shape[1]
    n = b * s
    depth = norm1_g.shape[0]
    tt = TOKEN_TILE
    tiles_per_batch = s // tt
    assert depth == 2 and s % tt == 0 and s % ATTN_TILE == 0 and d == A_GROUPS * LANE
    f32 = jnp.float32
    row = lambda v: v.reshape(1, -1).astype(f32)

    kn_all, mem_v = pl.pallas_call(
        _memkv_kernel,
        grid=(b,),
        in_specs=[pl.BlockSpec((m, d), lambda i: (i, 0)), _const_spec((1, d)),
                  _const_spec((d, 2 * MEM_W)), _const_spec((depth, 1, MEM_HEAD_DIM))],
        out_specs=[pl.BlockSpec((depth, m, MEM_W), lambda i: (0, i, 0)),
                   pl.BlockSpec((m, MEM_W), lambda i: (i, 0))],
        out_shape=[jax.ShapeDtypeStruct((depth, b * m, MEM_W), jnp.bfloat16),
                   jax.ShapeDtypeStruct((b * m, MEM_W), jnp.bfloat16)],
        compiler_params=_params(("arbitrary",)),
        name="mem_kv",
    )(mem.reshape(b * m, d), row(mem_norm_g), _bf(w_mem_kv), mem_kn_g.reshape(depth, 1, MEM_HEAD_DIM))

    tok = lambda width: pl.BlockSpec((tt, width), lambda i: (i, 0))
    kn_spec = lambda layer: pl.BlockSpec((None, m, MEM_W), lambda i: (layer, i // tiles_per_batch, 0))
    mv_spec = pl.BlockSpec((m, MEM_W), lambda i: (i // tiles_per_batch, 0))
    route_out_specs = [tok(d), tok(d // 2), tok(LANE), pl.BlockSpec((1, LANE), lambda i: (0, 0))]
    route_out_shape = [jax.ShapeDtypeStruct((n, d), f32), jax.ShapeDtypeStruct((n, d // 2), jnp.uint32),
                       jax.ShapeDtypeStruct((n, LANE), f32), jax.ShapeDtypeStruct((1, LANE), f32)]
    route_scratch = [pltpu.VMEM((1, LANE), f32)]

    x2 = x.reshape(n, d)

    wr0, br0 = _router_weights(moe_w_group[0], moe_b_group[0], moe_w_expert[0], moe_b_expert[0])
    a_in = a_w_in.shape[-1]
    bias_s = jnp.repeat(a_b_s[0].T, LANE, axis=1).astype(f32)
    x2, xn, slab, counts = pl.pallas_call(
        _layer0_kernel,
        grid=(n // tt,),
        in_specs=[tok(d), _const_spec((1, d)), _const_spec((d, a_in)), _const_spec((1, d)),
                  _const_spec((1, d)), _const_spec((A_GROUPS, CHUNK, CHUNK)), _const_spec((CHUNK, d)),
                  kn_spec(0), mv_spec, _const_spec((1, MEM_HEAD_DIM)),
                  _const_spec((d + MEM_W, d)), _const_spec((1, d)), _const_spec((d, LANE)),
                  _const_spec((1, LANE))],
        out_specs=route_out_specs,
        out_shape=route_out_shape,
        scratch_shapes=route_scratch,
        compiler_params=_params(("arbitrary",)),
        name="layer0_mixer",
    )(x2, row(norm1_g[0]), _bf(a_w_in[0]), row(a_ln_g[0]), row(a_ln_b[0]), _bf(a_w_s[0]), bias_s,
      kn_all, mem_v, row(mem_qn_g[0]), _bf(a_w_out[0]), row(norm2_g[0]), wr0, br0)
    x2 = _moe(x2, xn, slab, counts, moe_w_gate[0], moe_w_up[0], moe_w_down[0])

    hq = MLA_HEADS
    o1, o2, o3 = Q_LORA, Q_LORA + KV_LORA, Q_LORA + KV_LORA + ROPE_DIM
    w_in = b_w_in[0]
    w_in_p = jnp.concatenate([w_in[:, :o2], w_in[:, o3:], _pad_rope_cols(w_in[:, o2:o3])], axis=1)
    wq = b_w_q_up[0].reshape(Q_LORA, hq, QK_DIM)
    wq_p = jnp.concatenate([wq[..., :NOPE_DIM], _pad_rope_cols(wq[..., NOPE_DIM:])], axis=-1)
    wq_p = wq_p.reshape(Q_LORA, hq * QK_PAD)
    wkv = b_w_kv_up[0].reshape(KV_LORA, hq, NOPE_DIM + V_DIM)
    wkv_p = jnp.concatenate([wkv[..., :NOPE_DIM].reshape(KV_LORA, hq * NOPE_DIM),
                             wkv[..., NOPE_DIM:].reshape(KV_LORA, hq * V_DIM)], axis=1)
    pad_gain = lambda g: jnp.concatenate([g[:NOPE_DIM], _pad_rope_cols(g[NOPE_DIM:])]).reshape(1, QK_PAD)
    half = jnp.arange(ROPE_HALF, dtype=f32)
    inv = ROPE_BASE ** (-(half * 2.0 / ROPE_DIM))
    inv_l = _rope_lanes(inv).reshape(1, LANE)
    sgn_l = jnp.concatenate([-jnp.ones((2 * ROPE_HALF,), f32), jnp.ones((2 * ROPE_HALF,), f32)]).reshape(1, LANE)

    in_w = w_in_p.shape[1]
    head_spec = lambda width: pl.BlockSpec((None, hq, tt, width),
                                           lambda i: (i // tiles_per_batch, 0, i % tiles_per_batch, 0))
    ta = ATTN_TILE
    per_ta = ta // tt

    def vt_index(i):
        t = i % tiles_per_batch
        return (i // tiles_per_batch, 0, t // per_ta, 0, t % per_ta)

    vt_spec = pl.BlockSpec((None, hq, None, V_DIM, tt), vt_index)
    q, k, vt, mem_o = pl.pallas_call(
        _layer1_proj_kernel,
        grid=(n // tt,),
        in_specs=[tok(d), tok(1), _const_spec((1, d)), _const_spec((d, in_w)), _const_spec((1, Q_LORA)),
                  _const_spec((1, KV_LORA)), _const_spec((Q_LORA, hq * QK_PAD)),
                  _const_spec((KV_LORA, hq * (NOPE_DIM + V_DIM))), _const_spec((1, QK_PAD)),
                  _const_spec((1, QK_PAD)), _const_spec((1, LANE)), _const_spec((1, LANE)),
                  kn_spec(1), mv_spec, _const_spec((1, MEM_HEAD_DIM))],
        out_specs=[head_spec(QK_PAD), head_spec(QK_PAD), vt_spec, tok(MEM_W)],
        out_shape=[jax.ShapeDtypeStruct((b, hq, s, QK_PAD), jnp.bfloat16),
                   jax.ShapeDtypeStruct((b, hq, s, QK_PAD), jnp.bfloat16),
                   jax.ShapeDtypeStruct((b, hq, s // ta, V_DIM, ta), jnp.bfloat16),
                   jax.ShapeDtypeStruct((n, MEM_W), jnp.bfloat16)],
        compiler_params=_params(("arbitrary",)),
        name="layer1_proj",
    )(x2, positions.reshape(n, 1), row(norm1_g[1]), _bf(w_in_p), row(b_q_norm_g[0]), row(b_kv_norm_g[0]),
      _bf(wq_p), _bf(wkv_p), pad_gain(b_qn_g[0]).astype(f32), pad_gain(b_kn_g[0]).astype(f32),
      inv_l, sgn_l, kn_all, mem_v, row(mem_qn_g[1]))

    qb = s // ta
    hp = ATTN_HEADS_PER_STEP
    attn = pl.pallas_call(
        _attn_kernel,
        grid=(b, hq // hp, qb),
        in_specs=[pl.BlockSpec((None, hp, ta, QK_PAD), lambda bi, hi, i: (bi, hi, i, 0)),
                  pl.BlockSpec((None, hp, s, QK_PAD), lambda bi, hi, i: (bi, hi, 0, 0)),
                  pl.BlockSpec((None, hp, qb, V_DIM, ta), lambda bi, hi, i: (bi, hi, 0, 0, 0))],
        out_specs=pl.BlockSpec((ta, hp * V_DIM), lambda bi, hi, i: (bi * qb + i, hi)),
        out_shape=jax.ShapeDtypeStruct((n, hq * V_DIM), jnp.bfloat16),
        scratch_shapes=[pltpu.VMEM((hp, 1, ta), f32), pltpu.VMEM((hp, V_DIM + ATTN_SUM_ROWS, ta), f32)],
        compiler_params=_params(("arbitrary", "arbitrary", "arbitrary")),
        name="causal_attention",
    )(q, k, vt)

    wr1, br1 = _router_weights(moe_w_group[1], moe_b_group[1], moe_w_expert[1], moe_b_expert[1])
    x2, xn, slab, counts = pl.pallas_call(
        _layer1_out_kernel,
        grid=(n // tt,),
        in_specs=[tok(d), tok(hq * V_DIM), tok(MEM_W), _const_spec((hq * V_DIM + MEM_W, d)),
                  _const_spec((1, d)), _const_spec((d, LANE)), _const_spec((1, LANE))],
        out_specs=route_out_specs,
        out_shape=route_out_shape,
        scratch_shapes=route_scratch,
        compiler_params=_params(("arbitrary",)),
        name="layer1_out",
    )(x2, attn, mem_o, _bf(b_w_out[0]), row(norm2_g[1]), wr1, br1)
    x2 = _moe(x2, xn, slab, counts, moe_w_gate[1], moe_w_up[1], moe_w_down[1])
    return x2.reshape(b, s, d)
```

```python
import jax
import jax.numpy as jnp
from jax import lax
from jax.experimental import pallas as pl
from jax.experimental.pallas import tpu as pltpu
from jax.experimental.pallas import tpu_sc as plsc

EPS = 1e-6
LANE = 128
SUBLANE = 8
MEM_HEADS = 4
MEM_HEAD_DIM = 128
MEM_W = MEM_HEADS * MEM_HEAD_DIM
CHUNK = 128
A_GROUPS = 8
MLA_HEADS = 8
Q_LORA = 512
KV_LORA = 256
NOPE_DIM = 128
ROPE_DIM = 64
ROPE_HALF = ROPE_DIM // 2
V_DIM = 128
QK_DIM = NOPE_DIM + ROPE_DIM
QK_PAD = 2 * LANE
ROPE_BASE = 10000.0
N_GROUPS = 4
EXPERTS_PER_GROUP = 8
N_EXPERTS = N_GROUPS * EXPERTS_PER_GROUP
EXPERT_FF = 256
LOG2E = 1.4426950408889634

TOKEN_TILE = 512
SUB_TILE = 256
ROW_TILE = 256
ROW_TILES_PER_STEP = 2
ATTN_TILE = 512
ATTN_HEADS_PER_STEP = 2
ATTN_SUM_ROWS = 16
VMEM_LIMIT = 56 * 1024 * 1024
NEG_BIG = -1e30

SC_CORES = 2
SC_SUBCORES = 16
SC_WORKERS = SC_CORES * SC_SUBCORES
SC_INDEX_GROUP = 128
SC_CHUNK = 64

_NT = (((1,), (1,)), ((), ()))


def _const_spec(shape):
    nd = len(shape)
    return pl.BlockSpec(shape, lambda *_: (0,) * nd, pipeline_mode=pl.Buffered(1))


def _params(sem):
    return pltpu.CompilerParams(dimension_semantics=sem, vmem_limit_bytes=VMEM_LIMIT)


def _run_staggered(gens):
    waiting = list(gens)
    active = []
    while waiting or active:
        if waiting:
            active.append(waiting.pop(0))
        for g in list(active):
            try:
                next(g)
            except StopIteration:
                active.remove(g)


def _sub_rows(t):
    return [pl.ds(k * SUB_TILE, SUB_TILE) for k in range(t // SUB_TILE)]


def _rms(x, g):
    return x * lax.rsqrt(jnp.mean(x * x, axis=-1, keepdims=True) + EPS) * g


def _gelu(x):
    return 0.5 * x * (1.0 + lax.erf(x * (2.0 ** -0.5)))


def _bf(x):
    return x.astype(jnp.bfloat16)


def _dot(a, b):
    return jnp.dot(a, b, preferred_element_type=jnp.float32)


def _memkv_kernel(mem_ref, g_ref, w_ref, kng_ref, kn_ref, v_ref):
    h = _bf(_rms(mem_ref[...], g_ref[...]))
    kv = _dot(h, w_ref[...])
    v_ref[...] = _bf(kv[:, MEM_W:])
    for layer in range(kn_ref.shape[0]):
        g = kng_ref[layer]
        for hh in range(MEM_HEADS):
            k = kv[:, hh * LANE:(hh + 1) * LANE]
            kn_ref[layer, :, hh * LANE:(hh + 1) * LANE] = _bf(_rms(k, g))


def _mem_attention(qm, kn_ref, v_ref, qg):
    outs = []
    for hh in range(MEM_HEADS):
        sl = slice(hh * LANE, (hh + 1) * LANE)
        q = _rms(qm[:, sl], qg) * (MEM_HEAD_DIM ** -0.5)
        s = lax.dot_general(_bf(q), kn_ref[:, sl], _NT, preferred_element_type=jnp.float32)
        p = jnp.exp(s - jnp.max(s, axis=-1, keepdims=True))
        l = jnp.sum(p, axis=-1, keepdims=True)
        outs.append(_dot(_bf(p), v_ref[:, sl]) / l)
    return jnp.concatenate(outs, axis=-1)


def _pack_rows(x):
    w = x.shape[1] // 2
    bits = lambda v: lax.bitcast_convert_type(_bf(v).astype(jnp.float32), jnp.uint32)
    return (bits(x[:, :w]) >> 16) | (bits(x[:, w:]) & jnp.uint32(0xFFFF0000))


def _unpack_rows(p):
    lo = lax.bitcast_convert_type(p << 16, jnp.float32)
    hi = lax.bitcast_convert_type(p & jnp.uint32(0xFFFF0000), jnp.float32)
    return jnp.concatenate([lo, hi], axis=-1)


def _route_stages(x, rows, g2_ref, wr_ref, br_ref, carry_ref, xn_ref, slab_ref, slabt_ref, cnt_ref):
    t = x.shape[0]
    xn = _rms(x, g2_ref[...])
    xn_ref[rows, :] = _pack_rows(xn)
    logits = _dot(_bf(xn), wr_ref[...]) + br_ref[...]
    yield
    lane = lax.broadcasted_iota(jnp.int32, (t, LANE), 1)

    def first_max(v):
        m = jnp.max(v, axis=-1, keepdims=True)
        idx = jnp.min(jnp.where(v == m, lane, LANE), axis=-1, keepdims=True)
        return m, idx

    lg = jnp.where(lane < N_GROUPS, logits, NEG_BIG)
    gmax, gidx = first_max(lg)
    g_w = 1.0 / jnp.sum(jnp.exp(lg - gmax), axis=-1, keepdims=True)

    eid = lane - N_GROUPS
    in_grp = (eid >= 0) & (eid < N_EXPERTS) & ((eid >> 3) == gidx)
    le = jnp.where(in_grp, logits, NEG_BIG)
    m1, i1 = first_max(le)
    yield
    m2, i2 = first_max(jnp.where(lane == i1, NEG_BIG, le))
    r = jnp.exp(m2 - m1)
    w1 = g_w / (1.0 + r)
    w2 = w1 * r
    e1 = i1 - N_GROUPS
    e2 = i2 - N_GROUPS
    yield

    oh1 = lane == e1
    oh2 = lane == e2
    oh = jnp.where(oh1 | oh2, 1.0, 0.0)
    row = lax.broadcasted_iota(jnp.int32, (t, t), 0)
    col = lax.broadcasted_iota(jnp.int32, (t, t), 1)
    earlier = jnp.where(row > col, 1.0, 0.0).astype(jnp.bfloat16)
    before = _dot(earlier, _bf(oh)) + carry_ref[...]
    r1 = jnp.sum(jnp.where(oh1, before, 0.0), axis=-1, keepdims=True)
    r2 = jnp.sum(jnp.where(oh2, before, 0.0), axis=-1, keepdims=True)
    carry_ref[...] += jnp.sum(oh, axis=0, keepdims=True)
    cnt_ref[...] = carry_ref[...]

    vals = (e1.astype(jnp.float32), e2.astype(jnp.float32), w1, w2, r1, r2)
    slab = jnp.zeros((t, LANE), jnp.float32)
    for i, v in enumerate(vals):
        slab = jnp.where(lane == i, v, slab)
    slab_ref[rows, :] = slab
    slabt_ref[:, rows] = slab.T[:SUBLANE, :]


def _reset_carry(carry_ref):
    @pl.when(pl.program_id(0) == 0)
    def _():
        carry_ref[...] = jnp.zeros_like(carry_ref)


def _layer0_kernel(x_ref, g1_ref, win_ref, lng_ref, lnb_ref, ws_ref, bs_ref, kn_ref, v_ref, qg_ref,
                   wout_ref, g2_ref, wr_ref, br_ref,
                   xo_ref, xn_ref, slab_ref, slabt_ref, cnt_ref, carry_ref):
    d = x_ref.shape[1]
    _reset_carry(carry_ref)
    row = lax.broadcasted_iota(jnp.int32, (CHUNK, CHUNK), 0)
    col = lax.broadcasted_iota(jnp.int32, (CHUNK, CHUNK), 1)
    causal = row >= col

    def stages(rows):
        x = x_ref[rows, :]
        t = x.shape[0]
        h = _bf(_rms(x, g1_ref[...]))
        z = _dot(h, win_ref[...])
        yield
        u = _gelu(z[:, :d])
        v = _gelu(z[:, d:2 * d])
        mu = jnp.mean(v, axis=-1, keepdims=True)
        vc = v - mu
        var = jnp.mean(vc * vc, axis=-1, keepdims=True)
        v = _bf(vc * lax.rsqrt(var + EPS) * lng_ref[...] + lnb_ref[...])
        yield
        chunks = []
        for c in range(t // CHUNK):
            cols = []
            for g in range(A_GROUPS):
                w = jnp.where(causal, ws_ref[g], jnp.zeros((), ws_ref.dtype))
                cols.append(_dot(w, v[c * CHUNK:(c + 1) * CHUNK, g * LANE:(g + 1) * LANE]))
            chunks.append(jnp.concatenate(cols, axis=-1) + bs_ref[...])
        mix = _bf(u * jnp.concatenate(chunks, axis=0))
        yield
        mem = _bf(_mem_attention(z[:, 2 * d:], kn_ref, v_ref, qg_ref[...]))
        yield
        xo = x + _dot(mix, wout_ref[:d, :]) + _dot(mem, wout_ref[d:, :])
        xo_ref[rows, :] = xo
        yield
        yield from _route_stages(xo, rows, g2_ref, wr_ref, br_ref, carry_ref,
                                 xn_ref, slab_ref, slabt_ref, cnt_ref)

    _run_staggered([stages(rows) for rows in _sub_rows(x_ref.shape[0])])


def _layer1_proj_kernel(x_ref, pos_ref, g1_ref, win_ref, qng_ref, kvng_ref, wq_ref, wkv_ref,
                        qg_ref, kg_ref, inv_ref, sgn_ref, kn_ref, mv_ref, mqg_ref,
                        q_ref, k_ref, v_ref, mem_ref):
    o1 = Q_LORA
    o2 = o1 + KV_LORA
    o3 = o2 + MEM_W
    q_scale = (QK_DIM ** -0.5) * LOG2E

    def stages(rows):
        x = x_ref[rows, :]
        h = _bf(_rms(x, g1_ref[...]))
        z = _dot(h, win_ref[...])
        yield
        cq = _bf(_rms(z[:, :o1], qng_ref[...]))
        ckv = _bf(_rms(z[:, o1:o2], kvng_ref[...]))
        k_rope = z[:, o3:o3 + LANE]
        ang = pos_ref[rows, :].astype(jnp.float32) * inv_ref[...]
        cos = jnp.cos(ang)
        sin = jnp.sin(ang) * sgn_ref[...]

        def rope(r):
            return r * cos + pltpu.roll(r, LANE // 2, 1) * sin

        q = _dot(cq, wq_ref[...])
        kv = _dot(ckv, wkv_ref[...])
        yield
        qg = qg_ref[...]
        kg = kg_ref[...]
        kr_ss = jnp.sum(k_rope * k_rope, axis=-1, keepdims=True)
        kr = rope(k_rope * kg[:, LANE:])
        for hh in range(MLA_HEADS):
            qh = q[:, hh * QK_PAD:(hh + 1) * QK_PAD]
            rq = lax.rsqrt(jnp.sum(qh * qh, axis=-1, keepdims=True) * (1.0 / QK_DIM) + EPS) * q_scale
            qh = qh * rq * qg
            q_ref[hh, rows, :LANE] = _bf(qh[:, :LANE])
            q_ref[hh, rows, LANE:] = _bf(rope(qh[:, LANE:]))
            kn = kv[:, hh * LANE:(hh + 1) * LANE]
            rk = lax.rsqrt((jnp.sum(kn * kn, axis=-1, keepdims=True) + kr_ss) * (1.0 / QK_DIM) + EPS)
            k_ref[hh, rows, :LANE] = _bf(kn * rk * kg[:, :LANE])
            k_ref[hh, rows, LANE:] = _bf(kr * rk)
            v_ref[hh, :, rows] = _bf(kv[:, (MLA_HEADS + hh) * LANE:(MLA_HEADS + hh + 1) * LANE].T)
            if hh % 4 == 3:
                yield
        mem_ref[rows, :] = _bf(_mem_attention(z[:, o2:o3], kn_ref, mv_ref, mqg_ref[...]))

    _run_staggered([stages(rows) for rows in _sub_rows(x_ref.shape[0])])


def _attn_kernel(q_ref, k_ref, vt_ref, o_ref, m_ref, acc_ref):
    i = pl.program_id(2)
    heads, tq = q_ref.shape[0], q_ref.shape[1]
    m_ref[...] = jnp.full_like(m_ref, NEG_BIG)
    acc_ref[...] = jnp.zeros_like(acc_ref)

    def scores(hh, j, masked):
        start = pl.multiple_of(j * tq, tq)
        s = lax.dot_general(k_ref[hh, pl.ds(start, tq), :], q_ref[hh], _NT,
                            preferred_element_type=jnp.float32)
        if masked:
            key = lax.broadcasted_iota(jnp.int32, (tq, tq), 0)
            qry = lax.broadcasted_iota(jnp.int32, (tq, tq), 1)
            s = jnp.where(key <= qry, s, NEG_BIG)
        return s

    ones_rows = jnp.ones((ATTN_SUM_ROWS, tq), jnp.bfloat16)

    def update(hh, j, s):
        m = m_ref[hh]
        m_new = jnp.maximum(m, jnp.max(s, axis=0, keepdims=True))
        alpha = jnp.exp2(m - m_new)
        p = _bf(jnp.exp2(s - m_new))
        m_ref[hh] = m_new
        vt1 = jnp.concatenate([vt_ref[hh, j], ones_rows], axis=0)
        acc_ref[hh] = alpha * acc_ref[hh] + _dot(vt1, p)

    def run(items, masked):
        ss = {}
        for t in range(len(items) + 2):
            if t < len(items):
                ss[t] = scores(*items[t], masked)
            if t >= 2:
                update(*items[t - 2], ss.pop(t - 2))

    def body(jj, c):
        run([(hh, 2 * jj + u) for u in range(2) for hh in range(heads)], False)
        return c

    lax.fori_loop(0, i // 2, body, 0)

    @pl.when(i % 2 == 1)
    def _():
        run([(hh, i - 1) for hh in range(heads)], False)

    run([(hh, i) for hh in range(heads)], True)
    for hh in range(heads):
        acc = acc_ref[hh]
        o_ref[:, hh * V_DIM:(hh + 1) * V_DIM] = _bf((acc[:V_DIM] / acc[V_DIM:V_DIM + 1]).T)


def _layer1_out_kernel(x_ref, o_ref, mem_ref, wout_ref, g2_ref, wr_ref, br_ref,
                       xo_ref, xn_ref, slab_ref, slabt_ref, cnt_ref, carry_ref):
    d = o_ref.shape[1]
    _reset_carry(carry_ref)

    def stages(rows):
        xo = (x_ref[rows, :] + _dot(o_ref[rows, :], wout_ref[:d, :])
              + _dot(mem_ref[rows, :], wout_ref[d:, :]))
        xo_ref[rows, :] = xo
        yield
        yield from _route_stages(xo, rows, g2_ref, wr_ref, br_ref, carry_ref,
                                 xn_ref, slab_ref, slabt_ref, cnt_ref)

    _run_staggered([stages(rows) for rows in _sub_rows(x_ref.shape[0])])


def _sc_mesh():
    return plsc.VectorSubcoreMesh(core_axis_name="c", subcore_axis_name="s")


def _sc_worker_base(rows_per_worker):
    return (lax.axis_index("c") * SC_SUBCORES + lax.axis_index("s")) * rows_per_worker


def _sc_scatter_rows(x, idx0, idx1, p_rows):
    n, w = x.shape
    per = n // SC_WORKERS
    chunks = SC_INDEX_GROUP // SC_CHUNK
    assert n % SC_WORKERS == 0 and per % SC_INDEX_GROUP == 0

    @pl.kernel(out_type=jax.ShapeDtypeStruct((p_rows, w), x.dtype), mesh=_sc_mesh(),
               scratch_types=[pltpu.VMEM((1, SC_INDEX_GROUP), jnp.int32),
                              pltpu.VMEM((1, SC_INDEX_GROUP), jnp.int32),
                              pltpu.VMEM((SC_CHUNK, w), x.dtype), pltpu.VMEM((SC_CHUNK, w), x.dtype),
                              pltpu.SemaphoreType.DMA, pltpu.SemaphoreType.DMA],
               name="moe_dispatch_sc")
    def scatter(x_hbm, i0_hbm, i1_hbm, o_hbm, i0_v, i1_v, buf_a, buf_b, sem_a, sem_b):
        base = _sc_worker_base(per)

        @pl.loop(0, per // SC_INDEX_GROUP)
        def _(g):
            off = pl.multiple_of(base + g * SC_INDEX_GROUP, SC_INDEX_GROUP)
            pltpu.sync_copy(i0_hbm.at[:, pl.ds(off, SC_INDEX_GROUP)], i0_v)
            pltpu.sync_copy(i1_hbm.at[:, pl.ds(off, SC_INDEX_GROUP)], i1_v)
            pending = []
            for c in range(chunks):
                buf, sem = ((buf_a, sem_a), (buf_b, sem_b))[c % 2]
                if c >= 2:
                    for cp in pending[c - 2]:
                        cp.wait()
                pltpu.sync_copy(x_hbm.at[pl.ds(off + c * SC_CHUNK, SC_CHUNK)], buf)
                sl = pl.ds(c * SC_CHUNK, SC_CHUNK)
                pending.append((pltpu.async_copy(buf, o_hbm.at[i0_v.at[0, sl]], sem),
                                pltpu.async_copy(buf, o_hbm.at[i1_v.at[0, sl]], sem)))
            for cps in pending[max(chunks - 2, 0):]:
                for cp in cps:
                    cp.wait()

    return scatter(x, idx0.reshape(1, n), idx1.reshape(1, n))


def _sc_gather_rows(table, idx):
    m = idx.shape[0]
    w = table.shape[1]
    per = m // SC_WORKERS
    chunks = SC_INDEX_GROUP // SC_CHUNK
    assert m % SC_WORKERS == 0 and per % SC_INDEX_GROUP == 0

    @pl.kernel(out_type=jax.ShapeDtypeStruct((m, w), table.dtype), mesh=_sc_mesh(),
               scratch_types=[pltpu.VMEM((1, SC_INDEX_GROUP), jnp.int32),
                              pltpu.VMEM((SC_CHUNK, w), table.dtype), pltpu.VMEM((SC_CHUNK, w), table.dtype),
                              pltpu.SemaphoreType.DMA, pltpu.SemaphoreType.DMA],
               name="moe_combine_sc")
    def gather(t_hbm, i_hbm, o_hbm, i_v, buf_a, buf_b, sem_a, sem_b):
        base = _sc_worker_base(per)

        @pl.loop(0, per // SC_INDEX_GROUP)
        def _(g):
            off = pl.multiple_of(base + g * SC_INDEX_GROUP, SC_INDEX_GROUP)
            pltpu.sync_copy(i_hbm.at[:, pl.ds(off, SC_INDEX_GROUP)], i_v)
            pending = []
            for c in range(chunks):
                buf, sem = ((buf_a, sem_a), (buf_b, sem_b))[c % 2]
                if c >= 2:
                    pending[c - 2].wait()
                pltpu.sync_copy(t_hbm.at[i_v.at[0, pl.ds(c * SC_CHUNK, SC_CHUNK)]], buf)
                pending.append(pltpu.async_copy(buf, o_hbm.at[pl.ds(off + c * SC_CHUNK, SC_CHUNK)], sem))
            for cp in pending[max(chunks - 2, 0):]:
                cp.wait()

    return gather(table, idx.reshape(1, m))


def _ffn_kernel(exp_ref, rows_ref, xs_ref, *refs):
    k = ROW_TILES_PER_STEP
    w_refs = [refs[3 * s:3 * s + 3] for s in range(k)]
    ys_ref = refs[3 * k]
    w_bf = [refs[3 * k + 1 + 3 * s:3 * k + 4 + 3 * s] for s in range(k)]
    j = pl.program_id(0)
    rt = xs_ref.shape[0] // k

    for s in range(k):
        tile = k * j + s

        @pl.when((j == 0) | (exp_ref[tile] != exp_ref[jnp.maximum(tile - k, 0)]))
        def _():
            for src, dst in zip(w_refs[s], w_bf[s]):
                dst[...] = _bf(src[...])

    def stages(s):
        rows = pl.ds(s * rt, rt)
        x = _unpack_rows(xs_ref[rows, :])
        live = lax.broadcasted_iota(jnp.int32, x.shape, 0) < rows_ref[k * j + s]
        x = _bf(jnp.where(live, x, 0.0))
        wg, wu, wd = w_bf[s]
        g = _dot(x, wg[...])
        u = _dot(x, wu[...])
        yield
        act = _bf(g * jax.nn.sigmoid(g) * u)
        yield
        ys_ref[rows, :] = _pack_rows(_dot(act, wd[...]))

    _run_staggered([stages(s) for s in range(k)])


def _combine_kernel(x_ref, slab_ref, y0_ref, y1_ref, out_ref):
    slab = slab_ref[...]
    out_ref[...] = (x_ref[...] + slab[:, 2:3] * _unpack_rows(y0_ref[...])
                    + slab[:, 3:4] * _unpack_rows(y1_ref[...]))


def _moe(x, xn, slab, slabt, counts, w_gate, w_up, w_down):
    n, d = x.shape
    wp = xn.shape[1]
    tt = TOKEN_TILE
    rt = ROW_TILE
    k = ROW_TILES_PER_STEP
    p_max = 2 * n + N_EXPERTS * rt
    n_tiles = p_max // rt
    assert n_tiles % k == 0

    cnt = counts[0, :N_EXPERTS].astype(jnp.int32)
    padded = ((cnt + rt - 1) // rt) * rt
    off = jnp.concatenate([jnp.zeros((1,), jnp.int32), jnp.cumsum(padded)]).astype(jnp.int32)
    experts = jnp.arange(N_EXPERTS, dtype=jnp.int32)[:, None]

    def position(e_row, r_row):
        e = e_row.astype(jnp.int32)[None, :]
        return jnp.sum(jnp.where(e == experts, off[:N_EXPERTS, None], 0), axis=0) + r_row.astype(jnp.int32)

    pos0 = position(slabt[0], slabt[4])
    pos1 = position(slabt[1], slabt[5])
    tile_start = jnp.arange(n_tiles, dtype=jnp.int32) * rt
    tile_exp = jnp.minimum(jnp.sum(off[1:][None, :] <= tile_start[:, None], axis=1),
                           N_EXPERTS - 1).astype(jnp.int32)
    tile_rows = jnp.clip(off[tile_exp] + cnt[tile_exp] - tile_start, 0, rt).astype(jnp.int32)

    xs = _sc_scatter_rows(xn, pos0, pos1, p_max)

    f = w_gate.shape[-1]
    w_specs = []
    for s in range(k):
        pick = lambda j, ex, rw, s=s: (ex[k * j + s], 0, 0)
        w_specs += [pl.BlockSpec((None, d, f), pick), pl.BlockSpec((None, d, f), pick),
                    pl.BlockSpec((None, f, d), pick)]
    bf_scratch = [pltpu.VMEM((d, f), jnp.bfloat16), pltpu.VMEM((d, f), jnp.bfloat16),
                  pltpu.VMEM((f, d), jnp.bfloat16)] * k
    ys = pl.pallas_call(
        _ffn_kernel,
        grid_spec=pltpu.PrefetchScalarGridSpec(
            num_scalar_prefetch=2, grid=(n_tiles // k,),
            in_specs=[pl.BlockSpec((k * rt, wp), lambda j, ex, rw: (j, 0))] + w_specs,
            out_specs=pl.BlockSpec((k * rt, wp), lambda j, ex, rw: (j, 0)),
            scratch_shapes=bf_scratch),
        out_shape=jax.ShapeDtypeStruct((p_max, wp), jnp.uint32),
        compiler_params=_params(("arbitrary",)),
        name="moe_ffn",
    )(tile_exp, tile_rows, xs, *([w_gate, w_up, w_down] * k))

    picked = _sc_gather_rows(ys, jnp.concatenate([pos0, pos1]))

    tok_spec = pl.BlockSpec((tt, d), lambda i: (i, 0))
    nb = n // tt
    return pl.pallas_call(
        _combine_kernel,
        grid=(nb,),
        in_specs=[tok_spec, pl.BlockSpec((tt, LANE), lambda i: (i, 0)),
                  pl.BlockSpec((tt, wp), lambda i: (i, 0)),
                  pl.BlockSpec((tt, wp), lambda i: (i + nb, 0))],
        out_specs=tok_spec,
        out_shape=jax.ShapeDtypeStruct((n, d), jnp.float32),
        compiler_params=_params(("arbitrary",)),
        name="moe_combine",
    )(x, slab, picked, picked)


def _router_weights(w_group, b_group, w_expert, b_expert):
    d = w_group.shape[0]
    pad = LANE - N_GROUPS - N_EXPERTS
    wr = jnp.concatenate([w_group, w_expert, jnp.zeros((d, pad), w_group.dtype)], axis=1)
    br = jnp.concatenate([b_group, b_expert, jnp.zeros((pad,), b_group.dtype)])
    return _bf(wr), br.reshape(1, LANE).astype(jnp.float32)


def _rope_lanes(vec_half):
    z = jnp.zeros_like(vec_half)
    return jnp.concatenate([vec_half, z, vec_half, z], axis=-1)


def _pad_rope_cols(w):
    z = jnp.zeros(w.shape[:-1] + (ROPE_HALF,), w.dtype)
    return jnp.concatenate([w[..., :ROPE_HALF], z, w[..., ROPE_HALF:], z], axis=-1)


def kernel(x, mem, positions, mem_norm_g, w_mem_kv, mem_qn_g, mem_kn_g, norm1_g, norm2_g, a_w_in, a_ln_g, a_ln_b, a_w_s, a_b_s, a_w_out, b_w_in, b_q_norm_g, b_kv_norm_g, b_w_q_up, b_w_kv_up, b_qn_g, b_kn_g, b_w_out, moe_w_group, moe_b_group, moe_w_expert, moe_b_expert, moe_w_gate, moe_w_up, moe_w_down):
    b, s, d = x.shape
    m = mem.shape[1]
    n = b * s
    depth = norm1_g.shape[0]
    tt = TOKEN_TILE
    ta = ATTN_TILE
    tiles_per_batch = s // tt
    assert depth == 2 and s % tt == 0 and tt % SUB_TILE == 0 and d == A_GROUPS * LANE
    assert s % ta == 0 and (ta % tt == 0 or tt % ta == 0)
    f32 = jnp.float32
    row = lambda v: v.reshape(1, -1).astype(f32)

    kn_all, mem_v = pl.pallas_call(
        _memkv_kernel,
        grid=(b,),
        in_specs=[pl.BlockSpec((m, d), lambda i: (i, 0)), _const_spec((1, d)),
                  _const_spec((d, 2 * MEM_W)), _const_spec((depth, 1, MEM_HEAD_DIM))],
        out_specs=[pl.BlockSpec((depth, m, MEM_W), lambda i: (0, i, 0)),
                   pl.BlockSpec((m, MEM_W), lambda i: (i, 0))],
        out_shape=[jax.ShapeDtypeStruct((depth, b * m, MEM_W), jnp.bfloat16),
                   jax.ShapeDtypeStruct((b * m, MEM_W), jnp.bfloat16)],
        compiler_params=_params(("arbitrary",)),
        name="mem_kv",
    )(mem.reshape(b * m, d), row(mem_norm_g), _bf(w_mem_kv), mem_kn_g.reshape(depth, 1, MEM_HEAD_DIM))

    tok = lambda width: pl.BlockSpec((tt, width), lambda i: (i, 0))
    kn_spec = lambda layer: pl.BlockSpec((None, m, MEM_W), lambda i: (layer, i // tiles_per_batch, 0))
    mv_spec = pl.BlockSpec((m, MEM_W), lambda i: (i // tiles_per_batch, 0))
    route_out_specs = [tok(d), tok(d // 2), tok(LANE), pl.BlockSpec((SUBLANE, tt), lambda i: (0, i)),
                       pl.BlockSpec((1, LANE), lambda i: (0, 0))]
    route_out_shape = [jax.ShapeDtypeStruct((n, d), f32), jax.ShapeDtypeStruct((n, d // 2), jnp.uint32),
                       jax.ShapeDtypeStruct((n, LANE), f32), jax.ShapeDtypeStruct((SUBLANE, n), f32),
                       jax.ShapeDtypeStruct((1, LANE), f32)]
    route_scratch = [pltpu.VMEM((1, LANE), f32)]

    x2 = x.reshape(n, d)

    wr0, br0 = _router_weights(moe_w_group[0], moe_b_group[0], moe_w_expert[0], moe_b_expert[0])
    a_in = a_w_in.shape[-1]
    bias_s = jnp.repeat(a_b_s[0].T, LANE, axis=1).astype(f32)
    x2, xn, slab, slabt, counts = pl.pallas_call(
        _layer0_kernel,
        grid=(n // tt,),
        in_specs=[tok(d), _const_spec((1, d)), _const_spec((d, a_in)), _const_spec((1, d)),
                  _const_spec((1, d)), _const_spec((A_GROUPS, CHUNK, CHUNK)), _const_spec((CHUNK, d)),
                  kn_spec(0), mv_spec, _const_spec((1, MEM_HEAD_DIM)),
                  _const_spec((d + MEM_W, d)), _const_spec((1, d)), _const_spec((d, LANE)),
                  _const_spec((1, LANE))],
        out_specs=route_out_specs,
        out_shape=route_out_shape,
        scratch_shapes=route_scratch,
        compiler_params=_params(("arbitrary",)),
        name="layer0_mixer",
    )(x2, row(norm1_g[0]), _bf(a_w_in[0]), row(a_ln_g[0]), row(a_ln_b[0]), _bf(a_w_s[0]), bias_s,
      kn_all, mem_v, row(mem_qn_g[0]), _bf(a_w_out[0]), row(norm2_g[0]), wr0, br0)
    x2 = _moe(x2, xn, slab, slabt, counts, moe_w_gate[0], moe_w_up[0], moe_w_down[0])

    hq = MLA_HEADS
    o1, o2, o3 = Q_LORA, Q_LORA + KV_LORA, Q_LORA + KV_LORA + ROPE_DIM
    w_in = b_w_in[0]
    w_in_p = jnp.concatenate([w_in[:, :o2], w_in[:, o3:], _pad_rope_cols(w_in[:, o2:o3])], axis=1)
    wq = b_w_q_up[0].reshape(Q_LORA, hq, QK_DIM)
    wq_p = jnp.concatenate([wq[..., :NOPE_DIM], _pad_rope_cols(wq[..., NOPE_DIM:])], axis=-1)
    wq_p = wq_p.reshape(Q_LORA, hq * QK_PAD)
    wkv = b_w_kv_up[0].reshape(KV_LORA, hq, NOPE_DIM + V_DIM)
    wkv_p = jnp.concatenate([wkv[..., :NOPE_DIM].reshape(KV_LORA, hq * NOPE_DIM),
                             wkv[..., NOPE_DIM:].reshape(KV_LORA, hq * V_DIM)], axis=1)
    pad_gain = lambda g: jnp.concatenate([g[:NOPE_DIM], _pad_rope_cols(g[NOPE_DIM:])]).reshape(1, QK_PAD)
    half = jnp.arange(ROPE_HALF, dtype=f32)
    inv = ROPE_BASE ** (-(half * 2.0 / ROPE_DIM))
    inv_l = _rope_lanes(inv).reshape(1, LANE)
    sgn_l = jnp.concatenate([-jnp.ones((2 * ROPE_HALF,), f32), jnp.ones((2 * ROPE_HALF,), f32)]).reshape(1, LANE)

    in_w = w_in_p.shape[1]
    head_spec = lambda width: pl.BlockSpec((None, hq, tt, width),
                                           lambda i: (i // tiles_per_batch, 0, i % tiles_per_batch, 0))
    vt_w = min(tt, ta)
    per_ta = ta // vt_w

    def vt_index(i):
        t = (i % tiles_per_batch) * (tt // vt_w)
        return (i // tiles_per_batch, 0, t // per_ta, 0, t % per_ta)

    assert tt <= ta
    vt_spec = pl.BlockSpec((None, hq, None, V_DIM, tt), vt_index)
    q, k, vt, mem_o = pl.pallas_call(
        _layer1_proj_kernel,
        grid=(n // tt,),
        in_specs=[tok(d), tok(1), _const_spec((1, d)), _const_spec((d, in_w)), _const_spec((1, Q_LORA)),
                  _const_spec((1, KV_LORA)), _const_spec((Q_LORA, hq * QK_PAD)),
                  _const_spec((KV_LORA, hq * (NOPE_DIM + V_DIM))), _const_spec((1, QK_PAD)),
                  _const_spec((1, QK_PAD)), _const_spec((1, LANE)), _const_spec((1, LANE)),
                  kn_spec(1), mv_spec, _const_spec((1, MEM_HEAD_DIM))],
        out_specs=[head_spec(QK_PAD), head_spec(QK_PAD), vt_spec, tok(MEM_W)],
        out_shape=[jax.ShapeDtypeStruct((b, hq, s, QK_PAD), jnp.bfloat16),
                   jax.ShapeDtypeStruct((b, hq, s, QK_PAD), jnp.bfloat16),
                   jax.ShapeDtypeStruct((b, hq, s // ta, V_DIM, ta), jnp.bfloat16),
                   jax.ShapeDtypeStruct((n, MEM_W), jnp.bfloat16)],
        compiler_params=_params(("arbitrary",)),
        name="layer1_proj",
    )(x2, positions.reshape(n, 1), row(norm1_g[1]), _bf(w_in_p), row(b_q_norm_g[0]), row(b_kv_norm_g[0]),
      _bf(wq_p), _bf(wkv_p), pad_gain(b_qn_g[0]).astype(f32), pad_gain(b_kn_g[0]).astype(f32),
      inv_l, sgn_l, kn_all, mem_v, row(mem_qn_g[1]))

    qb = s // ta
    hp = ATTN_HEADS_PER_STEP
    attn = pl.pallas_call(
        _attn_kernel,
        grid=(b, hq // hp, qb),
        in_specs=[pl.BlockSpec((None, hp, ta, QK_PAD), lambda bi, hi, i: (bi, hi, i, 0)),
                  pl.BlockSpec((None, hp, s, QK_PAD), lambda bi, hi, i: (bi, hi, 0, 0)),
                  pl.BlockSpec((None, hp, qb, V_DIM, ta), lambda bi, hi, i: (bi, hi, 0, 0, 0))],
        out_specs=pl.BlockSpec((ta, hp * V_DIM), lambda bi, hi, i: (bi * qb + i, hi)),
        out_shape=jax.ShapeDtypeStruct((n, hq * V_DIM), jnp.bfloat16),
        scratch_shapes=[pltpu.VMEM((hp, 1, ta), f32), pltpu.VMEM((hp, V_DIM + ATTN_SUM_ROWS, ta), f32)],
        compiler_params=_params(("arbitrary", "arbitrary", "arbitrary")),
        name="causal_attention",
    )(q, k, vt)

    wr1, br1 = _router_weights(moe_w_group[1], moe_b_group[1], moe_w_expert[1], moe_b_expert[1])
    x2, xn, slab, slabt, counts = pl.pallas_call(
        _layer1_out_kernel,
        grid=(n // tt,),
        in_specs=[tok(d), tok(hq * V_DIM), tok(MEM_W), _const_spec((hq * V_DIM + MEM_W, d)),
                  _const_spec((1, d)), _const_spec((d, LANE)), _const_spec((1, LANE))],
        out_specs=route_out_specs,
        out_shape=route_out_shape,
        scratch_shapes=route_scratch,
        compiler_params=_params(("arbitrary",)),
        name="layer1_out",
    )(x2, attn, mem_o, _bf(b_w_out[0]), row(norm2_g[1]), wr1, br1)
    x2 = _moe(x2, xn, slab, slabt, counts, moe_w_gate[1], moe_w_up[1], moe_w_down[1])
    return x2.reshape(b, s, d)
```

```python
import jax
import jax.numpy as jnp
from jax import lax
from jax.experimental import pallas as pl
from jax.experimental.pallas import tpu as pltpu
from jax.experimental.pallas import tpu_sc as plsc

EPS = 1e-6
LANE = 128
SUBLANE = 8
MEM_HEADS = 4
MEM_HEAD_DIM = 128
MEM_W = MEM_HEADS * MEM_HEAD_DIM
CHUNK = 128
A_GROUPS = 8
MLA_HEADS = 8
Q_LORA = 512
KV_LORA = 256
NOPE_DIM = 128
ROPE_DIM = 64
ROPE_HALF = ROPE_DIM // 2
V_DIM = 128
QK_DIM = NOPE_DIM + ROPE_DIM
QK_PAD = 2 * LANE
ROPE_BASE = 10000.0
N_GROUPS = 4
EXPERTS_PER_GROUP = 8
N_EXPERTS = N_GROUPS * EXPERTS_PER_GROUP
EXPERT_FF = 256
LOG2E = 1.4426950408889634

TOKEN_TILE = 512
SUB_TILE = 256
ROW_TILE = 256
ROW_TILES_PER_STEP = 2
ATTN_TILE = 512
ATTN_HEADS_PER_STEP = 2
ATTN_KEY_TILES_PER_TRIP = 4
ATTN_LOOKAHEAD = 2
ATTN_SUM_ROWS = 16
VMEM_LIMIT = 56 * 1024 * 1024
NEG_BIG = -1e30

SC_CORES = 2
SC_SUBCORES = 16
SC_WORKERS = SC_CORES * SC_SUBCORES
SC_INDEX_GROUP = 128
SC_CHUNK = 64

_NT = (((1,), (1,)), ((), ()))


def _const_spec(shape):
    nd = len(shape)
    return pl.BlockSpec(shape, lambda *_: (0,) * nd, pipeline_mode=pl.Buffered(1))


def _params(sem):
    return pltpu.CompilerParams(dimension_semantics=sem, vmem_limit_bytes=VMEM_LIMIT)


def _run_staggered(gens):
    waiting = list(gens)
    active = []
    while waiting or active:
        if waiting:
            active.append(waiting.pop(0))
        for g in list(active):
            try:
                next(g)
            except StopIteration:
                active.remove(g)


def _sub_rows(t):
    return [pl.ds(k * SUB_TILE, SUB_TILE) for k in range(t // SUB_TILE)]


def _rms(x, g):
    return x * lax.rsqrt(jnp.mean(x * x, axis=-1, keepdims=True) + EPS) * g


def _gelu(x):
    return 0.5 * x * (1.0 + lax.erf(x * (2.0 ** -0.5)))


def _bf(x):
    return x.astype(jnp.bfloat16)


def _dot(a, b):
    return jnp.dot(a, b, preferred_element_type=jnp.float32)


def _memkv_kernel(mem_ref, g_ref, w_ref, kng_ref, kn_ref, v_ref):
    h = _bf(_rms(mem_ref[...], g_ref[...]))
    kv = _dot(h, w_ref[...])
    v_ref[...] = _bf(kv[:, MEM_W:])
    for layer in range(kn_ref.shape[0]):
        g = kng_ref[layer]
        for hh in range(MEM_HEADS):
            k = kv[:, hh * LANE:(hh + 1) * LANE]
            kn_ref[layer, :, hh * LANE:(hh + 1) * LANE] = _bf(_rms(k, g))


def _mem_attention(qm, kn_ref, v_ref, qg):
    outs = []
    for hh in range(MEM_HEADS):
        sl = slice(hh * LANE, (hh + 1) * LANE)
        q = _rms(qm[:, sl], qg) * (MEM_HEAD_DIM ** -0.5)
        s = lax.dot_general(_bf(q), kn_ref[:, sl], _NT, preferred_element_type=jnp.float32)
        p = jnp.exp(s - jnp.max(s, axis=-1, keepdims=True))
        l = jnp.sum(p, axis=-1, keepdims=True)
        outs.append(_dot(_bf(p), v_ref[:, sl]) / l)
    return jnp.concatenate(outs, axis=-1)


def _pack_rows(x):
    w = x.shape[1] // 2
    bits = lambda v: lax.bitcast_convert_type(_bf(v).astype(jnp.float32), jnp.uint32)
    return (bits(x[:, :w]) >> 16) | (bits(x[:, w:]) & jnp.uint32(0xFFFF0000))


def _unpack_rows(p):
    lo = lax.bitcast_convert_type(p << 16, jnp.float32)
    hi = lax.bitcast_convert_type(p & jnp.uint32(0xFFFF0000), jnp.float32)
    return jnp.concatenate([lo, hi], axis=-1)


def _route_stages(x, rows, g2_ref, wr_ref, br_ref, carry_ref, xn_ref, slab_ref, slabt_ref, cnt_ref):
    t = x.shape[0]
    xn = _rms(x, g2_ref[...])
    xn_ref[rows, :] = _pack_rows(xn)
    logits = _dot(_bf(xn), wr_ref[...]) + br_ref[...]
    yield
    lane = lax.broadcasted_iota(jnp.int32, (t, LANE), 1)

    def first_max(v):
        m = jnp.max(v, axis=-1, keepdims=True)
        idx = jnp.min(jnp.where(v == m, lane, LANE), axis=-1, keepdims=True)
        return m, idx

    lg = jnp.where(lane < N_GROUPS, logits, NEG_BIG)
    gmax, gidx = first_max(lg)
    g_w = 1.0 / jnp.sum(jnp.exp(lg - gmax), axis=-1, keepdims=True)

    eid = lane - N_GROUPS
    in_grp = (eid >= 0) & (eid < N_EXPERTS) & ((eid >> 3) == gidx)
    le = jnp.where(in_grp, logits, NEG_BIG)
    m1, i1 = first_max(le)
    yield
    m2, i2 = first_max(jnp.where(lane == i1, NEG_BIG, le))
    r = jnp.exp(m2 - m1)
    w1 = g_w / (1.0 + r)
    w2 = w1 * r
    e1 = i1 - N_GROUPS
    e2 = i2 - N_GROUPS
    yield

    oh1 = lane == e1
    oh2 = lane == e2
    oh = jnp.where(oh1 | oh2, 1.0, 0.0)
    row = lax.broadcasted_iota(jnp.int32, (t, t), 0)
    col = lax.broadcasted_iota(jnp.int32, (t, t), 1)
    earlier = jnp.where(row > col, 1.0, 0.0).astype(jnp.bfloat16)
    before = _dot(earlier, _bf(oh)) + carry_ref[...]
    r1 = jnp.sum(jnp.where(oh1, before, 0.0), axis=-1, keepdims=True)
    r2 = jnp.sum(jnp.where(oh2, before, 0.0), axis=-1, keepdims=True)
    carry_ref[...] += jnp.sum(oh, axis=0, keepdims=True)
    cnt_ref[...] = carry_ref[...]

    vals = (e1.astype(jnp.float32), e2.astype(jnp.float32), w1, w2, r1, r2)
    slab = jnp.zeros((t, LANE), jnp.float32)
    for i, v in enumerate(vals):
        slab = jnp.where(lane == i, v, slab)
    slab_ref[rows, :] = slab
    slabt_ref[:, rows] = slab.T[:SUBLANE, :]


def _reset_carry(carry_ref):
    @pl.when(pl.program_id(0) == 0)
    def _():
        carry_ref[...] = jnp.zeros_like(carry_ref)


def _layer0_kernel(x_ref, g1_ref, win_ref, lng_ref, lnb_ref, ws_ref, bs_ref, kn_ref, v_ref, qg_ref,
                   wout_ref, g2_ref, wr_ref, br_ref,
                   xo_ref, xn_ref, slab_ref, slabt_ref, cnt_ref, carry_ref):
    d = x_ref.shape[1]
    _reset_carry(carry_ref)
    row = lax.broadcasted_iota(jnp.int32, (CHUNK, CHUNK), 0)
    col = lax.broadcasted_iota(jnp.int32, (CHUNK, CHUNK), 1)
    causal = row >= col

    def stages(rows):
        x = x_ref[rows, :]
        t = x.shape[0]
        h = _bf(_rms(x, g1_ref[...]))
        z = _dot(h, win_ref[...])
        yield
        u = _gelu(z[:, :d])
        v = _gelu(z[:, d:2 * d])
        mu = jnp.mean(v, axis=-1, keepdims=True)
        vc = v - mu
        var = jnp.mean(vc * vc, axis=-1, keepdims=True)
        v = _bf(vc * lax.rsqrt(var + EPS) * lng_ref[...] + lnb_ref[...])
        yield
        chunks = []
        for c in range(t // CHUNK):
            cols = []
            for g in range(A_GROUPS):
                w = jnp.where(causal, ws_ref[g], jnp.zeros((), ws_ref.dtype))
                cols.append(_dot(w, v[c * CHUNK:(c + 1) * CHUNK, g * LANE:(g + 1) * LANE]))
            chunks.append(jnp.concatenate(cols, axis=-1) + bs_ref[...])
        mix = _bf(u * jnp.concatenate(chunks, axis=0))
        yield
        mem = _bf(_mem_attention(z[:, 2 * d:], kn_ref, v_ref, qg_ref[...]))
        yield
        xo = x + _dot(mix, wout_ref[:d, :]) + _dot(mem, wout_ref[d:, :])
        xo_ref[rows, :] = xo
        yield
        yield from _route_stages(xo, rows, g2_ref, wr_ref, br_ref, carry_ref,
                                 xn_ref, slab_ref, slabt_ref, cnt_ref)

    _run_staggered([stages(rows) for rows in _sub_rows(x_ref.shape[0])])


def _layer1_proj_kernel(x_ref, slab_ref, y0_ref, y1_ref, pos_ref, g1_ref, win_ref, qng_ref, kvng_ref,
                        wq_ref, wkv_ref, qg_ref, kg_ref, inv_ref, sgn_ref, kn_ref, mv_ref, mqg_ref,
                        xo_ref, q_ref, k_ref, v_ref, mem_ref):
    o1 = Q_LORA
    o2 = o1 + KV_LORA
    o3 = o2 + MEM_W
    q_scale = (QK_DIM ** -0.5) * LOG2E

    def stages(rows):
        x = _combined(x_ref, slab_ref, y0_ref, y1_ref, rows)
        xo_ref[rows, :] = x
        h = _bf(_rms(x, g1_ref[...]))
        z = _dot(h, win_ref[...])
        yield
        cq = _bf(_rms(z[:, :o1], qng_ref[...]))
        ckv = _bf(_rms(z[:, o1:o2], kvng_ref[...]))
        k_rope = z[:, o3:o3 + LANE]
        ang = pos_ref[rows, :].astype(jnp.float32) * inv_ref[...]
        cos = jnp.cos(ang)
        sin = jnp.sin(ang) * sgn_ref[...]

        def rope(r):
            return r * cos + pltpu.roll(r, LANE // 2, 1) * sin

        q = _dot(cq, wq_ref[...])
        kv = _dot(ckv, wkv_ref[...])
        yield
        qg = qg_ref[...]
        kg = kg_ref[...]
        kr_ss = jnp.sum(k_rope * k_rope, axis=-1, keepdims=True)
        kr = rope(k_rope * kg[:, LANE:])
        for hh in range(MLA_HEADS):
            qh = q[:, hh * QK_PAD:(hh + 1) * QK_PAD]
            rq = lax.rsqrt(jnp.sum(qh * qh, axis=-1, keepdims=True) * (1.0 / QK_DIM) + EPS) * q_scale
            qh = qh * rq * qg
            q_ref[hh, rows, :LANE] = _bf(qh[:, :LANE])
            q_ref[hh, rows, LANE:] = _bf(rope(qh[:, LANE:]))
            kn = kv[:, hh * LANE:(hh + 1) * LANE]
            rk = lax.rsqrt((jnp.sum(kn * kn, axis=-1, keepdims=True) + kr_ss) * (1.0 / QK_DIM) + EPS)
            k_ref[hh, rows, :LANE] = _bf(kn * rk * kg[:, :LANE])
            k_ref[hh, rows, LANE:] = _bf(kr * rk)
            v_ref[hh, :, rows] = _bf(kv[:, (MLA_HEADS + hh) * LANE:(MLA_HEADS + hh + 1) * LANE].T)
            if hh % 4 == 3:
                yield
        mem_ref[rows, :] = _bf(_mem_attention(z[:, o2:o3], kn_ref, mv_ref, mqg_ref[...]))

    _run_staggered([stages(rows) for rows in _sub_rows(x_ref.shape[0])])


def _attn_kernel(q_ref, k_ref, vt_ref, o_ref, m_ref, acc_ref):
    i = pl.program_id(2)
    heads, tq = q_ref.shape[0], q_ref.shape[1]
    m_ref[...] = jnp.full_like(m_ref, NEG_BIG)
    acc_ref[...] = jnp.zeros_like(acc_ref)

    def scores(hh, j, masked):
        start = pl.multiple_of(j * tq, tq)
        s = lax.dot_general(k_ref[hh, pl.ds(start, tq), :], q_ref[hh], _NT,
                            preferred_element_type=jnp.float32)
        if masked:
            key = lax.broadcasted_iota(jnp.int32, (tq, tq), 0)
            qry = lax.broadcasted_iota(jnp.int32, (tq, tq), 1)
            s = jnp.where(key <= qry, s, NEG_BIG)
        return s

    ones_rows = jnp.ones((ATTN_SUM_ROWS, tq), jnp.bfloat16)

    def update(hh, j, s):
        m = m_ref[hh]
        m_new = jnp.maximum(m, jnp.max(s, axis=0, keepdims=True))
        alpha = jnp.exp2(m - m_new)
        p = _bf(jnp.exp2(s - m_new))
        m_ref[hh] = m_new
        vt1 = jnp.concatenate([vt_ref[hh, j], ones_rows], axis=0)
        acc_ref[hh] = alpha * acc_ref[hh] + _dot(vt1, p)

    def run(items, masked):
        ss = {}
        ahead = ATTN_LOOKAHEAD
        for t in range(len(items) + ahead):
            if t < len(items):
                ss[t] = scores(*items[t], masked)
            if t >= ahead:
                update(*items[t - ahead], ss.pop(t - ahead))

    def full_tiles(first, count):
        run([(hh, first + u) for u in range(count) for hh in range(heads)], False)

    unroll = ATTN_KEY_TILES_PER_TRIP

    def body(jj, c):
        full_tiles(unroll * jj, unroll)
        return c

    lax.fori_loop(0, i // unroll, body, 0)
    piece = unroll // 2
    while piece >= 1:
        @pl.when((i & piece) != 0)
        def _():
            full_tiles((i // (2 * piece)) * (2 * piece), piece)
        piece //= 2

    run([(hh, i) for hh in range(heads)], True)
    for hh in range(heads):
        acc = acc_ref[hh]
        o_ref[:, hh * V_DIM:(hh + 1) * V_DIM] = _bf((acc[:V_DIM] / acc[V_DIM:V_DIM + 1]).T)


def _layer1_out_kernel(x_ref, o_ref, mem_ref, wout_ref, g2_ref, wr_ref, br_ref,
                       xo_ref, xn_ref, slab_ref, slabt_ref, cnt_ref, carry_ref):
    d = o_ref.shape[1]
    _reset_carry(carry_ref)

    def stages(rows):
        xo = (x_ref[rows, :] + _dot(o_ref[rows, :], wout_ref[:d, :])
              + _dot(mem_ref[rows, :], wout_ref[d:, :]))
        xo_ref[rows, :] = xo
        yield
        yield from _route_stages(xo, rows, g2_ref, wr_ref, br_ref, carry_ref,
                                 xn_ref, slab_ref, slabt_ref, cnt_ref)

    _run_staggered([stages(rows) for rows in _sub_rows(x_ref.shape[0])])


def _sc_mesh():
    return plsc.VectorSubcoreMesh(core_axis_name="c", subcore_axis_name="s")


def _sc_worker_base(rows_per_worker):
    return (lax.axis_index("c") * SC_SUBCORES + lax.axis_index("s")) * rows_per_worker


def _sc_scatter_rows(x, idx0, idx1, p_rows):
    n, w = x.shape
    per = n // SC_WORKERS
    chunks = SC_INDEX_GROUP // SC_CHUNK
    assert n % SC_WORKERS == 0 and per % SC_INDEX_GROUP == 0

    @pl.kernel(out_type=jax.ShapeDtypeStruct((p_rows, w), x.dtype), mesh=_sc_mesh(),
               scratch_types=[pltpu.VMEM((1, SC_INDEX_GROUP), jnp.int32),
                              pltpu.VMEM((1, SC_INDEX_GROUP), jnp.int32),
                              pltpu.VMEM((SC_CHUNK, w), x.dtype), pltpu.VMEM((SC_CHUNK, w), x.dtype),
                              pltpu.SemaphoreType.DMA, pltpu.SemaphoreType.DMA],
               name="moe_dispatch_sc")
    def scatter(x_hbm, i0_hbm, i1_hbm, o_hbm, i0_v, i1_v, buf_a, buf_b, sem_a, sem_b):
        base = _sc_worker_base(per)

        @pl.loop(0, per // SC_INDEX_GROUP)
        def _(g):
            off = pl.multiple_of(base + g * SC_INDEX_GROUP, SC_INDEX_GROUP)
            pltpu.sync_copy(i0_hbm.at[:, pl.ds(off, SC_INDEX_GROUP)], i0_v)
            pltpu.sync_copy(i1_hbm.at[:, pl.ds(off, SC_INDEX_GROUP)], i1_v)
            pending = []
            for c in range(chunks):
                buf, sem = ((buf_a, sem_a), (buf_b, sem_b))[c % 2]
                if c >= 2:
                    for cp in pending[c - 2]:
                        cp.wait()
                pltpu.sync_copy(x_hbm.at[pl.ds(off + c * SC_CHUNK, SC_CHUNK)], buf)
                sl = pl.ds(c * SC_CHUNK, SC_CHUNK)
                pending.append((pltpu.async_copy(buf, o_hbm.at[i0_v.at[0, sl]], sem),
                                pltpu.async_copy(buf, o_hbm.at[i1_v.at[0, sl]], sem)))
            for cps in pending[max(chunks - 2, 0):]:
                for cp in cps:
                    cp.wait()

    return scatter(x, idx0.reshape(1, n), idx1.reshape(1, n))


def _sc_gather_rows(table, idx):
    m = idx.shape[0]
    w = table.shape[1]
    per = m // SC_WORKERS
    chunks = SC_INDEX_GROUP // SC_CHUNK
    assert m % SC_WORKERS == 0 and per % SC_INDEX_GROUP == 0

    @pl.kernel(out_type=jax.ShapeDtypeStruct((m, w), table.dtype), mesh=_sc_mesh(),
               scratch_types=[pltpu.VMEM((1, SC_INDEX_GROUP), jnp.int32),
                              pltpu.VMEM((SC_CHUNK, w), table.dtype), pltpu.VMEM((SC_CHUNK, w), table.dtype),
                              pltpu.SemaphoreType.DMA, pltpu.SemaphoreType.DMA],
               name="moe_combine_sc")
    def gather(t_hbm, i_hbm, o_hbm, i_v, buf_a, buf_b, sem_a, sem_b):
        base = _sc_worker_base(per)

        @pl.loop(0, per // SC_INDEX_GROUP)
        def _(g):
            off = pl.multiple_of(base + g * SC_INDEX_GROUP, SC_INDEX_GROUP)
            pltpu.sync_copy(i_hbm.at[:, pl.ds(off, SC_INDEX_GROUP)], i_v)
            pending = []
            for c in range(chunks):
                buf, sem = ((buf_a, sem_a), (buf_b, sem_b))[c % 2]
                if c >= 2:
                    pending[c - 2].wait()
                pltpu.sync_copy(t_hbm.at[i_v.at[0, pl.ds(c * SC_CHUNK, SC_CHUNK)]], buf)
                pending.append(pltpu.async_copy(buf, o_hbm.at[pl.ds(off + c * SC_CHUNK, SC_CHUNK)], sem))
            for cp in pending[max(chunks - 2, 0):]:
                cp.wait()

    return gather(table, idx.reshape(1, m))


def _ffn_kernel(exp_ref, rows_ref, xs_ref, *refs):
    k = ROW_TILES_PER_STEP
    w_refs = [refs[3 * s:3 * s + 3] for s in range(k)]
    ys_ref = refs[3 * k]
    w_bf = [refs[3 * k + 1 + 3 * s:3 * k + 4 + 3 * s] for s in range(k)]
    j = pl.program_id(0)
    rt = xs_ref.shape[0] // k

    for s in range(k):
        tile = k * j + s

        @pl.when((j == 0) | (exp_ref[tile] != exp_ref[jnp.maximum(tile - k, 0)]))
        def _():
            for src, dst in zip(w_refs[s], w_bf[s]):
                dst[...] = _bf(src[...])

    def stages(s):
        rows = pl.ds(s * rt, rt)
        x = _unpack_rows(xs_ref[rows, :])
        live = lax.broadcasted_iota(jnp.int32, x.shape, 0) < rows_ref[k * j + s]
        x = _bf(jnp.where(live, x, 0.0))
        wg, wu, wd = w_bf[s]
        g = _dot(x, wg[...])
        u = _dot(x, wu[...])
        yield
        act = _bf(g * jax.nn.sigmoid(g) * u)
        yield
        ys_ref[rows, :] = _pack_rows(_dot(act, wd[...]))

    _run_staggered([stages(s) for s in range(k)])


def _combined(x_ref, slab_ref, y0_ref, y1_ref, rows):
    slab = slab_ref[rows, :]
    return (x_ref[rows, :] + slab[:, 2:3] * _unpack_rows(y0_ref[rows, :])
            + slab[:, 3:4] * _unpack_rows(y1_ref[rows, :]))


def _combine_kernel(x_ref, slab_ref, y0_ref, y1_ref, out_ref):
    out_ref[...] = _combined(x_ref, slab_ref, y0_ref, y1_ref, slice(None))


def _moe(layer, xn, slabt, counts, w_gate, w_up, w_down):
    n, wp = xn.shape
    d = w_gate.shape[-2]
    rt = ROW_TILE
    k = ROW_TILES_PER_STEP
    p_max = 2 * n + N_EXPERTS * rt
    n_tiles = p_max // rt
    assert n_tiles % k == 0

    cnt = counts[0, :N_EXPERTS].astype(jnp.int32)
    padded = ((cnt + rt - 1) // rt) * rt
    off = jnp.concatenate([jnp.zeros((1,), jnp.int32), jnp.cumsum(padded)]).astype(jnp.int32)
    experts = jnp.arange(N_EXPERTS, dtype=jnp.int32)[:, None]

    def position(e_row, r_row):
        e = e_row.astype(jnp.int32)[None, :]
        return jnp.sum(jnp.where(e == experts, off[:N_EXPERTS, None], 0), axis=0) + r_row.astype(jnp.int32)

    pos0 = position(slabt[0], slabt[4])
    pos1 = position(slabt[1], slabt[5])
    tile_start = jnp.arange(n_tiles, dtype=jnp.int32) * rt
    tile_exp = jnp.minimum(jnp.sum(off[1:][None, :] <= tile_start[:, None], axis=1),
                           N_EXPERTS - 1).astype(jnp.int32)
    tile_rows = jnp.clip(off[tile_exp] + cnt[tile_exp] - tile_start, 0, rt).astype(jnp.int32)

    xs = _sc_scatter_rows(xn, pos0, pos1, p_max)

    f = w_gate.shape[-1]
    w_specs = []
    for s in range(k):
        pick = lambda j, ex, rw, s=s: (layer, ex[k * j + s], 0, 0)
        w_specs += [pl.BlockSpec((None, None, d, f), pick), pl.BlockSpec((None, None, d, f), pick),
                    pl.BlockSpec((None, None, f, d), pick)]
    bf_scratch = [pltpu.VMEM((d, f), jnp.bfloat16), pltpu.VMEM((d, f), jnp.bfloat16),
                  pltpu.VMEM((f, d), jnp.bfloat16)] * k
    ys = pl.pallas_call(
        _ffn_kernel,
        grid_spec=pltpu.PrefetchScalarGridSpec(
            num_scalar_prefetch=2, grid=(n_tiles // k,),
            in_specs=[pl.BlockSpec((k * rt, wp), lambda j, ex, rw: (j, 0))] + w_specs,
            out_specs=pl.BlockSpec((k * rt, wp), lambda j, ex, rw: (j, 0)),
            scratch_shapes=bf_scratch),
        out_shape=jax.ShapeDtypeStruct((p_max, wp), jnp.uint32),
        compiler_params=_params(("arbitrary",)),
        name="moe_ffn",
    )(tile_exp, tile_rows, xs, *([w_gate, w_up, w_down] * k))

    return _sc_gather_rows(ys, jnp.concatenate([pos0, pos1]))


def _picked_specs(n, wp):
    nb = n // TOKEN_TILE
    return [pl.BlockSpec((TOKEN_TILE, wp), lambda i: (i, 0)),
            pl.BlockSpec((TOKEN_TILE, wp), lambda i: (i + nb, 0))]


def _router_weights(w_group, b_group, w_expert, b_expert):
    d = w_group.shape[0]
    pad = LANE - N_GROUPS - N_EXPERTS
    wr = jnp.concatenate([w_group, w_expert, jnp.zeros((d, pad), w_group.dtype)], axis=1)
    br = jnp.concatenate([b_group, b_expert, jnp.zeros((pad,), b_group.dtype)])
    return _bf(wr), br.reshape(1, LANE).astype(jnp.float32)


def _rope_lanes(vec_half):
    z = jnp.zeros_like(vec_half)
    return jnp.concatenate([vec_half, z, vec_half, z], axis=-1)


def _pad_rope_cols(w):
    z = jnp.zeros(w.shape[:-1] + (ROPE_HALF,), w.dtype)
    return jnp.concatenate([w[..., :ROPE_HALF], z, w[..., ROPE_HALF:], z], axis=-1)


def kernel(x, mem, positions, mem_norm_g, w_mem_kv, mem_qn_g, mem_kn_g, norm1_g, norm2_g, a_w_in, a_ln_g, a_ln_b, a_w_s, a_b_s, a_w_out, b_w_in, b_q_norm_g, b_kv_norm_g, b_w_q_up, b_w_kv_up, b_qn_g, b_kn_g, b_w_out, moe_w_group, moe_b_group, moe_w_expert, moe_b_expert, moe_w_gate, moe_w_up, moe_w_down):
    b, s, d = x.shape
    m = mem.shape[1]
    n = b * s
    depth = norm1_g.shape[0]
    tt = TOKEN_TILE
    ta = ATTN_TILE
    tiles_per_batch = s // tt
    assert depth == 2 and s % tt == 0 and tt % SUB_TILE == 0 and d == A_GROUPS * LANE
    assert s % ta == 0 and ta % tt == 0
    f32 = jnp.float32
    row = lambda v: v.reshape(1, -1).astype(f32)

    kn_all, mem_v = pl.pallas_call(
        _memkv_kernel,
        grid=(b,),
        in_specs=[pl.BlockSpec((m, d), lambda i: (i, 0)), _const_spec((1, d)),
                  _const_spec((d, 2 * MEM_W)), _const_spec((depth, 1, MEM_HEAD_DIM))],
        out_specs=[pl.BlockSpec((depth, m, MEM_W), lambda i: (0, i, 0)),
                   pl.BlockSpec((m, MEM_W), lambda i: (i, 0))],
        out_shape=[jax.ShapeDtypeStruct((depth, b * m, MEM_W), jnp.bfloat16),
                   jax.ShapeDtypeStruct((b * m, MEM_W), jnp.bfloat16)],
        compiler_params=_params(("arbitrary",)),
        name="mem_kv",
    )(mem.reshape(b * m, d), row(mem_norm_g), _bf(w_mem_kv), mem_kn_g.reshape(depth, 1, MEM_HEAD_DIM))

    tok = lambda width: pl.BlockSpec((tt, width), lambda i: (i, 0))
    kn_spec = lambda layer: pl.BlockSpec((None, m, MEM_W), lambda i: (layer, i // tiles_per_batch, 0))
    mv_spec = pl.BlockSpec((m, MEM_W), lambda i: (i // tiles_per_batch, 0))
    route_out_specs = [tok(d), tok(d // 2), tok(LANE), pl.BlockSpec((SUBLANE, tt), lambda i: (0, i)),
                       pl.BlockSpec((1, LANE), lambda i: (0, 0))]
    route_out_shape = [jax.ShapeDtypeStruct((n, d), f32), jax.ShapeDtypeStruct((n, d // 2), jnp.uint32),
                       jax.ShapeDtypeStruct((n, LANE), f32), jax.ShapeDtypeStruct((SUBLANE, n), f32),
                       jax.ShapeDtypeStruct((1, LANE), f32)]
    route_scratch = [pltpu.VMEM((1, LANE), f32)]

    x2 = x.reshape(n, d)

    wr0, br0 = _router_weights(moe_w_group[0], moe_b_group[0], moe_w_expert[0], moe_b_expert[0])
    a_in = a_w_in.shape[-1]
    bias_s = jnp.repeat(a_b_s[0].T, LANE, axis=1).astype(f32)
    x2, xn, slab, slabt, counts = pl.pallas_call(
        _layer0_kernel,
        grid=(n // tt,),
        in_specs=[tok(d), _const_spec((1, d)), _const_spec((d, a_in)), _const_spec((1, d)),
                  _const_spec((1, d)), _const_spec((A_GROUPS, CHUNK, CHUNK)), _const_spec((CHUNK, d)),
                  kn_spec(0), mv_spec, _const_spec((1, MEM_HEAD_DIM)),
                  _const_spec((d + MEM_W, d)), _const_spec((1, d)), _const_spec((d, LANE)),
                  _const_spec((1, LANE))],
        out_specs=route_out_specs,
        out_shape=route_out_shape,
        scratch_shapes=route_scratch,
        compiler_params=_params(("arbitrary",)),
        name="layer0_mixer",
    )(x2, row(norm1_g[0]), _bf(a_w_in[0]), row(a_ln_g[0]), row(a_ln_b[0]), _bf(a_w_s[0]), bias_s,
      kn_all, mem_v, row(mem_qn_g[0]), _bf(a_w_out[0]), row(norm2_g[0]), wr0, br0)
    picked = _moe(0, xn, slabt, counts, moe_w_gate, moe_w_up, moe_w_down)

    hq = MLA_HEADS
    o1, o2, o3 = Q_LORA, Q_LORA + KV_LORA, Q_LORA + KV_LORA + ROPE_DIM
    w_in = b_w_in[0]
    w_in_p = jnp.concatenate([w_in[:, :o2], w_in[:, o3:], _pad_rope_cols(w_in[:, o2:o3])], axis=1)
    wq = b_w_q_up[0].reshape(Q_LORA, hq, QK_DIM)
    wq_p = jnp.concatenate([wq[..., :NOPE_DIM], _pad_rope_cols(wq[..., NOPE_DIM:])], axis=-1)
    wq_p = wq_p.reshape(Q_LORA, hq * QK_PAD)
    wkv = b_w_kv_up[0].reshape(KV_LORA, hq, NOPE_DIM + V_DIM)
    wkv_p = jnp.concatenate([wkv[..., :NOPE_DIM].reshape(KV_LORA, hq * NOPE_DIM),
                             wkv[..., NOPE_DIM:].reshape(KV_LORA, hq * V_DIM)], axis=1)
    pad_gain = lambda g: jnp.concatenate([g[:NOPE_DIM], _pad_rope_cols(g[NOPE_DIM:])]).reshape(1, QK_PAD)
    half = jnp.arange(ROPE_HALF, dtype=f32)
    inv = ROPE_BASE ** (-(half * 2.0 / ROPE_DIM))
    inv_l = _rope_lanes(inv).reshape(1, LANE)
    sgn_l = jnp.concatenate([-jnp.ones((2 * ROPE_HALF,), f32), jnp.ones((2 * ROPE_HALF,), f32)]).reshape(1, LANE)

    in_w = w_in_p.shape[1]
    head_spec = lambda width: pl.BlockSpec((None, hq, tt, width),
                                           lambda i: (i // tiles_per_batch, 0, i % tiles_per_batch, 0))
    per_ta = ta // tt

    def vt_index(i):
        t = i % tiles_per_batch
        return (i // tiles_per_batch, 0, t // per_ta, 0, t % per_ta)

    vt_spec = pl.BlockSpec((None, hq, None, V_DIM, tt), vt_index)
    x2, q, k, vt, mem_o = pl.pallas_call(
        _layer1_proj_kernel,
        grid=(n // tt,),
        in_specs=[tok(d), tok(LANE), *_picked_specs(n, d // 2),
                  tok(1), _const_spec((1, d)), _const_spec((d, in_w)), _const_spec((1, Q_LORA)),
                  _const_spec((1, KV_LORA)), _const_spec((Q_LORA, hq * QK_PAD)),
                  _const_spec((KV_LORA, hq * (NOPE_DIM + V_DIM))), _const_spec((1, QK_PAD)),
                  _const_spec((1, QK_PAD)), _const_spec((1, LANE)), _const_spec((1, LANE)),
                  kn_spec(1), mv_spec, _const_spec((1, MEM_HEAD_DIM))],
        out_specs=[tok(d), head_spec(QK_PAD), head_spec(QK_PAD), vt_spec, tok(MEM_W)],
        out_shape=[jax.ShapeDtypeStruct((n, d), f32),
                   jax.ShapeDtypeStruct((b, hq, s, QK_PAD), jnp.bfloat16),
                   jax.ShapeDtypeStruct((b, hq, s, QK_PAD), jnp.bfloat16),
                   jax.ShapeDtypeStruct((b, hq, s // ta, V_DIM, ta), jnp.bfloat16),
                   jax.ShapeDtypeStruct((n, MEM_W), jnp.bfloat16)],
        compiler_params=_params(("arbitrary",)),
        name="layer1_proj",
    )(x2, slab, picked, picked,
      positions.reshape(n, 1), row(norm1_g[1]), _bf(w_in_p), row(b_q_norm_g[0]), row(b_kv_norm_g[0]),
      _bf(wq_p), _bf(wkv_p), pad_gain(b_qn_g[0]).astype(f32), pad_gain(b_kn_g[0]).astype(f32),
      inv_l, sgn_l, kn_all, mem_v, row(mem_qn_g[1]))

    qb = s // ta
    hp = ATTN_HEADS_PER_STEP
    attn = pl.pallas_call(
        _attn_kernel,
        grid=(b, hq // hp, qb),
        in_specs=[pl.BlockSpec((None, hp, ta, QK_PAD), lambda bi, hi, i: (bi, hi, i, 0)),
                  pl.BlockSpec((None, hp, s, QK_PAD), lambda bi, hi, i: (bi, hi, 0, 0)),
                  pl.BlockSpec((None, hp, qb, V_DIM, ta), lambda bi, hi, i: (bi, hi, 0, 0, 0))],
        out_specs=pl.BlockSpec((ta, hp * V_DIM), lambda bi, hi, i: (bi * qb + i, hi)),
        out_shape=jax.ShapeDtypeStruct((n, hq * V_DIM), jnp.bfloat16),
        scratch_shapes=[pltpu.VMEM((hp, 1, ta), f32), pltpu.VMEM((hp, V_DIM + ATTN_SUM_ROWS, ta), f32)],
        compiler_params=_params(("arbitrary", "arbitrary", "arbitrary")),
        name="causal_attention",
    )(q, k, vt)

    wr1, br1 = _router_weights(moe_w_group[1], moe_b_group[1], moe_w_expert[1], moe_b_expert[1])
    x2, xn, slab, slabt, counts = pl.pallas_call(
        _layer1_out_kernel,
        grid=(n // tt,),
        in_specs=[tok(d), tok(hq * V_DIM), tok(MEM_W), _const_spec((hq * V_DIM + MEM_W, d)),
                  _const_spec((1, d)), _const_spec((d, LANE)), _const_spec((1, LANE))],
        out_specs=route_out_specs,
        out_shape=route_out_shape,
        scratch_shapes=route_scratch,
        compiler_params=_params(("arbitrary",)),
        name="layer1_out",
    )(x2, attn, mem_o, _bf(b_w_out[0]), row(norm2_g[1]), wr1, br1)
    picked = _moe(1, xn, slabt, counts, moe_w_gate, moe_w_up, moe_w_down)
    out = pl.pallas_call(
        _combine_kernel,
        grid=(n // tt,),
        in_specs=[tok(d), tok(LANE), *_picked_specs(n, d // 2)],
        out_specs=tok(d),
        out_shape=jax.ShapeDtypeStruct((n, d), f32),
        compiler_params=_params(("arbitrary",)),
        name="moe_combine",
    )(x2, slab, picked, picked)
    return out.reshape(b, s, d)
```

```python
import jax
import jax.numpy as jnp
from jax import lax
from jax.experimental import pallas as pl
from jax.experimental.pallas import tpu as pltpu
from jax.experimental.pallas import tpu_sc as plsc

EPS = 1e-6
LANE = 128
SUBLANE = 8
MEM_HEADS = 4
MEM_HEAD_DIM = 128
MEM_W = MEM_HEADS * MEM_HEAD_DIM
CHUNK = 128
A_GROUPS = 8
MLA_HEADS = 8
Q_LORA = 512
KV_LORA = 256
NOPE_DIM = 128
ROPE_DIM = 64
ROPE_HALF = ROPE_DIM // 2
V_DIM = 128
QK_DIM = NOPE_DIM + ROPE_DIM
QK_PAD = 2 * LANE
ROPE_BASE = 10000.0
N_GROUPS = 4
EXPERTS_PER_GROUP = 8
N_EXPERTS = N_GROUPS * EXPERTS_PER_GROUP
EXPERT_FF = 256
LOG2E = 1.4426950408889634

BATCH_GROUPS = 2
TOKEN_TILE = 512
SUB_TILE = 256
ROW_TILE = 256
ROW_TILES_PER_STEP = 2
ATTN_TILE = 512
ATTN_HEADS_PER_STEP = 2
ATTN_KEY_TILES_PER_TRIP = 4
ATTN_LOOKAHEAD = 2
ATTN_SUM_ROWS = 16
VMEM_LIMIT = 56 * 1024 * 1024
NEG_BIG = -1e30

SC_CORES = 2
SC_SUBCORES = 16
SC_WORKERS = SC_CORES * SC_SUBCORES
SC_INDEX_GROUP = 128
SC_CHUNK = 64

_NT = (((1,), (1,)), ((), ()))


def _const_spec(shape):
    nd = len(shape)
    return pl.BlockSpec(shape, lambda *_: (0,) * nd, pipeline_mode=pl.Buffered(1))


def _params(sem):
    return pltpu.CompilerParams(dimension_semantics=sem, vmem_limit_bytes=VMEM_LIMIT)


def _run_staggered(gens):
    waiting = list(gens)
    active = []
    while waiting or active:
        if waiting:
            active.append(waiting.pop(0))
        for g in list(active):
            try:
                next(g)
            except StopIteration:
                active.remove(g)


def _sub_rows(t):
    return [pl.ds(k * SUB_TILE, SUB_TILE) for k in range(t // SUB_TILE)]


def _rms(x, g):
    return x * lax.rsqrt(jnp.mean(x * x, axis=-1, keepdims=True) + EPS) * g


def _gelu(x):
    return 0.5 * x * (1.0 + lax.erf(x * (2.0 ** -0.5)))


def _bf(x):
    return x.astype(jnp.bfloat16)


def _dot(a, b):
    return jnp.dot(a, b, preferred_element_type=jnp.float32)


def _memkv_kernel(mem_ref, g_ref, w_ref, kng_ref, kn_ref, v_ref):
    h = _bf(_rms(mem_ref[...], g_ref[...]))
    kv = _dot(h, w_ref[...])
    v_ref[...] = _bf(kv[:, MEM_W:])
    for layer in range(kn_ref.shape[0]):
        g = kng_ref[layer]
        for hh in range(MEM_HEADS):
            k = kv[:, hh * LANE:(hh + 1) * LANE]
            kn_ref[layer, :, hh * LANE:(hh + 1) * LANE] = _bf(_rms(k, g))


def _mem_attention(qm, kn_ref, v_ref, qg):
    outs = []
    for hh in range(MEM_HEADS):
        sl = slice(hh * LANE, (hh + 1) * LANE)
        q = _rms(qm[:, sl], qg) * (MEM_HEAD_DIM ** -0.5)
        s = lax.dot_general(_bf(q), kn_ref[:, sl], _NT, preferred_element_type=jnp.float32)
        p = jnp.exp(s - jnp.max(s, axis=-1, keepdims=True))
        l = jnp.sum(p, axis=-1, keepdims=True)
        outs.append(_dot(_bf(p), v_ref[:, sl]) / l)
    return jnp.concatenate(outs, axis=-1)


def _pack_rows(x):
    w = x.shape[1] // 2
    bits = lambda v: lax.bitcast_convert_type(_bf(v).astype(jnp.float32), jnp.uint32)
    return (bits(x[:, :w]) >> 16) | (bits(x[:, w:]) & jnp.uint32(0xFFFF0000))


def _unpack_rows(p):
    lo = lax.bitcast_convert_type(p << 16, jnp.float32)
    hi = lax.bitcast_convert_type(p & jnp.uint32(0xFFFF0000), jnp.float32)
    return jnp.concatenate([lo, hi], axis=-1)


def _route_stages(x, rows, g2_ref, wr_ref, br_ref, carry_ref, xn_ref, slab_ref, slabt_ref, cnt_ref):
    t = x.shape[0]
    xn = _rms(x, g2_ref[...])
    xn_ref[rows, :] = _pack_rows(xn)
    logits = _dot(_bf(xn), wr_ref[...]) + br_ref[...]
    yield
    lane = lax.broadcasted_iota(jnp.int32, (t, LANE), 1)

    def first_max(v):
        m = jnp.max(v, axis=-1, keepdims=True)
        idx = jnp.min(jnp.where(v == m, lane, LANE), axis=-1, keepdims=True)
        return m, idx

    lg = jnp.where(lane < N_GROUPS, logits, NEG_BIG)
    gmax, gidx = first_max(lg)
    g_w = 1.0 / jnp.sum(jnp.exp(lg - gmax), axis=-1, keepdims=True)

    eid = lane - N_GROUPS
    in_grp = (eid >= 0) & (eid < N_EXPERTS) & ((eid >> 3) == gidx)
    le = jnp.where(in_grp, logits, NEG_BIG)
    m1, i1 = first_max(le)
    yield
    m2, i2 = first_max(jnp.where(lane == i1, NEG_BIG, le))
    r = jnp.exp(m2 - m1)
    w1 = g_w / (1.0 + r)
    w2 = w1 * r
    e1 = i1 - N_GROUPS
    e2 = i2 - N_GROUPS
    yield

    oh1 = lane == e1
    oh2 = lane == e2
    oh = jnp.where(oh1 | oh2, 1.0, 0.0)
    row = lax.broadcasted_iota(jnp.int32, (t, t), 0)
    col = lax.broadcasted_iota(jnp.int32, (t, t), 1)
    earlier = jnp.where(row > col, 1.0, 0.0).astype(jnp.bfloat16)
    before = _dot(earlier, _bf(oh)) + carry_ref[...]
    r1 = jnp.sum(jnp.where(oh1, before, 0.0), axis=-1, keepdims=True)
    r2 = jnp.sum(jnp.where(oh2, before, 0.0), axis=-1, keepdims=True)
    carry_ref[...] += jnp.sum(oh, axis=0, keepdims=True)
    cnt_ref[...] = carry_ref[...]

    vals = (e1.astype(jnp.float32), e2.astype(jnp.float32), w1, w2, r1, r2)
    slab = jnp.zeros((t, LANE), jnp.float32)
    for i, v in enumerate(vals):
        slab = jnp.where(lane == i, v, slab)
    slab_ref[rows, :] = slab
    slabt_ref[:, rows] = slab.T[:SUBLANE, :]


def _reset_carry(carry_ref):
    @pl.when(pl.program_id(0) == 0)
    def _():
        carry_ref[...] = jnp.zeros_like(carry_ref)


def _layer0_kernel(x_ref, g1_ref, win_ref, lng_ref, lnb_ref, ws_ref, bs_ref, kn_ref, v_ref, qg_ref,
                   wout_ref, g2_ref, wr_ref, br_ref,
                   xo_ref, xn_ref, slab_ref, slabt_ref, cnt_ref, carry_ref):
    d = x_ref.shape[1]
    _reset_carry(carry_ref)
    row = lax.broadcasted_iota(jnp.int32, (CHUNK, CHUNK), 0)
    col = lax.broadcasted_iota(jnp.int32, (CHUNK, CHUNK), 1)
    causal = row >= col

    def stages(rows):
        x = x_ref[rows, :]
        t = x.shape[0]
        h = _bf(_rms(x, g1_ref[...]))
        z = _dot(h, win_ref[...])
        yield
        u = _gelu(z[:, :d])
        v = _gelu(z[:, d:2 * d])
        mu = jnp.mean(v, axis=-1, keepdims=True)
        vc = v - mu
        var = jnp.mean(vc * vc, axis=-1, keepdims=True)
        v = _bf(vc * lax.rsqrt(var + EPS) * lng_ref[...] + lnb_ref[...])
        yield
        chunks = []
        for c in range(t // CHUNK):
            cols = []
            for g in range(A_GROUPS):
                w = jnp.where(causal, ws_ref[g], jnp.zeros((), ws_ref.dtype))
                cols.append(_dot(w, v[c * CHUNK:(c + 1) * CHUNK, g * LANE:(g + 1) * LANE]))
            chunks.append(jnp.concatenate(cols, axis=-1) + bs_ref[...])
        mix = _bf(u * jnp.concatenate(chunks, axis=0))
        yield
        mem = _bf(_mem_attention(z[:, 2 * d:], kn_ref, v_ref, qg_ref[...]))
        yield
        xo = x + _dot(mix, wout_ref[:d, :]) + _dot(mem, wout_ref[d:, :])
        xo_ref[rows, :] = xo
        yield
        yield from _route_stages(xo, rows, g2_ref, wr_ref, br_ref, carry_ref,
                                 xn_ref, slab_ref, slabt_ref, cnt_ref)

    _run_staggered([stages(rows) for rows in _sub_rows(x_ref.shape[0])])


def _layer1_proj_kernel(x_ref, slab_ref, y0_ref, y1_ref, pos_ref, g1_ref, win_ref, qng_ref, kvng_ref,
                        wq_ref, wkv_ref, qg_ref, kg_ref, inv_ref, sgn_ref, kn_ref, mv_ref, mqg_ref,
                        xo_ref, q_ref, k_ref, v_ref, mem_ref):
    o1 = Q_LORA
    o2 = o1 + KV_LORA
    o3 = o2 + MEM_W
    q_scale = (QK_DIM ** -0.5) * LOG2E

    def stages(rows):
        x = _combined(x_ref, slab_ref, y0_ref, y1_ref, rows)
        xo_ref[rows, :] = x
        h = _bf(_rms(x, g1_ref[...]))
        z = _dot(h, win_ref[...])
        yield
        cq = _bf(_rms(z[:, :o1], qng_ref[...]))
        ckv = _bf(_rms(z[:, o1:o2], kvng_ref[...]))
        k_rope = z[:, o3:o3 + LANE]
        ang = pos_ref[rows, :].astype(jnp.float32) * inv_ref[...]
        cos = jnp.cos(ang)
        sin = jnp.sin(ang) * sgn_ref[...]

        def rope(r):
            return r * cos + pltpu.roll(r, LANE // 2, 1) * sin

        q = _dot(cq, wq_ref[...])
        kv = _dot(ckv, wkv_ref[...])
        yield
        qg = qg_ref[...]
        kg = kg_ref[...]
        kr_ss = jnp.sum(k_rope * k_rope, axis=-1, keepdims=True)
        kr = rope(k_rope * kg[:, LANE:])
        for hh in range(MLA_HEADS):
            qh = q[:, hh * QK_PAD:(hh + 1) * QK_PAD]
            rq = lax.rsqrt(jnp.sum(qh * qh, axis=-1, keepdims=True) * (1.0 / QK_DIM) + EPS) * q_scale
            qh = qh * rq * qg
            q_ref[hh, rows, :LANE] = _bf(qh[:, :LANE])
            q_ref[hh, rows, LANE:] = _bf(rope(qh[:, LANE:]))
            kn = kv[:, hh * LANE:(hh + 1) * LANE]
            rk = lax.rsqrt((jnp.sum(kn * kn, axis=-1, keepdims=True) + kr_ss) * (1.0 / QK_DIM) + EPS)
            k_ref[hh, rows, :LANE] = _bf(kn * rk * kg[:, :LANE])
            k_ref[hh, rows, LANE:] = _bf(kr * rk)
            v_ref[hh, :, rows] = _bf(kv[:, (MLA_HEADS + hh) * LANE:(MLA_HEADS + hh + 1) * LANE].T)
            if hh % 4 == 3:
                yield
        mem_ref[rows, :] = _bf(_mem_attention(z[:, o2:o3], kn_ref, mv_ref, mqg_ref[...]))

    _run_staggered([stages(rows) for rows in _sub_rows(x_ref.shape[0])])


def _attn_kernel(q_ref, k_ref, vt_ref, o_ref, m_ref, acc_ref):
    i = pl.program_id(2)
    heads, tq = q_ref.shape[0], q_ref.shape[1]
    m_ref[...] = jnp.full_like(m_ref, NEG_BIG)
    acc_ref[...] = jnp.zeros_like(acc_ref)

    def scores(hh, j, masked):
        start = pl.multiple_of(j * tq, tq)
        s = lax.dot_general(k_ref[hh, pl.ds(start, tq), :], q_ref[hh], _NT,
                            preferred_element_type=jnp.float32)
        if masked:
            key = lax.broadcasted_iota(jnp.int32, (tq, tq), 0)
            qry = lax.broadcasted_iota(jnp.int32, (tq, tq), 1)
            s = jnp.where(key <= qry, s, NEG_BIG)
        return s

    ones_rows = jnp.ones((ATTN_SUM_ROWS, tq), jnp.bfloat16)

    def update(hh, j, s):
        m = m_ref[hh]
        m_new = jnp.maximum(m, jnp.max(s, axis=0, keepdims=True))
        alpha = jnp.exp2(m - m_new)
        p = _bf(jnp.exp2(s - m_new))
        m_ref[hh] = m_new
        vt1 = jnp.concatenate([vt_ref[hh, j], ones_rows], axis=0)
        acc_ref[hh] = alpha * acc_ref[hh] + _dot(vt1, p)

    def run(items, masked):
        ss = {}
        ahead = ATTN_LOOKAHEAD
        for t in range(len(items) + ahead):
            if t < len(items):
                ss[t] = scores(*items[t], masked)
            if t >= ahead:
                update(*items[t - ahead], ss.pop(t - ahead))

    def full_tiles(first, count):
        run([(hh, first + u) for u in range(count) for hh in range(heads)], False)

    unroll = ATTN_KEY_TILES_PER_TRIP

    def body(jj, c):
        full_tiles(unroll * jj, unroll)
        return c

    lax.fori_loop(0, i // unroll, body, 0)
    piece = unroll // 2
    while piece >= 1:
        @pl.when((i & piece) != 0)
        def _():
            full_tiles((i // (2 * piece)) * (2 * piece), piece)
        piece //= 2

    run([(hh, i) for hh in range(heads)], True)
    for hh in range(heads):
        acc = acc_ref[hh]
        o_ref[:, hh * V_DIM:(hh + 1) * V_DIM] = _bf((acc[:V_DIM] / acc[V_DIM:V_DIM + 1]).T)


def _layer1_out_kernel(x_ref, o_ref, mem_ref, wout_ref, g2_ref, wr_ref, br_ref,
                       xo_ref, xn_ref, slab_ref, slabt_ref, cnt_ref, carry_ref):
    d = o_ref.shape[1]
    _reset_carry(carry_ref)

    def stages(rows):
        xo = (x_ref[rows, :] + _dot(o_ref[rows, :], wout_ref[:d, :])
              + _dot(mem_ref[rows, :], wout_ref[d:, :]))
        xo_ref[rows, :] = xo
        yield
        yield from _route_stages(xo, rows, g2_ref, wr_ref, br_ref, carry_ref,
                                 xn_ref, slab_ref, slabt_ref, cnt_ref)

    _run_staggered([stages(rows) for rows in _sub_rows(x_ref.shape[0])])


def _sc_mesh():
    return plsc.VectorSubcoreMesh(core_axis_name="c", subcore_axis_name="s")


def _sc_worker_base(rows_per_worker):
    return (lax.axis_index("c") * SC_SUBCORES + lax.axis_index("s")) * rows_per_worker


def _sc_scatter_rows(x, idx0, idx1, p_rows):
    n, w = x.shape
    per = n // SC_WORKERS
    chunks = SC_INDEX_GROUP // SC_CHUNK
    assert n % SC_WORKERS == 0 and per % SC_INDEX_GROUP == 0

    @pl.kernel(out_type=jax.ShapeDtypeStruct((p_rows, w), x.dtype), mesh=_sc_mesh(),
               scratch_types=[pltpu.VMEM((1, SC_INDEX_GROUP), jnp.int32),
                              pltpu.VMEM((1, SC_INDEX_GROUP), jnp.int32),
                              pltpu.VMEM((SC_CHUNK, w), x.dtype), pltpu.VMEM((SC_CHUNK, w), x.dtype),
                              pltpu.SemaphoreType.DMA, pltpu.SemaphoreType.DMA],
               name="moe_dispatch_sc")
    def scatter(x_hbm, i0_hbm, i1_hbm, o_hbm, i0_v, i1_v, buf_a, buf_b, sem_a, sem_b):
        base = _sc_worker_base(per)

        @pl.loop(0, per // SC_INDEX_GROUP)
        def _(g):
            off = pl.multiple_of(base + g * SC_INDEX_GROUP, SC_INDEX_GROUP)
            pltpu.sync_copy(i0_hbm.at[:, pl.ds(off, SC_INDEX_GROUP)], i0_v)
            pltpu.sync_copy(i1_hbm.at[:, pl.ds(off, SC_INDEX_GROUP)], i1_v)
            pending = []
            for c in range(chunks):
                buf, sem = ((buf_a, sem_a), (buf_b, sem_b))[c % 2]
                if c >= 2:
                    for cp in pending[c - 2]:
                        cp.wait()
                pltpu.sync_copy(x_hbm.at[pl.ds(off + c * SC_CHUNK, SC_CHUNK)], buf)
                sl = pl.ds(c * SC_CHUNK, SC_CHUNK)
                pending.append((pltpu.async_copy(buf, o_hbm.at[i0_v.at[0, sl]], sem),
                                pltpu.async_copy(buf, o_hbm.at[i1_v.at[0, sl]], sem)))
            for cps in pending[max(chunks - 2, 0):]:
                for cp in cps:
                    cp.wait()

    return scatter(x, idx0.reshape(1, n), idx1.reshape(1, n))


def _sc_gather_rows(table, idx):
    m = idx.shape[0]
    w = table.shape[1]
    per = m // SC_WORKERS
    chunks = SC_INDEX_GROUP // SC_CHUNK
    assert m % SC_WORKERS == 0 and per % SC_INDEX_GROUP == 0

    @pl.kernel(out_type=jax.ShapeDtypeStruct((m, w), table.dtype), mesh=_sc_mesh(),
               scratch_types=[pltpu.VMEM((1, SC_INDEX_GROUP), jnp.int32),
                              pltpu.VMEM((SC_CHUNK, w), table.dtype), pltpu.VMEM((SC_CHUNK, w), table.dtype),
                              pltpu.SemaphoreType.DMA, pltpu.SemaphoreType.DMA],
               name="moe_combine_sc")
    def gather(t_hbm, i_hbm, o_hbm, i_v, buf_a, buf_b, sem_a, sem_b):
        base = _sc_worker_base(per)

        @pl.loop(0, per // SC_INDEX_GROUP)
        def _(g):
            off = pl.multiple_of(base + g * SC_INDEX_GROUP, SC_INDEX_GROUP)
            pltpu.sync_copy(i_hbm.at[:, pl.ds(off, SC_INDEX_GROUP)], i_v)
            pending = []
            for c in range(chunks):
                buf, sem = ((buf_a, sem_a), (buf_b, sem_b))[c % 2]
                if c >= 2:
                    pending[c - 2].wait()
                pltpu.sync_copy(t_hbm.at[i_v.at[0, pl.ds(c * SC_CHUNK, SC_CHUNK)]], buf)
                pending.append(pltpu.async_copy(buf, o_hbm.at[pl.ds(off + c * SC_CHUNK, SC_CHUNK)], sem))
            for cp in pending[max(chunks - 2, 0):]:
                cp.wait()

    return gather(table, idx.reshape(1, m))


def _ffn_kernel(exp_ref, rows_ref, xs_ref, *refs):
    k = ROW_TILES_PER_STEP
    w_refs = [refs[3 * s:3 * s + 3] for s in range(k)]
    ys_ref = refs[3 * k]
    w_bf = [refs[3 * k + 1 + 3 * s:3 * k + 4 + 3 * s] for s in range(k)]
    j = pl.program_id(0)
    rt = xs_ref.shape[0] // k

    for s in range(k):
        tile = k * j + s

        @pl.when((j == 0) | (exp_ref[tile] != exp_ref[jnp.maximum(tile - k, 0)]))
        def _():
            for src, dst in zip(w_refs[s], w_bf[s]):
                dst[...] = _bf(src[...])

    def stages(s):
        rows = pl.ds(s * rt, rt)
        x = _unpack_rows(xs_ref[rows, :])
        live = lax.broadcasted_iota(jnp.int32, x.shape, 0) < rows_ref[k * j + s]
        x = _bf(jnp.where(live, x, 0.0))
        wg, wu, wd = w_bf[s]
        g = _dot(x, wg[...])
        u = _dot(x, wu[...])
        yield
        act = _bf(g * jax.nn.sigmoid(g) * u)
        yield
        ys_ref[rows, :] = _pack_rows(_dot(act, wd[...]))

    _run_staggered([stages(s) for s in range(k)])


def _combined(x_ref, slab_ref, y0_ref, y1_ref, rows):
    slab = slab_ref[rows, :]
    return (x_ref[rows, :] + slab[:, 2:3] * _unpack_rows(y0_ref[rows, :])
            + slab[:, 3:4] * _unpack_rows(y1_ref[rows, :]))


def _combine_kernel(x_ref, slab_ref, y0_ref, y1_ref, *refs):
    refs[-1][...] = _combined(x_ref, slab_ref, y0_ref, y1_ref, slice(None))


def _moe(layer, xn, slabt, counts, w_gate, w_up, w_down):
    n, wp = xn.shape
    d = w_gate.shape[-2]
    rt = ROW_TILE
    k = ROW_TILES_PER_STEP
    p_max = 2 * n + N_EXPERTS * rt
    n_tiles = p_max // rt
    assert n_tiles % k == 0

    cnt = counts[0, :N_EXPERTS].astype(jnp.int32)
    padded = ((cnt + rt - 1) // rt) * rt
    off = jnp.concatenate([jnp.zeros((1,), jnp.int32), jnp.cumsum(padded)]).astype(jnp.int32)
    experts = jnp.arange(N_EXPERTS, dtype=jnp.int32)[:, None]

    def position(e_row, r_row):
        e = e_row.astype(jnp.int32)[None, :]
        return jnp.sum(jnp.where(e == experts, off[:N_EXPERTS, None], 0), axis=0) + r_row.astype(jnp.int32)

    pos0 = position(slabt[0], slabt[4])
    pos1 = position(slabt[1], slabt[5])
    tile_start = jnp.arange(n_tiles, dtype=jnp.int32) * rt
    tile_exp = jnp.minimum(jnp.sum(off[1:][None, :] <= tile_start[:, None], axis=1),
                           N_EXPERTS - 1).astype(jnp.int32)
    tile_rows = jnp.clip(off[tile_exp] + cnt[tile_exp] - tile_start, 0, rt).astype(jnp.int32)

    xs = _sc_scatter_rows(xn, pos0, pos1, p_max)

    f = w_gate.shape[-1]
    w_specs = []
    for s in range(k):
        pick = lambda j, ex, rw, s=s: (layer, ex[k * j + s], 0, 0)
        w_specs += [pl.BlockSpec((None, None, d, f), pick), pl.BlockSpec((None, None, d, f), pick),
                    pl.BlockSpec((None, None, f, d), pick)]
    bf_scratch = [pltpu.VMEM((d, f), jnp.bfloat16), pltpu.VMEM((d, f), jnp.bfloat16),
                  pltpu.VMEM((f, d), jnp.bfloat16)] * k
    ys = pl.pallas_call(
        _ffn_kernel,
        grid_spec=pltpu.PrefetchScalarGridSpec(
            num_scalar_prefetch=2, grid=(n_tiles // k,),
            in_specs=[pl.BlockSpec((k * rt, wp), lambda j, ex, rw: (j, 0))] + w_specs,
            out_specs=pl.BlockSpec((k * rt, wp), lambda j, ex, rw: (j, 0)),
            scratch_shapes=bf_scratch),
        out_shape=jax.ShapeDtypeStruct((p_max, wp), jnp.uint32),
        compiler_params=_params(("arbitrary",)),
        name="moe_ffn",
    )(tile_exp, tile_rows, xs, *([w_gate, w_up, w_down] * k))

    return _sc_gather_rows(ys, jnp.concatenate([pos0, pos1]))


def _picked_specs(n, wp):
    nb = n // TOKEN_TILE
    return [pl.BlockSpec((TOKEN_TILE, wp), lambda i: (i, 0)),
            pl.BlockSpec((TOKEN_TILE, wp), lambda i: (i + nb, 0))]


def _router_weights(w_group, b_group, w_expert, b_expert):
    d = w_group.shape[0]
    pad = LANE - N_GROUPS - N_EXPERTS
    wr = jnp.concatenate([w_group, w_expert, jnp.zeros((d, pad), w_group.dtype)], axis=1)
    br = jnp.concatenate([b_group, b_expert, jnp.zeros((pad,), b_group.dtype)])
    return _bf(wr), br.reshape(1, LANE).astype(jnp.float32)


def _rope_lanes(vec_half):
    z = jnp.zeros_like(vec_half)
    return jnp.concatenate([vec_half, z, vec_half, z], axis=-1)


def _pad_rope_cols(w):
    z = jnp.zeros(w.shape[:-1] + (ROPE_HALF,), w.dtype)
    return jnp.concatenate([w[..., :ROPE_HALF], z, w[..., ROPE_HALF:], z], axis=-1)


def _trunk(out_prev, chunk, x_all, mem, positions, mem_norm_g, w_mem_kv, mem_qn_g, mem_kn_g, norm1_g,
           norm2_g, a_w_in, a_ln_g, a_ln_b, a_w_s, a_b_s, a_w_out, b_w_in, b_q_norm_g, b_kv_norm_g, b_w_q_up,
           b_w_kv_up, b_qn_g, b_kn_g, b_w_out, moe_w_group, moe_b_group, moe_w_expert, moe_b_expert,
           moe_w_gate, moe_w_up, moe_w_down):
    n_total, d = x_all.shape
    b, s = positions.shape
    m = mem.shape[1]
    n = b * s
    depth = norm1_g.shape[0]
    tt = TOKEN_TILE
    ta = ATTN_TILE
    tiles_per_batch = s // tt
    assert depth == 2 and s % tt == 0 and tt % SUB_TILE == 0 and d == A_GROUPS * LANE
    assert s % ta == 0 and ta % tt == 0
    f32 = jnp.float32
    row = lambda v: v.reshape(1, -1).astype(f32)

    kn_all, mem_v = pl.pallas_call(
        _memkv_kernel,
        grid=(b,),
        in_specs=[pl.BlockSpec((m, d), lambda i: (i, 0)), _const_spec((1, d)),
                  _const_spec((d, 2 * MEM_W)), _const_spec((depth, 1, MEM_HEAD_DIM))],
        out_specs=[pl.BlockSpec((depth, m, MEM_W), lambda i: (0, i, 0)),
                   pl.BlockSpec((m, MEM_W), lambda i: (i, 0))],
        out_shape=[jax.ShapeDtypeStruct((depth, b * m, MEM_W), jnp.bfloat16),
                   jax.ShapeDtypeStruct((b * m, MEM_W), jnp.bfloat16)],
        compiler_params=_params(("arbitrary",)),
        name="mem_kv",
    )(mem.reshape(b * m, d), row(mem_norm_g), _bf(w_mem_kv), mem_kn_g.reshape(depth, 1, MEM_HEAD_DIM))

    tok = lambda width: pl.BlockSpec((tt, width), lambda i: (i, 0))
    kn_spec = lambda layer: pl.BlockSpec((None, m, MEM_W), lambda i: (layer, i // tiles_per_batch, 0))
    mv_spec = pl.BlockSpec((m, MEM_W), lambda i: (i // tiles_per_batch, 0))
    route_out_specs = [tok(d), tok(d // 2), tok(LANE), pl.BlockSpec((SUBLANE, tt), lambda i: (0, i)),
                       pl.BlockSpec((1, LANE), lambda i: (0, 0))]
    route_out_shape = [jax.ShapeDtypeStruct((n, d), f32), jax.ShapeDtypeStruct((n, d // 2), jnp.uint32),
                       jax.ShapeDtypeStruct((n, LANE), f32), jax.ShapeDtypeStruct((SUBLANE, n), f32),
                       jax.ShapeDtypeStruct((1, LANE), f32)]
    route_scratch = [pltpu.VMEM((1, LANE), f32)]

    first_block = chunk * (n // tt)
    group_tok = pl.BlockSpec((tt, d), lambda i: (i + first_block, 0))

    wr0, br0 = _router_weights(moe_w_group[0], moe_b_group[0], moe_w_expert[0], moe_b_expert[0])
    a_in = a_w_in.shape[-1]
    bias_s = jnp.repeat(a_b_s[0].T, LANE, axis=1).astype(f32)
    x2, xn, slab, slabt, counts = pl.pallas_call(
        _layer0_kernel,
        grid=(n // tt,),
        in_specs=[group_tok, _const_spec((1, d)), _const_spec((d, a_in)), _const_spec((1, d)),
                  _const_spec((1, d)), _const_spec((A_GROUPS, CHUNK, CHUNK)), _const_spec((CHUNK, d)),
                  kn_spec(0), mv_spec, _const_spec((1, MEM_HEAD_DIM)),
                  _const_spec((d + MEM_W, d)), _const_spec((1, d)), _const_spec((d, LANE)),
                  _const_spec((1, LANE))],
        out_specs=route_out_specs,
        out_shape=route_out_shape,
        scratch_shapes=route_scratch,
        compiler_params=_params(("arbitrary",)),
        name="layer0_mixer",
    )(x_all, row(norm1_g[0]), _bf(a_w_in[0]), row(a_ln_g[0]), row(a_ln_b[0]), _bf(a_w_s[0]), bias_s,
      kn_all, mem_v, row(mem_qn_g[0]), _bf(a_w_out[0]), row(norm2_g[0]), wr0, br0)
    picked = _moe(0, xn, slabt, counts, moe_w_gate, moe_w_up, moe_w_down)

    hq = MLA_HEADS
    o1, o2, o3 = Q_LORA, Q_LORA + KV_LORA, Q_LORA + KV_LORA + ROPE_DIM
    w_in = b_w_in[0]
    w_in_p = jnp.concatenate([w_in[:, :o2], w_in[:, o3:], _pad_rope_cols(w_in[:, o2:o3])], axis=1)
    wq = b_w_q_up[0].reshape(Q_LORA, hq, QK_DIM)
    wq_p = jnp.concatenate([wq[..., :NOPE_DIM], _pad_rope_cols(wq[..., NOPE_DIM:])], axis=-1)
    wq_p = wq_p.reshape(Q_LORA, hq * QK_PAD)
    wkv = b_w_kv_up[0].reshape(KV_LORA, hq, NOPE_DIM + V_DIM)
    wkv_p = jnp.concatenate([wkv[..., :NOPE_DIM].reshape(KV_LORA, hq * NOPE_DIM),
                             wkv[..., NOPE_DIM:].reshape(KV_LORA, hq * V_DIM)], axis=1)
    pad_gain = lambda g: jnp.concatenate([g[:NOPE_DIM], _pad_rope_cols(g[NOPE_DIM:])]).reshape(1, QK_PAD)
    half = jnp.arange(ROPE_HALF, dtype=f32)
    inv = ROPE_BASE ** (-(half * 2.0 / ROPE_DIM))
    inv_l = _rope_lanes(inv).reshape(1, LANE)
    sgn_l = jnp.concatenate([-jnp.ones((2 * ROPE_HALF,), f32), jnp.ones((2 * ROPE_HALF,), f32)]).reshape(1, LANE)

    in_w = w_in_p.shape[1]
    head_spec = lambda width: pl.BlockSpec((None, hq, tt, width),
                                           lambda i: (i // tiles_per_batch, 0, i % tiles_per_batch, 0))
    per_ta = ta // tt

    def vt_index(i):
        t = i % tiles_per_batch
        return (i // tiles_per_batch, 0, t // per_ta, 0, t % per_ta)

    vt_spec = pl.BlockSpec((None, hq, None, V_DIM, tt), vt_index)
    x2, q, k, vt, mem_o = pl.pallas_call(
        _layer1_proj_kernel,
        grid=(n // tt,),
        in_specs=[tok(d), tok(LANE), *_picked_specs(n, d // 2),
                  tok(1), _const_spec((1, d)), _const_spec((d, in_w)), _const_spec((1, Q_LORA)),
                  _const_spec((1, KV_LORA)), _const_spec((Q_LORA, hq * QK_PAD)),
                  _const_spec((KV_LORA, hq * (NOPE_DIM + V_DIM))), _const_spec((1, QK_PAD)),
                  _const_spec((1, QK_PAD)), _const_spec((1, LANE)), _const_spec((1, LANE)),
                  kn_spec(1), mv_spec, _const_spec((1, MEM_HEAD_DIM))],
        out_specs=[tok(d), head_spec(QK_PAD), head_spec(QK_PAD), vt_spec, tok(MEM_W)],
        out_shape=[jax.ShapeDtypeStruct((n, d), f32),
                   jax.ShapeDtypeStruct((b, hq, s, QK_PAD), jnp.bfloat16),
                   jax.ShapeDtypeStruct((b, hq, s, QK_PAD), jnp.bfloat16),
                   jax.ShapeDtypeStruct((b, hq, s // ta, V_DIM, ta), jnp.bfloat16),
                   jax.ShapeDtypeStruct((n, MEM_W), jnp.bfloat16)],
        compiler_params=_params(("arbitrary",)),
        name="layer1_proj",
    )(x2, slab, picked, picked,
      positions.reshape(n, 1), row(norm1_g[1]), _bf(w_in_p), row(b_q_norm_g[0]), row(b_kv_norm_g[0]),
      _bf(wq_p), _bf(wkv_p), pad_gain(b_qn_g[0]).astype(f32), pad_gain(b_kn_g[0]).astype(f32),
      inv_l, sgn_l, kn_all, mem_v, row(mem_qn_g[1]))

    qb = s // ta
    hp = ATTN_HEADS_PER_STEP
    attn = pl.pallas_call(
        _attn_kernel,
        grid=(b, hq // hp, qb),
        in_specs=[pl.BlockSpec((None, hp, ta, QK_PAD), lambda bi, hi, i: (bi, hi, i, 0)),
                  pl.BlockSpec((None, hp, s, QK_PAD), lambda bi, hi, i: (bi, hi, 0, 0)),
                  pl.BlockSpec((None, hp, qb, V_DIM, ta), lambda bi, hi, i: (bi, hi, 0, 0, 0))],
        out_specs=pl.BlockSpec((ta, hp * V_DIM), lambda bi, hi, i: (bi * qb + i, hi)),
        out_shape=jax.ShapeDtypeStruct((n, hq * V_DIM), jnp.bfloat16),
        scratch_shapes=[pltpu.VMEM((hp, 1, ta), f32), pltpu.VMEM((hp, V_DIM + ATTN_SUM_ROWS, ta), f32)],
        compiler_params=_params(("arbitrary", "arbitrary", "arbitrary")),
        name="causal_attention",
    )(q, k, vt)

    wr1, br1 = _router_weights(moe_w_group[1], moe_b_group[1], moe_w_expert[1], moe_b_expert[1])
    x2, xn, slab, slabt, counts = pl.pallas_call(
        _layer1_out_kernel,
        grid=(n // tt,),
        in_specs=[tok(d), tok(hq * V_DIM), tok(MEM_W), _const_spec((hq * V_DIM + MEM_W, d)),
                  _const_spec((1, d)), _const_spec((d, LANE)), _const_spec((1, LANE))],
        out_specs=route_out_specs,
        out_shape=route_out_shape,
        scratch_shapes=route_scratch,
        compiler_params=_params(("arbitrary",)),
        name="layer1_out",
    )(x2, attn, mem_o, _bf(b_w_out[0]), row(norm2_g[1]), wr1, br1)
    picked = _moe(1, xn, slabt, counts, moe_w_gate, moe_w_up, moe_w_down)
    prev = [] if out_prev is None else [out_prev]
    return pl.pallas_call(
        _combine_kernel,
        grid=(n // tt,),
        in_specs=[tok(d), tok(LANE), *_picked_specs(n, d // 2)] + [pl.BlockSpec(memory_space=pl.ANY)] * len(prev),
        out_specs=group_tok,
        out_shape=jax.ShapeDtypeStruct((n_total, d), f32),
        input_output_aliases={4: 0} if prev else {},
        compiler_params=_params(("arbitrary",)),
        name="moe_combine",
    )(x2, slab, picked, picked, *prev)


def kernel(x, mem, positions, mem_norm_g, w_mem_kv, mem_qn_g, mem_kn_g, norm1_g, norm2_g, a_w_in, a_ln_g, a_ln_b, a_w_s, a_b_s, a_w_out, b_w_in, b_q_norm_g, b_kv_norm_g, b_w_q_up, b_w_kv_up, b_qn_g, b_kn_g, b_w_out, moe_w_group, moe_b_group, moe_w_expert, moe_b_expert, moe_w_gate, moe_w_up, moe_w_down):
    params = (mem_norm_g, w_mem_kv, mem_qn_g, mem_kn_g, norm1_g, norm2_g, a_w_in, a_ln_g, a_ln_b, a_w_s,
              a_b_s, a_w_out, b_w_in, b_q_norm_g, b_kv_norm_g, b_w_q_up, b_w_kv_up, b_qn_g, b_kn_g, b_w_out,
              moe_w_group, moe_b_group, moe_w_expert, moe_b_expert, moe_w_gate, moe_w_up, moe_w_down)
    b, s, d = x.shape
    assert b % BATCH_GROUPS == 0
    g = b // BATCH_GROUPS
    x_all = x.reshape(b * s, d)
    out = None
    for c in range(BATCH_GROUPS):
        rows = slice(c * g, (c + 1) * g)
        out = _trunk(out, c, x_all, mem[rows], positions[rows], *params)
    return out.reshape(b, s, d)
```

```python
import jax
import jax.numpy as jnp
from jax import lax
from jax.experimental import pallas as pl
from jax.experimental.pallas import tpu as pltpu
from jax.experimental.pallas import tpu_sc as plsc

EPS = 1e-6
LANE = 128
SUBLANE = 8
MEM_HEADS = 4
MEM_HEAD_DIM = 128
MEM_W = MEM_HEADS * MEM_HEAD_DIM
CHUNK = 128
A_GROUPS = 8
MLA_HEADS = 8
Q_LORA = 512
KV_LORA = 256
NOPE_DIM = 128
ROPE_DIM = 64
ROPE_HALF = ROPE_DIM // 2
V_DIM = 128
QK_DIM = NOPE_DIM + ROPE_DIM
QK_PAD = 2 * LANE
ROPE_BASE = 10000.0
N_GROUPS = 4
EXPERTS_PER_GROUP = 8
N_EXPERTS = N_GROUPS * EXPERTS_PER_GROUP
EXPERT_FF = 256
LOG2E = 1.4426950408889634

BATCH_GROUPS = 2
TOKEN_TILE = 512
SUB_TILE = 256
ROW_TILE = 256
ROW_TILES_PER_STEP = 2
ATTN_TILE = 512
ATTN_HEADS_PER_STEP = 2
ATTN_KEY_TILES_PER_TRIP = 4
ATTN_LOOKAHEAD = 2
ATTN_SUM_ROWS = 16
VMEM_LIMIT = 56 * 1024 * 1024
NEG_BIG = -1e30

SC_CORES = 2
SC_SUBCORES = 16
SC_WORKERS = SC_CORES * SC_SUBCORES
SC_INDEX_GROUP = 128
SC_CHUNK = 64

_NT = (((1,), (1,)), ((), ()))


def _const_spec(shape):
    nd = len(shape)
    return pl.BlockSpec(shape, lambda *_: (0,) * nd, pipeline_mode=pl.Buffered(1))


def _params(sem):
    return pltpu.CompilerParams(dimension_semantics=sem, vmem_limit_bytes=VMEM_LIMIT)


def _run_staggered(gens):
    waiting = list(gens)
    active = []
    while waiting or active:
        if waiting:
            active.append(waiting.pop(0))
        for g in list(active):
            try:
                next(g)
            except StopIteration:
                active.remove(g)


def _sub_rows(t):
    return [pl.ds(k * SUB_TILE, SUB_TILE) for k in range(t // SUB_TILE)]


def _rms(x, g):
    return x * lax.rsqrt(jnp.mean(x * x, axis=-1, keepdims=True) + EPS) * g


def _gelu(x):
    return 0.5 * x * (1.0 + lax.erf(x * (2.0 ** -0.5)))


def _bf(x):
    return x.astype(jnp.bfloat16)


def _dot(a, b):
    return jnp.dot(a, b, preferred_element_type=jnp.float32)


def _memkv_kernel(mem_ref, g_ref, w_ref, kng_ref, kn_ref, v_ref):
    h = _bf(_rms(mem_ref[...], g_ref[...]))
    kv = _dot(h, w_ref[...])
    v_ref[...] = _bf(kv[:, MEM_W:])
    for layer in range(kn_ref.shape[0]):
        g = kng_ref[layer]
        for hh in range(MEM_HEADS):
            k = kv[:, hh * LANE:(hh + 1) * LANE]
            kn_ref[layer, :, hh * LANE:(hh + 1) * LANE] = _bf(_rms(k, g))


def _mem_attention(qm, kn_ref, v_ref, qg):
    outs = []
    for hh in range(MEM_HEADS):
        sl = slice(hh * LANE, (hh + 1) * LANE)
        q = _rms(qm[:, sl], qg) * (MEM_HEAD_DIM ** -0.5)
        s = lax.dot_general(_bf(q), kn_ref[:, sl], _NT, preferred_element_type=jnp.float32)
        p = jnp.exp(s - jnp.max(s, axis=-1, keepdims=True))
        l = jnp.sum(p, axis=-1, keepdims=True)
        outs.append(_dot(_bf(p), v_ref[:, sl]) / l)
    return jnp.concatenate(outs, axis=-1)


def _pack_rows(x):
    w = x.shape[1] // 2
    bits = lambda v: lax.bitcast_convert_type(_bf(v).astype(jnp.float32), jnp.uint32)
    return (bits(x[:, :w]) >> 16) | (bits(x[:, w:]) & jnp.uint32(0xFFFF0000))


def _unpack_rows(p):
    lo = lax.bitcast_convert_type(p << 16, jnp.float32)
    hi = lax.bitcast_convert_type(p & jnp.uint32(0xFFFF0000), jnp.float32)
    return jnp.concatenate([lo, hi], axis=-1)


def _route_stages(x, rows, g2_ref, wr_ref, br_ref, carry_ref, xn_ref, slab_ref, slabt_ref, cnt_ref):
    t = x.shape[0]
    xn = _rms(x, g2_ref[...])
    xn_ref[rows, :] = _pack_rows(xn)
    logits = _dot(_bf(xn), wr_ref[...]) + br_ref[...]
    yield
    lane = lax.broadcasted_iota(jnp.int32, (t, LANE), 1)

    def first_max(v):
        m = jnp.max(v, axis=-1, keepdims=True)
        idx = jnp.min(jnp.where(v == m, lane, LANE), axis=-1, keepdims=True)
        return m, idx

    lg = jnp.where(lane < N_GROUPS, logits, NEG_BIG)
    gmax, gidx = first_max(lg)
    g_w = 1.0 / jnp.sum(jnp.exp(lg - gmax), axis=-1, keepdims=True)

    eid = lane - N_GROUPS
    in_grp = (eid >= 0) & (eid < N_EXPERTS) & ((eid >> 3) == gidx)
    le = jnp.where(in_grp, logits, NEG_BIG)
    m1, i1 = first_max(le)
    yield
    m2, i2 = first_max(jnp.where(lane == i1, NEG_BIG, le))
    r = jnp.exp(m2 - m1)
    w1 = g_w / (1.0 + r)
    w2 = w1 * r
    e1 = i1 - N_GROUPS
    e2 = i2 - N_GROUPS
    yield

    oh1 = lane == e1
    oh2 = lane == e2
    oh = jnp.where(oh1 | oh2, 1.0, 0.0)
    row = lax.broadcasted_iota(jnp.int32, (t, t), 0)
    col = lax.broadcasted_iota(jnp.int32, (t, t), 1)
    earlier = jnp.where(row > col, 1.0, 0.0).astype(jnp.bfloat16)
    before = _dot(earlier, _bf(oh)) + carry_ref[...]
    r1 = jnp.sum(jnp.where(oh1, before, 0.0), axis=-1, keepdims=True)
    r2 = jnp.sum(jnp.where(oh2, before, 0.0), axis=-1, keepdims=True)
    carry_ref[...] += jnp.sum(oh, axis=0, keepdims=True)
    cnt_ref[...] = carry_ref[...]

    vals = (e1.astype(jnp.float32), e2.astype(jnp.float32), w1, w2, r1, r2)
    slab = jnp.zeros((t, LANE), jnp.float32)
    for i, v in enumerate(vals):
        slab = jnp.where(lane == i, v, slab)
    slab_ref[rows, :] = slab
    slabt_ref[:, rows] = slab.T[:SUBLANE, :]


def _reset_carry(carry_ref):
    @pl.when(pl.program_id(0) == 0)
    def _():
        carry_ref[...] = jnp.zeros_like(carry_ref)


def _layer0_kernel(x_ref, g1_ref, win_ref, lng_ref, lnb_ref, ws_ref, bs_ref, kn_ref, v_ref, qg_ref,
                   wout_ref, g2_ref, wr_ref, br_ref,
                   xo_ref, xn_ref, slab_ref, slabt_ref, cnt_ref, carry_ref):
    d = x_ref.shape[1]
    _reset_carry(carry_ref)
    row = lax.broadcasted_iota(jnp.int32, (CHUNK, CHUNK), 0)
    col = lax.broadcasted_iota(jnp.int32, (CHUNK, CHUNK), 1)
    causal = row >= col

    def stages(rows):
        x = x_ref[rows, :]
        t = x.shape[0]
        h = _bf(_rms(x, g1_ref[...]))
        z = _dot(h, win_ref[...])
        yield
        u = _gelu(z[:, :d])
        v = _gelu(z[:, d:2 * d])
        mu = jnp.mean(v, axis=-1, keepdims=True)
        vc = v - mu
        var = jnp.mean(vc * vc, axis=-1, keepdims=True)
        v = _bf(vc * lax.rsqrt(var + EPS) * lng_ref[...] + lnb_ref[...])
        yield
        chunks = []
        for c in range(t // CHUNK):
            cols = []
            for g in range(A_GROUPS):
                w = jnp.where(causal, ws_ref[g], jnp.zeros((), ws_ref.dtype))
                cols.append(_dot(w, v[c * CHUNK:(c + 1) * CHUNK, g * LANE:(g + 1) * LANE]))
            chunks.append(jnp.concatenate(cols, axis=-1) + bs_ref[...])
        mix = _bf(u * jnp.concatenate(chunks, axis=0))
        yield
        mem = _bf(_mem_attention(z[:, 2 * d:], kn_ref, v_ref, qg_ref[...]))
        yield
        xo = x + _dot(mix, wout_ref[:d, :]) + _dot(mem, wout_ref[d:, :])
        xo_ref[rows, :] = xo
        yield
        yield from _route_stages(xo, rows, g2_ref, wr_ref, br_ref, carry_ref,
                                 xn_ref, slab_ref, slabt_ref, cnt_ref)

    _run_staggered([stages(rows) for rows in _sub_rows(x_ref.shape[0])])


def _layer1_proj_kernel(x_ref, slab_ref, y0_ref, y1_ref, pos_ref, g1_ref, win_ref, qng_ref, kvng_ref,
                        wq_ref, wkv_ref, qg_ref, kg_ref, inv_ref, sgn_ref, kn_ref, mv_ref, mqg_ref,
                        xo_ref, q_ref, k_ref, v_ref, mem_ref):
    o1 = Q_LORA
    o2 = o1 + KV_LORA
    o3 = o2 + MEM_W
    q_scale = (QK_DIM ** -0.5) * LOG2E

    def stages(rows):
        x = _combined(x_ref, slab_ref, y0_ref, y1_ref, rows)
        xo_ref[rows, :] = x
        h = _bf(_rms(x, g1_ref[...]))
        z = _dot(h, win_ref[...])
        yield
        cq = _bf(_rms(z[:, :o1], qng_ref[...]))
        ckv = _bf(_rms(z[:, o1:o2], kvng_ref[...]))
        k_rope = z[:, o3:o3 + LANE]
        ang = pos_ref[rows, :].astype(jnp.float32) * inv_ref[...]
        cos = jnp.cos(ang)
        sin = jnp.sin(ang) * sgn_ref[...]

        def rope(r):
            return r * cos + pltpu.roll(r, LANE // 2, 1) * sin

        q = _dot(cq, wq_ref[...])
        kv = _dot(ckv, wkv_ref[...])
        yield
        qg = qg_ref[...]
        kg = kg_ref[...]
        kr_ss = jnp.sum(k_rope * k_rope, axis=-1, keepdims=True)
        kr = rope(k_rope * kg[:, LANE:])
        for hh in range(MLA_HEADS):
            qh = q[:, hh * QK_PAD:(hh + 1) * QK_PAD]
            rq = lax.rsqrt(jnp.sum(qh * qh, axis=-1, keepdims=True) * (1.0 / QK_DIM) + EPS) * q_scale
            qh = qh * rq * qg
            q_ref[hh, rows, :LANE] = _bf(qh[:, :LANE])
            q_ref[hh, rows, LANE:] = _bf(rope(qh[:, LANE:]))
            kn = kv[:, hh * LANE:(hh + 1) * LANE]
            rk = lax.rsqrt((jnp.sum(kn * kn, axis=-1, keepdims=True) + kr_ss) * (1.0 / QK_DIM) + EPS)
            k_ref[hh, rows, :LANE] = _bf(kn * rk * kg[:, :LANE])
            k_ref[hh, rows, LANE:] = _bf(kr * rk)
            v_ref[hh, :, rows] = _bf(kv[:, (MLA_HEADS + hh) * LANE:(MLA_HEADS + hh + 1) * LANE].T)
            if hh % 4 == 3:
                yield
        mem_ref[rows, :] = _bf(_mem_attention(z[:, o2:o3], kn_ref, mv_ref, mqg_ref[...]))

    _run_staggered([stages(rows) for rows in _sub_rows(x_ref.shape[0])])


def _attn_kernel(q_ref, k_ref, vt_ref, o_ref, m_ref, acc_ref):
    i = pl.program_id(2)
    heads, tq = q_ref.shape[0], q_ref.shape[1]
    m_ref[...] = jnp.full_like(m_ref, NEG_BIG)
    acc_ref[...] = jnp.zeros_like(acc_ref)

    def scores(hh, j, masked):
        start = pl.multiple_of(j * tq, tq)
        s = lax.dot_general(k_ref[hh, pl.ds(start, tq), :], q_ref[hh], _NT,
                            preferred_element_type=jnp.float32)
        if masked:
            key = lax.broadcasted_iota(jnp.int32, (tq, tq), 0)
            qry = lax.broadcasted_iota(jnp.int32, (tq, tq), 1)
            s = jnp.where(key <= qry, s, NEG_BIG)
        return s

    ones_rows = jnp.ones((ATTN_SUM_ROWS, tq), jnp.bfloat16)

    def update(hh, j, s):
        m = m_ref[hh]
        m_new = jnp.maximum(m, jnp.max(s, axis=0, keepdims=True))
        alpha = jnp.exp2(m - m_new)
        p = _bf(jnp.exp2(s - m_new))
        m_ref[hh] = m_new
        vt1 = jnp.concatenate([vt_ref[hh, j], ones_rows], axis=0)
        acc_ref[hh] = alpha * acc_ref[hh] + _dot(vt1, p)

    def run(items, masked):
        ss = {}
        ahead = ATTN_LOOKAHEAD
        for t in range(len(items) + ahead):
            if t < len(items):
                ss[t] = scores(*items[t], masked)
            if t >= ahead:
                update(*items[t - ahead], ss.pop(t - ahead))

    def full_tiles(first, count):
        run([(hh, first + u) for u in range(count) for hh in range(heads)], False)

    unroll = ATTN_KEY_TILES_PER_TRIP

    def body(jj, c):
        full_tiles(unroll * jj, unroll)
        return c

    lax.fori_loop(0, i // unroll, body, 0)
    piece = unroll // 2
    while piece >= 1:
        @pl.when((i & piece) != 0)
        def _():
            full_tiles((i // (2 * piece)) * (2 * piece), piece)
        piece //= 2

    run([(hh, i) for hh in range(heads)], True)
    for hh in range(heads):
        acc = acc_ref[hh]
        o_ref[:, hh * V_DIM:(hh + 1) * V_DIM] = _bf((acc[:V_DIM] / acc[V_DIM:V_DIM + 1]).T)


def _layer1_out_kernel(x_ref, o_ref, mem_ref, wout_ref, g2_ref, wr_ref, br_ref,
                       xo_ref, xn_ref, slab_ref, slabt_ref, cnt_ref, carry_ref):
    d = o_ref.shape[1]
    _reset_carry(carry_ref)

    def stages(rows):
        xo = (x_ref[rows, :] + _dot(o_ref[rows, :], wout_ref[:d, :])
              + _dot(mem_ref[rows, :], wout_ref[d:, :]))
        xo_ref[rows, :] = xo
        yield
        yield from _route_stages(xo, rows, g2_ref, wr_ref, br_ref, carry_ref,
                                 xn_ref, slab_ref, slabt_ref, cnt_ref)

    _run_staggered([stages(rows) for rows in _sub_rows(x_ref.shape[0])])


def _sc_mesh():
    return plsc.VectorSubcoreMesh(core_axis_name="c", subcore_axis_name="s")


def _sc_worker_base(rows_per_worker):
    return (lax.axis_index("c") * SC_SUBCORES + lax.axis_index("s")) * rows_per_worker


def _sc_scatter_rows(x, idx0, idx1, p_rows):
    n, w = x.shape
    per = n // SC_WORKERS
    chunks = SC_INDEX_GROUP // SC_CHUNK
    assert n % SC_WORKERS == 0 and per % SC_INDEX_GROUP == 0

    @pl.kernel(out_type=jax.ShapeDtypeStruct((p_rows, w), x.dtype), mesh=_sc_mesh(),
               scratch_types=[pltpu.VMEM((1, SC_INDEX_GROUP), jnp.int32),
                              pltpu.VMEM((1, SC_INDEX_GROUP), jnp.int32),
                              pltpu.VMEM((SC_CHUNK, w), x.dtype), pltpu.VMEM((SC_CHUNK, w), x.dtype),
                              pltpu.SemaphoreType.DMA, pltpu.SemaphoreType.DMA],
               name="moe_dispatch_sc")
    def scatter(x_hbm, i0_hbm, i1_hbm, o_hbm, i0_v, i1_v, buf_a, buf_b, sem_a, sem_b):
        base = _sc_worker_base(per)

        @pl.loop(0, per // SC_INDEX_GROUP)
        def _(g):
            off = pl.multiple_of(base + g * SC_INDEX_GROUP, SC_INDEX_GROUP)
            pltpu.sync_copy(i0_hbm.at[:, pl.ds(off, SC_INDEX_GROUP)], i0_v)
            pltpu.sync_copy(i1_hbm.at[:, pl.ds(off, SC_INDEX_GROUP)], i1_v)
            pending = []
            for c in range(chunks):
                buf, sem = ((buf_a, sem_a), (buf_b, sem_b))[c % 2]
                if c >= 2:
                    for cp in pending[c - 2]:
                        cp.wait()
                pltpu.sync_copy(x_hbm.at[pl.ds(off + c * SC_CHUNK, SC_CHUNK)], buf)
                sl = pl.ds(c * SC_CHUNK, SC_CHUNK)
                pending.append((pltpu.async_copy(buf, o_hbm.at[i0_v.at[0, sl]], sem),
                                pltpu.async_copy(buf, o_hbm.at[i1_v.at[0, sl]], sem)))
            for cps in pending[max(chunks - 2, 0):]:
                for cp in cps:
                    cp.wait()

    return scatter(x, idx0.reshape(1, n), idx1.reshape(1, n))


def _sc_gather_rows(table, idx):
    m = idx.shape[0]
    w = table.shape[1]
    per = m // SC_WORKERS
    chunks = SC_INDEX_GROUP // SC_CHUNK
    assert m % SC_WORKERS == 0 and per % SC_INDEX_GROUP == 0

    @pl.kernel(out_type=jax.ShapeDtypeStruct((m, w), table.dtype), mesh=_sc_mesh(),
               scratch_types=[pltpu.VMEM((1, SC_INDEX_GROUP), jnp.int32),
                              pltpu.VMEM((SC_CHUNK, w), table.dtype), pltpu.VMEM((SC_CHUNK, w), table.dtype),
                              pltpu.SemaphoreType.DMA, pltpu.SemaphoreType.DMA],
               name="moe_combine_sc")
    def gather(t_hbm, i_hbm, o_hbm, i_v, buf_a, buf_b, sem_a, sem_b):
        base = _sc_worker_base(per)

        @pl.loop(0, per // SC_INDEX_GROUP)
        def _(g):
            off = pl.multiple_of(base + g * SC_INDEX_GROUP, SC_INDEX_GROUP)
            pltpu.sync_copy(i_hbm.at[:, pl.ds(off, SC_INDEX_GROUP)], i_v)
            pending = []
            for c in range(chunks):
                buf, sem = ((buf_a, sem_a), (buf_b, sem_b))[c % 2]
                if c >= 2:
                    pending[c - 2].wait()
                pltpu.sync_copy(t_hbm.at[i_v.at[0, pl.ds(c * SC_CHUNK, SC_CHUNK)]], buf)
                pending.append(pltpu.async_copy(buf, o_hbm.at[pl.ds(off + c * SC_CHUNK, SC_CHUNK)], sem))
            for cp in pending[max(chunks - 2, 0):]:
                cp.wait()

    return gather(table, idx.reshape(1, m))


def _ffn_kernel(exp_ref, rows_ref, xs_ref, *refs):
    k = ROW_TILES_PER_STEP
    w_refs = [refs[3 * s:3 * s + 3] for s in range(k)]
    ys_ref = refs[3 * k]
    w_bf = [refs[3 * k + 1 + 3 * s:3 * k + 4 + 3 * s] for s in range(k)]
    j = pl.program_id(0)
    steps = pl.num_programs(0)

    for s in range(k):
        tile = s * steps + j

        @pl.when((j == 0) | (exp_ref[tile] != exp_ref[jnp.maximum(tile - 1, 0)]))
        def _():
            for src, dst in zip(w_refs[s], w_bf[s]):
                dst[...] = _bf(src[...])

    def stages(s):
        x = _unpack_rows(xs_ref[s])
        live = lax.broadcasted_iota(jnp.int32, x.shape, 0) < rows_ref[s * steps + j]
        x = _bf(jnp.where(live, x, 0.0))
        wg, wu, wd = w_bf[s]
        g = _dot(x, wg[...])
        u = _dot(x, wu[...])
        yield
        act = _bf(g * jax.nn.sigmoid(g) * u)
        yield
        ys_ref[s] = _pack_rows(_dot(act, wd[...]))

    _run_staggered([stages(s) for s in range(k)])


def _combined(x_ref, slab_ref, y0_ref, y1_ref, rows):
    slab = slab_ref[rows, :]
    return (x_ref[rows, :] + slab[:, 2:3] * _unpack_rows(y0_ref[rows, :])
            + slab[:, 3:4] * _unpack_rows(y1_ref[rows, :]))


def _combine_kernel(x_ref, slab_ref, y0_ref, y1_ref, *refs):
    refs[-1][...] = _combined(x_ref, slab_ref, y0_ref, y1_ref, slice(None))


def _moe(layer, xn, slabt, counts, w_gate, w_up, w_down):
    n, wp = xn.shape
    d = w_gate.shape[-2]
    rt = ROW_TILE
    k = ROW_TILES_PER_STEP
    p_max = 2 * n + N_EXPERTS * rt
    n_tiles = p_max // rt
    assert n_tiles % k == 0

    cnt = counts[0, :N_EXPERTS].astype(jnp.int32)
    padded = ((cnt + rt - 1) // rt) * rt
    off = jnp.concatenate([jnp.zeros((1,), jnp.int32), jnp.cumsum(padded)]).astype(jnp.int32)
    experts = jnp.arange(N_EXPERTS, dtype=jnp.int32)[:, None]

    def position(e_row, r_row):
        e = e_row.astype(jnp.int32)[None, :]
        return jnp.sum(jnp.where(e == experts, off[:N_EXPERTS, None], 0), axis=0) + r_row.astype(jnp.int32)

    pos0 = position(slabt[0], slabt[4])
    pos1 = position(slabt[1], slabt[5])
    tile_start = jnp.arange(n_tiles, dtype=jnp.int32) * rt
    tile_exp = jnp.minimum(jnp.sum(off[1:][None, :] <= tile_start[:, None], axis=1),
                           N_EXPERTS - 1).astype(jnp.int32)
    tile_rows = jnp.clip(off[tile_exp] + cnt[tile_exp] - tile_start, 0, rt).astype(jnp.int32)

    xs = _sc_scatter_rows(xn, pos0, pos1, p_max)

    f = w_gate.shape[-1]
    steps = n_tiles // k
    w_specs = []
    for s in range(k):
        pick = lambda j, ex, rw, s=s: (layer, ex[s * steps + j], 0, 0)
        w_specs += [pl.BlockSpec((None, None, d, f), pick), pl.BlockSpec((None, None, d, f), pick),
                    pl.BlockSpec((None, None, f, d), pick)]
    bf_scratch = [pltpu.VMEM((d, f), jnp.bfloat16), pltpu.VMEM((d, f), jnp.bfloat16),
                  pltpu.VMEM((f, d), jnp.bfloat16)] * k
    slot_rows = pl.BlockSpec((k, rt, wp), lambda j, ex, rw: (0, j, 0))
    ys = pl.pallas_call(
        _ffn_kernel,
        grid_spec=pltpu.PrefetchScalarGridSpec(
            num_scalar_prefetch=2, grid=(steps,),
            in_specs=[slot_rows] + w_specs,
            out_specs=slot_rows,
            scratch_shapes=bf_scratch),
        out_shape=jax.ShapeDtypeStruct((k, steps * rt, wp), jnp.uint32),
        compiler_params=_params(("arbitrary",)),
        name="moe_ffn",
    )(tile_exp, tile_rows, xs.reshape(k, steps * rt, wp), *([w_gate, w_up, w_down] * k))

    return _sc_gather_rows(ys.reshape(p_max, wp), jnp.concatenate([pos0, pos1]))


def _picked_specs(n, wp):
    nb = n // TOKEN_TILE
    return [pl.BlockSpec((TOKEN_TILE, wp), lambda i: (i, 0)),
            pl.BlockSpec((TOKEN_TILE, wp), lambda i: (i + nb, 0))]


def _router_weights(w_group, b_group, w_expert, b_expert):
    d = w_group.shape[0]
    pad = LANE - N_GROUPS - N_EXPERTS
    wr = jnp.concatenate([w_group, w_expert, jnp.zeros((d, pad), w_group.dtype)], axis=1)
    br = jnp.concatenate([b_group, b_expert, jnp.zeros((pad,), b_group.dtype)])
    return _bf(wr), br.reshape(1, LANE).astype(jnp.float32)


def _rope_lanes(vec_half):
    z = jnp.zeros_like(vec_half)
    return jnp.concatenate([vec_half, z, vec_half, z], axis=-1)


def _pad_rope_cols(w):
    z = jnp.zeros(w.shape[:-1] + (ROPE_HALF,), w.dtype)
    return jnp.concatenate([w[..., :ROPE_HALF], z, w[..., ROPE_HALF:], z], axis=-1)


def _trunk(out_prev, chunk, x_all, mem, positions, mem_norm_g, w_mem_kv, mem_qn_g, mem_kn_g, norm1_g,
           norm2_g, a_w_in, a_ln_g, a_ln_b, a_w_s, a_b_s, a_w_out, b_w_in, b_q_norm_g, b_kv_norm_g, b_w_q_up,
           b_w_kv_up, b_qn_g, b_kn_g, b_w_out, moe_w_group, moe_b_group, moe_w_expert, moe_b_expert,
           moe_w_gate, moe_w_up, moe_w_down):
    n_total, d = x_all.shape
    b, s = positions.shape
    m = mem.shape[1]
    n = b * s
    depth = norm1_g.shape[0]
    tt = TOKEN_TILE
    ta = ATTN_TILE
    tiles_per_batch = s // tt
    assert depth == 2 and s % tt == 0 and tt % SUB_TILE == 0 and d == A_GROUPS * LANE
    assert s % ta == 0 and ta % tt == 0
    f32 = jnp.float32
    row = lambda v: v.reshape(1, -1).astype(f32)

    kn_all, mem_v = pl.pallas_call(
        _memkv_kernel,
        grid=(b,),
        in_specs=[pl.BlockSpec((m, d), lambda i: (i, 0)), _const_spec((1, d)),
                  _const_spec((d, 2 * MEM_W)), _const_spec((depth, 1, MEM_HEAD_DIM))],
        out_specs=[pl.BlockSpec((depth, m, MEM_W), lambda i: (0, i, 0)),
                   pl.BlockSpec((m, MEM_W), lambda i: (i, 0))],
        out_shape=[jax.ShapeDtypeStruct((depth, b * m, MEM_W), jnp.bfloat16),
                   jax.ShapeDtypeStruct((b * m, MEM_W), jnp.bfloat16)],
        compiler_params=_params(("arbitrary",)),
        name="mem_kv",
    )(mem.reshape(b * m, d), row(mem_norm_g), _bf(w_mem_kv), mem_kn_g.reshape(depth, 1, MEM_HEAD_DIM))

    tok = lambda width: pl.BlockSpec((tt, width), lambda i: (i, 0))
    kn_spec = lambda layer: pl.BlockSpec((None, m, MEM_W), lambda i: (layer, i // tiles_per_batch, 0))
    mv_spec = pl.BlockSpec((m, MEM_W), lambda i: (i // tiles_per_batch, 0))
    route_out_specs = [tok(d), tok(d // 2), tok(LANE), pl.BlockSpec((SUBLANE, tt), lambda i: (0, i)),
                       pl.BlockSpec((1, LANE), lambda i: (0, 0))]
    route_out_shape = [jax.ShapeDtypeStruct((n, d), f32), jax.ShapeDtypeStruct((n, d // 2), jnp.uint32),
                       jax.ShapeDtypeStruct((n, LANE), f32), jax.ShapeDtypeStruct((SUBLANE, n), f32),
                       jax.ShapeDtypeStruct((1, LANE), f32)]
    route_scratch = [pltpu.VMEM((1, LANE), f32)]

    first_block = chunk * (n // tt)
    group_tok = pl.BlockSpec((tt, d), lambda i: (i + first_block, 0))

    wr0, br0 = _router_weights(moe_w_group[0], moe_b_group[0], moe_w_expert[0], moe_b_expert[0])
    a_in = a_w_in.shape[-1]
    bias_s = jnp.repeat(a_b_s[0].T, LANE, axis=1).astype(f32)
    x2, xn, slab, slabt, counts = pl.pallas_call(
        _layer0_kernel,
        grid=(n // tt,),
        in_specs=[group_tok, _const_spec((1, d)), _const_spec((d, a_in)), _const_spec((1, d)),
                  _const_spec((1, d)), _const_spec((A_GROUPS, CHUNK, CHUNK)), _const_spec((CHUNK, d)),
                  kn_spec(0), mv_spec, _const_spec((1, MEM_HEAD_DIM)),
                  _const_spec((d + MEM_W, d)), _const_spec((1, d)), _const_spec((d, LANE)),
                  _const_spec((1, LANE))],
        out_specs=route_out_specs,
        out_shape=route_out_shape,
        scratch_shapes=route_scratch,
        compiler_params=_params(("arbitrary",)),
        name="layer0_mixer",
    )(x_all, row(norm1_g[0]), _bf(a_w_in[0]), row(a_ln_g[0]), row(a_ln_b[0]), _bf(a_w_s[0]), bias_s,
      kn_all, mem_v, row(mem_qn_g[0]), _bf(a_w_out[0]), row(norm2_g[0]), wr0, br0)
    picked = _moe(0, xn, slabt, counts, moe_w_gate, moe_w_up, moe_w_down)

    hq = MLA_HEADS
    o1, o2, o3 = Q_LORA, Q_LORA + KV_LORA, Q_LORA + KV_LORA + ROPE_DIM
    w_in = b_w_in[0]
    w_in_p = jnp.concatenate([w_in[:, :o2], w_in[:, o3:], _pad_rope_cols(w_in[:, o2:o3])], axis=1)
    wq = b_w_q_up[0].reshape(Q_LORA, hq, QK_DIM)
    wq_p = jnp.concatenate([wq[..., :NOPE_DIM], _pad_rope_cols(wq[..., NOPE_DIM:])], axis=-1)
    wq_p = wq_p.reshape(Q_LORA, hq * QK_PAD)
    wkv = b_w_kv_up[0].reshape(KV_LORA, hq, NOPE_DIM + V_DIM)
    wkv_p = jnp.concatenate([wkv[..., :NOPE_DIM].reshape(KV_LORA, hq * NOPE_DIM),
                             wkv[..., NOPE_DIM:].reshape(KV_LORA, hq * V_DIM)], axis=1)
    pad_gain = lambda g: jnp.concatenate([g[:NOPE_DIM], _pad_rope_cols(g[NOPE_DIM:])]).reshape(1, QK_PAD)
    half = jnp.arange(ROPE_HALF, dtype=f32)
    inv = ROPE_BASE ** (-(half * 2.0 / ROPE_DIM))
    inv_l = _rope_lanes(inv).reshape(1, LANE)
    sgn_l = jnp.concatenate([-jnp.ones((2 * ROPE_HALF,), f32), jnp.ones((2 * ROPE_HALF,), f32)]).reshape(1, LANE)

    in_w = w_in_p.shape[1]
    head_spec = lambda width: pl.BlockSpec((None, hq, tt, width),
                                           lambda i: (i // tiles_per_batch, 0, i % tiles_per_batch, 0))
    per_ta = ta // tt

    def vt_index(i):
        t = i % tiles_per_batch
        return (i // tiles_per_batch, 0, t // per_ta, 0, t % per_ta)

    vt_spec = pl.BlockSpec((None, hq, None, V_DIM, tt), vt_index)
    x2, q, k, vt, mem_o = pl.pallas_call(
        _layer1_proj_kernel,
        grid=(n // tt,),
        in_specs=[tok(d), tok(LANE), *_picked_specs(n, d // 2),
                  tok(1), _const_spec((1, d)), _const_spec((d, in_w)), _const_spec((1, Q_LORA)),
                  _const_spec((1, KV_LORA)), _const_spec((Q_LORA, hq * QK_PAD)),
                  _const_spec((KV_LORA, hq * (NOPE_DIM + V_DIM))), _const_spec((1, QK_PAD)),
                  _const_spec((1, QK_PAD)), _const_spec((1, LANE)), _const_spec((1, LANE)),
                  kn_spec(1), mv_spec, _const_spec((1, MEM_HEAD_DIM))],
        out_specs=[tok(d), head_spec(QK_PAD), head_spec(QK_PAD), vt_spec, tok(MEM_W)],
        out_shape=[jax.ShapeDtypeStruct((n, d), f32),
                   jax.ShapeDtypeStruct((b, hq, s, QK_PAD), jnp.bfloat16),
                   jax.ShapeDtypeStruct((b, hq, s, QK_PAD), jnp.bfloat16),
                   jax.ShapeDtypeStruct((b, hq, s // ta, V_DIM, ta), jnp.bfloat16),
                   jax.ShapeDtypeStruct((n, MEM_W), jnp.bfloat16)],
        compiler_params=_params(("arbitrary",)),
        name="layer1_proj",
    )(x2, slab, picked, picked,
      positions.reshape(n, 1), row(norm1_g[1]), _bf(w_in_p), row(b_q_norm_g[0]), row(b_kv_norm_g[0]),
      _bf(wq_p), _bf(wkv_p), pad_gain(b_qn_g[0]).astype(f32), pad_gain(b_kn_g[0]).astype(f32),
      inv_l, sgn_l, kn_all, mem_v, row(mem_qn_g[1]))

    qb = s // ta
    hp = ATTN_HEADS_PER_STEP
    attn = pl.pallas_call(
        _attn_kernel,
        grid=(b, hq // hp, qb),
        in_specs=[pl.BlockSpec((None, hp, ta, QK_PAD), lambda bi, hi, i: (bi, hi, i, 0)),
                  pl.BlockSpec((None, hp, s, QK_PAD), lambda bi, hi, i: (bi, hi, 0, 0)),
                  pl.BlockSpec((None, hp, qb, V_DIM, ta), lambda bi, hi, i: (bi, hi, 0, 0, 0))],
        out_specs=pl.BlockSpec((ta, hp * V_DIM), lambda bi, hi, i: (bi * qb + i, hi)),
        out_shape=jax.ShapeDtypeStruct((n, hq * V_DIM), jnp.bfloat16),
        scratch_shapes=[pltpu.VMEM((hp, 1, ta), f32), pltpu.VMEM((hp, V_DIM + ATTN_SUM_ROWS, ta), f32)],
        compiler_params=_params(("arbitrary", "arbitrary", "arbitrary")),
        name="causal_attention",
    )(q, k, vt)

    wr1, br1 = _router_weights(moe_w_group[1], moe_b_group[1], moe_w_expert[1], moe_b_expert[1])
    x2, xn, slab, slabt, counts = pl.pallas_call(
        _layer1_out_kernel,
        grid=(n // tt,),
        in_specs=[tok(d), tok(hq * V_DIM), tok(MEM_W), _const_spec((hq * V_DIM + MEM_W, d)),
                  _const_spec((1, d)), _const_spec((d, LANE)), _const_spec((1, LANE))],
        out_specs=route_out_specs,
        out_shape=route_out_shape,
        scratch_shapes=route_scratch,
        compiler_params=_params(("arbitrary",)),
        name="layer1_out",
    )(x2, attn, mem_o, _bf(b_w_out[0]), row(norm2_g[1]), wr1, br1)
    picked = _moe(1, xn, slabt, counts, moe_w_gate, moe_w_up, moe_w_down)
    prev = [] if out_prev is None else [out_prev]
    return pl.pallas_call(
        _combine_kernel,
        grid=(n // tt,),
        in_specs=[tok(d), tok(LANE), *_picked_specs(n, d // 2)] + [pl.BlockSpec(memory_space=pl.ANY)] * len(prev),
        out_specs=group_tok,
        out_shape=jax.ShapeDtypeStruct((n_total, d), f32),
        input_output_aliases={4: 0} if prev else {},
        compiler_params=_params(("arbitrary",)),
        name="moe_combine",
    )(x2, slab, picked, picked, *prev)


def kernel(x, mem, positions, mem_norm_g, w_mem_kv, mem_qn_g, mem_kn_g, norm1_g, norm2_g, a_w_in, a_ln_g, a_ln_b, a_w_s, a_b_s, a_w_out, b_w_in, b_q_norm_g, b_kv_norm_g, b_w_q_up, b_w_kv_up, b_qn_g, b_kn_g, b_w_out, moe_w_group, moe_b_group, moe_w_expert, moe_b_expert, moe_w_gate, moe_w_up, moe_w_down):
    params = (mem_norm_g, w_mem_kv, mem_qn_g, mem_kn_g, norm1_g, norm2_g, a_w_in, a_ln_g, a_ln_b, a_w_s,
              a_b_s, a_w_out, b_w_in, b_q_norm_g, b_kv_norm_g, b_w_q_up, b_w_kv_up, b_qn_g, b_kn_g, b_w_out,
              moe_w_group, moe_b_group, moe_w_expert, moe_b_expert, moe_w_gate, moe_w_up, moe_w_down)
    b, s, d = x.shape
    assert b % BATCH_GROUPS == 0
    g = b // BATCH_GROUPS
    x_all = x.reshape(b * s, d)
    out = None
    for c in range(BATCH_GROUPS):
        rows = slice(c * g, (c + 1) * g)
        out = _trunk(out, c, x_all, mem[rows], positions[rows], *params)
    return out.reshape(b, s, d)
```

```python
import jax
import jax.numpy as jnp
from jax import lax
from jax.experimental import pallas as pl
from jax.experimental.pallas import tpu as pltpu
from jax.experimental.pallas import tpu_sc as plsc

EPS = 1e-6
LANE = 128
SUBLANE = 8
MEM_HEADS = 4
MEM_HEAD_DIM = 128
MEM_W = MEM_HEADS * MEM_HEAD_DIM
CHUNK = 128
A_GROUPS = 8
MLA_HEADS = 8
Q_LORA = 512
KV_LORA = 256
NOPE_DIM = 128
ROPE_DIM = 64
ROPE_HALF = ROPE_DIM // 2
V_DIM = 128
QK_DIM = NOPE_DIM + ROPE_DIM
QK_PAD = 2 * LANE
ROPE_BASE = 10000.0
N_GROUPS = 4
EXPERTS_PER_GROUP = 8
N_EXPERTS = N_GROUPS * EXPERTS_PER_GROUP
EXPERT_FF = 256
LOG2E = 1.4426950408889634

BATCH_GROUPS = 2
TOKEN_TILE = 512
SUB_TILE = 256
ROW_TILE = 256
ROW_TILES_PER_STEP = 2
ATTN_TILE = 512
ATTN_HEADS_PER_STEP = 2
ATTN_KEY_TILES_PER_TRIP = 4
ATTN_QUERY_SPLIT = 2
ATTN_LOOKAHEAD = 3
ATTN_SUM_ROWS = 16
VMEM_LIMIT = 56 * 1024 * 1024
NEG_BIG = -1e30

SC_CORES = 2
SC_SUBCORES = 16
SC_WORKERS = SC_CORES * SC_SUBCORES
SC_INDEX_GROUP = 128
SC_CHUNK = 64

_NT = (((1,), (1,)), ((), ()))


def _const_spec(shape):
    nd = len(shape)
    return pl.BlockSpec(shape, lambda *_: (0,) * nd, pipeline_mode=pl.Buffered(1))


def _params(sem):
    return pltpu.CompilerParams(dimension_semantics=sem, vmem_limit_bytes=VMEM_LIMIT)


def _run_staggered(gens):
    waiting = list(gens)
    active = []
    while waiting or active:
        if waiting:
            active.append(waiting.pop(0))
        for g in list(active):
            try:
                next(g)
            except StopIteration:
                active.remove(g)


def _sub_rows(t):
    return [pl.ds(k * SUB_TILE, SUB_TILE) for k in range(t // SUB_TILE)]


def _rms(x, g):
    return x * lax.rsqrt(jnp.mean(x * x, axis=-1, keepdims=True) + EPS) * g


def _gelu(x):
    return 0.5 * x * (1.0 + lax.erf(x * (2.0 ** -0.5)))


def _bf(x):
    return x.astype(jnp.bfloat16)


def _dot(a, b):
    return jnp.dot(a, b, preferred_element_type=jnp.float32)


def _memkv_kernel(mem_ref, g_ref, w_ref, kng_ref, kn_ref, v_ref):
    h = _bf(_rms(mem_ref[...], g_ref[...]))
    kv = _dot(h, w_ref[...])
    v_ref[...] = _bf(kv[:, MEM_W:])
    for layer in range(kn_ref.shape[0]):
        g = kng_ref[layer]
        for hh in range(MEM_HEADS):
            k = kv[:, hh * LANE:(hh + 1) * LANE]
            kn_ref[layer, :, hh * LANE:(hh + 1) * LANE] = _bf(_rms(k, g))


def _mem_attention(qm, kn_ref, v_ref, qg):
    outs = []
    for hh in range(MEM_HEADS):
        sl = slice(hh * LANE, (hh + 1) * LANE)
        q = _rms(qm[:, sl], qg) * (MEM_HEAD_DIM ** -0.5)
        s = lax.dot_general(_bf(q), kn_ref[:, sl], _NT, preferred_element_type=jnp.float32)
        p = jnp.exp(s - jnp.max(s, axis=-1, keepdims=True))
        l = jnp.sum(p, axis=-1, keepdims=True)
        outs.append(_dot(_bf(p), v_ref[:, sl]) / l)
    return jnp.concatenate(outs, axis=-1)


def _pack_rows(x):
    w = x.shape[1] // 2
    bits = lambda v: lax.bitcast_convert_type(_bf(v).astype(jnp.float32), jnp.uint32)
    return (bits(x[:, :w]) >> 16) | (bits(x[:, w:]) & jnp.uint32(0xFFFF0000))


def _unpack_rows(p):
    lo = lax.bitcast_convert_type(p << 16, jnp.float32)
    hi = lax.bitcast_convert_type(p & jnp.uint32(0xFFFF0000), jnp.float32)
    return jnp.concatenate([lo, hi], axis=-1)


def _route_stages(x, rows, g2_ref, wr_ref, br_ref, carry_ref, xn_ref, slab_ref, slabt_ref, cnt_ref):
    t = x.shape[0]
    xn = _rms(x, g2_ref[...])
    xn_ref[rows, :] = _pack_rows(xn)
    logits = _dot(_bf(xn), wr_ref[...]) + br_ref[...]
    yield
    lane = lax.broadcasted_iota(jnp.int32, (t, LANE), 1)

    def first_max(v):
        m = jnp.max(v, axis=-1, keepdims=True)
        idx = jnp.min(jnp.where(v == m, lane, LANE), axis=-1, keepdims=True)
        return m, idx

    lg = jnp.where(lane < N_GROUPS, logits, NEG_BIG)
    gmax, gidx = first_max(lg)
    g_w = 1.0 / jnp.sum(jnp.exp(lg - gmax), axis=-1, keepdims=True)

    eid = lane - N_GROUPS
    in_grp = (eid >= 0) & (eid < N_EXPERTS) & ((eid >> 3) == gidx)
    le = jnp.where(in_grp, logits, NEG_BIG)
    m1, i1 = first_max(le)
    yield
    m2, i2 = first_max(jnp.where(lane == i1, NEG_BIG, le))
    r = jnp.exp(m2 - m1)
    w1 = g_w / (1.0 + r)
    w2 = w1 * r
    e1 = i1 - N_GROUPS
    e2 = i2 - N_GROUPS
    yield

    oh1 = lane == e1
    oh2 = lane == e2
    oh = jnp.where(oh1 | oh2, 1.0, 0.0)
    row = lax.broadcasted_iota(jnp.int32, (t, t), 0)
    col = lax.broadcasted_iota(jnp.int32, (t, t), 1)
    earlier = jnp.where(row > col, 1.0, 0.0).astype(jnp.bfloat16)
    before = _dot(earlier, _bf(oh)) + carry_ref[...]
    r1 = jnp.sum(jnp.where(oh1, before, 0.0), axis=-1, keepdims=True)
    r2 = jnp.sum(jnp.where(oh2, before, 0.0), axis=-1, keepdims=True)
    carry_ref[...] += jnp.sum(oh, axis=0, keepdims=True)
    cnt_ref[...] = carry_ref[...]

    vals = (e1.astype(jnp.float32), e2.astype(jnp.float32), w1, w2, r1, r2)
    slab = jnp.zeros((t, LANE), jnp.float32)
    for i, v in enumerate(vals):
        slab = jnp.where(lane == i, v, slab)
    slab_ref[rows, :] = slab
    slabt_ref[:, rows] = slab.T[:SUBLANE, :]


def _reset_carry(carry_ref):
    @pl.when(pl.program_id(0) == 0)
    def _():
        carry_ref[...] = jnp.zeros_like(carry_ref)


def _layer0_kernel(x_ref, g1_ref, win_ref, lng_ref, lnb_ref, ws_ref, bs_ref, kn_ref, v_ref, qg_ref,
                   wout_ref, g2_ref, wr_ref, br_ref,
                   xo_ref, xn_ref, slab_ref, slabt_ref, cnt_ref, carry_ref):
    d = x_ref.shape[1]
    _reset_carry(carry_ref)
    row = lax.broadcasted_iota(jnp.int32, (CHUNK, CHUNK), 0)
    col = lax.broadcasted_iota(jnp.int32, (CHUNK, CHUNK), 1)
    causal = row >= col

    def stages(rows):
        x = x_ref[rows, :]
        t = x.shape[0]
        h = _bf(_rms(x, g1_ref[...]))
        z = _dot(h, win_ref[...])
        yield
        u = _gelu(z[:, :d])
        v = _gelu(z[:, d:2 * d])
        mu = jnp.mean(v, axis=-1, keepdims=True)
        vc = v - mu
        var = jnp.mean(vc * vc, axis=-1, keepdims=True)
        v = _bf(vc * lax.rsqrt(var + EPS) * lng_ref[...] + lnb_ref[...])
        yield
        chunks = []
        for c in range(t // CHUNK):
            cols = []
            for g in range(A_GROUPS):
                w = jnp.where(causal, ws_ref[g], jnp.zeros((), ws_ref.dtype))
                cols.append(_dot(w, v[c * CHUNK:(c + 1) * CHUNK, g * LANE:(g + 1) * LANE]))
            chunks.append(jnp.concatenate(cols, axis=-1) + bs_ref[...])
        mix = _bf(u * jnp.concatenate(chunks, axis=0))
        yield
        mem = _bf(_mem_attention(z[:, 2 * d:], kn_ref, v_ref, qg_ref[...]))
        yield
        xo = x + _dot(mix, wout_ref[:d, :]) + _dot(mem, wout_ref[d:, :])
        xo_ref[rows, :] = xo
        yield
        yield from _route_stages(xo, rows, g2_ref, wr_ref, br_ref, carry_ref,
                                 xn_ref, slab_ref, slabt_ref, cnt_ref)

    _run_staggered([stages(rows) for rows in _sub_rows(x_ref.shape[0])])


def _layer1_proj_kernel(x_ref, slab_ref, y0_ref, y1_ref, pos_ref, g1_ref, win_ref, qng_ref, kvng_ref,
                        wq_ref, wkv_ref, qg_ref, kg_ref, inv_ref, sgn_ref, kn_ref, mv_ref, mqg_ref,
                        xo_ref, q_ref, k_ref, v_ref, mem_ref):
    o1 = Q_LORA
    o2 = o1 + KV_LORA
    o3 = o2 + MEM_W
    q_scale = (QK_DIM ** -0.5) * LOG2E

    def stages(rows):
        x = _combined(x_ref, slab_ref, y0_ref, y1_ref, rows)
        xo_ref[rows, :] = x
        h = _bf(_rms(x, g1_ref[...]))
        z = _dot(h, win_ref[...])
        yield
        cq = _bf(_rms(z[:, :o1], qng_ref[...]))
        ckv = _bf(_rms(z[:, o1:o2], kvng_ref[...]))
        k_rope = z[:, o3:o3 + LANE]
        ang = pos_ref[rows, :].astype(jnp.float32) * inv_ref[...]
        cos = jnp.cos(ang)
        sin = jnp.sin(ang) * sgn_ref[...]

        def rope(r):
            return r * cos + pltpu.roll(r, LANE // 2, 1) * sin

        q = _dot(cq, wq_ref[...])
        kv = _dot(ckv, wkv_ref[...])
        yield
        qg = qg_ref[...]
        kg = kg_ref[...]
        kr_ss = jnp.sum(k_rope * k_rope, axis=-1, keepdims=True)
        kr = rope(k_rope * kg[:, LANE:])
        for hh in range(MLA_HEADS):
            qh = q[:, hh * QK_PAD:(hh + 1) * QK_PAD]
            rq = lax.rsqrt(jnp.sum(qh * qh, axis=-1, keepdims=True) * (1.0 / QK_DIM) + EPS) * q_scale
            qh = qh * rq * qg
            q_ref[hh, rows, :LANE] = _bf(qh[:, :LANE])
            q_ref[hh, rows, LANE:] = _bf(rope(qh[:, LANE:]))
            kn = kv[:, hh * LANE:(hh + 1) * LANE]
            rk = lax.rsqrt((jnp.sum(kn * kn, axis=-1, keepdims=True) + kr_ss) * (1.0 / QK_DIM) + EPS)
            k_ref[hh, rows, :LANE] = _bf(kn * rk * kg[:, :LANE])
            k_ref[hh, rows, LANE:] = _bf(kr * rk)
            v_ref[hh, :, rows] = _bf(kv[:, (MLA_HEADS + hh) * LANE:(MLA_HEADS + hh + 1) * LANE].T)
            if hh % 4 == 3:
                yield
        mem_ref[rows, :] = _bf(_mem_attention(z[:, o2:o3], kn_ref, mv_ref, mqg_ref[...]))

    _run_staggered([stages(rows) for rows in _sub_rows(x_ref.shape[0])])


def _attn_kernel(q_ref, k_ref, vt_ref, o_ref, m_ref, acc_ref):
    i = pl.program_id(2)
    heads, tq = q_ref.shape[0], q_ref.shape[1]
    m_ref[...] = jnp.full_like(m_ref, NEG_BIG)
    acc_ref[...] = jnp.zeros_like(acc_ref)

    qw = tq // ATTN_QUERY_SPLIT

    def scores(hh, j, part, masked):
        keys = (part + 1) * qw if masked else tq
        start = pl.multiple_of(j * tq, tq)
        s = lax.dot_general(k_ref[hh, pl.ds(start, keys), :], q_ref[hh, part * qw:(part + 1) * qw, :], _NT,
                            preferred_element_type=jnp.float32)
        if masked:
            key = lax.broadcasted_iota(jnp.int32, (keys, qw), 0)
            qry = lax.broadcasted_iota(jnp.int32, (keys, qw), 1) + part * qw
            s = jnp.where(key <= qry, s, NEG_BIG)
        return s

    def update(hh, j, part, s):
        keys = s.shape[0]
        lanes = slice(part * qw, (part + 1) * qw)
        m = m_ref[hh, :, lanes]
        m_new = jnp.maximum(m, jnp.max(s, axis=0, keepdims=True))
        alpha = jnp.exp2(m - m_new)
        p = _bf(jnp.exp2(s - m_new))
        m_ref[hh, :, lanes] = m_new
        vt1 = jnp.concatenate([vt_ref[hh, j, :, :keys], jnp.ones((ATTN_SUM_ROWS, keys), jnp.bfloat16)], axis=0)
        acc_ref[hh, :, lanes] = alpha * acc_ref[hh, :, lanes] + _dot(vt1, p)

    def run(items, masked):
        ss = {}
        ahead = ATTN_LOOKAHEAD
        for t in range(len(items) + ahead):
            if t < len(items):
                ss[t] = scores(*items[t], masked)
            if t >= ahead:
                update(*items[t - ahead], ss.pop(t - ahead))

    def full_tiles(first, count):
        run([(hh, first + u, part) for u in range(count) for hh in range(heads)
             for part in range(ATTN_QUERY_SPLIT)], False)

    unroll = ATTN_KEY_TILES_PER_TRIP

    def body(jj, c):
        full_tiles(unroll * jj, unroll)
        return c

    lax.fori_loop(0, i // unroll, body, 0)
    piece = unroll // 2
    while piece >= 1:
        @pl.when((i & piece) != 0)
        def _():
            full_tiles((i // (2 * piece)) * (2 * piece), piece)
        piece //= 2

    run([(hh, i, part) for hh in range(heads) for part in range(ATTN_QUERY_SPLIT)], True)
    for hh in range(heads):
        acc = acc_ref[hh]
        o_ref[:, hh * V_DIM:(hh + 1) * V_DIM] = _bf((acc[:V_DIM] / acc[V_DIM:V_DIM + 1]).T)


def _layer1_out_kernel(x_ref, o_ref, mem_ref, wout_ref, g2_ref, wr_ref, br_ref,
                       xo_ref, xn_ref, slab_ref, slabt_ref, cnt_ref, carry_ref):
    d = o_ref.shape[1]
    _reset_carry(carry_ref)

    def stages(rows):
        xo = (x_ref[rows, :] + _dot(o_ref[rows, :], wout_ref[:d, :])
              + _dot(mem_ref[rows, :], wout_ref[d:, :]))
        xo_ref[rows, :] = xo
        yield
        yield from _route_stages(xo, rows, g2_ref, wr_ref, br_ref, carry_ref,
                                 xn_ref, slab_ref, slabt_ref, cnt_ref)

    _run_staggered([stages(rows) for rows in _sub_rows(x_ref.shape[0])])


def _sc_mesh():
    return plsc.VectorSubcoreMesh(core_axis_name="c", subcore_axis_name="s")


def _sc_worker_base(rows_per_worker):
    return (lax.axis_index("c") * SC_SUBCORES + lax.axis_index("s")) * rows_per_worker


def _sc_scatter_rows(x, idx0, idx1, p_rows):
    n, w = x.shape
    per = n // SC_WORKERS
    chunks = SC_INDEX_GROUP // SC_CHUNK
    assert n % SC_WORKERS == 0 and per % SC_INDEX_GROUP == 0

    @pl.kernel(out_type=jax.ShapeDtypeStruct((p_rows, w), x.dtype), mesh=_sc_mesh(),
               scratch_types=[pltpu.VMEM((1, SC_INDEX_GROUP), jnp.int32),
                              pltpu.VMEM((1, SC_INDEX_GROUP), jnp.int32),
                              pltpu.VMEM((SC_CHUNK, w), x.dtype), pltpu.VMEM((SC_CHUNK, w), x.dtype),
                              pltpu.SemaphoreType.DMA, pltpu.SemaphoreType.DMA],
               name="moe_dispatch_sc")
    def scatter(x_hbm, i0_hbm, i1_hbm, o_hbm, i0_v, i1_v, buf_a, buf_b, sem_a, sem_b):
        base = _sc_worker_base(per)

        @pl.loop(0, per // SC_INDEX_GROUP)
        def _(g):
            off = pl.multiple_of(base + g * SC_INDEX_GROUP, SC_INDEX_GROUP)
            pltpu.sync_copy(i0_hbm.at[:, pl.ds(off, SC_INDEX_GROUP)], i0_v)
            pltpu.sync_copy(i1_hbm.at[:, pl.ds(off, SC_INDEX_GROUP)], i1_v)
            pending = []
            for c in range(chunks):
                buf, sem = ((buf_a, sem_a), (buf_b, sem_b))[c % 2]
                if c >= 2:
                    for cp in pending[c - 2]:
                        cp.wait()
                pltpu.sync_copy(x_hbm.at[pl.ds(off + c * SC_CHUNK, SC_CHUNK)], buf)
                sl = pl.ds(c * SC_CHUNK, SC_CHUNK)
                pending.append((pltpu.async_copy(buf, o_hbm.at[i0_v.at[0, sl]], sem),
                                pltpu.async_copy(buf, o_hbm.at[i1_v.at[0, sl]], sem)))
            for cps in pending[max(chunks - 2, 0):]:
                for cp in cps:
                    cp.wait()

    return scatter(x, idx0.reshape(1, n), idx1.reshape(1, n))


def _sc_gather_rows(table, idx):
    m = idx.shape[0]
    w = table.shape[1]
    per = m // SC_WORKERS
    chunks = SC_INDEX_GROUP // SC_CHUNK
    assert m % SC_WORKERS == 0 and per % SC_INDEX_GROUP == 0

    @pl.kernel(out_type=jax.ShapeDtypeStruct((m, w), table.dtype), mesh=_sc_mesh(),
               scratch_types=[pltpu.VMEM((1, SC_INDEX_GROUP), jnp.int32),
                              pltpu.VMEM((SC_CHUNK, w), table.dtype), pltpu.VMEM((SC_CHUNK, w), table.dtype),
                              pltpu.SemaphoreType.DMA, pltpu.SemaphoreType.DMA],
               name="moe_combine_sc")
    def gather(t_hbm, i_hbm, o_hbm, i_v, buf_a, buf_b, sem_a, sem_b):
        base = _sc_worker_base(per)

        @pl.loop(0, per // SC_INDEX_GROUP)
        def _(g):
            off = pl.multiple_of(base + g * SC_INDEX_GROUP, SC_INDEX_GROUP)
            pltpu.sync_copy(i_hbm.at[:, pl.ds(off, SC_INDEX_GROUP)], i_v)
            pending = []
            for c in range(chunks):
                buf, sem = ((buf_a, sem_a), (buf_b, sem_b))[c % 2]
                if c >= 2:
                    pending[c - 2].wait()
                pltpu.sync_copy(t_hbm.at[i_v.at[0, pl.ds(c * SC_CHUNK, SC_CHUNK)]], buf)
                pending.append(pltpu.async_copy(buf, o_hbm.at[pl.ds(off + c * SC_CHUNK, SC_CHUNK)], sem))
            for cp in pending[max(chunks - 2, 0):]:
                cp.wait()

    return gather(table, idx.reshape(1, m))


def _ffn_kernel(exp_ref, rows_ref, xs_ref, *refs):
    k = ROW_TILES_PER_STEP
    w_refs = [refs[3 * s:3 * s + 3] for s in range(k)]
    ys_ref = refs[3 * k]
    w_bf = [refs[3 * k + 1 + 3 * s:3 * k + 4 + 3 * s] for s in range(k)]
    j = pl.program_id(0)
    steps = pl.num_programs(0)

    for s in range(k):
        tile = s * steps + j

        @pl.when((j == 0) | (exp_ref[tile] != exp_ref[jnp.maximum(tile - 1, 0)]))
        def _():
            for src, dst in zip(w_refs[s], w_bf[s]):
                dst[...] = _bf(src[...])

    def stages(s):
        packed = xs_ref[s]
        live = lax.broadcasted_iota(jnp.int32, packed.shape, 0) < rows_ref[s * steps + j]
        x = _bf(_unpack_rows(jnp.where(live, packed, jnp.uint32(0))))
        wg, wu, wd = w_bf[s]
        g = _dot(x, wg[...])
        u = _dot(x, wu[...])
        yield
        act = _bf(g * jax.nn.sigmoid(g) * u)
        yield
        ys_ref[s] = _pack_rows(_dot(act, wd[...]))

    _run_staggered([stages(s) for s in range(k)])


def _combined(x_ref, slab_ref, y0_ref, y1_ref, rows):
    slab = slab_ref[rows, :]
    return (x_ref[rows, :] + slab[:, 2:3] * _unpack_rows(y0_ref[rows, :])
            + slab[:, 3:4] * _unpack_rows(y1_ref[rows, :]))


def _combine_kernel(x_ref, slab_ref, y0_ref, y1_ref, *refs):
    refs[-1][...] = _combined(x_ref, slab_ref, y0_ref, y1_ref, slice(None))


def _moe(layer, xn, slabt, counts, w_gate, w_up, w_down):
    n, wp = xn.shape
    d = w_gate.shape[-2]
    rt = ROW_TILE
    k = ROW_TILES_PER_STEP
    p_max = 2 * n + N_EXPERTS * rt
    n_tiles = p_max // rt
    assert n_tiles % k == 0

    cnt = counts[0, :N_EXPERTS].astype(jnp.int32)
    padded = ((cnt + rt - 1) // rt) * rt
    off = jnp.concatenate([jnp.zeros((1,), jnp.int32), jnp.cumsum(padded)]).astype(jnp.int32)
    experts = jnp.arange(N_EXPERTS, dtype=jnp.int32)[:, None]

    def position(e_row, r_row):
        e = e_row.astype(jnp.int32)[None, :]
        return jnp.sum(jnp.where(e == experts, off[:N_EXPERTS, None], 0), axis=0) + r_row.astype(jnp.int32)

    pos0 = position(slabt[0], slabt[4])
    pos1 = position(slabt[1], slabt[5])
    tile_start = jnp.arange(n_tiles, dtype=jnp.int32) * rt
    tile_exp = jnp.minimum(jnp.sum(off[1:][None, :] <= tile_start[:, None], axis=1),
                           N_EXPERTS - 1).astype(jnp.int32)
    tile_rows = jnp.clip(off[tile_exp] + cnt[tile_exp] - tile_start, 0, rt).astype(jnp.int32)

    xs = _sc_scatter_rows(xn, pos0, pos1, p_max)

    f = w_gate.shape[-1]
    steps = n_tiles // k
    w_specs = []
    for s in range(k):
        pick = lambda j, ex, rw, s=s: (layer, ex[s * steps + j], 0, 0)
        w_specs += [pl.BlockSpec((None, None, d, f), pick), pl.BlockSpec((None, None, d, f), pick),
                    pl.BlockSpec((None, None, f, d), pick)]
    bf_scratch = [pltpu.VMEM((d, f), jnp.bfloat16), pltpu.VMEM((d, f), jnp.bfloat16),
                  pltpu.VMEM((f, d), jnp.bfloat16)] * k
    slot_rows = pl.BlockSpec((k, rt, wp), lambda j, ex, rw: (0, j, 0))
    ys = pl.pallas_call(
        _ffn_kernel,
        grid_spec=pltpu.PrefetchScalarGridSpec(
            num_scalar_prefetch=2, grid=(steps,),
            in_specs=[slot_rows] + w_specs,
            out_specs=slot_rows,
            scratch_shapes=bf_scratch),
        out_shape=jax.ShapeDtypeStruct((k, steps * rt, wp), jnp.uint32),
        compiler_params=_params(("arbitrary",)),
        name="moe_ffn",
    )(tile_exp, tile_rows, xs.reshape(k, steps * rt, wp), *([w_gate, w_up, w_down] * k))

    return _sc_gather_rows(ys.reshape(p_max, wp), jnp.concatenate([pos0, pos1]))


def _picked_specs(n, wp):
    nb = n // TOKEN_TILE
    return [pl.BlockSpec((TOKEN_TILE, wp), lambda i: (i, 0)),
            pl.BlockSpec((TOKEN_TILE, wp), lambda i: (i + nb, 0))]


def _router_weights(w_group, b_group, w_expert, b_expert):
    d = w_group.shape[0]
    pad = LANE - N_GROUPS - N_EXPERTS
    wr = jnp.concatenate([w_group, w_expert, jnp.zeros((d, pad), w_group.dtype)], axis=1)
    br = jnp.concatenate([b_group, b_expert, jnp.zeros((pad,), b_group.dtype)])
    return _bf(wr), br.reshape(1, LANE).astype(jnp.float32)


def _rope_lanes(vec_half):
    z = jnp.zeros_like(vec_half)
    return jnp.concatenate([vec_half, z, vec_half, z], axis=-1)


def _pad_rope_cols(w):
    z = jnp.zeros(w.shape[:-1] + (ROPE_HALF,), w.dtype)
    return jnp.concatenate([w[..., :ROPE_HALF], z, w[..., ROPE_HALF:], z], axis=-1)


def _trunk(out_prev, chunk, x_all, mem, positions, mem_norm_g, w_mem_kv, mem_qn_g, mem_kn_g, norm1_g,
           norm2_g, a_w_in, a_ln_g, a_ln_b, a_w_s, a_b_s, a_w_out, b_w_in, b_q_norm_g, b_kv_norm_g, b_w_q_up,
           b_w_kv_up, b_qn_g, b_kn_g, b_w_out, moe_w_group, moe_b_group, moe_w_expert, moe_b_expert,
           moe_w_gate, moe_w_up, moe_w_down):
    n_total, d = x_all.shape
    b, s = positions.shape
    m = mem.shape[1]
    n = b * s
    depth = norm1_g.shape[0]
    tt = TOKEN_TILE
    ta = ATTN_TILE
    tiles_per_batch = s // tt
    assert depth == 2 and s % tt == 0 and tt % SUB_TILE == 0 and d == A_GROUPS * LANE
    assert s % ta == 0 and ta % tt == 0
    f32 = jnp.float32
    row = lambda v: v.reshape(1, -1).astype(f32)

    kn_all, mem_v = pl.pallas_call(
        _memkv_kernel,
        grid=(b,),
        in_specs=[pl.BlockSpec((m, d), lambda i: (i, 0)), _const_spec((1, d)),
                  _const_spec((d, 2 * MEM_W)), _const_spec((depth, 1, MEM_HEAD_DIM))],
        out_specs=[pl.BlockSpec((depth, m, MEM_W), lambda i: (0, i, 0)),
                   pl.BlockSpec((m, MEM_W), lambda i: (i, 0))],
        out_shape=[jax.ShapeDtypeStruct((depth, b * m, MEM_W), jnp.bfloat16),
                   jax.ShapeDtypeStruct((b * m, MEM_W), jnp.bfloat16)],
        compiler_params=_params(("arbitrary",)),
        name="mem_kv",
    )(mem.reshape(b * m, d), row(mem_norm_g), _bf(w_mem_kv), mem_kn_g.reshape(depth, 1, MEM_HEAD_DIM))

    tok = lambda width: pl.BlockSpec((tt, width), lambda i: (i, 0))
    kn_spec = lambda layer: pl.BlockSpec((None, m, MEM_W), lambda i: (layer, i // tiles_per_batch, 0))
    mv_spec = pl.BlockSpec((m, MEM_W), lambda i: (i // tiles_per_batch, 0))
    route_out_specs = [tok(d), tok(d // 2), tok(LANE), pl.BlockSpec((SUBLANE, tt), lambda i: (0, i)),
                       pl.BlockSpec((1, LANE), lambda i: (0, 0))]
    route_out_shape = [jax.ShapeDtypeStruct((n, d), f32), jax.ShapeDtypeStruct((n, d // 2), jnp.uint32),
                       jax.ShapeDtypeStruct((n, LANE), f32), jax.ShapeDtypeStruct((SUBLANE, n), f32),
                       jax.ShapeDtypeStruct((1, LANE), f32)]
    route_scratch = [pltpu.VMEM((1, LANE), f32)]

    first_block = chunk * (n // tt)
    group_tok = pl.BlockSpec((tt, d), lambda i: (i + first_block, 0))

    wr0, br0 = _router_weights(moe_w_group[0], moe_b_group[0], moe_w_expert[0], moe_b_expert[0])
    a_in = a_w_in.shape[-1]
    bias_s = jnp.repeat(a_b_s[0].T, LANE, axis=1).astype(f32)
    x2, xn, slab, slabt, counts = pl.pallas_call(
        _layer0_kernel,
        grid=(n // tt,),
        in_specs=[group_tok, _const_spec((1, d)), _const_spec((d, a_in)), _const_spec((1, d)),
                  _const_spec((1, d)), _const_spec((A_GROUPS, CHUNK, CHUNK)), _const_spec((CHUNK, d)),
                  kn_spec(0), mv_spec, _const_spec((1, MEM_HEAD_DIM)),
                  _const_spec((d + MEM_W, d)), _const_spec((1, d)), _const_spec((d, LANE)),
                  _const_spec((1, LANE))],
        out_specs=route_out_specs,
        out_shape=route_out_shape,
        scratch_shapes=route_scratch,
        compiler_params=_params(("arbitrary",)),
        name="layer0_mixer",
    )(x_all, row(norm1_g[0]), _bf(a_w_in[0]), row(a_ln_g[0]), row(a_ln_b[0]), _bf(a_w_s[0]), bias_s,
      kn_all, mem_v, row(mem_qn_g[0]), _bf(a_w_out[0]), row(norm2_g[0]), wr0, br0)
    picked = _moe(0, xn, slabt, counts, moe_w_gate, moe_w_up, moe_w_down)

    hq = MLA_HEADS
    o1, o2, o3 = Q_LORA, Q_LORA + KV_LORA, Q_LORA + KV_LORA + ROPE_DIM
    w_in = b_w_in[0]
    w_in_p = jnp.concatenate([w_in[:, :o2], w_in[:, o3:], _pad_rope_cols(w_in[:, o2:o3])], axis=1)
    wq = b_w_q_up[0].reshape(Q_LORA, hq, QK_DIM)
    wq_p = jnp.concatenate([wq[..., :NOPE_DIM], _pad_rope_cols(wq[..., NOPE_DIM:])], axis=-1)
    wq_p = wq_p.reshape(Q_LORA, hq * QK_PAD)
    wkv = b_w_kv_up[0].reshape(KV_LORA, hq, NOPE_DIM + V_DIM)
    wkv_p = jnp.concatenate([wkv[..., :NOPE_DIM].reshape(KV_LORA, hq * NOPE_DIM),
                             wkv[..., NOPE_DIM:].reshape(KV_LORA, hq * V_DIM)], axis=1)
    pad_gain = lambda g: jnp.concatenate([g[:NOPE_DIM], _pad_rope_cols(g[NOPE_DIM:])]).reshape(1, QK_PAD)
    half = jnp.arange(ROPE_HALF, dtype=f32)
    inv = ROPE_BASE ** (-(half * 2.0 / ROPE_DIM))
    inv_l = _rope_lanes(inv).reshape(1, LANE)
    sgn_l = jnp.concatenate([-jnp.ones((2 * ROPE_HALF,), f32), jnp.ones((2 * ROPE_HALF,), f32)]).reshape(1, LANE)

    in_w = w_in_p.shape[1]
    head_spec = lambda width: pl.BlockSpec((None, hq, tt, width),
                                           lambda i: (i // tiles_per_batch, 0, i % tiles_per_batch, 0))
    per_ta = ta // tt

    def vt_index(i):
        t = i % tiles_per_batch
        return (i // tiles_per_batch, 0, t // per_ta, 0, t % per_ta)

    vt_spec = pl.BlockSpec((None, hq, None, V_DIM, tt), vt_index)
    x2, q, k, vt, mem_o = pl.pallas_call(
        _layer1_proj_kernel,
        grid=(n // tt,),
        in_specs=[tok(d), tok(LANE), *_picked_specs(n, d // 2),
                  tok(1), _const_spec((1, d)), _const_spec((d, in_w)), _const_spec((1, Q_LORA)),
                  _const_spec((1, KV_LORA)), _const_spec((Q_LORA, hq * QK_PAD)),
                  _const_spec((KV_LORA, hq * (NOPE_DIM + V_DIM))), _const_spec((1, QK_PAD)),
                  _const_spec((1, QK_PAD)), _const_spec((1, LANE)), _const_spec((1, LANE)),
                  kn_spec(1), mv_spec, _const_spec((1, MEM_HEAD_DIM))],
        out_specs=[tok(d), head_spec(QK_PAD), head_spec(QK_PAD), vt_spec, tok(MEM_W)],
        out_shape=[jax.ShapeDtypeStruct((n, d), f32),
                   jax.ShapeDtypeStruct((b, hq, s, QK_PAD), jnp.bfloat16),
                   jax.ShapeDtypeStruct((b, hq, s, QK_PAD), jnp.bfloat16),
                   jax.ShapeDtypeStruct((b, hq, s // ta, V_DIM, ta), jnp.bfloat16),
                   jax.ShapeDtypeStruct((n, MEM_W), jnp.bfloat16)],
        compiler_params=_params(("arbitrary",)),
        name="layer1_proj",
    )(x2, slab, picked, picked,
      positions.reshape(n, 1), row(norm1_g[1]), _bf(w_in_p), row(b_q_norm_g[0]), row(b_kv_norm_g[0]),
      _bf(wq_p), _bf(wkv_p), pad_gain(b_qn_g[0]).astype(f32), pad_gain(b_kn_g[0]).astype(f32),
      inv_l, sgn_l, kn_all, mem_v, row(mem_qn_g[1]))

    qb = s // ta
    hp = ATTN_HEADS_PER_STEP
    attn = pl.pallas_call(
        _attn_kernel,
        grid=(b, hq // hp, qb),
        in_specs=[pl.BlockSpec((None, hp, ta, QK_PAD), lambda bi, hi, i: (bi, hi, i, 0)),
                  pl.BlockSpec((None, hp, s, QK_PAD), lambda bi, hi, i: (bi, hi, 0, 0)),
                  pl.BlockSpec((None, hp, qb, V_DIM, ta), lambda bi, hi, i: (bi, hi, 0, 0, 0))],
        out_specs=pl.BlockSpec((ta, hp * V_DIM), lambda bi, hi, i: (bi * qb + i, hi)),
        out_shape=jax.ShapeDtypeStruct((n, hq * V_DIM), jnp.bfloat16),
        scratch_shapes=[pltpu.VMEM((hp, 1, ta), f32), pltpu.VMEM((hp, V_DIM + ATTN_SUM_ROWS, ta), f32)],
        compiler_params=_params(("arbitrary", "arbitrary", "arbitrary")),
        name="causal_attention",
    )(q, k, vt)

    wr1, br1 = _router_weights(moe_w_group[1], moe_b_group[1], moe_w_expert[1], moe_b_expert[1])
    x2, xn, slab, slabt, counts = pl.pallas_call(
        _layer1_out_kernel,
        grid=(n // tt,),
        in_specs=[tok(d), tok(hq * V_DIM), tok(MEM_W), _const_spec((hq * V_DIM + MEM_W, d)),
                  _const_spec((1, d)), _const_spec((d, LANE)), _const_spec((1, LANE))],
        out_specs=route_out_specs,
        out_shape=route_out_shape,
        scratch_shapes=route_scratch,
        compiler_params=_params(("arbitrary",)),
        name="layer1_out",
    )(x2, attn, mem_o, _bf(b_w_out[0]), row(norm2_g[1]), wr1, br1)
    picked = _moe(1, xn, slabt, counts, moe_w_gate, moe_w_up, moe_w_down)
    prev = [] if out_prev is None else [out_prev]
    return pl.pallas_call(
        _combine_kernel,
        grid=(n // tt,),
        in_specs=[tok(d), tok(LANE), *_picked_specs(n, d // 2)] + [pl.BlockSpec(memory_space=pl.ANY)] * len(prev),
        out_specs=group_tok,
        out_shape=jax.ShapeDtypeStruct((n_total, d), f32),
        input_output_aliases={4: 0} if prev else {},
        compiler_params=_params(("arbitrary",)),
        name="moe_combine",
    )(x2, slab, picked, picked, *prev)


def kernel(x, mem, positions, mem_norm_g, w_mem_kv, mem_qn_g, mem_kn_g, norm1_g, norm2_g, a_w_in, a_ln_g, a_ln_b, a_w_s, a_b_s, a_w_out, b_w_in, b_q_norm_g, b_kv_norm_g, b_w_q_up, b_w_kv_up, b_qn_g, b_kn_g, b_w_out, moe_w_group, moe_b_group, moe_w_expert, moe_b_expert, moe_w_gate, moe_w_up, moe_w_down):
    params = (mem_norm_g, w_mem_kv, mem_qn_g, mem_kn_g, norm1_g, norm2_g, a_w_in, a_ln_g, a_ln_b, a_w_s,
              a_b_s, a_w_out, b_w_in, b_q_norm_g, b_kv_norm_g, b_w_q_up, b_w_kv_up, b_qn_g, b_kn_g, b_w_out,
              moe_w_group, moe_b_group, moe_w_expert, moe_b_expert, moe_w_gate, moe_w_up, moe_w_down)
    b, s, d = x.shape
    assert b % BATCH_GROUPS == 0
    g = b // BATCH_GROUPS
    x_all = x.reshape(b * s, d)
    out = None
    for c in range(BATCH_GROUPS):
        rows = slice(c * g, (c + 1) * g)
        out = _trunk(out, c, x_all, mem[rows], positions[rows], *params)
    return out.reshape(b, s, d)
```

```python
import jax
import jax.numpy as jnp
from jax import lax
from jax.experimental import pallas as pl
from jax.experimental.pallas import tpu as pltpu
from jax.experimental.pallas import tpu_sc as plsc

EPS = 1e-6
LANE = 128
SUBLANE = 8
MEM_HEADS = 4
MEM_HEAD_DIM = 128
MEM_W = MEM_HEADS * MEM_HEAD_DIM
CHUNK = 128
A_GROUPS = 8
MLA_HEADS = 8
Q_LORA = 512
KV_LORA = 256
NOPE_DIM = 128
ROPE_DIM = 64
ROPE_HALF = ROPE_DIM // 2
V_DIM = 128
QK_DIM = NOPE_DIM + ROPE_DIM
QK_PAD = 2 * LANE
ROPE_BASE = 10000.0
N_GROUPS = 4
EXPERTS_PER_GROUP = 8
N_EXPERTS = N_GROUPS * EXPERTS_PER_GROUP
ROUTE_ROWS = 40
EXPERT_FF = 256
LOG2E = 1.4426950408889634

BATCH_GROUPS = 2
TOKEN_TILE = 512
SUB_TILE = 256
ROW_TILE = 512
ATTN_TILE = 512
ATTN_HEADS_PER_STEP = 2
ATTN_KEY_TILES_PER_TRIP = 4
ATTN_QUERY_SPLIT = 2
ATTN_LOOKAHEAD = 3
ATTN_SUM_ROWS = 16
VMEM_LIMIT = 56 * 1024 * 1024
NEG_BIG = -1e30

SC_CORES = 2
SC_SUBCORES = 16
SC_WORKERS = SC_CORES * SC_SUBCORES
SC_INDEX_GROUP = 128
SC_CHUNK = 64

_NT = (((1,), (1,)), ((), ()))


def _const_spec(shape):
    nd = len(shape)
    return pl.BlockSpec(shape, lambda *_: (0,) * nd, pipeline_mode=pl.Buffered(1))


def _params(sem):
    return pltpu.CompilerParams(dimension_semantics=sem, vmem_limit_bytes=VMEM_LIMIT)


def _run_staggered(gens):
    waiting = list(gens)
    active = []
    while waiting or active:
        if waiting:
            active.append(waiting.pop(0))
        for g in list(active):
            try:
                next(g)
            except StopIteration:
                active.remove(g)


def _sub_rows(t):
    return [pl.ds(k * SUB_TILE, SUB_TILE) for k in range(t // SUB_TILE)]


def _rms(x, g):
    return x * lax.rsqrt(jnp.mean(x * x, axis=-1, keepdims=True) + EPS) * g


def _gelu(x):
    return 0.5 * x * (1.0 + lax.erf(x * (2.0 ** -0.5)))


def _bf(x):
    return x.astype(jnp.bfloat16)


def _dot(a, b):
    return jnp.dot(a, b, preferred_element_type=jnp.float32)


def _memkv_kernel(mem_ref, g_ref, w_ref, kng_ref, kn_ref, v_ref):
    h = _bf(_rms(mem_ref[...], g_ref[...]))
    kv = _dot(h, w_ref[...])
    v_ref[...] = _bf(kv[:, MEM_W:])
    for layer in range(kn_ref.shape[0]):
        g = kng_ref[layer]
        for hh in range(MEM_HEADS):
            k = kv[:, hh * LANE:(hh + 1) * LANE]
            kn_ref[layer, :, hh * LANE:(hh + 1) * LANE] = _bf(_rms(k, g))


def _mem_attention(qm, kn_ref, v_ref, qg):
    outs = []
    for hh in range(MEM_HEADS):
        sl = slice(hh * LANE, (hh + 1) * LANE)
        q = _rms(qm[:, sl], qg) * (MEM_HEAD_DIM ** -0.5)
        s = lax.dot_general(_bf(q), kn_ref[:, sl], _NT, preferred_element_type=jnp.float32)
        p = jnp.exp(s - jnp.max(s, axis=-1, keepdims=True))
        l = jnp.sum(p, axis=-1, keepdims=True)
        outs.append(_dot(_bf(p), v_ref[:, sl]) / l)
    return jnp.concatenate(outs, axis=-1)


def _pack_rows(x):
    w = x.shape[1] // 2
    bits = lambda v: lax.bitcast_convert_type(_bf(v).astype(jnp.float32), jnp.uint32)
    return (bits(x[:, :w]) >> 16) | (bits(x[:, w:]) & jnp.uint32(0xFFFF0000))


def _unpack_rows(p):
    lo = lax.bitcast_convert_type(p << 16, jnp.float32)
    hi = lax.bitcast_convert_type(p & jnp.uint32(0xFFFF0000), jnp.float32)
    return jnp.concatenate([lo, hi], axis=-1)


def _route_stages(x, rows, g2_ref, wr_ref, br_ref, carry_ref, xn_ref, slab_ref, slabt_ref, cnt_ref):
    t = x.shape[0]
    xn = _rms(x, g2_ref[...])
    xn_ref[rows, :] = _pack_rows(xn)
    logits = _dot(_bf(xn), wr_ref[...]) + br_ref[...]
    yield
    lt = logits.T[:ROUTE_ROWS, :]
    row = lax.broadcasted_iota(jnp.int32, lt.shape, 0)

    def first_max(v):
        m = jnp.max(v, axis=0, keepdims=True)
        idx = jnp.min(jnp.where(v == m, row, ROUTE_ROWS), axis=0, keepdims=True)
        return m, idx

    lg = jnp.where(row < N_GROUPS, lt, NEG_BIG)
    gmax, gidx = first_max(lg)
    g_w = 1.0 / jnp.sum(jnp.exp(lg - gmax), axis=0, keepdims=True)

    eid = row - N_GROUPS
    in_grp = (eid >= 0) & (eid < N_EXPERTS) & ((eid >> 3) == gidx)
    le = jnp.where(in_grp, lt, NEG_BIG)
    m1, i1 = first_max(le)
    m2, i2 = first_max(jnp.where(row == i1, NEG_BIG, le))
    r = jnp.exp(m2 - m1)
    w1 = g_w / (1.0 + r)
    w2 = w1 * r
    e1 = i1 - N_GROUPS
    e2 = i2 - N_GROUPS
    yield

    expert = lax.broadcasted_iota(jnp.int32, (N_EXPERTS, t), 0)
    oh1 = expert == e1
    oh2 = expert == e2
    oh = jnp.where(oh1 | oh2, 1.0, 0.0)
    src = lax.broadcasted_iota(jnp.int32, (t, t), 0)
    dst = lax.broadcasted_iota(jnp.int32, (t, t), 1)
    earlier = jnp.where(src < dst, 1.0, 0.0).astype(jnp.bfloat16)
    before = _dot(_bf(oh), earlier) + carry_ref[...]
    r1 = jnp.sum(jnp.where(oh1, before, 0.0), axis=0, keepdims=True)
    r2 = jnp.sum(jnp.where(oh2, before, 0.0), axis=0, keepdims=True)
    carry_ref[...] += jnp.sum(oh, axis=1, keepdims=True)
    cnt_ref[...] = jnp.broadcast_to(carry_ref[...], cnt_ref.shape)

    table = jnp.concatenate([e1.astype(jnp.float32), e2.astype(jnp.float32), w1, w2, r1, r2,
                             jnp.zeros((LANE - 6, t), jnp.float32)], axis=0)
    slabt_ref[:, rows] = table[:SUBLANE, :]
    slab_ref[rows, :] = table.T


def _reset_carry(carry_ref):
    @pl.when(pl.program_id(0) == 0)
    def _():
        carry_ref[...] = jnp.zeros_like(carry_ref)


def _layer0_kernel(x_ref, g1_ref, win_ref, lng_ref, lnb_ref, ws_ref, bs_ref, kn_ref, v_ref, qg_ref,
                   wout_ref, g2_ref, wr_ref, br_ref,
                   xo_ref, xn_ref, slab_ref, slabt_ref, cnt_ref, carry_ref):
    d = x_ref.shape[1]
    _reset_carry(carry_ref)
    row = lax.broadcasted_iota(jnp.int32, (CHUNK, CHUNK), 0)
    col = lax.broadcasted_iota(jnp.int32, (CHUNK, CHUNK), 1)
    causal = row >= col

    def stages(rows):
        x = x_ref[rows, :]
        t = x.shape[0]
        h = _bf(_rms(x, g1_ref[...]))
        z = _dot(h, win_ref[...])
        yield
        u = _gelu(z[:, :d])
        v = _gelu(z[:, d:2 * d])
        mu = jnp.mean(v, axis=-1, keepdims=True)
        vc = v - mu
        var = jnp.mean(vc * vc, axis=-1, keepdims=True)
        v = _bf(vc * lax.rsqrt(var + EPS) * lng_ref[...] + lnb_ref[...])
        yield
        chunks = []
        for c in range(t // CHUNK):
            cols = []
            for g in range(A_GROUPS):
                w = jnp.where(causal, ws_ref[g], jnp.zeros((), ws_ref.dtype))
                cols.append(_dot(w, v[c * CHUNK:(c + 1) * CHUNK, g * LANE:(g + 1) * LANE]))
            chunks.append(jnp.concatenate(cols, axis=-1) + bs_ref[...])
        mix = _bf(u * jnp.concatenate(chunks, axis=0))
        yield
        mem = _bf(_mem_attention(z[:, 2 * d:], kn_ref, v_ref, qg_ref[...]))
        yield
        xo = x + _dot(mix, wout_ref[:d, :]) + _dot(mem, wout_ref[d:, :])
        xo_ref[rows, :] = xo
        yield
        yield from _route_stages(xo, rows, g2_ref, wr_ref, br_ref, carry_ref,
                                 xn_ref, slab_ref, slabt_ref, cnt_ref)

    _run_staggered([stages(rows) for rows in _sub_rows(x_ref.shape[0])])


def _layer1_proj_kernel(x_ref, slab_ref, y0_ref, y1_ref, pos_ref, g1_ref, win_ref, qng_ref, kvng_ref,
                        wq_ref, wkv_ref, qg_ref, kg_ref, inv_ref, sgn_ref, kn_ref, mv_ref, mqg_ref,
                        xo_ref, q_ref, k_ref, v_ref, mem_ref):
    o1 = Q_LORA
    o2 = o1 + KV_LORA
    o3 = o2 + MEM_W
    q_scale = (QK_DIM ** -0.5) * LOG2E

    def stages(rows):
        x = _combined(x_ref, slab_ref, y0_ref, y1_ref, rows)
        xo_ref[rows, :] = x
        h = _bf(_rms(x, g1_ref[...]))
        z = _dot(h, win_ref[...])
        yield
        cq = _bf(_rms(z[:, :o1], qng_ref[...]))
        ckv = _bf(_rms(z[:, o1:o2], kvng_ref[...]))
        k_rope = z[:, o3:o3 + LANE]
        ang = pos_ref[rows, :].astype(jnp.float32) * inv_ref[...]
        cos = jnp.cos(ang)
        sin = jnp.sin(ang) * sgn_ref[...]

        def rope(r):
            return r * cos + pltpu.roll(r, LANE // 2, 1) * sin

        q = _dot(cq, wq_ref[...])
        kv = _dot(ckv, wkv_ref[...])
        yield
        qg = qg_ref[...]
        kg = kg_ref[...]
        kr_ss = jnp.sum(k_rope * k_rope, axis=-1, keepdims=True)
        kr = rope(k_rope * kg[:, LANE:])
        for hh in range(MLA_HEADS):
            qh = q[:, hh * QK_PAD:(hh + 1) * QK_PAD]
            rq = lax.rsqrt(jnp.sum(qh * qh, axis=-1, keepdims=True) * (1.0 / QK_DIM) + EPS) * q_scale
            qh = qh * rq * qg
            q_ref[hh, rows, :LANE] = _bf(qh[:, :LANE])
            q_ref[hh, rows, LANE:] = _bf(rope(qh[:, LANE:]))
            kn = kv[:, hh * LANE:(hh + 1) * LANE]
            rk = lax.rsqrt((jnp.sum(kn * kn, axis=-1, keepdims=True) + kr_ss) * (1.0 / QK_DIM) + EPS)
            k_ref[hh, rows, :LANE] = _bf(kn * rk * kg[:, :LANE])
            k_ref[hh, rows, LANE:] = _bf(kr * rk)
            v_ref[hh, :, rows] = _bf(kv[:, (MLA_HEADS + hh) * LANE:(MLA_HEADS + hh + 1) * LANE].T)
            if hh % 4 == 3:
                yield
        mem_ref[rows, :] = _bf(_mem_attention(z[:, o2:o3], kn_ref, mv_ref, mqg_ref[...]))

    _run_staggered([stages(rows) for rows in _sub_rows(x_ref.shape[0])])


def _attn_kernel(q_ref, k_ref, vt_ref, o_ref, m_ref, acc_ref):
    i = pl.program_id(2)
    heads, tq = q_ref.shape[0], q_ref.shape[1]
    m_ref[...] = jnp.full_like(m_ref, NEG_BIG)
    acc_ref[...] = jnp.zeros_like(acc_ref)

    qw = tq // ATTN_QUERY_SPLIT

    def scores(hh, j, part, masked):
        keys = (part + 1) * qw if masked else tq
        start = pl.multiple_of(j * tq, tq)
        s = lax.dot_general(k_ref[hh, pl.ds(start, keys), :], q_ref[hh, part * qw:(part + 1) * qw, :], _NT,
                            preferred_element_type=jnp.float32)
        if masked:
            key = lax.broadcasted_iota(jnp.int32, (keys, qw), 0)
            qry = lax.broadcasted_iota(jnp.int32, (keys, qw), 1) + part * qw
            s = jnp.where(key <= qry, s, NEG_BIG)
        return s

    def update(hh, j, part, s):
        keys = s.shape[0]
        lanes = slice(part * qw, (part + 1) * qw)
        m = m_ref[hh, :, lanes]
        m_new = jnp.maximum(m, jnp.max(s, axis=0, keepdims=True))
        alpha = jnp.exp2(m - m_new)
        p = _bf(jnp.exp2(s - m_new))
        m_ref[hh, :, lanes] = m_new
        vt1 = jnp.concatenate([vt_ref[hh, j, :, :keys], jnp.ones((ATTN_SUM_ROWS, keys), jnp.bfloat16)], axis=0)
        acc_ref[hh, :, lanes] = alpha * acc_ref[hh, :, lanes] + _dot(vt1, p)

    def run(items, masked):
        ss = {}
        ahead = ATTN_LOOKAHEAD
        for t in range(len(items) + ahead):
            if t < len(items):
                ss[t] = scores(*items[t], masked)
            if t >= ahead:
                update(*items[t - ahead], ss.pop(t - ahead))

    def full_tiles(first, count):
        run([(hh, first + u, part) for u in range(count) for hh in range(heads)
             for part in range(ATTN_QUERY_SPLIT)], False)

    unroll = ATTN_KEY_TILES_PER_TRIP

    def body(jj, c):
        full_tiles(unroll * jj, unroll)
        return c

    lax.fori_loop(0, i // unroll, body, 0)
    piece = unroll // 2
    while piece >= 1:
        @pl.when((i & piece) != 0)
        def _():
            full_tiles((i // (2 * piece)) * (2 * piece), piece)
        piece //= 2

    run([(hh, i, part) for hh in range(heads) for part in range(ATTN_QUERY_SPLIT)], True)
    for hh in range(heads):
        acc = acc_ref[hh]
        o_ref[:, hh * V_DIM:(hh + 1) * V_DIM] = _bf((acc[:V_DIM] / acc[V_DIM:V_DIM + 1]).T)


def _layer1_out_kernel(x_ref, o_ref, mem_ref, wout_ref, g2_ref, wr_ref, br_ref,
                       xo_ref, xn_ref, slab_ref, slabt_ref, cnt_ref, carry_ref):
    d = o_ref.shape[1]
    _reset_carry(carry_ref)

    def stages(rows):
        xo = (x_ref[rows, :] + _dot(o_ref[rows, :], wout_ref[:d, :])
              + _dot(mem_ref[rows, :], wout_ref[d:, :]))
        xo_ref[rows, :] = xo
        yield
        yield from _route_stages(xo, rows, g2_ref, wr_ref, br_ref, carry_ref,
                                 xn_ref, slab_ref, slabt_ref, cnt_ref)

    _run_staggered([stages(rows) for rows in _sub_rows(x_ref.shape[0])])


def _sc_mesh():
    return plsc.VectorSubcoreMesh(core_axis_name="c", subcore_axis_name="s")


def _sc_worker_base(rows_per_worker):
    return (lax.axis_index("c") * SC_SUBCORES + lax.axis_index("s")) * rows_per_worker


def _sc_scatter_rows(x, idx0, idx1, p_rows):
    n, w = x.shape
    per = n // SC_WORKERS
    chunks = SC_INDEX_GROUP // SC_CHUNK
    assert n % SC_WORKERS == 0 and per % SC_INDEX_GROUP == 0

    @pl.kernel(out_type=jax.ShapeDtypeStruct((p_rows, w), x.dtype), mesh=_sc_mesh(),
               scratch_types=[pltpu.VMEM((1, SC_INDEX_GROUP), jnp.int32),
                              pltpu.VMEM((1, SC_INDEX_GROUP), jnp.int32),
                              pltpu.VMEM((SC_CHUNK, w), x.dtype), pltpu.VMEM((SC_CHUNK, w), x.dtype),
                              pltpu.SemaphoreType.DMA, pltpu.SemaphoreType.DMA],
               name="moe_dispatch_sc")
    def scatter(x_hbm, i0_hbm, i1_hbm, o_hbm, i0_v, i1_v, buf_a, buf_b, sem_a, sem_b):
        base = _sc_worker_base(per)

        @pl.loop(0, per // SC_INDEX_GROUP)
        def _(g):
            off = pl.multiple_of(base + g * SC_INDEX_GROUP, SC_INDEX_GROUP)
            pltpu.sync_copy(i0_hbm.at[:, pl.ds(off, SC_INDEX_GROUP)], i0_v)
            pltpu.sync_copy(i1_hbm.at[:, pl.ds(off, SC_INDEX_GROUP)], i1_v)
            pending = []
            for c in range(chunks):
                buf, sem = ((buf_a, sem_a), (buf_b, sem_b))[c % 2]
                if c >= 2:
                    for cp in pending[c - 2]:
                        cp.wait()
                pltpu.sync_copy(x_hbm.at[pl.ds(off + c * SC_CHUNK, SC_CHUNK)], buf)
                sl = pl.ds(c * SC_CHUNK, SC_CHUNK)
                pending.append((pltpu.async_copy(buf, o_hbm.at[i0_v.at[0, sl]], sem),
                                pltpu.async_copy(buf, o_hbm.at[i1_v.at[0, sl]], sem)))
            for cps in pending[max(chunks - 2, 0):]:
                for cp in cps:
                    cp.wait()

    return scatter(x, idx0.reshape(1, n), idx1.reshape(1, n))


def _sc_gather_rows(table, idx):
    m = idx.shape[0]
    w = table.shape[1]
    per = m // SC_WORKERS
    chunks = SC_INDEX_GROUP // SC_CHUNK
    assert m % SC_WORKERS == 0 and per % SC_INDEX_GROUP == 0

    @pl.kernel(out_type=jax.ShapeDtypeStruct((m, w), table.dtype), mesh=_sc_mesh(),
               scratch_types=[pltpu.VMEM((1, SC_INDEX_GROUP), jnp.int32),
                              pltpu.VMEM((SC_CHUNK, w), table.dtype), pltpu.VMEM((SC_CHUNK, w), table.dtype),
                              pltpu.SemaphoreType.DMA, pltpu.SemaphoreType.DMA],
               name="moe_combine_sc")
    def gather(t_hbm, i_hbm, o_hbm, i_v, buf_a, buf_b, sem_a, sem_b):
        base = _sc_worker_base(per)

        @pl.loop(0, per // SC_INDEX_GROUP)
        def _(g):
            off = pl.multiple_of(base + g * SC_INDEX_GROUP, SC_INDEX_GROUP)
            pltpu.sync_copy(i_hbm.at[:, pl.ds(off, SC_INDEX_GROUP)], i_v)
            pending = []
            for c in range(chunks):
                buf, sem = ((buf_a, sem_a), (buf_b, sem_b))[c % 2]
                if c >= 2:
                    pending[c - 2].wait()
                pltpu.sync_copy(t_hbm.at[i_v.at[0, pl.ds(c * SC_CHUNK, SC_CHUNK)]], buf)
                pending.append(pltpu.async_copy(buf, o_hbm.at[pl.ds(off + c * SC_CHUNK, SC_CHUNK)], sem))
            for cp in pending[max(chunks - 2, 0):]:
                cp.wait()

    return gather(table, idx.reshape(1, m))


def _ffn_kernel(exp_ref, rows_ref, xs_ref, wg_ref, wu_ref, wd_ref, ys_ref, wg_bf, wu_bf, wd_bf):
    j = pl.program_id(0)

    @pl.when((j == 0) | (exp_ref[j] != exp_ref[jnp.maximum(j - 1, 0)]))
    def _():
        wg_bf[...] = _bf(wg_ref[...])
        wu_bf[...] = _bf(wu_ref[...])
        wd_bf[...] = _bf(wd_ref[...])

    def stages(k, rows):
        packed = xs_ref[rows, :]
        row_id = lax.broadcasted_iota(jnp.int32, packed.shape, 0) + k * SUB_TILE
        x = _bf(_unpack_rows(jnp.where(row_id < rows_ref[j], packed, jnp.uint32(0))))
        g = _dot(x, wg_bf[...])
        u = _dot(x, wu_bf[...])
        yield
        act = _bf(g * jax.nn.sigmoid(g) * u)
        yield
        ys_ref[rows, :] = _pack_rows(_dot(act, wd_bf[...]))

    _run_staggered([stages(k, rows) for k, rows in enumerate(_sub_rows(xs_ref.shape[0]))])


def _combined(x_ref, slab_ref, y0_ref, y1_ref, rows):
    slab = slab_ref[rows, :]
    return (x_ref[rows, :] + slab[:, 2:3] * _unpack_rows(y0_ref[rows, :])
            + slab[:, 3:4] * _unpack_rows(y1_ref[rows, :]))


def _combine_kernel(x_ref, slab_ref, y0_ref, y1_ref, *refs):
    refs[-1][...] = _combined(x_ref, slab_ref, y0_ref, y1_ref, slice(None))


def _moe(layer, xn, slabt, counts, w_gate, w_up, w_down):
    n, wp = xn.shape
    d = w_gate.shape[-2]
    rt = ROW_TILE
    p_max = 2 * n + N_EXPERTS * rt
    n_tiles = p_max // rt

    cnt = counts[:, 0].astype(jnp.int32)
    padded = ((cnt + rt - 1) // rt) * rt
    experts = jnp.arange(N_EXPERTS, dtype=jnp.int32)
    start = jnp.sum(jnp.where(experts[None, :] < experts[:, None], padded[None, :], 0), axis=1)
    end = start + padded

    def position(e_row, r_row):
        e = e_row.astype(jnp.int32)[None, :]
        return jnp.sum(jnp.where(e == experts[:, None], start[:, None], 0), axis=0) + r_row.astype(jnp.int32)

    pos0 = position(slabt[0], slabt[4])
    pos1 = position(slabt[1], slabt[5])
    tile_start = jnp.arange(n_tiles, dtype=jnp.int32) * rt
    owns = (start[None, :] <= tile_start[:, None]) & (tile_start[:, None] < end[None, :])
    past = tile_start >= end[N_EXPERTS - 1]
    tile_exp = jnp.where(past, N_EXPERTS - 1, jnp.sum(jnp.where(owns, experts[None, :], 0), axis=1))
    used = jnp.sum(jnp.where(owns, (start + cnt)[None, :], 0), axis=1)
    tile_rows = jnp.clip(used - tile_start, 0, rt).astype(jnp.int32)

    xs = _sc_scatter_rows(xn, pos0, pos1, p_max)

    f = w_gate.shape[-1]
    pick = lambda j, ex, rw: (layer, ex[j], 0, 0)
    rows_spec = pl.BlockSpec((rt, wp), lambda j, ex, rw: (j, 0))
    ys = pl.pallas_call(
        _ffn_kernel,
        grid_spec=pltpu.PrefetchScalarGridSpec(
            num_scalar_prefetch=2, grid=(n_tiles,),
            in_specs=[rows_spec, pl.BlockSpec((None, None, d, f), pick),
                      pl.BlockSpec((None, None, d, f), pick), pl.BlockSpec((None, None, f, d), pick)],
            out_specs=rows_spec,
            scratch_shapes=[pltpu.VMEM((d, f), jnp.bfloat16), pltpu.VMEM((d, f), jnp.bfloat16),
                            pltpu.VMEM((f, d), jnp.bfloat16)]),
        out_shape=jax.ShapeDtypeStruct((p_max, wp), jnp.uint32),
        compiler_params=_params(("arbitrary",)),
        name="moe_ffn",
    )(tile_exp.astype(jnp.int32), tile_rows, xs, w_gate, w_up, w_down)

    return _sc_gather_rows(ys, jnp.concatenate([pos0, pos1]))


def _picked_specs(n, wp):
    nb = n // TOKEN_TILE
    return [pl.BlockSpec((TOKEN_TILE, wp), lambda i: (i, 0)),
            pl.BlockSpec((TOKEN_TILE, wp), lambda i: (i + nb, 0))]


def _router_weights(w_group, b_group, w_expert, b_expert):
    d = w_group.shape[0]
    pad = LANE - N_GROUPS - N_EXPERTS
    wr = jnp.concatenate([w_group, w_expert, jnp.zeros((d, pad), w_group.dtype)], axis=1)
    br = jnp.concatenate([b_group, b_expert, jnp.zeros((pad,), b_group.dtype)])
    return _bf(wr), br.reshape(1, LANE).astype(jnp.float32)


def _rope_lanes(vec_half):
    z = jnp.zeros_like(vec_half)
    return jnp.concatenate([vec_half, z, vec_half, z], axis=-1)


def _pad_rope_cols(w):
    z = jnp.zeros(w.shape[:-1] + (ROPE_HALF,), w.dtype)
    return jnp.concatenate([w[..., :ROPE_HALF], z, w[..., ROPE_HALF:], z], axis=-1)


def _trunk(out_prev, chunk, x_all, mem, positions, mem_norm_g, w_mem_kv, mem_qn_g, mem_kn_g, norm1_g,
           norm2_g, a_w_in, a_ln_g, a_ln_b, a_w_s, a_b_s, a_w_out, b_w_in, b_q_norm_g, b_kv_norm_g, b_w_q_up,
           b_w_kv_up, b_qn_g, b_kn_g, b_w_out, moe_w_group, moe_b_group, moe_w_expert, moe_b_expert,
           moe_w_gate, moe_w_up, moe_w_down):
    n_total, d = x_all.shape
    b, s = positions.shape
    m = mem.shape[1]
    n = b * s
    depth = norm1_g.shape[0]
    tt = TOKEN_TILE
    ta = ATTN_TILE
    tiles_per_batch = s // tt
    assert depth == 2 and s % tt == 0 and tt % SUB_TILE == 0 and d == A_GROUPS * LANE
    assert s % ta == 0 and ta % tt == 0
    f32 = jnp.float32
    row = lambda v: v.reshape(1, -1).astype(f32)

    kn_all, mem_v = pl.pallas_call(
        _memkv_kernel,
        grid=(b,),
        in_specs=[pl.BlockSpec((m, d), lambda i: (i, 0)), _const_spec((1, d)),
                  _const_spec((d, 2 * MEM_W)), _const_spec((depth, 1, MEM_HEAD_DIM))],
        out_specs=[pl.BlockSpec((depth, m, MEM_W), lambda i: (0, i, 0)),
                   pl.BlockSpec((m, MEM_W), lambda i: (i, 0))],
        out_shape=[jax.ShapeDtypeStruct((depth, b * m, MEM_W), jnp.bfloat16),
                   jax.ShapeDtypeStruct((b * m, MEM_W), jnp.bfloat16)],
        compiler_params=_params(("arbitrary",)),
        name="mem_kv",
    )(mem.reshape(b * m, d), row(mem_norm_g), _bf(w_mem_kv), mem_kn_g.reshape(depth, 1, MEM_HEAD_DIM))

    tok = lambda width: pl.BlockSpec((tt, width), lambda i: (i, 0))
    kn_spec = lambda layer: pl.BlockSpec((None, m, MEM_W), lambda i: (layer, i // tiles_per_batch, 0))
    mv_spec = pl.BlockSpec((m, MEM_W), lambda i: (i // tiles_per_batch, 0))
    route_out_specs = [tok(d), tok(d // 2), tok(LANE), pl.BlockSpec((SUBLANE, tt), lambda i: (0, i)),
                       pl.BlockSpec((N_EXPERTS, LANE), lambda i: (0, 0))]
    route_out_shape = [jax.ShapeDtypeStruct((n, d), f32), jax.ShapeDtypeStruct((n, d // 2), jnp.uint32),
                       jax.ShapeDtypeStruct((n, LANE), f32), jax.ShapeDtypeStruct((SUBLANE, n), f32),
                       jax.ShapeDtypeStruct((N_EXPERTS, LANE), f32)]
    route_scratch = [pltpu.VMEM((N_EXPERTS, 1), f32)]

    first_block = chunk * (n // tt)
    group_tok = pl.BlockSpec((tt, d), lambda i: (i + first_block, 0))

    wr0, br0 = _router_weights(moe_w_group[0], moe_b_group[0], moe_w_expert[0], moe_b_expert[0])
    a_in = a_w_in.shape[-1]
    bias_s = jnp.repeat(a_b_s[0].T, LANE, axis=1).astype(f32)
    x2, xn, slab, slabt, counts = pl.pallas_call(
        _layer0_kernel,
        grid=(n // tt,),
        in_specs=[group_tok, _const_spec((1, d)), _const_spec((d, a_in)), _const_spec((1, d)),
                  _const_spec((1, d)), _const_spec((A_GROUPS, CHUNK, CHUNK)), _const_spec((CHUNK, d)),
                  kn_spec(0), mv_spec, _const_spec((1, MEM_HEAD_DIM)),
                  _const_spec((d + MEM_W, d)), _const_spec((1, d)), _const_spec((d, LANE)),
                  _const_spec((1, LANE))],
        out_specs=route_out_specs,
        out_shape=route_out_shape,
        scratch_shapes=route_scratch,
        compiler_params=_params(("arbitrary",)),
        name="layer0_mixer",
    )(x_all, row(norm1_g[0]), _bf(a_w_in[0]), row(a_ln_g[0]), row(a_ln_b[0]), _bf(a_w_s[0]), bias_s,
      kn_all, mem_v, row(mem_qn_g[0]), _bf(a_w_out[0]), row(norm2_g[0]), wr0, br0)
    picked = _moe(0, xn, slabt, counts, moe_w_gate, moe_w_up, moe_w_down)

    hq = MLA_HEADS
    o1, o2, o3 = Q_LORA, Q_LORA + KV_LORA, Q_LORA + KV_LORA + ROPE_DIM
    w_in = b_w_in[0]
    w_in_p = jnp.concatenate([w_in[:, :o2], w_in[:, o3:], _pad_rope_cols(w_in[:, o2:o3])], axis=1)
    wq = b_w_q_up[0].reshape(Q_LORA, hq, QK_DIM)
    wq_p = jnp.concatenate([wq[..., :NOPE_DIM], _pad_rope_cols(wq[..., NOPE_DIM:])], axis=-1)
    wq_p = wq_p.reshape(Q_LORA, hq * QK_PAD)
    wkv = b_w_kv_up[0].reshape(KV_LORA, hq, NOPE_DIM + V_DIM)
    wkv_p = jnp.concatenate([wkv[..., :NOPE_DIM].reshape(KV_LORA, hq * NOPE_DIM),
                             wkv[..., NOPE_DIM:].reshape(KV_LORA, hq * V_DIM)], axis=1)
    pad_gain = lambda g: jnp.concatenate([g[:NOPE_DIM], _pad_rope_cols(g[NOPE_DIM:])]).reshape(1, QK_PAD)
    half = jnp.arange(ROPE_HALF, dtype=f32)
    inv = ROPE_BASE ** (-(half * 2.0 / ROPE_DIM))
    inv_l = _rope_lanes(inv).reshape(1, LANE)
    sgn_l = jnp.concatenate([-jnp.ones((2 * ROPE_HALF,), f32), jnp.ones((2 * ROPE_HALF,), f32)]).reshape(1, LANE)

    in_w = w_in_p.shape[1]
    head_spec = lambda width: pl.BlockSpec((None, hq, tt, width),
                                           lambda i: (i // tiles_per_batch, 0, i % tiles_per_batch, 0))
    per_ta = ta // tt

    def vt_index(i):
        t = i % tiles_per_batch
        return (i // tiles_per_batch, 0, t // per_ta, 0, t % per_ta)

    vt_spec = pl.BlockSpec((None, hq, None, V_DIM, tt), vt_index)
    x2, q, k, vt, mem_o = pl.pallas_call(
        _layer1_proj_kernel,
        grid=(n // tt,),
        in_specs=[tok(d), tok(LANE), *_picked_specs(n, d // 2),
                  tok(1), _const_spec((1, d)), _const_spec((d, in_w)), _const_spec((1, Q_LORA)),
                  _const_spec((1, KV_LORA)), _const_spec((Q_LORA, hq * QK_PAD)),
                  _const_spec((KV_LORA, hq * (NOPE_DIM + V_DIM))), _const_spec((1, QK_PAD)),
                  _const_spec((1, QK_PAD)), _const_spec((1, LANE)), _const_spec((1, LANE)),
                  kn_spec(1), mv_spec, _const_spec((1, MEM_HEAD_DIM))],
        out_specs=[tok(d), head_spec(QK_PAD), head_spec(QK_PAD), vt_spec, tok(MEM_W)],
        out_shape=[jax.ShapeDtypeStruct((n, d), f32),
                   jax.ShapeDtypeStruct((b, hq, s, QK_PAD), jnp.bfloat16),
                   jax.ShapeDtypeStruct((b, hq, s, QK_PAD), jnp.bfloat16),
                   jax.ShapeDtypeStruct((b, hq, s // ta, V_DIM, ta), jnp.bfloat16),
                   jax.ShapeDtypeStruct((n, MEM_W), jnp.bfloat16)],
        compiler_params=_params(("arbitrary",)),
        name="layer1_proj",
    )(x2, slab, picked, picked,
      positions.reshape(n, 1), row(norm1_g[1]), _bf(w_in_p), row(b_q_norm_g[0]), row(b_kv_norm_g[0]),
      _bf(wq_p), _bf(wkv_p), pad_gain(b_qn_g[0]).astype(f32), pad_gain(b_kn_g[0]).astype(f32),
      inv_l, sgn_l, kn_all, mem_v, row(mem_qn_g[1]))

    qb = s // ta
    hp = ATTN_HEADS_PER_STEP
    attn = pl.pallas_call(
        _attn_kernel,
        grid=(b, hq // hp, qb),
        in_specs=[pl.BlockSpec((None, hp, ta, QK_PAD), lambda bi, hi, i: (bi, hi, i, 0)),
                  pl.BlockSpec((None, hp, s, QK_PAD), lambda bi, hi, i: (bi, hi, 0, 0)),
                  pl.BlockSpec((None, hp, qb, V_DIM, ta), lambda bi, hi, i: (bi, hi, 0, 0, 0))],
        out_specs=pl.BlockSpec((ta, hp * V_DIM), lambda bi, hi, i: (bi * qb + i, hi)),
        out_shape=jax.ShapeDtypeStruct((n, hq * V_DIM), jnp.bfloat16),
        scratch_shapes=[pltpu.VMEM((hp, 1, ta), f32), pltpu.VMEM((hp, V_DIM + ATTN_SUM_ROWS, ta), f32)],
        compiler_params=_params(("arbitrary", "arbitrary", "arbitrary")),
        name="causal_attention",
    )(q, k, vt)

    wr1, br1 = _router_weights(moe_w_group[1], moe_b_group[1], moe_w_expert[1], moe_b_expert[1])
    x2, xn, slab, slabt, counts = pl.pallas_call(
        _layer1_out_kernel,
        grid=(n // tt,),
        in_specs=[tok(d), tok(hq * V_DIM), tok(MEM_W), _const_spec((hq * V_DIM + MEM_W, d)),
                  _const_spec((1, d)), _const_spec((d, LANE)), _const_spec((1, LANE))],
        out_specs=route_out_specs,
        out_shape=route_out_shape,
        scratch_shapes=route_scratch,
        compiler_params=_params(("arbitrary",)),
        name="layer1_out",
    )(x2, attn, mem_o, _bf(b_w_out[0]), row(norm2_g[1]), wr1, br1)
    picked = _moe(1, xn, slabt, counts, moe_w_gate, moe_w_up, moe_w_down)
    prev = [] if out_prev is None else [out_prev]
    return pl.pallas_call(
        _combine_kernel,
        grid=(n // tt,),
        in_specs=[tok(d), tok(LANE), *_picked_specs(n, d // 2)] + [pl.BlockSpec(memory_space=pl.ANY)] * len(prev),
        out_specs=group_tok,
        out_shape=jax.ShapeDtypeStruct((n_total, d), f32),
        input_output_aliases={4: 0} if prev else {},
        compiler_params=_params(("arbitrary",)),
        name="moe_combine",
    )(x2, slab, picked, picked, *prev)


def kernel(x, mem, positions, mem_norm_g, w_mem_kv, mem_qn_g, mem_kn_g, norm1_g, norm2_g, a_w_in, a_ln_g, a_ln_b, a_w_s, a_b_s, a_w_out, b_w_in, b_q_norm_g, b_kv_norm_g, b_w_q_up, b_w_kv_up, b_qn_g, b_kn_g, b_w_out, moe_w_group, moe_b_group, moe_w_expert, moe_b_expert, moe_w_gate, moe_w_up, moe_w_down):
    params = (mem_norm_g, w_mem_kv, mem_qn_g, mem_kn_g, norm1_g, norm2_g, a_w_in, a_ln_g, a_ln_b, a_w_s,
              a_b_s, a_w_out, b_w_in, b_q_norm_g, b_kv_norm_g, b_w_q_up, b_w_kv_up, b_qn_g, b_kn_g, b_w_out,
              moe_w_group, moe_b_group, moe_w_expert, moe_b_expert, moe_w_gate, moe_w_up, moe_w_down)
    b, s, d = x.shape
    assert b % BATCH_GROUPS == 0
    g = b // BATCH_GROUPS
    x_all = x.reshape(b * s, d)
    out = None
    for c in range(BATCH_GROUPS):
        rows = slice(c * g, (c + 1) * g)
        out = _trunk(out, c, x_all, mem[rows], positions[rows], *params)
    return out.reshape(b, s, d)
```

```python
import jax
import jax.numpy as jnp
from jax import lax
from jax.experimental import pallas as pl
from jax.experimental.pallas import tpu as pltpu
from jax.experimental.pallas import tpu_sc as plsc

EPS = 1e-6
LANE = 128
SUBLANE = 8
MEM_HEADS = 4
MEM_HEAD_DIM = 128
MEM_W = MEM_HEADS * MEM_HEAD_DIM
CHUNK = 128
A_GROUPS = 8
MLA_HEADS = 8
Q_LORA = 512
KV_LORA = 256
NOPE_DIM = 128
ROPE_DIM = 64
ROPE_HALF = ROPE_DIM // 2
V_DIM = 128
QK_DIM = NOPE_DIM + ROPE_DIM
QK_PAD = 2 * LANE
ROPE_BASE = 10000.0
N_GROUPS = 4
EXPERTS_PER_GROUP = 8
N_EXPERTS = N_GROUPS * EXPERTS_PER_GROUP
ROUTE_ROWS = 40
EXPERT_FF = 256
LOG2E = 1.4426950408889634

BATCH_GROUPS = 2
TOKEN_TILE = 512
SUB_TILE = 256
ROW_TILE = 512
ATTN_TILE = 512
ATTN_HEADS_PER_STEP = 2
ATTN_KEY_TILES_PER_TRIP = 4
ATTN_QUERY_SPLIT = 2
ATTN_LOOKAHEAD = 3
ATTN_SUM_ROWS = 16
VMEM_LIMIT = 56 * 1024 * 1024
NEG_BIG = -1e30

SC_CORES = 2
SC_SUBCORES = 16
SC_WORKERS = SC_CORES * SC_SUBCORES
SC_INDEX_GROUP = 128
SC_CHUNK = 64

_NT = (((1,), (1,)), ((), ()))


def _const_spec(shape):
    nd = len(shape)
    return pl.BlockSpec(shape, lambda *_: (0,) * nd, pipeline_mode=pl.Buffered(1))


def _params(sem):
    return pltpu.CompilerParams(dimension_semantics=sem, vmem_limit_bytes=VMEM_LIMIT)


def _run_staggered(gens):
    waiting = list(gens)
    active = []
    while waiting or active:
        if waiting:
            active.append(waiting.pop(0))
        for g in list(active):
            try:
                next(g)
            except StopIteration:
                active.remove(g)


def _sub_rows(t):
    return [pl.ds(k * SUB_TILE, SUB_TILE) for k in range(t // SUB_TILE)]


def _rms(x, g):
    return x * lax.rsqrt(jnp.mean(x * x, axis=-1, keepdims=True) + EPS) * g


def _gelu(x):
    return 0.5 * x * (1.0 + lax.erf(x * (2.0 ** -0.5)))


def _bf(x):
    return x.astype(jnp.bfloat16)


def _dot(a, b):
    return jnp.dot(a, b, preferred_element_type=jnp.float32)


def _memkv_kernel(mem_ref, g_ref, w_ref, kng_ref, kn_ref, v_ref):
    h = _bf(_rms(mem_ref[...], g_ref[...]))
    kv = _dot(h, w_ref[...])
    v_ref[...] = _bf(kv[:, MEM_W:])
    for layer in range(kn_ref.shape[0]):
        g = kng_ref[layer]
        for hh in range(MEM_HEADS):
            k = kv[:, hh * LANE:(hh + 1) * LANE]
            kn_ref[layer, :, hh * LANE:(hh + 1) * LANE] = _bf(_rms(k, g))


def _mem_attention(qm, kn_ref, v_ref, qg):
    outs = []
    for hh in range(MEM_HEADS):
        sl = slice(hh * LANE, (hh + 1) * LANE)
        q = _rms(qm[:, sl], qg) * (MEM_HEAD_DIM ** -0.5)
        s = lax.dot_general(_bf(q), kn_ref[:, sl], _NT, preferred_element_type=jnp.float32)
        p = jnp.exp(s - jnp.max(s, axis=-1, keepdims=True))
        l = jnp.sum(p, axis=-1, keepdims=True)
        outs.append(_dot(_bf(p), v_ref[:, sl]) / l)
    return jnp.concatenate(outs, axis=-1)


def _pack_rows(x):
    w = x.shape[1] // 2
    bits = lambda v: lax.bitcast_convert_type(_bf(v).astype(jnp.float32), jnp.uint32)
    return (bits(x[:, :w]) >> 16) | (bits(x[:, w:]) & jnp.uint32(0xFFFF0000))


def _unpack_rows(p):
    lo = lax.bitcast_convert_type(p << 16, jnp.float32)
    hi = lax.bitcast_convert_type(p & jnp.uint32(0xFFFF0000), jnp.float32)
    return jnp.concatenate([lo, hi], axis=-1)


def _route_stages(x, rows, g2_ref, wr_ref, br_ref, carry_ref, xn_ref, slab_ref, slabt_ref, cnt_ref):
    t = x.shape[0]
    xn = _rms(x, g2_ref[...])
    xn_ref[rows, :] = _pack_rows(xn)
    logits = _dot(_bf(xn), wr_ref[...]) + br_ref[...]
    yield
    lt = logits.T[:ROUTE_ROWS, :]
    row = lax.broadcasted_iota(jnp.int32, lt.shape, 0)

    def first_max(v):
        m = jnp.max(v, axis=0, keepdims=True)
        idx = jnp.min(jnp.where(v == m, row, ROUTE_ROWS), axis=0, keepdims=True)
        return m, idx

    lg = jnp.where(row < N_GROUPS, lt, NEG_BIG)
    gmax, gidx = first_max(lg)
    g_w = 1.0 / jnp.sum(jnp.exp(lg - gmax), axis=0, keepdims=True)

    eid = row - N_GROUPS
    in_grp = (eid >= 0) & (eid < N_EXPERTS) & ((eid >> 3) == gidx)
    le = jnp.where(in_grp, lt, NEG_BIG)
    m1, i1 = first_max(le)
    m2, i2 = first_max(jnp.where(row == i1, NEG_BIG, le))
    r = jnp.exp(m2 - m1)
    w1 = g_w / (1.0 + r)
    w2 = w1 * r
    e1 = i1 - N_GROUPS
    e2 = i2 - N_GROUPS
    yield

    expert = lax.broadcasted_iota(jnp.int32, (N_EXPERTS, t), 0)
    oh1 = expert == e1
    oh2 = expert == e2
    oh = jnp.where(oh1 | oh2, 1.0, 0.0)
    src = lax.broadcasted_iota(jnp.int32, (t, t), 0)
    dst = lax.broadcasted_iota(jnp.int32, (t, t), 1)
    earlier = jnp.where(src < dst, 1.0, 0.0).astype(jnp.bfloat16)
    before = _dot(_bf(oh), earlier) + carry_ref[...]
    r1 = jnp.sum(jnp.where(oh1, before, 0.0), axis=0, keepdims=True)
    r2 = jnp.sum(jnp.where(oh2, before, 0.0), axis=0, keepdims=True)
    carry_ref[...] += jnp.sum(oh, axis=1, keepdims=True)
    cnt_ref[...] = jnp.broadcast_to(carry_ref[...], cnt_ref.shape)

    table = jnp.concatenate([e1.astype(jnp.float32), e2.astype(jnp.float32), w1, w2, r1, r2,
                             jnp.zeros((LANE - 6, t), jnp.float32)], axis=0)
    slabt_ref[:, rows] = table[:SUBLANE, :]
    slab_ref[rows, :] = table.T


def _reset_carry(carry_ref):
    @pl.when(pl.program_id(0) == 0)
    def _():
        carry_ref[...] = jnp.zeros_like(carry_ref)


def _layer0_kernel(x_ref, g1_ref, win_ref, lng_ref, lnb_ref, ws_ref, bs_ref, kn_ref, v_ref, qg_ref,
                   wout_ref, g2_ref, wr_ref, br_ref,
                   xo_ref, xn_ref, slab_ref, slabt_ref, cnt_ref, carry_ref):
    d = x_ref.shape[1]
    _reset_carry(carry_ref)
    row = lax.broadcasted_iota(jnp.int32, (CHUNK, CHUNK), 0)
    col = lax.broadcasted_iota(jnp.int32, (CHUNK, CHUNK), 1)
    causal = row >= col

    def stages(rows):
        x = x_ref[rows, :]
        t = x.shape[0]
        h = _bf(_rms(x, g1_ref[...]))
        z = _dot(h, win_ref[...])
        yield
        u = _gelu(z[:, :d])
        v = _gelu(z[:, d:2 * d])
        mu = jnp.mean(v, axis=-1, keepdims=True)
        vc = v - mu
        var = jnp.mean(vc * vc, axis=-1, keepdims=True)
        v = _bf(vc * lax.rsqrt(var + EPS) * lng_ref[...] + lnb_ref[...])
        yield
        chunks = []
        for c in range(t // CHUNK):
            cols = []
            for g in range(A_GROUPS):
                w = jnp.where(causal, ws_ref[g], jnp.zeros((), ws_ref.dtype))
                cols.append(_dot(w, v[c * CHUNK:(c + 1) * CHUNK, g * LANE:(g + 1) * LANE]))
            chunks.append(jnp.concatenate(cols, axis=-1) + bs_ref[...])
        mix = _bf(u * jnp.concatenate(chunks, axis=0))
        yield
        mem = _bf(_mem_attention(z[:, 2 * d:], kn_ref, v_ref, qg_ref[...]))
        yield
        xo = x + _dot(mix, wout_ref[:d, :]) + _dot(mem, wout_ref[d:, :])
        xo_ref[rows, :] = xo
        yield
        yield from _route_stages(xo, rows, g2_ref, wr_ref, br_ref, carry_ref,
                                 xn_ref, slab_ref, slabt_ref, cnt_ref)

    _run_staggered([stages(rows) for rows in _sub_rows(x_ref.shape[0])])


def _layer1_proj_kernel(x_ref, slab_ref, y0_ref, y1_ref, pos_ref, g1_ref, win_ref, qng_ref, kvng_ref,
                        wq_ref, wkv_ref, qg_ref, kg_ref, inv_ref, sgn_ref, kn_ref, mv_ref, mqg_ref,
                        xo_ref, q_ref, k_ref, v_ref, mem_ref):
    o1 = Q_LORA
    o2 = o1 + KV_LORA
    o3 = o2 + MEM_W
    q_scale = (QK_DIM ** -0.5) * LOG2E

    def stages(rows):
        x = _combined(x_ref, slab_ref, y0_ref, y1_ref, rows)
        xo_ref[rows, :] = x
        h = _bf(_rms(x, g1_ref[...]))
        z = _dot(h, win_ref[...])
        yield
        cq = _bf(_rms(z[:, :o1], qng_ref[...]))
        ckv = _bf(_rms(z[:, o1:o2], kvng_ref[...]))
        k_rope = z[:, o3:o3 + LANE]
        ang = pos_ref[rows, :].astype(jnp.float32) * inv_ref[...]
        cos = jnp.cos(ang)
        sin = jnp.sin(ang) * sgn_ref[...]

        def rope(r):
            return r * cos + pltpu.roll(r, LANE // 2, 1) * sin

        q = _dot(cq, wq_ref[...])
        kv = _dot(ckv, wkv_ref[...])
        yield
        qg = qg_ref[...]
        kg = kg_ref[...]
        kr_ss = jnp.sum(k_rope * k_rope, axis=-1, keepdims=True)
        kr = rope(k_rope * kg[:, LANE:])
        for hh in range(MLA_HEADS):
            qh = q[:, hh * QK_PAD:(hh + 1) * QK_PAD]
            rq = lax.rsqrt(jnp.sum(qh * qh, axis=-1, keepdims=True) * (1.0 / QK_DIM) + EPS) * q_scale
            qh = qh * rq * qg
            q_ref[hh, rows, :LANE] = _bf(qh[:, :LANE])
            q_ref[hh, rows, LANE:] = _bf(rope(qh[:, LANE:]))
            kn = kv[:, hh * LANE:(hh + 1) * LANE]
            rk = lax.rsqrt((jnp.sum(kn * kn, axis=-1, keepdims=True) + kr_ss) * (1.0 / QK_DIM) + EPS)
            k_ref[hh, rows, :LANE] = _bf(kn * rk * kg[:, :LANE])
            k_ref[hh, rows, LANE:] = _bf(kr * rk)
            v_ref[hh, :, rows] = _bf(kv[:, (MLA_HEADS + hh) * LANE:(MLA_HEADS + hh + 1) * LANE].T)
            if hh % 4 == 3:
                yield
        mem_ref[rows, :] = _bf(_mem_attention(z[:, o2:o3], kn_ref, mv_ref, mqg_ref[...]))

    _run_staggered([stages(rows) for rows in _sub_rows(x_ref.shape[0])])


def _attn_kernel(q_ref, k_ref, vt_ref, o_ref, m_ref, acc_ref):
    i = pl.program_id(2)
    heads, tq = q_ref.shape[0], q_ref.shape[1]
    m_ref[...] = jnp.full_like(m_ref, NEG_BIG)
    acc_ref[...] = jnp.zeros_like(acc_ref)

    qw = tq // ATTN_QUERY_SPLIT

    def scores(hh, j, part, masked):
        keys = (part + 1) * qw if masked else tq
        start = pl.multiple_of(j * tq, tq)
        s = lax.dot_general(k_ref[hh, pl.ds(start, keys), :], q_ref[hh, part * qw:(part + 1) * qw, :], _NT,
                            preferred_element_type=jnp.float32)
        if masked:
            key = lax.broadcasted_iota(jnp.int32, (keys, qw), 0)
            qry = lax.broadcasted_iota(jnp.int32, (keys, qw), 1) + part * qw
            s = jnp.where(key <= qry, s, NEG_BIG)
        return s

    def update(hh, j, part, s):
        keys = s.shape[0]
        lanes = slice(part * qw, (part + 1) * qw)
        m = m_ref[hh, :, lanes]
        m_new = jnp.maximum(m, jnp.max(s, axis=0, keepdims=True))
        alpha = jnp.exp2(m - m_new)
        p = _bf(jnp.exp2(s - m_new))
        m_ref[hh, :, lanes] = m_new
        vt1 = jnp.concatenate([vt_ref[hh, j, :, :keys], jnp.ones((ATTN_SUM_ROWS, keys), jnp.bfloat16)], axis=0)
        acc_ref[hh, :, lanes] = alpha * acc_ref[hh, :, lanes] + _dot(vt1, p)

    def run(items, masked):
        ss = {}
        ahead = ATTN_LOOKAHEAD
        for t in range(len(items) + ahead):
            if t < len(items):
                ss[t] = scores(*items[t], masked)
            if t >= ahead:
                update(*items[t - ahead], ss.pop(t - ahead))

    def full_tiles(first, count):
        run([(hh, first + u, part) for u in range(count) for hh in range(heads)
             for part in range(ATTN_QUERY_SPLIT)], False)

    unroll = ATTN_KEY_TILES_PER_TRIP

    def body(jj, c):
        full_tiles(unroll * jj, unroll)
        return c

    lax.fori_loop(0, i // unroll, body, 0)
    piece = unroll // 2
    while piece >= 1:
        @pl.when((i & piece) != 0)
        def _():
            full_tiles((i // (2 * piece)) * (2 * piece), piece)
        piece //= 2

    run([(hh, i, part) for hh in range(heads) for part in range(ATTN_QUERY_SPLIT)], True)
    for hh in range(heads):
        acc = acc_ref[hh]
        o_ref[:, hh * V_DIM:(hh + 1) * V_DIM] = _bf((acc[:V_DIM] / acc[V_DIM:V_DIM + 1]).T)


def _layer1_out_kernel(x_ref, o_ref, mem_ref, wout_ref, g2_ref, wr_ref, br_ref,
                       xo_ref, xn_ref, slab_ref, slabt_ref, cnt_ref, carry_ref):
    d = o_ref.shape[1]
    _reset_carry(carry_ref)

    def stages(rows):
        xo = (x_ref[rows, :] + _dot(o_ref[rows, :], wout_ref[:d, :])
              + _dot(mem_ref[rows, :], wout_ref[d:, :]))
        xo_ref[rows, :] = xo
        yield
        yield from _route_stages(xo, rows, g2_ref, wr_ref, br_ref, carry_ref,
                                 xn_ref, slab_ref, slabt_ref, cnt_ref)

    _run_staggered([stages(rows) for rows in _sub_rows(x_ref.shape[0])])


def _sc_mesh():
    return plsc.VectorSubcoreMesh(core_axis_name="c", subcore_axis_name="s")


def _sc_worker_base(rows_per_worker):
    return (lax.axis_index("c") * SC_SUBCORES + lax.axis_index("s")) * rows_per_worker


def _sc_scatter_rows(x, idx0, idx1, p_rows):
    n, w = x.shape
    per = n // SC_WORKERS
    chunks = SC_INDEX_GROUP // SC_CHUNK
    assert n % SC_WORKERS == 0 and per % SC_INDEX_GROUP == 0

    @pl.kernel(out_type=jax.ShapeDtypeStruct((p_rows, w), x.dtype), mesh=_sc_mesh(),
               scratch_types=[pltpu.VMEM((1, SC_INDEX_GROUP), jnp.int32),
                              pltpu.VMEM((1, SC_INDEX_GROUP), jnp.int32),
                              pltpu.VMEM((SC_CHUNK, w), x.dtype), pltpu.VMEM((SC_CHUNK, w), x.dtype),
                              pltpu.SemaphoreType.DMA, pltpu.SemaphoreType.DMA],
               name="moe_dispatch_sc")
    def scatter(x_hbm, i0_hbm, i1_hbm, o_hbm, i0_v, i1_v, buf_a, buf_b, sem_a, sem_b):
        base = _sc_worker_base(per)

        @pl.loop(0, per // SC_INDEX_GROUP)
        def _(g):
            off = pl.multiple_of(base + g * SC_INDEX_GROUP, SC_INDEX_GROUP)
            pltpu.sync_copy(i0_hbm.at[:, pl.ds(off, SC_INDEX_GROUP)], i0_v)
            pltpu.sync_copy(i1_hbm.at[:, pl.ds(off, SC_INDEX_GROUP)], i1_v)
            pending = []
            for c in range(chunks):
                buf, sem = ((buf_a, sem_a), (buf_b, sem_b))[c % 2]
                if c >= 2:
                    for cp in pending[c - 2]:
                        cp.wait()
                pltpu.sync_copy(x_hbm.at[pl.ds(off + c * SC_CHUNK, SC_CHUNK)], buf)
                sl = pl.ds(c * SC_CHUNK, SC_CHUNK)
                pending.append((pltpu.async_copy(buf, o_hbm.at[i0_v.at[0, sl]], sem),
                                pltpu.async_copy(buf, o_hbm.at[i1_v.at[0, sl]], sem)))
            for cps in pending[max(chunks - 2, 0):]:
                for cp in cps:
                    cp.wait()

    return scatter(x, idx0.reshape(1, n), idx1.reshape(1, n))


def _sc_gather_rows(table, idx):
    m = idx.shape[0]
    w = table.shape[1]
    per = m // SC_WORKERS
    chunks = SC_INDEX_GROUP // SC_CHUNK
    assert m % SC_WORKERS == 0 and per % SC_INDEX_GROUP == 0

    @pl.kernel(out_type=jax.ShapeDtypeStruct((m, w), table.dtype), mesh=_sc_mesh(),
               scratch_types=[pltpu.VMEM((1, SC_INDEX_GROUP), jnp.int32),
                              pltpu.VMEM((SC_CHUNK, w), table.dtype), pltpu.VMEM((SC_CHUNK, w), table.dtype),
                              pltpu.SemaphoreType.DMA, pltpu.SemaphoreType.DMA],
               name="moe_combine_sc")
    def gather(t_hbm, i_hbm, o_hbm, i_v, buf_a, buf_b, sem_a, sem_b):
        base = _sc_worker_base(per)

        @pl.loop(0, per // SC_INDEX_GROUP)
        def _(g):
            off = pl.multiple_of(base + g * SC_INDEX_GROUP, SC_INDEX_GROUP)
            pltpu.sync_copy(i_hbm.at[:, pl.ds(off, SC_INDEX_GROUP)], i_v)
            pending = []
            for c in range(chunks):
                buf, sem = ((buf_a, sem_a), (buf_b, sem_b))[c % 2]
                if c >= 2:
                    pending[c - 2].wait()
                pltpu.sync_copy(t_hbm.at[i_v.at[0, pl.ds(c * SC_CHUNK, SC_CHUNK)]], buf)
                pending.append(pltpu.async_copy(buf, o_hbm.at[pl.ds(off + c * SC_CHUNK, SC_CHUNK)], sem))
            for cp in pending[max(chunks - 2, 0):]:
                cp.wait()

    return gather(table, idx.reshape(1, m))


def _ffn_kernel(exp_ref, rows_ref, xs_ref, wg_ref, wu_ref, wd_ref, ys_ref):
    j = pl.program_id(0)

    def stages(k, rows):
        packed = xs_ref[rows, :]
        row_id = lax.broadcasted_iota(jnp.int32, packed.shape, 0) + k * SUB_TILE
        x = _bf(_unpack_rows(jnp.where(row_id < rows_ref[j], packed, jnp.uint32(0))))
        g = _dot(x, wg_ref[...])
        u = _dot(x, wu_ref[...])
        yield
        act = _bf(g * jax.nn.sigmoid(g) * u)
        yield
        ys_ref[rows, :] = _pack_rows(_dot(act, wd_ref[...]))

    _run_staggered([stages(k, rows) for k, rows in enumerate(_sub_rows(xs_ref.shape[0]))])


def _combined(x_ref, slab_ref, y0_ref, y1_ref, rows):
    slab = slab_ref[rows, :]
    return (x_ref[rows, :] + slab[:, 2:3] * _unpack_rows(y0_ref[rows, :])
            + slab[:, 3:4] * _unpack_rows(y1_ref[rows, :]))


def _combine_kernel(x_ref, slab_ref, y0_ref, y1_ref, *refs):
    refs[-1][...] = _combined(x_ref, slab_ref, y0_ref, y1_ref, slice(None))


def _moe(layer, xn, slabt, counts, w_gate, w_up, w_down):
    n, wp = xn.shape
    d = w_gate.shape[-2]
    rt = ROW_TILE
    p_max = 2 * n + N_EXPERTS * rt
    n_tiles = p_max // rt

    cnt = counts[:, 0].astype(jnp.int32)
    padded = ((cnt + rt - 1) // rt) * rt
    experts = jnp.arange(N_EXPERTS, dtype=jnp.int32)
    start = jnp.sum(jnp.where(experts[None, :] < experts[:, None], padded[None, :], 0), axis=1)
    end = start + padded

    def position(e_row, r_row):
        e = e_row.astype(jnp.int32)[None, :]
        return jnp.sum(jnp.where(e == experts[:, None], start[:, None], 0), axis=0) + r_row.astype(jnp.int32)

    pos0 = position(slabt[0], slabt[4])
    pos1 = position(slabt[1], slabt[5])
    tile_start = jnp.arange(n_tiles, dtype=jnp.int32) * rt
    owns = (start[None, :] <= tile_start[:, None]) & (tile_start[:, None] < end[None, :])
    past = tile_start >= end[N_EXPERTS - 1]
    tile_exp = jnp.where(past, N_EXPERTS - 1, jnp.sum(jnp.where(owns, experts[None, :], 0), axis=1))
    used = jnp.sum(jnp.where(owns, (start + cnt)[None, :], 0), axis=1)
    tile_rows = jnp.clip(used - tile_start, 0, rt).astype(jnp.int32)

    xs = _sc_scatter_rows(xn, pos0, pos1, p_max)

    f = w_gate.shape[-1]
    pick = lambda j, ex, rw: (layer, ex[j], 0, 0)
    rows_spec = pl.BlockSpec((rt, wp), lambda j, ex, rw: (j, 0))
    ys = pl.pallas_call(
        _ffn_kernel,
        grid_spec=pltpu.PrefetchScalarGridSpec(
            num_scalar_prefetch=2, grid=(n_tiles,),
            in_specs=[rows_spec, pl.BlockSpec((None, None, d, f), pick),
                      pl.BlockSpec((None, None, d, f), pick), pl.BlockSpec((None, None, f, d), pick)],
            out_specs=rows_spec),
        out_shape=jax.ShapeDtypeStruct((p_max, wp), jnp.uint32),
        compiler_params=_params(("arbitrary",)),
        name="moe_ffn",
    )(tile_exp.astype(jnp.int32), tile_rows, xs, w_gate, w_up, w_down)

    return _sc_gather_rows(ys, jnp.concatenate([pos0, pos1]))


def _picked_specs(n, wp):
    nb = n // TOKEN_TILE
    return [pl.BlockSpec((TOKEN_TILE, wp), lambda i: (i, 0)),
            pl.BlockSpec((TOKEN_TILE, wp), lambda i: (i + nb, 0))]


def _router_weights(w_group, b_group, w_expert, b_expert):
    d = w_group.shape[0]
    pad = LANE - N_GROUPS - N_EXPERTS
    wr = jnp.concatenate([w_group, w_expert, jnp.zeros((d, pad), w_group.dtype)], axis=1)
    br = jnp.concatenate([b_group, b_expert, jnp.zeros((pad,), b_group.dtype)])
    return _bf(wr), br.reshape(1, LANE).astype(jnp.float32)


def _rope_lanes(vec_half):
    z = jnp.zeros_like(vec_half)
    return jnp.concatenate([vec_half, z, vec_half, z], axis=-1)


def _pad_rope_cols(w):
    z = jnp.zeros(w.shape[:-1] + (ROPE_HALF,), w.dtype)
    return jnp.concatenate([w[..., :ROPE_HALF], z, w[..., ROPE_HALF:], z], axis=-1)


def _trunk(out_prev, chunk, x_all, mem, positions, mem_norm_g, w_mem_kv, mem_qn_g, mem_kn_g, norm1_g,
           norm2_g, a_w_in, a_ln_g, a_ln_b, a_w_s, a_b_s, a_w_out, b_w_in, b_q_norm_g, b_kv_norm_g, b_w_q_up,
           b_w_kv_up, b_qn_g, b_kn_g, b_w_out, moe_w_group, moe_b_group, moe_w_expert, moe_b_expert,
           moe_w_gate, moe_w_up, moe_w_down):
    n_total, d = x_all.shape
    b, s = positions.shape
    m = mem.shape[1]
    n = b * s
    depth = norm1_g.shape[0]
    tt = TOKEN_TILE
    ta = ATTN_TILE
    tiles_per_batch = s // tt
    assert depth == 2 and s % tt == 0 and tt % SUB_TILE == 0 and d == A_GROUPS * LANE
    assert s % ta == 0 and ta % tt == 0
    f32 = jnp.float32
    row = lambda v: v.reshape(1, -1).astype(f32)

    kn_all, mem_v = pl.pallas_call(
        _memkv_kernel,
        grid=(b,),
        in_specs=[pl.BlockSpec((m, d), lambda i: (i, 0)), _const_spec((1, d)),
                  _const_spec((d, 2 * MEM_W)), _const_spec((depth, 1, MEM_HEAD_DIM))],
        out_specs=[pl.BlockSpec((depth, m, MEM_W), lambda i: (0, i, 0)),
                   pl.BlockSpec((m, MEM_W), lambda i: (i, 0))],
        out_shape=[jax.ShapeDtypeStruct((depth, b * m, MEM_W), jnp.bfloat16),
                   jax.ShapeDtypeStruct((b * m, MEM_W), jnp.bfloat16)],
        compiler_params=_params(("arbitrary",)),
        name="mem_kv",
    )(mem.reshape(b * m, d), row(mem_norm_g), _bf(w_mem_kv), mem_kn_g.reshape(depth, 1, MEM_HEAD_DIM))

    tok = lambda width: pl.BlockSpec((tt, width), lambda i: (i, 0))
    kn_spec = lambda layer: pl.BlockSpec((None, m, MEM_W), lambda i: (layer, i // tiles_per_batch, 0))
    mv_spec = pl.BlockSpec((m, MEM_W), lambda i: (i // tiles_per_batch, 0))
    route_out_specs = [tok(d), tok(d // 2), tok(LANE), pl.BlockSpec((SUBLANE, tt), lambda i: (0, i)),
                       pl.BlockSpec((N_EXPERTS, LANE), lambda i: (0, 0))]
    route_out_shape = [jax.ShapeDtypeStruct((n, d), f32), jax.ShapeDtypeStruct((n, d // 2), jnp.uint32),
                       jax.ShapeDtypeStruct((n, LANE), f32), jax.ShapeDtypeStruct((SUBLANE, n), f32),
                       jax.ShapeDtypeStruct((N_EXPERTS, LANE), f32)]
    route_scratch = [pltpu.VMEM((N_EXPERTS, 1), f32)]

    first_block = chunk * (n // tt)
    group_tok = pl.BlockSpec((tt, d), lambda i: (i + first_block, 0))

    wr0, br0 = _router_weights(moe_w_group[0], moe_b_group[0], moe_w_expert[0], moe_b_expert[0])
    a_in = a_w_in.shape[-1]
    bias_s = jnp.repeat(a_b_s[0].T, LANE, axis=1).astype(f32)
    x2, xn, slab, slabt, counts = pl.pallas_call(
        _layer0_kernel,
        grid=(n // tt,),
        in_specs=[group_tok, _const_spec((1, d)), _const_spec((d, a_in)), _const_spec((1, d)),
                  _const_spec((1, d)), _const_spec((A_GROUPS, CHUNK, CHUNK)), _const_spec((CHUNK, d)),
                  kn_spec(0), mv_spec, _const_spec((1, MEM_HEAD_DIM)),
                  _const_spec((d + MEM_W, d)), _const_spec((1, d)), _const_spec((d, LANE)),
                  _const_spec((1, LANE))],
        out_specs=route_out_specs,
        out_shape=route_out_shape,
        scratch_shapes=route_scratch,
        compiler_params=_params(("arbitrary",)),
        name="layer0_mixer",
    )(x_all, row(norm1_g[0]), _bf(a_w_in[0]), row(a_ln_g[0]), row(a_ln_b[0]), _bf(a_w_s[0]), bias_s,
      kn_all, mem_v, row(mem_qn_g[0]), _bf(a_w_out[0]), row(norm2_g[0]), wr0, br0)
    picked = _moe(0, xn, slabt, counts, moe_w_gate, moe_w_up, moe_w_down)

    hq = MLA_HEADS
    o1, o2, o3 = Q_LORA, Q_LORA + KV_LORA, Q_LORA + KV_LORA + ROPE_DIM
    w_in = b_w_in[0]
    w_in_p = jnp.concatenate([w_in[:, :o2], w_in[:, o3:], _pad_rope_cols(w_in[:, o2:o3])], axis=1)
    wq = b_w_q_up[0].reshape(Q_LORA, hq, QK_DIM)
    wq_p = jnp.concatenate([wq[..., :NOPE_DIM], _pad_rope_cols(wq[..., NOPE_DIM:])], axis=-1)
    wq_p = wq_p.reshape(Q_LORA, hq * QK_PAD)
    wkv = b_w_kv_up[0].reshape(KV_LORA, hq, NOPE_DIM + V_DIM)
    wkv_p = jnp.concatenate([wkv[..., :NOPE_DIM].reshape(KV_LORA, hq * NOPE_DIM),
                             wkv[..., NOPE_DIM:].reshape(KV_LORA, hq * V_DIM)], axis=1)
    pad_gain = lambda g: jnp.concatenate([g[:NOPE_DIM], _pad_rope_cols(g[NOPE_DIM:])]).reshape(1, QK_PAD)
    half = jnp.arange(ROPE_HALF, dtype=f32)
    inv = ROPE_BASE ** (-(half * 2.0 / ROPE_DIM))
    inv_l = _rope_lanes(inv).reshape(1, LANE)
    sgn_l = jnp.concatenate([-jnp.ones((2 * ROPE_HALF,), f32), jnp.ones((2 * ROPE_HALF,), f32)]).reshape(1, LANE)

    in_w = w_in_p.shape[1]
    head_spec = lambda width: pl.BlockSpec((None, hq, tt, width),
                                           lambda i: (i // tiles_per_batch, 0, i % tiles_per_batch, 0))
    per_ta = ta // tt

    def vt_index(i):
        t = i % tiles_per_batch
        return (i // tiles_per_batch, 0, t // per_ta, 0, t % per_ta)

    vt_spec = pl.BlockSpec((None, hq, None, V_DIM, tt), vt_index)
    x2, q, k, vt, mem_o = pl.pallas_call(
        _layer1_proj_kernel,
        grid=(n // tt,),
        in_specs=[tok(d), tok(LANE), *_picked_specs(n, d // 2),
                  tok(1), _const_spec((1, d)), _const_spec((d, in_w)), _const_spec((1, Q_LORA)),
                  _const_spec((1, KV_LORA)), _const_spec((Q_LORA, hq * QK_PAD)),
                  _const_spec((KV_LORA, hq * (NOPE_DIM + V_DIM))), _const_spec((1, QK_PAD)),
                  _const_spec((1, QK_PAD)), _const_spec((1, LANE)), _const_spec((1, LANE)),
                  kn_spec(1), mv_spec, _const_spec((1, MEM_HEAD_DIM))],
        out_specs=[tok(d), head_spec(QK_PAD), head_spec(QK_PAD), vt_spec, tok(MEM_W)],
        out_shape=[jax.ShapeDtypeStruct((n, d), f32),
                   jax.ShapeDtypeStruct((b, hq, s, QK_PAD), jnp.bfloat16),
                   jax.ShapeDtypeStruct((b, hq, s, QK_PAD), jnp.bfloat16),
                   jax.ShapeDtypeStruct((b, hq, s // ta, V_DIM, ta), jnp.bfloat16),
                   jax.ShapeDtypeStruct((n, MEM_W), jnp.bfloat16)],
        compiler_params=_params(("arbitrary",)),
        name="layer1_proj",
    )(x2, slab, picked, picked,
      positions.reshape(n, 1), row(norm1_g[1]), _bf(w_in_p), row(b_q_norm_g[0]), row(b_kv_norm_g[0]),
      _bf(wq_p), _bf(wkv_p), pad_gain(b_qn_g[0]).astype(f32), pad_gain(b_kn_g[0]).astype(f32),
      inv_l, sgn_l, kn_all, mem_v, row(mem_qn_g[1]))

    qb = s // ta
    hp = ATTN_HEADS_PER_STEP
    attn = pl.pallas_call(
        _attn_kernel,
        grid=(b, hq // hp, qb),
        in_specs=[pl.BlockSpec((None, hp, ta, QK_PAD), lambda bi, hi, i: (bi, hi, i, 0)),
                  pl.BlockSpec((None, hp, s, QK_PAD), lambda bi, hi, i: (bi, hi, 0, 0)),
                  pl.BlockSpec((None, hp, qb, V_DIM, ta), lambda bi, hi, i: (bi, hi, 0, 0, 0))],
        out_specs=pl.BlockSpec((ta, hp * V_DIM), lambda bi, hi, i: (bi * qb + i, hi)),
        out_shape=jax.ShapeDtypeStruct((n, hq * V_DIM), jnp.bfloat16),
        scratch_shapes=[pltpu.VMEM((hp, 1, ta), f32), pltpu.VMEM((hp, V_DIM + ATTN_SUM_ROWS, ta), f32)],
        compiler_params=_params(("arbitrary", "arbitrary", "arbitrary")),
        name="causal_attention",
    )(q, k, vt)

    wr1, br1 = _router_weights(moe_w_group[1], moe_b_group[1], moe_w_expert[1], moe_b_expert[1])
    x2, xn, slab, slabt, counts = pl.pallas_call(
        _layer1_out_kernel,
        grid=(n // tt,),
        in_specs=[tok(d), tok(hq * V_DIM), tok(MEM_W), _const_spec((hq * V_DIM + MEM_W, d)),
                  _const_spec((1, d)), _const_spec((d, LANE)), _const_spec((1, LANE))],
        out_specs=route_out_specs,
        out_shape=route_out_shape,
        scratch_shapes=route_scratch,
        compiler_params=_params(("arbitrary",)),
        name="layer1_out",
    )(x2, attn, mem_o, _bf(b_w_out[0]), row(norm2_g[1]), wr1, br1)
    picked = _moe(1, xn, slabt, counts, moe_w_gate, moe_w_up, moe_w_down)
    prev = [] if out_prev is None else [out_prev]
    return pl.pallas_call(
        _combine_kernel,
        grid=(n // tt,),
        in_specs=[tok(d), tok(LANE), *_picked_specs(n, d // 2)] + [pl.BlockSpec(memory_space=pl.ANY)] * len(prev),
        out_specs=group_tok,
        out_shape=jax.ShapeDtypeStruct((n_total, d), f32),
        input_output_aliases={4: 0} if prev else {},
        compiler_params=_params(("arbitrary",)),
        name="moe_combine",
    )(x2, slab, picked, picked, *prev)


def kernel(x, mem, positions, mem_norm_g, w_mem_kv, mem_qn_g, mem_kn_g, norm1_g, norm2_g, a_w_in, a_ln_g, a_ln_b, a_w_s, a_b_s, a_w_out, b_w_in, b_q_norm_g, b_kv_norm_g, b_w_q_up, b_w_kv_up, b_qn_g, b_kn_g, b_w_out, moe_w_group, moe_b_group, moe_w_expert, moe_b_expert, moe_w_gate, moe_w_up, moe_w_down):
    params = (mem_norm_g, w_mem_kv, mem_qn_g, mem_kn_g, norm1_g, norm2_g, a_w_in, a_ln_g, a_ln_b, a_w_s,
              a_b_s, a_w_out, b_w_in, b_q_norm_g, b_kv_norm_g, b_w_q_up, b_w_kv_up, b_qn_g, b_kn_g, b_w_out,
              moe_w_group, moe_b_group, moe_w_expert, moe_b_expert, moe_w_gate, moe_w_up, moe_w_down)
    b, s, d = x.shape
    assert b % BATCH_GROUPS == 0
    g = b // BATCH_GROUPS
    x_all = x.reshape(b * s, d)
    expert_w = (_bf(moe_w_gate), _bf(moe_w_up), _bf(moe_w_down))
    out = None
    for c in range(BATCH_GROUPS):
        rows = slice(c * g, (c + 1) * g)
        out = _trunk(out, c, x_all, mem[rows], positions[rows], *params[:-3], *expert_w)
    return out.reshape(b, s, d)
```

```python
import jax
import jax.numpy as jnp
from jax import lax
from jax.experimental import pallas as pl
from jax.experimental.pallas import tpu as pltpu
from jax.experimental.pallas import tpu_sc as plsc

EPS = 1e-6
LANE = 128
SUBLANE = 8
MEM_HEADS = 4
MEM_HEAD_DIM = 128
MEM_W = MEM_HEADS * MEM_HEAD_DIM
CHUNK = 128
A_GROUPS = 8
MLA_HEADS = 8
Q_LORA = 512
KV_LORA = 256
NOPE_DIM = 128
ROPE_DIM = 64
ROPE_HALF = ROPE_DIM // 2
V_DIM = 128
QK_DIM = NOPE_DIM + ROPE_DIM
QK_PAD = 2 * LANE
ROPE_BASE = 10000.0
N_GROUPS = 4
EXPERTS_PER_GROUP = 8
N_EXPERTS = N_GROUPS * EXPERTS_PER_GROUP
ROUTE_ROWS = 40
EXPERT_FF = 256
LOG2E = 1.4426950408889634

BATCH_GROUPS = 2
TOKEN_TILE = 512
SUB_TILE = 256
ROW_TILE = 512
ATTN_TILE = 512
ATTN_HEADS_PER_STEP = 2
ATTN_KEY_TILES_PER_TRIP = 4
ATTN_QUERY_SPLIT = 2
ATTN_LOOKAHEAD = 3
ATTN_SUM_ROWS = 16
VMEM_LIMIT = 56 * 1024 * 1024
NEG_BIG = -1e30

SC_CORES = 2
SC_SUBCORES = 16
SC_WORKERS = SC_CORES * SC_SUBCORES
SC_INDEX_GROUP = 128
SC_CHUNK = 64

_NT = (((1,), (1,)), ((), ()))
_ANY = pl.BlockSpec(memory_space=pl.ANY)


def _const_spec(shape):
    nd = len(shape)
    return pl.BlockSpec(shape, lambda *_: (0,) * nd, pipeline_mode=pl.Buffered(1))


def _params(sem):
    return pltpu.CompilerParams(dimension_semantics=sem, vmem_limit_bytes=VMEM_LIMIT)


def _run_staggered(gens):
    waiting = list(gens)
    active = []
    while waiting or active:
        if waiting:
            active.append(waiting.pop(0))
        for g in list(active):
            try:
                next(g)
            except StopIteration:
                active.remove(g)


def _sub_rows(t):
    return [pl.ds(k * SUB_TILE, SUB_TILE) for k in range(t // SUB_TILE)]


def _rms(x, g):
    return x * lax.rsqrt(jnp.mean(x * x, axis=-1, keepdims=True) + EPS) * g


def _gelu(x):
    return 0.5 * x * (1.0 + lax.erf(x * (2.0 ** -0.5)))


def _bf(x):
    return x.astype(jnp.bfloat16)


def _dot(a, b):
    return jnp.dot(a, b, preferred_element_type=jnp.float32)


def _memkv_kernel(mem_ref, g_ref, w_ref, kng_ref, kn_ref, v_ref):
    h = _bf(_rms(mem_ref[...], g_ref[...]))
    kv = _dot(h, w_ref[...])
    v_ref[...] = _bf(kv[:, MEM_W:])
    for layer in range(kn_ref.shape[0]):
        g = kng_ref[layer]
        for hh in range(MEM_HEADS):
            k = kv[:, hh * LANE:(hh + 1) * LANE]
            kn_ref[layer, :, hh * LANE:(hh + 1) * LANE] = _bf(_rms(k, g))


def _mem_attention(qm, kn_ref, v_ref, qg):
    outs = []
    for hh in range(MEM_HEADS):
        sl = slice(hh * LANE, (hh + 1) * LANE)
        q = _rms(qm[:, sl], qg) * (MEM_HEAD_DIM ** -0.5)
        s = lax.dot_general(_bf(q), kn_ref[:, sl], _NT, preferred_element_type=jnp.float32)
        p = jnp.exp(s - jnp.max(s, axis=-1, keepdims=True))
        l = jnp.sum(p, axis=-1, keepdims=True)
        outs.append(_dot(_bf(p), v_ref[:, sl]) / l)
    return jnp.concatenate(outs, axis=-1)


def _pack_rows(x):
    w = x.shape[1] // 2
    bits = lambda v: lax.bitcast_convert_type(_bf(v).astype(jnp.float32), jnp.uint32)
    return (bits(x[:, :w]) >> 16) | (bits(x[:, w:]) & jnp.uint32(0xFFFF0000))


def _unpack_rows(p):
    lo = lax.bitcast_convert_type(p << 16, jnp.float32)
    hi = lax.bitcast_convert_type(p & jnp.uint32(0xFFFF0000), jnp.float32)
    return jnp.concatenate([lo, hi], axis=-1)


def _route_stages(x, rows, g2_ref, wr_ref, br_ref, carry_ref, xn_ref, slab_ref, slabt_ref, cnt_ref):
    t = x.shape[0]
    xn = _rms(x, g2_ref[...])
    xn_ref[rows, :] = _pack_rows(xn)
    logits = _dot(_bf(xn), wr_ref[...]) + br_ref[...]
    yield
    lt = logits.T[:ROUTE_ROWS, :]
    row = lax.broadcasted_iota(jnp.int32, lt.shape, 0)

    def first_max(v):
        m = jnp.max(v, axis=0, keepdims=True)
        idx = jnp.min(jnp.where(v == m, row, ROUTE_ROWS), axis=0, keepdims=True)
        return m, idx

    lg = jnp.where(row < N_GROUPS, lt, NEG_BIG)
    gmax, gidx = first_max(lg)
    g_w = 1.0 / jnp.sum(jnp.exp(lg - gmax), axis=0, keepdims=True)

    eid = row - N_GROUPS
    in_grp = (eid >= 0) & (eid < N_EXPERTS) & ((eid >> 3) == gidx)
    le = jnp.where(in_grp, lt, NEG_BIG)
    m1, i1 = first_max(le)
    m2, i2 = first_max(jnp.where(row == i1, NEG_BIG, le))
    r = jnp.exp(m2 - m1)
    w1 = g_w / (1.0 + r)
    w2 = w1 * r
    e1 = i1 - N_GROUPS
    e2 = i2 - N_GROUPS
    yield

    expert = lax.broadcasted_iota(jnp.int32, (N_EXPERTS, t), 0)
    oh1 = expert == e1
    oh2 = expert == e2
    oh = jnp.where(oh1 | oh2, 1.0, 0.0)
    src = lax.broadcasted_iota(jnp.int32, (t, t), 0)
    dst = lax.broadcasted_iota(jnp.int32, (t, t), 1)
    earlier = jnp.where(src < dst, 1.0, 0.0).astype(jnp.bfloat16)
    before = _dot(_bf(oh), earlier) + carry_ref[...]
    r1 = jnp.sum(jnp.where(oh1, before, 0.0), axis=0, keepdims=True)
    r2 = jnp.sum(jnp.where(oh2, before, 0.0), axis=0, keepdims=True)
    carry_ref[...] += jnp.sum(oh, axis=1, keepdims=True)
    cnt_ref[...] = jnp.broadcast_to(carry_ref[...], cnt_ref.shape)

    table = jnp.concatenate([e1.astype(jnp.float32), e2.astype(jnp.float32), w1, w2, r1, r2,
                             jnp.zeros((LANE - 6, t), jnp.float32)], axis=0)
    slabt_ref[:, rows] = table[:SUBLANE, :]
    slab_ref[rows, :] = table.T


def _reset_carry(carry_ref):
    @pl.when(pl.program_id(0) == 0)
    def _():
        carry_ref[...] = jnp.zeros_like(carry_ref)


def _layer0_kernel(x_ref, g1_ref, win_ref, lng_ref, lnb_ref, ws_ref, bs_ref, kn_ref, v_ref, qg_ref,
                   wout_ref, g2_ref, wr_ref, br_ref,
                   xo_ref, xn_ref, slab_ref, slabt_ref, cnt_ref, carry_ref):
    d = x_ref.shape[1]
    _reset_carry(carry_ref)
    row = lax.broadcasted_iota(jnp.int32, (CHUNK, CHUNK), 0)
    col = lax.broadcasted_iota(jnp.int32, (CHUNK, CHUNK), 1)
    causal = row >= col

    def stages(rows):
        x = x_ref[rows, :]
        t = x.shape[0]
        h = _bf(_rms(x, g1_ref[...]))
        z = _dot(h, win_ref[...])
        yield
        u = _gelu(z[:, :d])
        v = _gelu(z[:, d:2 * d])
        mu = jnp.mean(v, axis=-1, keepdims=True)
        vc = v - mu
        var = jnp.mean(vc * vc, axis=-1, keepdims=True)
        v = _bf(vc * lax.rsqrt(var + EPS) * lng_ref[...] + lnb_ref[...])
        yield
        chunks = []
        for c in range(t // CHUNK):
            cols = []
            for g in range(A_GROUPS):
                w = jnp.where(causal, ws_ref[g], jnp.zeros((), ws_ref.dtype))
                cols.append(_dot(w, v[c * CHUNK:(c + 1) * CHUNK, g * LANE:(g + 1) * LANE]))
            chunks.append(jnp.concatenate(cols, axis=-1) + bs_ref[...])
        mix = _bf(u * jnp.concatenate(chunks, axis=0))
        yield
        mem = _bf(_mem_attention(z[:, 2 * d:], kn_ref, v_ref, qg_ref[...]))
        yield
        xo = x + _dot(mix, wout_ref[:d, :]) + _dot(mem, wout_ref[d:, :])
        xo_ref[rows, :] = xo
        yield
        yield from _route_stages(xo, rows, g2_ref, wr_ref, br_ref, carry_ref,
                                 xn_ref, slab_ref, slabt_ref, cnt_ref)

    _run_staggered([stages(rows) for rows in _sub_rows(x_ref.shape[0])])


def _layer1_proj_kernel(after_ref, x_ref, slab_ref, y0_ref, y1_ref, pos_ref, g1_ref, win_ref, qng_ref, kvng_ref,
                        wq_ref, wkv_ref, qg_ref, kg_ref, inv_ref, sgn_ref, kn_ref, mv_ref, mqg_ref,
                        xo_ref, q_ref, k_ref, v_ref, mem_ref):
    o1 = Q_LORA
    o2 = o1 + KV_LORA
    o3 = o2 + MEM_W
    q_scale = (QK_DIM ** -0.5) * LOG2E

    def stages(rows):
        x = _combined(x_ref, slab_ref, y0_ref, y1_ref, rows)
        xo_ref[rows, :] = x
        h = _bf(_rms(x, g1_ref[...]))
        z = _dot(h, win_ref[...])
        yield
        cq = _bf(_rms(z[:, :o1], qng_ref[...]))
        ckv = _bf(_rms(z[:, o1:o2], kvng_ref[...]))
        k_rope = z[:, o3:o3 + LANE]
        ang = pos_ref[rows, :].astype(jnp.float32) * inv_ref[...]
        cos = jnp.cos(ang)
        sin = jnp.sin(ang) * sgn_ref[...]

        def rope(r):
            return r * cos + pltpu.roll(r, LANE // 2, 1) * sin

        q = _dot(cq, wq_ref[...])
        kv = _dot(ckv, wkv_ref[...])
        yield
        qg = qg_ref[...]
        kg = kg_ref[...]
        kr_ss = jnp.sum(k_rope * k_rope, axis=-1, keepdims=True)
        kr = rope(k_rope * kg[:, LANE:])
        for hh in range(MLA_HEADS):
            qh = q[:, hh * QK_PAD:(hh + 1) * QK_PAD]
            rq = lax.rsqrt(jnp.sum(qh * qh, axis=-1, keepdims=True) * (1.0 / QK_DIM) + EPS) * q_scale
            qh = qh * rq * qg
            q_ref[hh, rows, :LANE] = _bf(qh[:, :LANE])
            q_ref[hh, rows, LANE:] = _bf(rope(qh[:, LANE:]))
            kn = kv[:, hh * LANE:(hh + 1) * LANE]
            rk = lax.rsqrt((jnp.sum(kn * kn, axis=-1, keepdims=True) + kr_ss) * (1.0 / QK_DIM) + EPS)
            k_ref[hh, rows, :LANE] = _bf(kn * rk * kg[:, :LANE])
            k_ref[hh, rows, LANE:] = _bf(kr * rk)
            v_ref[hh, :, rows] = _bf(kv[:, (MLA_HEADS + hh) * LANE:(MLA_HEADS + hh + 1) * LANE].T)
            if hh % 4 == 3:
                yield
        mem_ref[rows, :] = _bf(_mem_attention(z[:, o2:o3], kn_ref, mv_ref, mqg_ref[...]))

    _run_staggered([stages(rows) for rows in _sub_rows(x_ref.shape[0])])


def _attn_kernel(q_ref, k_ref, vt_ref, o_ref, m_ref, acc_ref):
    i = pl.program_id(2)
    heads, tq = q_ref.shape[0], q_ref.shape[1]
    m_ref[...] = jnp.full_like(m_ref, NEG_BIG)
    acc_ref[...] = jnp.zeros_like(acc_ref)

    qw = tq // ATTN_QUERY_SPLIT

    def scores(hh, j, part, masked):
        keys = (part + 1) * qw if masked else tq
        start = pl.multiple_of(j * tq, tq)
        s = lax.dot_general(k_ref[hh, pl.ds(start, keys), :], q_ref[hh, part * qw:(part + 1) * qw, :], _NT,
                            preferred_element_type=jnp.float32)
        if masked:
            key = lax.broadcasted_iota(jnp.int32, (keys, qw), 0)
            qry = lax.broadcasted_iota(jnp.int32, (keys, qw), 1) + part * qw
            s = jnp.where(key <= qry, s, NEG_BIG)
        return s

    def update(hh, j, part, s):
        keys = s.shape[0]
        lanes = slice(part * qw, (part + 1) * qw)
        m = m_ref[hh, :, lanes]
        m_new = jnp.maximum(m, jnp.max(s, axis=0, keepdims=True))
        alpha = jnp.exp2(m - m_new)
        p = _bf(jnp.exp2(s - m_new))
        m_ref[hh, :, lanes] = m_new
        vt1 = jnp.concatenate([vt_ref[hh, j, :, :keys], jnp.ones((ATTN_SUM_ROWS, keys), jnp.bfloat16)], axis=0)
        acc_ref[hh, :, lanes] = alpha * acc_ref[hh, :, lanes] + _dot(vt1, p)

    def run(items, masked):
        ss = {}
        ahead = ATTN_LOOKAHEAD
        for t in range(len(items) + ahead):
            if t < len(items):
                ss[t] = scores(*items[t], masked)
            if t >= ahead:
                update(*items[t - ahead], ss.pop(t - ahead))

    def full_tiles(first, count):
        run([(hh, first + u, part) for u in range(count) for hh in range(heads)
             for part in range(ATTN_QUERY_SPLIT)], False)

    unroll = ATTN_KEY_TILES_PER_TRIP

    def body(jj, c):
        full_tiles(unroll * jj, unroll)
        return c

    lax.fori_loop(0, i // unroll, body, 0)
    piece = unroll // 2
    while piece >= 1:
        @pl.when((i & piece) != 0)
        def _():
            full_tiles((i // (2 * piece)) * (2 * piece), piece)
        piece //= 2

    run([(hh, i, part) for hh in range(heads) for part in range(ATTN_QUERY_SPLIT)], True)
    for hh in range(heads):
        acc = acc_ref[hh]
        o_ref[:, hh * V_DIM:(hh + 1) * V_DIM] = _bf((acc[:V_DIM] / acc[V_DIM:V_DIM + 1]).T)


def _layer1_out_kernel(x_ref, o_ref, mem_ref, wout_ref, g2_ref, wr_ref, br_ref,
                       xo_ref, xn_ref, slab_ref, slabt_ref, cnt_ref, carry_ref):
    d = o_ref.shape[1]
    _reset_carry(carry_ref)

    def stages(rows):
        xo = (x_ref[rows, :] + _dot(o_ref[rows, :], wout_ref[:d, :])
              + _dot(mem_ref[rows, :], wout_ref[d:, :]))
        xo_ref[rows, :] = xo
        yield
        yield from _route_stages(xo, rows, g2_ref, wr_ref, br_ref, carry_ref,
                                 xn_ref, slab_ref, slabt_ref, cnt_ref)

    _run_staggered([stages(rows) for rows in _sub_rows(x_ref.shape[0])])


def _sc_mesh():
    return plsc.VectorSubcoreMesh(core_axis_name="c", subcore_axis_name="s")


def _sc_worker_base(rows_per_worker):
    return (lax.axis_index("c") * SC_SUBCORES + lax.axis_index("s")) * rows_per_worker


def _sc_scatter_rows(x, idx0, idx1, p_rows):
    n, w = x.shape
    per = n // SC_WORKERS
    chunks = SC_INDEX_GROUP // SC_CHUNK
    assert n % SC_WORKERS == 0 and per % SC_INDEX_GROUP == 0

    @pl.kernel(out_type=jax.ShapeDtypeStruct((p_rows, w), x.dtype), mesh=_sc_mesh(),
               scratch_types=[pltpu.VMEM((1, SC_INDEX_GROUP), jnp.int32),
                              pltpu.VMEM((1, SC_INDEX_GROUP), jnp.int32),
                              pltpu.VMEM((SC_CHUNK, w), x.dtype), pltpu.VMEM((SC_CHUNK, w), x.dtype),
                              pltpu.SemaphoreType.DMA, pltpu.SemaphoreType.DMA],
               name="moe_dispatch_sc")
    def scatter(x_hbm, i0_hbm, i1_hbm, o_hbm, i0_v, i1_v, buf_a, buf_b, sem_a, sem_b):
        base = _sc_worker_base(per)

        @pl.loop(0, per // SC_INDEX_GROUP)
        def _(g):
            off = pl.multiple_of(base + g * SC_INDEX_GROUP, SC_INDEX_GROUP)
            pltpu.sync_copy(i0_hbm.at[:, pl.ds(off, SC_INDEX_GROUP)], i0_v)
            pltpu.sync_copy(i1_hbm.at[:, pl.ds(off, SC_INDEX_GROUP)], i1_v)
            pending = []
            for c in range(chunks):
                buf, sem = ((buf_a, sem_a), (buf_b, sem_b))[c % 2]
                if c >= 2:
                    for cp in pending[c - 2]:
                        cp.wait()
                pltpu.sync_copy(x_hbm.at[pl.ds(off + c * SC_CHUNK, SC_CHUNK)], buf)
                sl = pl.ds(c * SC_CHUNK, SC_CHUNK)
                pending.append((pltpu.async_copy(buf, o_hbm.at[i0_v.at[0, sl]], sem),
                                pltpu.async_copy(buf, o_hbm.at[i1_v.at[0, sl]], sem)))
            for cps in pending[max(chunks - 2, 0):]:
                for cp in cps:
                    cp.wait()

    return scatter(x, idx0.reshape(1, n), idx1.reshape(1, n))


def _sc_gather_rows(table, idx):
    m = idx.shape[0]
    w = table.shape[1]
    per = m // SC_WORKERS
    chunks = SC_INDEX_GROUP // SC_CHUNK
    assert m % SC_WORKERS == 0 and per % SC_INDEX_GROUP == 0

    @pl.kernel(out_type=jax.ShapeDtypeStruct((m, w), table.dtype), mesh=_sc_mesh(),
               scratch_types=[pltpu.VMEM((1, SC_INDEX_GROUP), jnp.int32),
                              pltpu.VMEM((SC_CHUNK, w), table.dtype), pltpu.VMEM((SC_CHUNK, w), table.dtype),
                              pltpu.SemaphoreType.DMA, pltpu.SemaphoreType.DMA],
               name="moe_combine_sc")
    def gather(t_hbm, i_hbm, o_hbm, i_v, buf_a, buf_b, sem_a, sem_b):
        base = _sc_worker_base(per)

        @pl.loop(0, per // SC_INDEX_GROUP)
        def _(g):
            off = pl.multiple_of(base + g * SC_INDEX_GROUP, SC_INDEX_GROUP)
            pltpu.sync_copy(i_hbm.at[:, pl.ds(off, SC_INDEX_GROUP)], i_v)
            pending = []
            for c in range(chunks):
                buf, sem = ((buf_a, sem_a), (buf_b, sem_b))[c % 2]
                if c >= 2:
                    pending[c - 2].wait()
                pltpu.sync_copy(t_hbm.at[i_v.at[0, pl.ds(c * SC_CHUNK, SC_CHUNK)]], buf)
                pending.append(pltpu.async_copy(buf, o_hbm.at[pl.ds(off + c * SC_CHUNK, SC_CHUNK)], sem))
            for cp in pending[max(chunks - 2, 0):]:
                cp.wait()

    return gather(table, idx.reshape(1, m))


def _ffn_kernel(exp_ref, rows_ref, after_ref, xs_ref, wg_ref, wu_ref, wd_ref, ys_ref, wg_bf, wu_bf, wd_bf):
    j = pl.program_id(0)

    @pl.when((j == 0) | (exp_ref[j] != exp_ref[jnp.maximum(j - 1, 0)]))
    def _():
        wg_bf[...] = _bf(wg_ref[...])
        wu_bf[...] = _bf(wu_ref[...])
        wd_bf[...] = _bf(wd_ref[...])

    def stages(k, rows):
        packed = xs_ref[rows, :]
        row_id = lax.broadcasted_iota(jnp.int32, packed.shape, 0) + k * SUB_TILE
        x = _bf(_unpack_rows(jnp.where(row_id < rows_ref[j], packed, jnp.uint32(0))))
        g = _dot(x, wg_bf[...])
        u = _dot(x, wu_bf[...])
        yield
        act = _bf(g * jax.nn.sigmoid(g) * u)
        yield
        ys_ref[rows, :] = _pack_rows(_dot(act, wd_bf[...]))

    _run_staggered([stages(k, rows) for k, rows in enumerate(_sub_rows(xs_ref.shape[0]))])


def _combined(x_ref, slab_ref, y0_ref, y1_ref, rows):
    slab = slab_ref[rows, :]
    return (x_ref[rows, :] + slab[:, 2:3] * _unpack_rows(y0_ref[rows, :])
            + slab[:, 3:4] * _unpack_rows(y1_ref[rows, :]))


def _combine_kernel(after_ref, x_ref, slab_ref, y0_ref, y1_ref, out_ref):
    out_ref[...] = _combined(x_ref, slab_ref, y0_ref, y1_ref, slice(None))


def _moe(layer, xn, slabt, counts, w_gate, w_up, w_down, after):
    n, wp = xn.shape
    d = w_gate.shape[-2]
    rt = ROW_TILE
    p_max = 2 * n + N_EXPERTS * rt
    n_tiles = p_max // rt

    cnt = counts[:, 0].astype(jnp.int32)
    padded = ((cnt + rt - 1) // rt) * rt
    experts = jnp.arange(N_EXPERTS, dtype=jnp.int32)
    start = jnp.sum(jnp.where(experts[None, :] < experts[:, None], padded[None, :], 0), axis=1)
    end = start + padded

    def position(e_row, r_row):
        e = e_row.astype(jnp.int32)[None, :]
        return jnp.sum(jnp.where(e == experts[:, None], start[:, None], 0), axis=0) + r_row.astype(jnp.int32)

    pos0 = position(slabt[0], slabt[4])
    pos1 = position(slabt[1], slabt[5])
    tile_start = jnp.arange(n_tiles, dtype=jnp.int32) * rt
    owns = (start[None, :] <= tile_start[:, None]) & (tile_start[:, None] < end[None, :])
    past = tile_start >= end[N_EXPERTS - 1]
    tile_exp = jnp.where(past, N_EXPERTS - 1, jnp.sum(jnp.where(owns, experts[None, :], 0), axis=1))
    used = jnp.sum(jnp.where(owns, (start + cnt)[None, :], 0), axis=1)
    tile_rows = jnp.clip(used - tile_start, 0, rt).astype(jnp.int32)

    xs = _sc_scatter_rows(xn, pos0, pos1, p_max)

    f = w_gate.shape[-1]
    pick = lambda j, ex, rw: (layer, ex[j], 0, 0)
    rows_spec = pl.BlockSpec((rt, wp), lambda j, ex, rw: (j, 0))
    ys = pl.pallas_call(
        _ffn_kernel,
        grid_spec=pltpu.PrefetchScalarGridSpec(
            num_scalar_prefetch=2, grid=(n_tiles,),
            in_specs=[_ANY, rows_spec, pl.BlockSpec((None, None, d, f), pick),
                      pl.BlockSpec((None, None, d, f), pick), pl.BlockSpec((None, None, f, d), pick)],
            out_specs=rows_spec,
            scratch_shapes=[pltpu.VMEM((d, f), jnp.bfloat16), pltpu.VMEM((d, f), jnp.bfloat16),
                            pltpu.VMEM((f, d), jnp.bfloat16)]),
        out_shape=jax.ShapeDtypeStruct((p_max, wp), jnp.uint32),
        compiler_params=_params(("arbitrary",)),
        name="moe_ffn",
    )(tile_exp.astype(jnp.int32), tile_rows, after, xs, w_gate, w_up, w_down)

    return _sc_gather_rows(ys, jnp.concatenate([pos0, pos1])), ys


def _picked_specs(n, wp):
    nb = n // TOKEN_TILE
    return [pl.BlockSpec((TOKEN_TILE, wp), lambda i: (i, 0)),
            pl.BlockSpec((TOKEN_TILE, wp), lambda i: (i + nb, 0))]


def _router_weights(w_group, b_group, w_expert, b_expert):
    d = w_group.shape[0]
    pad = LANE - N_GROUPS - N_EXPERTS
    wr = jnp.concatenate([w_group, w_expert, jnp.zeros((d, pad), w_group.dtype)], axis=1)
    br = jnp.concatenate([b_group, b_expert, jnp.zeros((pad,), b_group.dtype)])
    return _bf(wr), br.reshape(1, LANE).astype(jnp.float32)


def _rope_lanes(vec_half):
    z = jnp.zeros_like(vec_half)
    return jnp.concatenate([vec_half, z, vec_half, z], axis=-1)


def _pad_rope_cols(w):
    z = jnp.zeros(w.shape[:-1] + (ROPE_HALF,), w.dtype)
    return jnp.concatenate([w[..., :ROPE_HALF], z, w[..., ROPE_HALF:], z], axis=-1)


TRUNK_PHASES = 5


def _trunk(chunk, x_all, mem, positions, mem_norm_g, w_mem_kv, mem_qn_g, mem_kn_g, norm1_g,
           norm2_g, a_w_in, a_ln_g, a_ln_b, a_w_s, a_b_s, a_w_out, b_w_in, b_q_norm_g, b_kv_norm_g, b_w_q_up,
           b_w_kv_up, b_qn_g, b_kn_g, b_w_out, moe_w_group, moe_b_group, moe_w_expert, moe_b_expert,
           moe_w_gate, moe_w_up, moe_w_down):
    n_total, d = x_all.shape
    b, s = positions.shape
    m = mem.shape[1]
    n = b * s
    depth = norm1_g.shape[0]
    tt = TOKEN_TILE
    ta = ATTN_TILE
    tiles_per_batch = s // tt
    assert depth == 2 and s % tt == 0 and tt % SUB_TILE == 0 and d == A_GROUPS * LANE
    assert s % ta == 0 and ta % tt == 0
    f32 = jnp.float32
    row = lambda v: v.reshape(1, -1).astype(f32)

    kn_all, mem_v = pl.pallas_call(
        _memkv_kernel,
        grid=(b,),
        in_specs=[pl.BlockSpec((m, d), lambda i: (i, 0)), _const_spec((1, d)),
                  _const_spec((d, 2 * MEM_W)), _const_spec((depth, 1, MEM_HEAD_DIM))],
        out_specs=[pl.BlockSpec((depth, m, MEM_W), lambda i: (0, i, 0)),
                   pl.BlockSpec((m, MEM_W), lambda i: (i, 0))],
        out_shape=[jax.ShapeDtypeStruct((depth, b * m, MEM_W), jnp.bfloat16),
                   jax.ShapeDtypeStruct((b * m, MEM_W), jnp.bfloat16)],
        compiler_params=_params(("arbitrary",)),
        name="mem_kv",
    )(mem.reshape(b * m, d), row(mem_norm_g), _bf(w_mem_kv), mem_kn_g.reshape(depth, 1, MEM_HEAD_DIM))

    tok = lambda width: pl.BlockSpec((tt, width), lambda i: (i, 0))
    kn_spec = lambda layer: pl.BlockSpec((None, m, MEM_W), lambda i: (layer, i // tiles_per_batch, 0))
    mv_spec = pl.BlockSpec((m, MEM_W), lambda i: (i // tiles_per_batch, 0))
    route_out_specs = [tok(d), tok(d // 2), tok(LANE), pl.BlockSpec((SUBLANE, tt), lambda i: (0, i)),
                       pl.BlockSpec((N_EXPERTS, LANE), lambda i: (0, 0))]
    route_out_shape = [jax.ShapeDtypeStruct((n, d), f32), jax.ShapeDtypeStruct((n, d // 2), jnp.uint32),
                       jax.ShapeDtypeStruct((n, LANE), f32), jax.ShapeDtypeStruct((SUBLANE, n), f32),
                       jax.ShapeDtypeStruct((N_EXPERTS, LANE), f32)]
    route_scratch = [pltpu.VMEM((N_EXPERTS, 1), f32)]

    first_block = chunk * (n // tt)
    group_tok = pl.BlockSpec((tt, d), lambda i: (i + first_block, 0))

    wr0, br0 = _router_weights(moe_w_group[0], moe_b_group[0], moe_w_expert[0], moe_b_expert[0])
    a_in = a_w_in.shape[-1]
    bias_s = jnp.repeat(a_b_s[0].T, LANE, axis=1).astype(f32)
    x2, xn, slab, slabt, counts = pl.pallas_call(
        _layer0_kernel,
        grid=(n // tt,),
        in_specs=[group_tok, _const_spec((1, d)), _const_spec((d, a_in)), _const_spec((1, d)),
                  _const_spec((1, d)), _const_spec((A_GROUPS, CHUNK, CHUNK)), _const_spec((CHUNK, d)),
                  kn_spec(0), mv_spec, _const_spec((1, MEM_HEAD_DIM)),
                  _const_spec((d + MEM_W, d)), _const_spec((1, d)), _const_spec((d, LANE)),
                  _const_spec((1, LANE))],
        out_specs=route_out_specs,
        out_shape=route_out_shape,
        scratch_shapes=route_scratch,
        compiler_params=_params(("arbitrary",)),
        name="layer0_mixer",
    )(x_all, row(norm1_g[0]), _bf(a_w_in[0]), row(a_ln_g[0]), row(a_ln_b[0]), _bf(a_w_s[0]), bias_s,
      kn_all, mem_v, row(mem_qn_g[0]), _bf(a_w_out[0]), row(norm2_g[0]), wr0, br0)
    after = yield slab
    picked, ys = _moe(0, xn, slabt, counts, moe_w_gate, moe_w_up, moe_w_down, after)
    after = yield ys

    hq = MLA_HEADS
    o1, o2, o3 = Q_LORA, Q_LORA + KV_LORA, Q_LORA + KV_LORA + ROPE_DIM
    w_in = b_w_in[0]
    w_in_p = jnp.concatenate([w_in[:, :o2], w_in[:, o3:], _pad_rope_cols(w_in[:, o2:o3])], axis=1)
    wq = b_w_q_up[0].reshape(Q_LORA, hq, QK_DIM)
    wq_p = jnp.concatenate([wq[..., :NOPE_DIM], _pad_rope_cols(wq[..., NOPE_DIM:])], axis=-1)
    wq_p = wq_p.reshape(Q_LORA, hq * QK_PAD)
    wkv = b_w_kv_up[0].reshape(KV_LORA, hq, NOPE_DIM + V_DIM)
    wkv_p = jnp.concatenate([wkv[..., :NOPE_DIM].reshape(KV_LORA, hq * NOPE_DIM),
                             wkv[..., NOPE_DIM:].reshape(KV_LORA, hq * V_DIM)], axis=1)
    pad_gain = lambda g: jnp.concatenate([g[:NOPE_DIM], _pad_rope_cols(g[NOPE_DIM:])]).reshape(1, QK_PAD)
    half = jnp.arange(ROPE_HALF, dtype=f32)
    inv = ROPE_BASE ** (-(half * 2.0 / ROPE_DIM))
    inv_l = _rope_lanes(inv).reshape(1, LANE)
    sgn_l = jnp.concatenate([-jnp.ones((2 * ROPE_HALF,), f32), jnp.ones((2 * ROPE_HALF,), f32)]).reshape(1, LANE)

    in_w = w_in_p.shape[1]
    head_spec = lambda width: pl.BlockSpec((None, hq, tt, width),
                                           lambda i: (i // tiles_per_batch, 0, i % tiles_per_batch, 0))
    per_ta = ta // tt

    def vt_index(i):
        t = i % tiles_per_batch
        return (i // tiles_per_batch, 0, t // per_ta, 0, t % per_ta)

    vt_spec = pl.BlockSpec((None, hq, None, V_DIM, tt), vt_index)
    x2, q, k, vt, mem_o = pl.pallas_call(
        _layer1_proj_kernel,
        grid=(n // tt,),
        in_specs=[_ANY, tok(d), tok(LANE), *_picked_specs(n, d // 2),
                  tok(1), _const_spec((1, d)), _const_spec((d, in_w)), _const_spec((1, Q_LORA)),
                  _const_spec((1, KV_LORA)), _const_spec((Q_LORA, hq * QK_PAD)),
                  _const_spec((KV_LORA, hq * (NOPE_DIM + V_DIM))), _const_spec((1, QK_PAD)),
                  _const_spec((1, QK_PAD)), _const_spec((1, LANE)), _const_spec((1, LANE)),
                  kn_spec(1), mv_spec, _const_spec((1, MEM_HEAD_DIM))],
        out_specs=[tok(d), head_spec(QK_PAD), head_spec(QK_PAD), vt_spec, tok(MEM_W)],
        out_shape=[jax.ShapeDtypeStruct((n, d), f32),
                   jax.ShapeDtypeStruct((b, hq, s, QK_PAD), jnp.bfloat16),
                   jax.ShapeDtypeStruct((b, hq, s, QK_PAD), jnp.bfloat16),
                   jax.ShapeDtypeStruct((b, hq, s // ta, V_DIM, ta), jnp.bfloat16),
                   jax.ShapeDtypeStruct((n, MEM_W), jnp.bfloat16)],
        compiler_params=_params(("arbitrary",)),
        name="layer1_proj",
    )(after, x2, slab, picked, picked,
      positions.reshape(n, 1), row(norm1_g[1]), _bf(w_in_p), row(b_q_norm_g[0]), row(b_kv_norm_g[0]),
      _bf(wq_p), _bf(wkv_p), pad_gain(b_qn_g[0]).astype(f32), pad_gain(b_kn_g[0]).astype(f32),
      inv_l, sgn_l, kn_all, mem_v, row(mem_qn_g[1]))

    qb = s // ta
    hp = ATTN_HEADS_PER_STEP
    attn = pl.pallas_call(
        _attn_kernel,
        grid=(b, hq // hp, qb),
        in_specs=[pl.BlockSpec((None, hp, ta, QK_PAD), lambda bi, hi, i: (bi, hi, i, 0)),
                  pl.BlockSpec((None, hp, s, QK_PAD), lambda bi, hi, i: (bi, hi, 0, 0)),
                  pl.BlockSpec((None, hp, qb, V_DIM, ta), lambda bi, hi, i: (bi, hi, 0, 0, 0))],
        out_specs=pl.BlockSpec((ta, hp * V_DIM), lambda bi, hi, i: (bi * qb + i, hi)),
        out_shape=jax.ShapeDtypeStruct((n, hq * V_DIM), jnp.bfloat16),
        scratch_shapes=[pltpu.VMEM((hp, 1, ta), f32), pltpu.VMEM((hp, V_DIM + ATTN_SUM_ROWS, ta), f32)],
        compiler_params=_params(("arbitrary", "arbitrary", "arbitrary")),
        name="causal_attention",
    )(q, k, vt)

    wr1, br1 = _router_weights(moe_w_group[1], moe_b_group[1], moe_w_expert[1], moe_b_expert[1])
    x2, xn, slab, slabt, counts = pl.pallas_call(
        _layer1_out_kernel,
        grid=(n // tt,),
        in_specs=[tok(d), tok(hq * V_DIM), tok(MEM_W), _const_spec((hq * V_DIM + MEM_W, d)),
                  _const_spec((1, d)), _const_spec((d, LANE)), _const_spec((1, LANE))],
        out_specs=route_out_specs,
        out_shape=route_out_shape,
        scratch_shapes=route_scratch,
        compiler_params=_params(("arbitrary",)),
        name="layer1_out",
    )(x2, attn, mem_o, _bf(b_w_out[0]), row(norm2_g[1]), wr1, br1)
    after = yield slab
    picked, ys = _moe(1, xn, slabt, counts, moe_w_gate, moe_w_up, moe_w_down, after)
    after = yield ys
    yield pl.pallas_call(
        _combine_kernel,
        grid=(n // tt,),
        in_specs=[_ANY, tok(d), tok(LANE), *_picked_specs(n, d // 2)],
        out_specs=group_tok,
        out_shape=jax.ShapeDtypeStruct((n_total, d), f32),
        input_output_aliases={0: 0} if chunk else {},
        compiler_params=_params(("arbitrary",)),
        name="moe_combine",
    )(after, x2, slab, picked, picked)


def kernel(x, mem, positions, mem_norm_g, w_mem_kv, mem_qn_g, mem_kn_g, norm1_g, norm2_g, a_w_in, a_ln_g, a_ln_b, a_w_s, a_b_s, a_w_out, b_w_in, b_q_norm_g, b_kv_norm_g, b_w_q_up, b_w_kv_up, b_qn_g, b_kn_g, b_w_out, moe_w_group, moe_b_group, moe_w_expert, moe_b_expert, moe_w_gate, moe_w_up, moe_w_down):
    params = (mem_norm_g, w_mem_kv, mem_qn_g, mem_kn_g, norm1_g, norm2_g, a_w_in, a_ln_g, a_ln_b, a_w_s,
              a_b_s, a_w_out, b_w_in, b_q_norm_g, b_kv_norm_g, b_w_q_up, b_w_kv_up, b_qn_g, b_kn_g, b_w_out,
              moe_w_group, moe_b_group, moe_w_expert, moe_b_expert, moe_w_gate, moe_w_up, moe_w_down)
    b, s, d = x.shape
    assert BATCH_GROUPS == 2 and b % BATCH_GROUPS == 0
    g = b // BATCH_GROUPS
    x_all = x.reshape(b * s, d)
    groups = [_trunk(c, x_all, mem[c * g:(c + 1) * g], positions[c * g:(c + 1) * g], *params)
              for c in range(BATCH_GROUPS)]
    done = [next(t) for t in groups]
    for _ in range(TRUNK_PHASES - 1):
        prev, done = done, []
        for c, t in enumerate(groups):
            done.append(t.send(prev[c + 1] if c + 1 < BATCH_GROUPS else done[0]))
    return done[-1].reshape(b, s, d)
```

```python
import jax
import jax.numpy as jnp
from jax import lax
from jax.experimental import pallas as pl
from jax.experimental.pallas import tpu as pltpu
from jax.experimental.pallas import tpu_sc as plsc

EPS = 1e-6
LANE = 128
SUBLANE = 8
MEM_HEADS = 4
MEM_HEAD_DIM = 128
MEM_W = MEM_HEADS * MEM_HEAD_DIM
CHUNK = 128
A_GROUPS = 8
MLA_HEADS = 8
Q_LORA = 512
KV_LORA = 256
NOPE_DIM = 128
ROPE_DIM = 64
ROPE_HALF = ROPE_DIM // 2
V_DIM = 128
QK_DIM = NOPE_DIM + ROPE_DIM
QK_PAD = 2 * LANE
ROPE_BASE = 10000.0
N_GROUPS = 4
EXPERTS_PER_GROUP = 8
N_EXPERTS = N_GROUPS * EXPERTS_PER_GROUP
ROUTE_ROWS = 40
EXPERT_FF = 256
LOG2E = 1.4426950408889634

BATCH_GROUPS = 2
TOKEN_TILE = 512
SUB_TILE = 256
ROW_TILE = 512
ATTN_TILE = 512
ATTN_HEADS_PER_STEP = 2
ATTN_KEY_TILES_PER_TRIP = 4
ATTN_QUERY_SPLIT = 2
ATTN_LOOKAHEAD = 3
ATTN_SUM_ROWS = 16
VMEM_LIMIT = 56 * 1024 * 1024
NEG_BIG = -1e30

SC_CORES = 2
SC_SUBCORES = 16
SC_WORKERS = SC_CORES * SC_SUBCORES
SC_INDEX_GROUP = 128
SC_CHUNK = 64

_NT = (((1,), (1,)), ((), ()))
_ANY = pl.BlockSpec(memory_space=pl.ANY)


def _const_spec(shape):
    nd = len(shape)
    return pl.BlockSpec(shape, lambda *_: (0,) * nd, pipeline_mode=pl.Buffered(1))


def _params(sem):
    return pltpu.CompilerParams(dimension_semantics=sem, vmem_limit_bytes=VMEM_LIMIT)


def _run_staggered(gens):
    waiting = list(gens)
    active = []
    while waiting or active:
        if waiting:
            active.append(waiting.pop(0))
        for g in list(active):
            try:
                next(g)
            except StopIteration:
                active.remove(g)


def _sub_rows(t):
    return [pl.ds(k * SUB_TILE, SUB_TILE) for k in range(t // SUB_TILE)]


def _rms(x, g):
    return x * lax.rsqrt(jnp.mean(x * x, axis=-1, keepdims=True) + EPS) * g


def _gelu(x):
    return 0.5 * x * (1.0 + lax.erf(x * (2.0 ** -0.5)))


def _bf(x):
    return x.astype(jnp.bfloat16)


def _dot(a, b):
    return jnp.dot(a, b, preferred_element_type=jnp.float32)


def _memkv_kernel(mem_ref, g_ref, w_ref, kng_ref, kn_ref, v_ref):
    h = _bf(_rms(mem_ref[...], g_ref[...]))
    kv = _dot(h, w_ref[...])
    v_ref[...] = _bf(kv[:, MEM_W:])
    for layer in range(kn_ref.shape[0]):
        g = kng_ref[layer]
        for hh in range(MEM_HEADS):
            k = kv[:, hh * LANE:(hh + 1) * LANE]
            kn_ref[layer, :, hh * LANE:(hh + 1) * LANE] = _bf(_rms(k, g))


def _mem_attention(qm, kn_ref, v_ref, qg):
    outs = []
    for hh in range(MEM_HEADS):
        sl = slice(hh * LANE, (hh + 1) * LANE)
        q = _rms(qm[:, sl], qg) * (MEM_HEAD_DIM ** -0.5)
        s = lax.dot_general(_bf(q), kn_ref[:, sl], _NT, preferred_element_type=jnp.float32)
        p = jnp.exp(s - jnp.max(s, axis=-1, keepdims=True))
        l = jnp.sum(p, axis=-1, keepdims=True)
        outs.append(_dot(_bf(p), v_ref[:, sl]) / l)
    return jnp.concatenate(outs, axis=-1)


def _pack_rows(x):
    w = x.shape[1] // 2
    bits = lambda v: lax.bitcast_convert_type(_bf(v).astype(jnp.float32), jnp.uint32)
    return (bits(x[:, :w]) >> 16) | (bits(x[:, w:]) & jnp.uint32(0xFFFF0000))


def _unpack_rows(p):
    lo = lax.bitcast_convert_type(p << 16, jnp.float32)
    hi = lax.bitcast_convert_type(p & jnp.uint32(0xFFFF0000), jnp.float32)
    return jnp.concatenate([lo, hi], axis=-1)


def _route_stages(x, rows, g2_ref, wr_ref, br_ref, carry_ref, xn_ref, slab_ref, slabt_ref, cnt_ref):
    t = x.shape[0]
    xn = _rms(x, g2_ref[...])
    xn_ref[rows, :] = _pack_rows(xn)
    logits = _dot(_bf(xn), wr_ref[...]) + br_ref[...]
    yield
    lt = logits.T[:ROUTE_ROWS, :]
    row = lax.broadcasted_iota(jnp.int32, lt.shape, 0)

    def first_max(v):
        m = jnp.max(v, axis=0, keepdims=True)
        idx = jnp.min(jnp.where(v == m, row, ROUTE_ROWS), axis=0, keepdims=True)
        return m, idx

    lg = jnp.where(row < N_GROUPS, lt, NEG_BIG)
    gmax, gidx = first_max(lg)
    g_w = 1.0 / jnp.sum(jnp.exp(lg - gmax), axis=0, keepdims=True)

    eid = row - N_GROUPS
    in_grp = (eid >= 0) & (eid < N_EXPERTS) & ((eid >> 3) == gidx)
    le = jnp.where(in_grp, lt, NEG_BIG)
    m1, i1 = first_max(le)
    m2, i2 = first_max(jnp.where(row == i1, NEG_BIG, le))
    r = jnp.exp(m2 - m1)
    w1 = g_w / (1.0 + r)
    w2 = w1 * r
    e1 = i1 - N_GROUPS
    e2 = i2 - N_GROUPS
    yield

    expert = lax.broadcasted_iota(jnp.int32, (N_EXPERTS, t), 0)
    oh1 = expert == e1
    oh2 = expert == e2
    oh = jnp.where(oh1 | oh2, 1.0, 0.0)
    src = lax.broadcasted_iota(jnp.int32, (t, t), 0)
    dst = lax.broadcasted_iota(jnp.int32, (t, t), 1)
    earlier = jnp.where(src < dst, 1.0, 0.0).astype(jnp.bfloat16)
    before = _dot(_bf(oh), earlier) + carry_ref[...]
    r1 = jnp.sum(jnp.where(oh1, before, 0.0), axis=0, keepdims=True)
    r2 = jnp.sum(jnp.where(oh2, before, 0.0), axis=0, keepdims=True)
    carry_ref[...] += jnp.sum(oh, axis=1, keepdims=True)
    cnt_ref[...] = jnp.broadcast_to(carry_ref[...], cnt_ref.shape)

    table = jnp.concatenate([e1.astype(jnp.float32), e2.astype(jnp.float32), w1, w2, r1, r2,
                             jnp.zeros((LANE - 6, t), jnp.float32)], axis=0)
    slabt_ref[:, rows] = table[:SUBLANE, :]
    slab_ref[rows, :] = table.T


def _reset_carry(carry_ref):
    @pl.when(pl.program_id(0) == 0)
    def _():
        carry_ref[...] = jnp.zeros_like(carry_ref)


def _layer0_kernel(x_ref, g1_ref, win_ref, lng_ref, lnb_ref, ws_ref, bs_ref, kn_ref, v_ref, qg_ref,
                   wout_ref, g2_ref, wr_ref, br_ref,
                   xo_ref, xn_ref, slab_ref, slabt_ref, cnt_ref, carry_ref):
    d = x_ref.shape[1]
    _reset_carry(carry_ref)
    row = lax.broadcasted_iota(jnp.int32, (CHUNK, CHUNK), 0)
    col = lax.broadcasted_iota(jnp.int32, (CHUNK, CHUNK), 1)
    causal = row >= col

    def stages(rows):
        x = x_ref[rows, :]
        t = x.shape[0]
        h = _bf(_rms(x, g1_ref[...]))
        z = _dot(h, win_ref[...])
        yield
        u = _gelu(z[:, :d])
        v = _gelu(z[:, d:2 * d])
        mu = jnp.mean(v, axis=-1, keepdims=True)
        vc = v - mu
        var = jnp.mean(vc * vc, axis=-1, keepdims=True)
        v = _bf(vc * lax.rsqrt(var + EPS) * lng_ref[...] + lnb_ref[...])
        yield
        chunks = []
        for c in range(t // CHUNK):
            cols = []
            for g in range(A_GROUPS):
                w = jnp.where(causal, ws_ref[g], jnp.zeros((), ws_ref.dtype))
                cols.append(_dot(w, v[c * CHUNK:(c + 1) * CHUNK, g * LANE:(g + 1) * LANE]))
            chunks.append(jnp.concatenate(cols, axis=-1) + bs_ref[...])
        mix = _bf(u * jnp.concatenate(chunks, axis=0))
        yield
        mem = _bf(_mem_attention(z[:, 2 * d:], kn_ref, v_ref, qg_ref[...]))
        yield
        xo = x + _dot(mix, wout_ref[:d, :]) + _dot(mem, wout_ref[d:, :])
        xo_ref[rows, :] = xo
        yield
        yield from _route_stages(xo, rows, g2_ref, wr_ref, br_ref, carry_ref,
                                 xn_ref, slab_ref, slabt_ref, cnt_ref)

    _run_staggered([stages(rows) for rows in _sub_rows(x_ref.shape[0])])


def _layer1_proj_kernel(after_ref, x_ref, slab_ref, y0_ref, y1_ref, pos_ref, g1_ref, win_ref, qng_ref, kvng_ref,
                        wq_ref, wkv_ref, qg_ref, kg_ref, inv_ref, sgn_ref, kn_ref, mv_ref, mqg_ref,
                        xo_ref, q_ref, k_ref, v_ref, mem_ref):
    o1 = Q_LORA
    o2 = o1 + KV_LORA
    o3 = o2 + MEM_W
    q_scale = (QK_DIM ** -0.5) * LOG2E

    def stages(rows):
        x = _combined(x_ref, slab_ref, y0_ref, y1_ref, rows)
        xo_ref[rows, :] = x
        h = _bf(_rms(x, g1_ref[...]))
        z = _dot(h, win_ref[...])
        yield
        cq = _bf(_rms(z[:, :o1], qng_ref[...]))
        ckv = _bf(_rms(z[:, o1:o2], kvng_ref[...]))
        k_rope = z[:, o3:o3 + LANE]
        ang = pos_ref[rows, :].astype(jnp.float32) * inv_ref[...]
        cos = jnp.cos(ang)
        sin = jnp.sin(ang) * sgn_ref[...]

        def rope(r):
            return r * cos + pltpu.roll(r, LANE // 2, 1) * sin

        q = _dot(cq, wq_ref[...])
        kv = _dot(ckv, wkv_ref[...])
        yield
        qg = qg_ref[...]
        kg = kg_ref[...]
        kr_ss = jnp.sum(k_rope * k_rope, axis=-1, keepdims=True)
        kr = rope(k_rope * kg[:, LANE:])
        for hh in range(MLA_HEADS):
            qh = q[:, hh * QK_PAD:(hh + 1) * QK_PAD]
            rq = lax.rsqrt(jnp.sum(qh * qh, axis=-1, keepdims=True) * (1.0 / QK_DIM) + EPS) * q_scale
            qh = qh * rq * qg
            q_ref[hh, rows, :LANE] = _bf(qh[:, :LANE])
            q_ref[hh, rows, LANE:] = _bf(rope(qh[:, LANE:]))
            kn = kv[:, hh * LANE:(hh + 1) * LANE]
            rk = lax.rsqrt((jnp.sum(kn * kn, axis=-1, keepdims=True) + kr_ss) * (1.0 / QK_DIM) + EPS)
            k_ref[hh, rows, :LANE] = _bf(kn * rk * kg[:, :LANE])
            k_ref[hh, rows, LANE:] = _bf(kr * rk)
            v_ref[hh, :, rows] = _bf(kv[:, (MLA_HEADS + hh) * LANE:(MLA_HEADS + hh + 1) * LANE].T)
            if hh % 4 == 3:
                yield
        mem_ref[rows, :] = _bf(_mem_attention(z[:, o2:o3], kn_ref, mv_ref, mqg_ref[...]))

    _run_staggered([stages(rows) for rows in _sub_rows(x_ref.shape[0])])


def _attn_kernel(q_ref, k_ref, vt_ref, o_ref, m_ref, acc_ref):
    i = pl.program_id(2)
    heads, tq = q_ref.shape[0], q_ref.shape[1]
    m_ref[...] = jnp.full_like(m_ref, NEG_BIG)
    acc_ref[...] = jnp.zeros_like(acc_ref)

    qw = tq // ATTN_QUERY_SPLIT

    def scores(hh, j, part, masked):
        keys = (part + 1) * qw if masked else tq
        start = pl.multiple_of(j * tq, tq)
        s = lax.dot_general(k_ref[hh, pl.ds(start, keys), :], q_ref[hh, part * qw:(part + 1) * qw, :], _NT,
                            preferred_element_type=jnp.float32)
        if masked:
            key = lax.broadcasted_iota(jnp.int32, (keys, qw), 0)
            qry = lax.broadcasted_iota(jnp.int32, (keys, qw), 1) + part * qw
            s = jnp.where(key <= qry, s, NEG_BIG)
        return s

    def update(hh, j, part, s):
        keys = s.shape[0]
        lanes = slice(part * qw, (part + 1) * qw)
        m = m_ref[hh, :, lanes]
        m_new = jnp.maximum(m, jnp.max(s, axis=0, keepdims=True))
        alpha = jnp.exp2(m - m_new)
        p = _bf(jnp.exp2(s - m_new))
        m_ref[hh, :, lanes] = m_new
        vt1 = jnp.concatenate([vt_ref[hh, j, :, :keys], jnp.ones((ATTN_SUM_ROWS, keys), jnp.bfloat16)], axis=0)
        acc_ref[hh, :, lanes] = alpha * acc_ref[hh, :, lanes] + _dot(vt1, p)

    def run(items):
        ss = {}
        ahead = ATTN_LOOKAHEAD
        for t in range(len(items) + ahead):
            if t < len(items):
                ss[t] = scores(*items[t])
            if t >= ahead:
                update(*items[t - ahead][:3], ss.pop(t - ahead))

    def tile_items(first, count, masked):
        return [(hh, first + u, part, masked) for u in range(count) for hh in range(heads)
                for part in range(ATTN_QUERY_SPLIT)]

    unroll = ATTN_KEY_TILES_PER_TRIP

    def body(jj, c):
        run(tile_items(unroll * jj, unroll, False))
        return c

    lax.fori_loop(0, i // unroll, body, 0)
    for left in range(unroll):
        @pl.when(i % unroll == left)
        def _():
            run(tile_items(i - left, left, False) + tile_items(i, 1, True))

    for hh in range(heads):
        acc = acc_ref[hh]
        o_ref[:, hh * V_DIM:(hh + 1) * V_DIM] = _bf((acc[:V_DIM] / acc[V_DIM:V_DIM + 1]).T)


def _layer1_out_kernel(x_ref, o_ref, mem_ref, wout_ref, g2_ref, wr_ref, br_ref,
                       xo_ref, xn_ref, slab_ref, slabt_ref, cnt_ref, carry_ref):
    d = o_ref.shape[1]
    _reset_carry(carry_ref)

    def stages(rows):
        xo = (x_ref[rows, :] + _dot(o_ref[rows, :], wout_ref[:d, :])
              + _dot(mem_ref[rows, :], wout_ref[d:, :]))
        xo_ref[rows, :] = xo
        yield
        yield from _route_stages(xo, rows, g2_ref, wr_ref, br_ref, carry_ref,
                                 xn_ref, slab_ref, slabt_ref, cnt_ref)

    _run_staggered([stages(rows) for rows in _sub_rows(x_ref.shape[0])])


def _sc_mesh():
    return plsc.VectorSubcoreMesh(core_axis_name="c", subcore_axis_name="s")


def _sc_worker_base(rows_per_worker):
    return (lax.axis_index("c") * SC_SUBCORES + lax.axis_index("s")) * rows_per_worker


def _sc_scatter_rows(x, idx0, idx1, p_rows):
    n, w = x.shape
    per = n // SC_WORKERS
    chunks = SC_INDEX_GROUP // SC_CHUNK
    assert n % SC_WORKERS == 0 and per % SC_INDEX_GROUP == 0

    @pl.kernel(out_type=jax.ShapeDtypeStruct((p_rows, w), x.dtype), mesh=_sc_mesh(),
               scratch_types=[pltpu.VMEM((1, SC_INDEX_GROUP), jnp.int32),
                              pltpu.VMEM((1, SC_INDEX_GROUP), jnp.int32),
                              pltpu.VMEM((SC_CHUNK, w), x.dtype), pltpu.VMEM((SC_CHUNK, w), x.dtype),
                              pltpu.SemaphoreType.DMA, pltpu.SemaphoreType.DMA],
               name="moe_dispatch_sc")
    def scatter(x_hbm, i0_hbm, i1_hbm, o_hbm, i0_v, i1_v, buf_a, buf_b, sem_a, sem_b):
        base = _sc_worker_base(per)

        @pl.loop(0, per // SC_INDEX_GROUP)
        def _(g):
            off = pl.multiple_of(base + g * SC_INDEX_GROUP, SC_INDEX_GROUP)
            pltpu.sync_copy(i0_hbm.at[:, pl.ds(off, SC_INDEX_GROUP)], i0_v)
            pltpu.sync_copy(i1_hbm.at[:, pl.ds(off, SC_INDEX_GROUP)], i1_v)
            pending = []
            for c in range(chunks):
                buf, sem = ((buf_a, sem_a), (buf_b, sem_b))[c % 2]
                if c >= 2:
                    for cp in pending[c - 2]:
                        cp.wait()
                pltpu.sync_copy(x_hbm.at[pl.ds(off + c * SC_CHUNK, SC_CHUNK)], buf)
                sl = pl.ds(c * SC_CHUNK, SC_CHUNK)
                pending.append((pltpu.async_copy(buf, o_hbm.at[i0_v.at[0, sl]], sem),
                                pltpu.async_copy(buf, o_hbm.at[i1_v.at[0, sl]], sem)))
            for cps in pending[max(chunks - 2, 0):]:
                for cp in cps:
                    cp.wait()

    return scatter(x, idx0.reshape(1, n), idx1.reshape(1, n))


def _sc_gather_rows(table, idx):
    m = idx.shape[0]
    w = table.shape[1]
    per = m // SC_WORKERS
    chunks = SC_INDEX_GROUP // SC_CHUNK
    assert m % SC_WORKERS == 0 and per % SC_INDEX_GROUP == 0

    @pl.kernel(out_type=jax.ShapeDtypeStruct((m, w), table.dtype), mesh=_sc_mesh(),
               scratch_types=[pltpu.VMEM((1, SC_INDEX_GROUP), jnp.int32),
                              pltpu.VMEM((SC_CHUNK, w), table.dtype), pltpu.VMEM((SC_CHUNK, w), table.dtype),
                              pltpu.SemaphoreType.DMA, pltpu.SemaphoreType.DMA],
               name="moe_combine_sc")
    def gather(t_hbm, i_hbm, o_hbm, i_v, buf_a, buf_b, sem_a, sem_b):
        base = _sc_worker_base(per)

        @pl.loop(0, per // SC_INDEX_GROUP)
        def _(g):
            off = pl.multiple_of(base + g * SC_INDEX_GROUP, SC_INDEX_GROUP)
            pltpu.sync_copy(i_hbm.at[:, pl.ds(off, SC_INDEX_GROUP)], i_v)
            pending = []
            for c in range(chunks):
                buf, sem = ((buf_a, sem_a), (buf_b, sem_b))[c % 2]
                if c >= 2:
                    pending[c - 2].wait()
                pltpu.sync_copy(t_hbm.at[i_v.at[0, pl.ds(c * SC_CHUNK, SC_CHUNK)]], buf)
                pending.append(pltpu.async_copy(buf, o_hbm.at[pl.ds(off + c * SC_CHUNK, SC_CHUNK)], sem))
            for cp in pending[max(chunks - 2, 0):]:
                cp.wait()

    return gather(table, idx.reshape(1, m))


def _ffn_kernel(exp_ref, rows_ref, after_ref, xs_ref, wg_ref, wu_ref, wd_ref, ys_ref, wg_bf, wu_bf, wd_bf):
    j = pl.program_id(0)

    @pl.when((j == 0) | (exp_ref[j] != exp_ref[jnp.maximum(j - 1, 0)]))
    def _():
        wg_bf[...] = _bf(wg_ref[...])
        wu_bf[...] = _bf(wu_ref[...])
        wd_bf[...] = _bf(wd_ref[...])

    def stages(k, rows):
        packed = xs_ref[rows, :]
        row_id = lax.broadcasted_iota(jnp.int32, packed.shape, 0) + k * SUB_TILE
        x = _bf(_unpack_rows(jnp.where(row_id < rows_ref[j], packed, jnp.uint32(0))))
        g = _dot(x, wg_bf[...])
        u = _dot(x, wu_bf[...])
        yield
        act = _bf(g * jax.nn.sigmoid(g) * u)
        yield
        ys_ref[rows, :] = _pack_rows(_dot(act, wd_bf[...]))

    _run_staggered([stages(k, rows) for k, rows in enumerate(_sub_rows(xs_ref.shape[0]))])


def _combined(x_ref, slab_ref, y0_ref, y1_ref, rows):
    slab = slab_ref[rows, :]
    return (x_ref[rows, :] + slab[:, 2:3] * _unpack_rows(y0_ref[rows, :])
            + slab[:, 3:4] * _unpack_rows(y1_ref[rows, :]))


def _combine_kernel(after_ref, x_ref, slab_ref, y0_ref, y1_ref, out_ref):
    out_ref[...] = _combined(x_ref, slab_ref, y0_ref, y1_ref, slice(None))


def _moe(layer, xn, slabt, counts, w_gate, w_up, w_down, after):
    n, wp = xn.shape
    d = w_gate.shape[-2]
    rt = ROW_TILE
    p_max = 2 * n + N_EXPERTS * rt
    n_tiles = p_max // rt

    cnt = counts[:, 0].astype(jnp.int32)
    padded = ((cnt + rt - 1) // rt) * rt
    experts = jnp.arange(N_EXPERTS, dtype=jnp.int32)
    start = jnp.sum(jnp.where(experts[None, :] < experts[:, None], padded[None, :], 0), axis=1)
    end = start + padded

    def position(e_row, r_row):
        e = e_row.astype(jnp.int32)[None, :]
        return jnp.sum(jnp.where(e == experts[:, None], start[:, None], 0), axis=0) + r_row.astype(jnp.int32)

    pos0 = position(slabt[0], slabt[4])
    pos1 = position(slabt[1], slabt[5])
    tile_start = jnp.arange(n_tiles, dtype=jnp.int32) * rt
    owns = (start[None, :] <= tile_start[:, None]) & (tile_start[:, None] < end[None, :])
    past = tile_start >= end[N_EXPERTS - 1]
    tile_exp = jnp.where(past, N_EXPERTS - 1, jnp.sum(jnp.where(owns, experts[None, :], 0), axis=1))
    used = jnp.sum(jnp.where(owns, (start + cnt)[None, :], 0), axis=1)
    tile_rows = jnp.clip(used - tile_start, 0, rt).astype(jnp.int32)

    xs = _sc_scatter_rows(xn, pos0, pos1, p_max)

    f = w_gate.shape[-1]
    pick = lambda j, ex, rw: (layer, ex[j], 0, 0)
    rows_spec = pl.BlockSpec((rt, wp), lambda j, ex, rw: (j, 0))
    ys = pl.pallas_call(
        _ffn_kernel,
        grid_spec=pltpu.PrefetchScalarGridSpec(
            num_scalar_prefetch=2, grid=(n_tiles,),
            in_specs=[_ANY, rows_spec, pl.BlockSpec((None, None, d, f), pick),
                      pl.BlockSpec((None, None, d, f), pick), pl.BlockSpec((None, None, f, d), pick)],
            out_specs=rows_spec,
            scratch_shapes=[pltpu.VMEM((d, f), jnp.bfloat16), pltpu.VMEM((d, f), jnp.bfloat16),
                            pltpu.VMEM((f, d), jnp.bfloat16)]),
        out_shape=jax.ShapeDtypeStruct((p_max, wp), jnp.uint32),
        compiler_params=_params(("arbitrary",)),
        name="moe_ffn",
    )(tile_exp.astype(jnp.int32), tile_rows, after, xs, w_gate, w_up, w_down)

    return _sc_gather_rows(ys, jnp.concatenate([pos0, pos1])), ys


def _picked_specs(n, wp):
    nb = n // TOKEN_TILE
    return [pl.BlockSpec((TOKEN_TILE, wp), lambda i: (i, 0)),
            pl.BlockSpec((TOKEN_TILE, wp), lambda i: (i + nb, 0))]


def _router_weights(w_group, b_group, w_expert, b_expert):
    d = w_group.shape[0]
    pad = LANE - N_GROUPS - N_EXPERTS
    wr = jnp.concatenate([w_group, w_expert, jnp.zeros((d, pad), w_group.dtype)], axis=1)
    br = jnp.concatenate([b_group, b_expert, jnp.zeros((pad,), b_group.dtype)])
    return _bf(wr), br.reshape(1, LANE).astype(jnp.float32)


def _rope_lanes(vec_half):
    z = jnp.zeros_like(vec_half)
    return jnp.concatenate([vec_half, z, vec_half, z], axis=-1)


def _pad_rope_cols(w):
    z = jnp.zeros(w.shape[:-1] + (ROPE_HALF,), w.dtype)
    return jnp.concatenate([w[..., :ROPE_HALF], z, w[..., ROPE_HALF:], z], axis=-1)


TRUNK_PHASES = 5


def _trunk(chunk, x_all, mem, positions, mem_norm_g, w_mem_kv, mem_qn_g, mem_kn_g, norm1_g,
           norm2_g, a_w_in, a_ln_g, a_ln_b, a_w_s, a_b_s, a_w_out, b_w_in, b_q_norm_g, b_kv_norm_g, b_w_q_up,
           b_w_kv_up, b_qn_g, b_kn_g, b_w_out, moe_w_group, moe_b_group, moe_w_expert, moe_b_expert,
           moe_w_gate, moe_w_up, moe_w_down):
    n_total, d = x_all.shape
    b, s = positions.shape
    m = mem.shape[1]
    n = b * s
    depth = norm1_g.shape[0]
    tt = TOKEN_TILE
    ta = ATTN_TILE
    tiles_per_batch = s // tt
    assert depth == 2 and s % tt == 0 and tt % SUB_TILE == 0 and d == A_GROUPS * LANE
    assert s % ta == 0 and ta % tt == 0
    f32 = jnp.float32
    row = lambda v: v.reshape(1, -1).astype(f32)

    kn_all, mem_v = pl.pallas_call(
        _memkv_kernel,
        grid=(b,),
        in_specs=[pl.BlockSpec((m, d), lambda i: (i, 0)), _const_spec((1, d)),
                  _const_spec((d, 2 * MEM_W)), _const_spec((depth, 1, MEM_HEAD_DIM))],
        out_specs=[pl.BlockSpec((depth, m, MEM_W), lambda i: (0, i, 0)),
                   pl.BlockSpec((m, MEM_W), lambda i: (i, 0))],
        out_shape=[jax.ShapeDtypeStruct((depth, b * m, MEM_W), jnp.bfloat16),
                   jax.ShapeDtypeStruct((b * m, MEM_W), jnp.bfloat16)],
        compiler_params=_params(("arbitrary",)),
        name="mem_kv",
    )(mem.reshape(b * m, d), row(mem_norm_g), _bf(w_mem_kv), mem_kn_g.reshape(depth, 1, MEM_HEAD_DIM))

    tok = lambda width: pl.BlockSpec((tt, width), lambda i: (i, 0))
    kn_spec = lambda layer: pl.BlockSpec((None, m, MEM_W), lambda i: (layer, i // tiles_per_batch, 0))
    mv_spec = pl.BlockSpec((m, MEM_W), lambda i: (i // tiles_per_batch, 0))
    route_out_specs = [tok(d), tok(d // 2), tok(LANE), pl.BlockSpec((SUBLANE, tt), lambda i: (0, i)),
                       pl.BlockSpec((N_EXPERTS, LANE), lambda i: (0, 0))]
    route_out_shape = [jax.ShapeDtypeStruct((n, d), f32), jax.ShapeDtypeStruct((n, d // 2), jnp.uint32),
                       jax.ShapeDtypeStruct((n, LANE), f32), jax.ShapeDtypeStruct((SUBLANE, n), f32),
                       jax.ShapeDtypeStruct((N_EXPERTS, LANE), f32)]
    route_scratch = [pltpu.VMEM((N_EXPERTS, 1), f32)]

    first_block = chunk * (n // tt)
    group_tok = pl.BlockSpec((tt, d), lambda i: (i + first_block, 0))

    wr0, br0 = _router_weights(moe_w_group[0], moe_b_group[0], moe_w_expert[0], moe_b_expert[0])
    a_in = a_w_in.shape[-1]
    bias_s = jnp.repeat(a_b_s[0].T, LANE, axis=1).astype(f32)
    x2, xn, slab, slabt, counts = pl.pallas_call(
        _layer0_kernel,
        grid=(n // tt,),
        in_specs=[group_tok, _const_spec((1, d)), _const_spec((d, a_in)), _const_spec((1, d)),
                  _const_spec((1, d)), _const_spec((A_GROUPS, CHUNK, CHUNK)), _const_spec((CHUNK, d)),
                  kn_spec(0), mv_spec, _const_spec((1, MEM_HEAD_DIM)),
                  _const_spec((d + MEM_W, d)), _const_spec((1, d)), _const_spec((d, LANE)),
                  _const_spec((1, LANE))],
        out_specs=route_out_specs,
        out_shape=route_out_shape,
        scratch_shapes=route_scratch,
        compiler_params=_params(("arbitrary",)),
        name="layer0_mixer",
    )(x_all, row(norm1_g[0]), _bf(a_w_in[0]), row(a_ln_g[0]), row(a_ln_b[0]), _bf(a_w_s[0]), bias_s,
      kn_all, mem_v, row(mem_qn_g[0]), _bf(a_w_out[0]), row(norm2_g[0]), wr0, br0)
    after = yield slab
    picked, ys = _moe(0, xn, slabt, counts, moe_w_gate, moe_w_up, moe_w_down, after)
    after = yield ys

    hq = MLA_HEADS
    o1, o2, o3 = Q_LORA, Q_LORA + KV_LORA, Q_LORA + KV_LORA + ROPE_DIM
    w_in = b_w_in[0]
    w_in_p = jnp.concatenate([w_in[:, :o2], w_in[:, o3:], _pad_rope_cols(w_in[:, o2:o3])], axis=1)
    wq = b_w_q_up[0].reshape(Q_LORA, hq, QK_DIM)
    wq_p = jnp.concatenate([wq[..., :NOPE_DIM], _pad_rope_cols(wq[..., NOPE_DIM:])], axis=-1)
    wq_p = wq_p.reshape(Q_LORA, hq * QK_PAD)
    wkv = b_w_kv_up[0].reshape(KV_LORA, hq, NOPE_DIM + V_DIM)
    wkv_p = jnp.concatenate([wkv[..., :NOPE_DIM].reshape(KV_LORA, hq * NOPE_DIM),
                             wkv[..., NOPE_DIM:].reshape(KV_LORA, hq * V_DIM)], axis=1)
    pad_gain = lambda g: jnp.concatenate([g[:NOPE_DIM], _pad_rope_cols(g[NOPE_DIM:])]).reshape(1, QK_PAD)
    half = jnp.arange(ROPE_HALF, dtype=f32)
    inv = ROPE_BASE ** (-(half * 2.0 / ROPE_DIM))
    inv_l = _rope_lanes(inv).reshape(1, LANE)
    sgn_l = jnp.concatenate([-jnp.ones((2 * ROPE_HALF,), f32), jnp.ones((2 * ROPE_HALF,), f32)]).reshape(1, LANE)

    in_w = w_in_p.shape[1]
    head_spec = lambda width: pl.BlockSpec((None, hq, tt, width),
                                           lambda i: (i // tiles_per_batch, 0, i % tiles_per_batch, 0))
    per_ta = ta // tt

    def vt_index(i):
        t = i % tiles_per_batch
        return (i // tiles_per_batch, 0, t // per_ta, 0, t % per_ta)

    vt_spec = pl.BlockSpec((None, hq, None, V_DIM, tt), vt_index)
    x2, q, k, vt, mem_o = pl.pallas_call(
        _layer1_proj_kernel,
        grid=(n // tt,),
        in_specs=[_ANY, tok(d), tok(LANE), *_picked_specs(n, d // 2),
                  tok(1), _const_spec((1, d)), _const_spec((d, in_w)), _const_spec((1, Q_LORA)),
                  _const_spec((1, KV_LORA)), _const_spec((Q_LORA, hq * QK_PAD)),
                  _const_spec((KV_LORA, hq * (NOPE_DIM + V_DIM))), _const_spec((1, QK_PAD)),
                  _const_spec((1, QK_PAD)), _const_spec((1, LANE)), _const_spec((1, LANE)),
                  kn_spec(1), mv_spec, _const_spec((1, MEM_HEAD_DIM))],
        out_specs=[tok(d), head_spec(QK_PAD), head_spec(QK_PAD), vt_spec, tok(MEM_W)],
        out_shape=[jax.ShapeDtypeStruct((n, d), f32),
                   jax.ShapeDtypeStruct((b, hq, s, QK_PAD), jnp.bfloat16),
                   jax.ShapeDtypeStruct((b, hq, s, QK_PAD), jnp.bfloat16),
                   jax.ShapeDtypeStruct((b, hq, s // ta, V_DIM, ta), jnp.bfloat16),
                   jax.ShapeDtypeStruct((n, MEM_W), jnp.bfloat16)],
        compiler_params=_params(("arbitrary",)),
        name="layer1_proj",
    )(after, x2, slab, picked, picked,
      positions.reshape(n, 1), row(norm1_g[1]), _bf(w_in_p), row(b_q_norm_g[0]), row(b_kv_norm_g[0]),
      _bf(wq_p), _bf(wkv_p), pad_gain(b_qn_g[0]).astype(f32), pad_gain(b_kn_g[0]).astype(f32),
      inv_l, sgn_l, kn_all, mem_v, row(mem_qn_g[1]))

    qb = s // ta
    hp = ATTN_HEADS_PER_STEP
    attn = pl.pallas_call(
        _attn_kernel,
        grid=(b, hq // hp, qb),
        in_specs=[pl.BlockSpec((None, hp, ta, QK_PAD), lambda bi, hi, i: (bi, hi, i, 0)),
                  pl.BlockSpec((None, hp, s, QK_PAD), lambda bi, hi, i: (bi, hi, 0, 0)),
                  pl.BlockSpec((None, hp, qb, V_DIM, ta), lambda bi, hi, i: (bi, hi, 0, 0, 0))],
        out_specs=pl.BlockSpec((ta, hp * V_DIM), lambda bi, hi, i: (bi * qb + i, hi)),
        out_shape=jax.ShapeDtypeStruct((n, hq * V_DIM), jnp.bfloat16),
        scratch_shapes=[pltpu.VMEM((hp, 1, ta), f32), pltpu.VMEM((hp, V_DIM + ATTN_SUM_ROWS, ta), f32)],
        compiler_params=_params(("arbitrary", "arbitrary", "arbitrary")),
        name="causal_attention",
    )(q, k, vt)

    wr1, br1 = _router_weights(moe_w_group[1], moe_b_group[1], moe_w_expert[1], moe_b_expert[1])
    x2, xn, slab, slabt, counts = pl.pallas_call(
        _layer1_out_kernel,
        grid=(n // tt,),
        in_specs=[tok(d), tok(hq * V_DIM), tok(MEM_W), _const_spec((hq * V_DIM + MEM_W, d)),
                  _const_spec((1, d)), _const_spec((d, LANE)), _const_spec((1, LANE))],
        out_specs=route_out_specs,
        out_shape=route_out_shape,
        scratch_shapes=route_scratch,
        compiler_params=_params(("arbitrary",)),
        name="layer1_out",
    )(x2, attn, mem_o, _bf(b_w_out[0]), row(norm2_g[1]), wr1, br1)
    after = yield slab
    picked, ys = _moe(1, xn, slabt, counts, moe_w_gate, moe_w_up, moe_w_down, after)
    after = yield ys
    yield pl.pallas_call(
        _combine_kernel,
        grid=(n // tt,),
        in_specs=[_ANY, tok(d), tok(LANE), *_picked_specs(n, d // 2)],
        out_specs=group_tok,
        out_shape=jax.ShapeDtypeStruct((n_total, d), f32),
        input_output_aliases={0: 0} if chunk else {},
        compiler_params=_params(("arbitrary",)),
        name="moe_combine",
    )(after, x2, slab, picked, picked)


def kernel(x, mem, positions, mem_norm_g, w_mem_kv, mem_qn_g, mem_kn_g, norm1_g, norm2_g, a_w_in, a_ln_g, a_ln_b, a_w_s, a_b_s, a_w_out, b_w_in, b_q_norm_g, b_kv_norm_g, b_w_q_up, b_w_kv_up, b_qn_g, b_kn_g, b_w_out, moe_w_group, moe_b_group, moe_w_expert, moe_b_expert, moe_w_gate, moe_w_up, moe_w_down):
    params = (mem_norm_g, w_mem_kv, mem_qn_g, mem_kn_g, norm1_g, norm2_g, a_w_in, a_ln_g, a_ln_b, a_w_s,
              a_b_s, a_w_out, b_w_in, b_q_norm_g, b_kv_norm_g, b_w_q_up, b_w_kv_up, b_qn_g, b_kn_g, b_w_out,
              moe_w_group, moe_b_group, moe_w_expert, moe_b_expert, moe_w_gate, moe_w_up, moe_w_down)
    b, s, d = x.shape
    assert BATCH_GROUPS == 2 and b % BATCH_GROUPS == 0
    g = b // BATCH_GROUPS
    x_all = x.reshape(b * s, d)
    groups = [_trunk(c, x_all, mem[c * g:(c + 1) * g], positions[c * g:(c + 1) * g], *params)
              for c in range(BATCH_GROUPS)]
    done = [next(t) for t in groups]
    for _ in range(TRUNK_PHASES - 1):
        prev, done = done, []
        for c, t in enumerate(groups):
            done.append(t.send(prev[c + 1] if c + 1 < BATCH_GROUPS else done[0]))
    return done[-1].reshape(b, s, d)
```

```python
import jax
import jax.numpy as jnp
from jax import lax
from jax.experimental import pallas as pl
from jax.experimental.pallas import tpu as pltpu
from jax.experimental.pallas import tpu_sc as plsc

EPS = 1e-6
LANE = 128
SUBLANE = 8
MEM_HEADS = 4
MEM_HEAD_DIM = 128
MEM_W = MEM_HEADS * MEM_HEAD_DIM
CHUNK = 128
A_GROUPS = 8
MLA_HEADS = 8
Q_LORA = 512
KV_LORA = 256
NOPE_DIM = 128
ROPE_DIM = 64
ROPE_HALF = ROPE_DIM // 2
V_DIM = 128
QK_DIM = NOPE_DIM + ROPE_DIM
QK_PAD = 2 * LANE
ROPE_BASE = 10000.0
N_GROUPS = 4
EXPERTS_PER_GROUP = 8
N_EXPERTS = N_GROUPS * EXPERTS_PER_GROUP
ROUTE_ROWS = 40
EXPERT_FF = 256
LOG2E = 1.4426950408889634

BATCH_GROUPS = 2
TOKEN_TILE = 512
ROUTE_TILE = 1024
SUB_TILE = 256
ROW_TILE = 512
ATTN_TILE = 512
ATTN_HEADS_PER_STEP = 2
ATTN_KEY_TILES_PER_TRIP = 4
ATTN_QUERY_SPLIT = 2
ATTN_LOOKAHEAD = 3
ATTN_SUM_ROWS = 16
VMEM_LIMIT = 56 * 1024 * 1024
NEG_BIG = -1e30

SC_CORES = 2
SC_SUBCORES = 16
SC_WORKERS = SC_CORES * SC_SUBCORES
SC_INDEX_GROUP = 128
SC_CHUNK = 64

_NT = (((1,), (1,)), ((), ()))
_ANY = pl.BlockSpec(memory_space=pl.ANY)


def _const_spec(shape):
    nd = len(shape)
    return pl.BlockSpec(shape, lambda *_: (0,) * nd, pipeline_mode=pl.Buffered(1))


def _params(sem):
    return pltpu.CompilerParams(dimension_semantics=sem, vmem_limit_bytes=VMEM_LIMIT)


def _run_staggered(gens):
    waiting = list(gens)
    active = []
    while waiting or active:
        if waiting:
            active.append(waiting.pop(0))
        for g in list(active):
            try:
                next(g)
            except StopIteration:
                active.remove(g)


def _sub_rows(t):
    return [pl.ds(k * SUB_TILE, SUB_TILE) for k in range(t // SUB_TILE)]


def _rms(x, g):
    return x * lax.rsqrt(jnp.mean(x * x, axis=-1, keepdims=True) + EPS) * g


def _gelu(x):
    return 0.5 * x * (1.0 + lax.erf(x * (2.0 ** -0.5)))


def _bf(x):
    return x.astype(jnp.bfloat16)


def _dot(a, b):
    return jnp.dot(a, b, preferred_element_type=jnp.float32)


def _memkv_kernel(mem_ref, g_ref, w_ref, kng_ref, kn_ref, v_ref):
    h = _bf(_rms(mem_ref[...], g_ref[...]))
    kv = _dot(h, w_ref[...])
    v_ref[...] = _bf(kv[:, MEM_W:])
    for layer in range(kn_ref.shape[0]):
        g = kng_ref[layer]
        for hh in range(MEM_HEADS):
            k = kv[:, hh * LANE:(hh + 1) * LANE]
            kn_ref[layer, :, hh * LANE:(hh + 1) * LANE] = _bf(_rms(k, g))


def _mem_attention(qm, kn_ref, v_ref, qg):
    outs = []
    for hh in range(MEM_HEADS):
        sl = slice(hh * LANE, (hh + 1) * LANE)
        q = _rms(qm[:, sl], qg) * (MEM_HEAD_DIM ** -0.5)
        s = lax.dot_general(_bf(q), kn_ref[:, sl], _NT, preferred_element_type=jnp.float32)
        p = jnp.exp(s - jnp.max(s, axis=-1, keepdims=True))
        l = jnp.sum(p, axis=-1, keepdims=True)
        outs.append(_dot(_bf(p), v_ref[:, sl]) / l)
    return jnp.concatenate(outs, axis=-1)


def _pack_rows(x):
    w = x.shape[1] // 2
    bits = lambda v: lax.bitcast_convert_type(_bf(v).astype(jnp.float32), jnp.uint32)
    return (bits(x[:, :w]) >> 16) | (bits(x[:, w:]) & jnp.uint32(0xFFFF0000))


def _unpack_rows(p):
    lo = lax.bitcast_convert_type(p << 16, jnp.float32)
    hi = lax.bitcast_convert_type(p & jnp.uint32(0xFFFF0000), jnp.float32)
    return jnp.concatenate([lo, hi], axis=-1)


def _route_stages(x, rows, g2_ref, wr_ref, br_ref, carry_ref, xn_ref, slab_ref, slabt_ref, cnt_ref):
    t = x.shape[0]
    xn = _rms(x, g2_ref[...])
    xn_ref[rows, :] = _pack_rows(xn)
    logits = _dot(_bf(xn), wr_ref[...]) + br_ref[...]
    yield
    lt = logits.T[:ROUTE_ROWS, :]
    row = lax.broadcasted_iota(jnp.int32, lt.shape, 0)

    def first_max(v):
        m = jnp.max(v, axis=0, keepdims=True)
        idx = jnp.min(jnp.where(v == m, row, ROUTE_ROWS), axis=0, keepdims=True)
        return m, idx

    lg = jnp.where(row < N_GROUPS, lt, NEG_BIG)
    gmax, gidx = first_max(lg)
    g_w = 1.0 / jnp.sum(jnp.exp(lg - gmax), axis=0, keepdims=True)

    eid = row - N_GROUPS
    in_grp = (eid >= 0) & (eid < N_EXPERTS) & ((eid >> 3) == gidx)
    le = jnp.where(in_grp, lt, NEG_BIG)
    m1, i1 = first_max(le)
    m2, i2 = first_max(jnp.where(row == i1, NEG_BIG, le))
    r = jnp.exp(m2 - m1)
    w1 = g_w / (1.0 + r)
    w2 = w1 * r
    e1 = i1 - N_GROUPS
    e2 = i2 - N_GROUPS
    yield

    expert = lax.broadcasted_iota(jnp.int32, (N_EXPERTS, t), 0)
    oh1 = expert == e1
    oh2 = expert == e2
    oh = jnp.where(oh1 | oh2, 1.0, 0.0)
    src = lax.broadcasted_iota(jnp.int32, (t, t), 0)
    dst = lax.broadcasted_iota(jnp.int32, (t, t), 1)
    earlier = jnp.where(src < dst, 1.0, 0.0).astype(jnp.bfloat16)
    before = _dot(_bf(oh), earlier) + carry_ref[...]
    r1 = jnp.sum(jnp.where(oh1, before, 0.0), axis=0, keepdims=True)
    r2 = jnp.sum(jnp.where(oh2, before, 0.0), axis=0, keepdims=True)
    carry_ref[...] += jnp.sum(oh, axis=1, keepdims=True)
    cnt_ref[...] = jnp.broadcast_to(carry_ref[...], cnt_ref.shape)

    table = jnp.concatenate([e1.astype(jnp.float32), e2.astype(jnp.float32), w1, w2, r1, r2,
                             jnp.zeros((LANE - 6, t), jnp.float32)], axis=0)
    slabt_ref[:, rows] = table[:SUBLANE, :]
    slab_ref[rows, :] = table.T


def _reset_carry(carry_ref):
    @pl.when(pl.program_id(0) == 0)
    def _():
        carry_ref[...] = jnp.zeros_like(carry_ref)


def _layer0_kernel(x_ref, g1_ref, win_ref, lng_ref, lnb_ref, ws_ref, bs_ref, kn_ref, v_ref, qg_ref,
                   wout_ref, g2_ref, wr_ref, br_ref,
                   xo_ref, xn_ref, slab_ref, slabt_ref, cnt_ref, carry_ref):
    d = x_ref.shape[1]
    _reset_carry(carry_ref)
    row = lax.broadcasted_iota(jnp.int32, (CHUNK, CHUNK), 0)
    col = lax.broadcasted_iota(jnp.int32, (CHUNK, CHUNK), 1)
    causal = row >= col

    def stages(rows):
        x = x_ref[rows, :]
        t = x.shape[0]
        h = _bf(_rms(x, g1_ref[...]))
        z = _dot(h, win_ref[...])
        yield
        u = _gelu(z[:, :d])
        v = _gelu(z[:, d:2 * d])
        mu = jnp.mean(v, axis=-1, keepdims=True)
        vc = v - mu
        var = jnp.mean(vc * vc, axis=-1, keepdims=True)
        v = _bf(vc * lax.rsqrt(var + EPS) * lng_ref[...] + lnb_ref[...])
        yield
        nc = t // CHUNK
        per_group = []
        for g in range(A_GROUPS):
            w = jnp.where(causal, ws_ref[g], jnp.zeros((), ws_ref.dtype))
            vg = jnp.concatenate([v[c * CHUNK:(c + 1) * CHUNK, g * LANE:(g + 1) * LANE] for c in range(nc)],
                                 axis=-1)
            per_group.append(_dot(w, vg))
        chunks = [jnp.concatenate([pg[:, c * LANE:(c + 1) * LANE] for pg in per_group], axis=-1) + bs_ref[...]
                  for c in range(nc)]
        mix = _bf(u * jnp.concatenate(chunks, axis=0))
        yield
        mem = _bf(_mem_attention(z[:, 2 * d:], kn_ref, v_ref, qg_ref[...]))
        yield
        xo = x + _dot(mix, wout_ref[:d, :]) + _dot(mem, wout_ref[d:, :])
        xo_ref[rows, :] = xo
        yield
        yield from _route_stages(xo, rows, g2_ref, wr_ref, br_ref, carry_ref,
                                 xn_ref, slab_ref, slabt_ref, cnt_ref)

    _run_staggered([stages(rows) for rows in _sub_rows(x_ref.shape[0])])


def _layer1_proj_kernel(after_ref, x_ref, slab_ref, y0_ref, y1_ref, pos_ref, g1_ref, win_ref, qng_ref, kvng_ref,
                        wq_ref, wkv_ref, qg_ref, kg_ref, inv_ref, sgn_ref, kn_ref, mv_ref, mqg_ref,
                        xo_ref, q_ref, k_ref, v_ref, mem_ref):
    o1 = Q_LORA
    o2 = o1 + KV_LORA
    o3 = o2 + MEM_W
    q_scale = (QK_DIM ** -0.5) * LOG2E

    def stages(rows):
        x = _combined(x_ref, slab_ref, y0_ref, y1_ref, rows)
        xo_ref[rows, :] = x
        h = _bf(_rms(x, g1_ref[...]))
        z = _dot(h, win_ref[...])
        yield
        cq = _bf(_rms(z[:, :o1], qng_ref[...]))
        ckv = _bf(_rms(z[:, o1:o2], kvng_ref[...]))
        k_rope = z[:, o3:o3 + LANE]
        ang = pos_ref[rows, :].astype(jnp.float32) * inv_ref[...]
        cos = jnp.cos(ang)
        sin = jnp.sin(ang) * sgn_ref[...]

        def rope(r):
            return r * cos + pltpu.roll(r, LANE // 2, 1) * sin

        q = _dot(cq, wq_ref[...])
        kv = _dot(ckv, wkv_ref[...])
        yield
        qg = qg_ref[...]
        kg = kg_ref[...]
        kr_ss = jnp.sum(k_rope * k_rope, axis=-1, keepdims=True)
        kr = rope(k_rope * kg[:, LANE:])
        for hh in range(MLA_HEADS):
            qh = q[:, hh * QK_PAD:(hh + 1) * QK_PAD]
            rq = lax.rsqrt(jnp.sum(qh * qh, axis=-1, keepdims=True) * (1.0 / QK_DIM) + EPS) * q_scale
            qh = qh * rq * qg
            q_ref[hh, rows, :LANE] = _bf(qh[:, :LANE])
            q_ref[hh, rows, LANE:] = _bf(rope(qh[:, LANE:]))
            kn = kv[:, hh * LANE:(hh + 1) * LANE]
            rk = lax.rsqrt((jnp.sum(kn * kn, axis=-1, keepdims=True) + kr_ss) * (1.0 / QK_DIM) + EPS)
            k_ref[hh, rows, :LANE] = _bf(kn * rk * kg[:, :LANE])
            k_ref[hh, rows, LANE:] = _bf(kr * rk)
            v_ref[hh, :, rows] = _bf(kv[:, (MLA_HEADS + hh) * LANE:(MLA_HEADS + hh + 1) * LANE].T)
            if hh % 4 == 3:
                yield
        mem_ref[rows, :] = _bf(_mem_attention(z[:, o2:o3], kn_ref, mv_ref, mqg_ref[...]))

    _run_staggered([stages(rows) for rows in _sub_rows(x_ref.shape[0])])


def _attn_kernel(q_ref, k_ref, vt_ref, o_ref, m_ref, acc_ref):
    i = pl.program_id(2)
    heads, tq = q_ref.shape[0], q_ref.shape[1]
    m_ref[...] = jnp.full_like(m_ref, NEG_BIG)
    acc_ref[...] = jnp.zeros_like(acc_ref)

    qw = tq // ATTN_QUERY_SPLIT

    def scores(hh, j, part, masked):
        keys = (part + 1) * qw if masked else tq
        start = pl.multiple_of(j * tq, tq)
        s = lax.dot_general(k_ref[hh, pl.ds(start, keys), :], q_ref[hh, part * qw:(part + 1) * qw, :], _NT,
                            preferred_element_type=jnp.float32)
        if masked:
            key = lax.broadcasted_iota(jnp.int32, (keys, qw), 0)
            qry = lax.broadcasted_iota(jnp.int32, (keys, qw), 1) + part * qw
            s = jnp.where(key <= qry, s, NEG_BIG)
        return s

    def update(hh, j, part, s):
        keys = s.shape[0]
        lanes = slice(part * qw, (part + 1) * qw)
        m = m_ref[hh, :, lanes]
        m_new = jnp.maximum(m, jnp.max(s, axis=0, keepdims=True))
        alpha = jnp.exp2(m - m_new)
        p = _bf(jnp.exp2(s - m_new))
        m_ref[hh, :, lanes] = m_new
        vt1 = jnp.concatenate([vt_ref[hh, j, :, :keys], jnp.ones((ATTN_SUM_ROWS, keys), jnp.bfloat16)], axis=0)
        acc_ref[hh, :, lanes] = alpha * acc_ref[hh, :, lanes] + _dot(vt1, p)

    def run(items):
        ss = {}
        ahead = ATTN_LOOKAHEAD
        for t in range(len(items) + ahead):
            if t < len(items):
                ss[t] = scores(*items[t])
            if t >= ahead:
                update(*items[t - ahead][:3], ss.pop(t - ahead))

    def tile_items(first, count, masked):
        return [(hh, first + u, part, masked) for u in range(count) for hh in range(heads)
                for part in range(ATTN_QUERY_SPLIT)]

    unroll = ATTN_KEY_TILES_PER_TRIP

    def body(jj, c):
        run(tile_items(unroll * jj, unroll, False))
        return c

    lax.fori_loop(0, i // unroll, body, 0)
    for left in range(unroll):
        @pl.when(i % unroll == left)
        def _():
            run(tile_items(i - left, left, False) + tile_items(i, 1, True))

    for hh in range(heads):
        acc = acc_ref[hh]
        o_ref[:, hh * V_DIM:(hh + 1) * V_DIM] = _bf((acc[:V_DIM] / acc[V_DIM:V_DIM + 1]).T)


def _layer1_out_kernel(x_ref, o_ref, mem_ref, wout_ref, g2_ref, wr_ref, br_ref,
                       xo_ref, xn_ref, slab_ref, slabt_ref, cnt_ref, carry_ref):
    d = o_ref.shape[1]
    _reset_carry(carry_ref)

    def stages(rows):
        xo = (x_ref[rows, :] + _dot(o_ref[rows, :], wout_ref[:d, :])
              + _dot(mem_ref[rows, :], wout_ref[d:, :]))
        xo_ref[rows, :] = xo
        yield
        yield from _route_stages(xo, rows, g2_ref, wr_ref, br_ref, carry_ref,
                                 xn_ref, slab_ref, slabt_ref, cnt_ref)

    _run_staggered([stages(rows) for rows in _sub_rows(x_ref.shape[0])])


def _sc_mesh():
    return plsc.VectorSubcoreMesh(core_axis_name="c", subcore_axis_name="s")


def _sc_worker_base(rows_per_worker):
    return (lax.axis_index("c") * SC_SUBCORES + lax.axis_index("s")) * rows_per_worker


def _sc_scatter_rows(x, idx0, idx1, p_rows):
    n, w = x.shape
    per = n // SC_WORKERS
    chunks = SC_INDEX_GROUP // SC_CHUNK
    assert n % SC_WORKERS == 0 and per % SC_INDEX_GROUP == 0

    @pl.kernel(out_type=jax.ShapeDtypeStruct((p_rows, w), x.dtype), mesh=_sc_mesh(),
               scratch_types=[pltpu.VMEM((1, SC_INDEX_GROUP), jnp.int32),
                              pltpu.VMEM((1, SC_INDEX_GROUP), jnp.int32),
                              pltpu.VMEM((SC_CHUNK, w), x.dtype), pltpu.VMEM((SC_CHUNK, w), x.dtype),
                              pltpu.SemaphoreType.DMA, pltpu.SemaphoreType.DMA],
               name="moe_dispatch_sc")
    def scatter(x_hbm, i0_hbm, i1_hbm, o_hbm, i0_v, i1_v, buf_a, buf_b, sem_a, sem_b):
        base = _sc_worker_base(per)

        @pl.loop(0, per // SC_INDEX_GROUP)
        def _(g):
            off = pl.multiple_of(base + g * SC_INDEX_GROUP, SC_INDEX_GROUP)
            pltpu.sync_copy(i0_hbm.at[:, pl.ds(off, SC_INDEX_GROUP)], i0_v)
            pltpu.sync_copy(i1_hbm.at[:, pl.ds(off, SC_INDEX_GROUP)], i1_v)
            pending = []
            for c in range(chunks):
                buf, sem = ((buf_a, sem_a), (buf_b, sem_b))[c % 2]
                if c >= 2:
                    for cp in pending[c - 2]:
                        cp.wait()
                pltpu.sync_copy(x_hbm.at[pl.ds(off + c * SC_CHUNK, SC_CHUNK)], buf)
                sl = pl.ds(c * SC_CHUNK, SC_CHUNK)
                pending.append((pltpu.async_copy(buf, o_hbm.at[i0_v.at[0, sl]], sem),
                                pltpu.async_copy(buf, o_hbm.at[i1_v.at[0, sl]], sem)))
            for cps in pending[max(chunks - 2, 0):]:
                for cp in cps:
                    cp.wait()

    return scatter(x, idx0.reshape(1, n), idx1.reshape(1, n))


def _sc_gather_rows(table, idx):
    m = idx.shape[0]
    w = table.shape[1]
    per = m // SC_WORKERS
    chunks = SC_INDEX_GROUP // SC_CHUNK
    assert m % SC_WORKERS == 0 and per % SC_INDEX_GROUP == 0

    @pl.kernel(out_type=jax.ShapeDtypeStruct((m, w), table.dtype), mesh=_sc_mesh(),
               scratch_types=[pltpu.VMEM((1, SC_INDEX_GROUP), jnp.int32),
                              pltpu.VMEM((SC_CHUNK, w), table.dtype), pltpu.VMEM((SC_CHUNK, w), table.dtype),
                              pltpu.SemaphoreType.DMA, pltpu.SemaphoreType.DMA],
               name="moe_combine_sc")
    def gather(t_hbm, i_hbm, o_hbm, i_v, buf_a, buf_b, sem_a, sem_b):
        base = _sc_worker_base(per)

        @pl.loop(0, per // SC_INDEX_GROUP)
        def _(g):
            off = pl.multiple_of(base + g * SC_INDEX_GROUP, SC_INDEX_GROUP)
            pltpu.sync_copy(i_hbm.at[:, pl.ds(off, SC_INDEX_GROUP)], i_v)
            pending = []
            for c in range(chunks):
                buf, sem = ((buf_a, sem_a), (buf_b, sem_b))[c % 2]
                if c >= 2:
                    pending[c - 2].wait()
                pltpu.sync_copy(t_hbm.at[i_v.at[0, pl.ds(c * SC_CHUNK, SC_CHUNK)]], buf)
                pending.append(pltpu.async_copy(buf, o_hbm.at[pl.ds(off + c * SC_CHUNK, SC_CHUNK)], sem))
            for cp in pending[max(chunks - 2, 0):]:
                cp.wait()

    return gather(table, idx.reshape(1, m))


def _ffn_kernel(exp_ref, rows_ref, after_ref, xs_ref, wg_ref, wu_ref, wd_ref, ys_ref, wg_bf, wu_bf, wd_bf):
    j = pl.program_id(0)

    @pl.when((j == 0) | (exp_ref[j] != exp_ref[jnp.maximum(j - 1, 0)]))
    def _():
        wg_bf[...] = _bf(wg_ref[...])
        wu_bf[...] = _bf(wu_ref[...])
        wd_bf[...] = _bf(wd_ref[...])

    def stages(k, rows):
        packed = xs_ref[rows, :]
        row_id = lax.broadcasted_iota(jnp.int32, packed.shape, 0) + k * SUB_TILE
        x = _bf(_unpack_rows(jnp.where(row_id < rows_ref[j], packed, jnp.uint32(0))))
        g = _dot(x, wg_bf[...])
        u = _dot(x, wu_bf[...])
        yield
        act = _bf(g * jax.nn.sigmoid(g) * u)
        yield
        ys_ref[rows, :] = _pack_rows(_dot(act, wd_bf[...]))

    _run_staggered([stages(k, rows) for k, rows in enumerate(_sub_rows(xs_ref.shape[0]))])


def _combined(x_ref, slab_ref, y0_ref, y1_ref, rows):
    slab = slab_ref[rows, :]
    return (x_ref[rows, :] + slab[:, 2:3] * _unpack_rows(y0_ref[rows, :])
            + slab[:, 3:4] * _unpack_rows(y1_ref[rows, :]))


def _combine_kernel(after_ref, x_ref, slab_ref, y0_ref, y1_ref, out_ref):
    out_ref[...] = _combined(x_ref, slab_ref, y0_ref, y1_ref, slice(None))


def _moe(layer, xn, slabt, counts, w_gate, w_up, w_down, after):
    n, wp = xn.shape
    d = w_gate.shape[-2]
    rt = ROW_TILE
    p_max = 2 * n + N_EXPERTS * rt
    n_tiles = p_max // rt

    cnt = counts[:, 0].astype(jnp.int32)
    padded = ((cnt + rt - 1) // rt) * rt
    experts = jnp.arange(N_EXPERTS, dtype=jnp.int32)
    start = jnp.sum(jnp.where(experts[None, :] < experts[:, None], padded[None, :], 0), axis=1)
    end = start + padded

    def position(e_row, r_row):
        e = e_row.astype(jnp.int32)[None, :]
        return jnp.sum(jnp.where(e == experts[:, None], start[:, None], 0), axis=0) + r_row.astype(jnp.int32)

    pos0 = position(slabt[0], slabt[4])
    pos1 = position(slabt[1], slabt[5])
    tile_start = jnp.arange(n_tiles, dtype=jnp.int32) * rt
    owns = (start[None, :] <= tile_start[:, None]) & (tile_start[:, None] < end[None, :])
    past = tile_start >= end[N_EXPERTS - 1]
    tile_exp = jnp.where(past, N_EXPERTS - 1, jnp.sum(jnp.where(owns, experts[None, :], 0), axis=1))
    used = jnp.sum(jnp.where(owns, (start + cnt)[None, :], 0), axis=1)
    tile_rows = jnp.clip(used - tile_start, 0, rt).astype(jnp.int32)

    xs = _sc_scatter_rows(xn, pos0, pos1, p_max)

    f = w_gate.shape[-1]
    pick = lambda j, ex, rw: (layer, ex[j], 0, 0)
    rows_spec = pl.BlockSpec((rt, wp), lambda j, ex, rw: (j, 0))
    ys = pl.pallas_call(
        _ffn_kernel,
        grid_spec=pltpu.PrefetchScalarGridSpec(
            num_scalar_prefetch=2, grid=(n_tiles,),
            in_specs=[_ANY, rows_spec, pl.BlockSpec((None, None, d, f), pick),
                      pl.BlockSpec((None, None, d, f), pick), pl.BlockSpec((None, None, f, d), pick)],
            out_specs=rows_spec,
            scratch_shapes=[pltpu.VMEM((d, f), jnp.bfloat16), pltpu.VMEM((d, f), jnp.bfloat16),
                            pltpu.VMEM((f, d), jnp.bfloat16)]),
        out_shape=jax.ShapeDtypeStruct((p_max, wp), jnp.uint32),
        compiler_params=_params(("arbitrary",)),
        name="moe_ffn",
    )(tile_exp.astype(jnp.int32), tile_rows, after, xs, w_gate, w_up, w_down)

    return _sc_gather_rows(ys, jnp.concatenate([pos0, pos1])), ys


def _picked_specs(n, wp):
    nb = n // TOKEN_TILE
    return [pl.BlockSpec((TOKEN_TILE, wp), lambda i: (i, 0)),
            pl.BlockSpec((TOKEN_TILE, wp), lambda i: (i + nb, 0))]


def _router_weights(w_group, b_group, w_expert, b_expert):
    d = w_group.shape[0]
    pad = LANE - N_GROUPS - N_EXPERTS
    wr = jnp.concatenate([w_group, w_expert, jnp.zeros((d, pad), w_group.dtype)], axis=1)
    br = jnp.concatenate([b_group, b_expert, jnp.zeros((pad,), b_group.dtype)])
    return _bf(wr), br.reshape(1, LANE).astype(jnp.float32)


def _rope_lanes(vec_half):
    z = jnp.zeros_like(vec_half)
    return jnp.concatenate([vec_half, z, vec_half, z], axis=-1)


def _pad_rope_cols(w):
    z = jnp.zeros(w.shape[:-1] + (ROPE_HALF,), w.dtype)
    return jnp.concatenate([w[..., :ROPE_HALF], z, w[..., ROPE_HALF:], z], axis=-1)


TRUNK_PHASES = 5


def _trunk(chunk, x_all, mem, positions, mem_norm_g, w_mem_kv, mem_qn_g, mem_kn_g, norm1_g,
           norm2_g, a_w_in, a_ln_g, a_ln_b, a_w_s, a_b_s, a_w_out, b_w_in, b_q_norm_g, b_kv_norm_g, b_w_q_up,
           b_w_kv_up, b_qn_g, b_kn_g, b_w_out, moe_w_group, moe_b_group, moe_w_expert, moe_b_expert,
           moe_w_gate, moe_w_up, moe_w_down):
    n_total, d = x_all.shape
    b, s = positions.shape
    m = mem.shape[1]
    n = b * s
    depth = norm1_g.shape[0]
    tt = TOKEN_TILE
    ta = ATTN_TILE
    tiles_per_batch = s // tt
    assert depth == 2 and s % tt == 0 and tt % SUB_TILE == 0 and d == A_GROUPS * LANE
    assert s % ta == 0 and ta % tt == 0 and s % ROUTE_TILE == 0 and ROUTE_TILE % SUB_TILE == 0
    f32 = jnp.float32
    row = lambda v: v.reshape(1, -1).astype(f32)

    kn_all, mem_v = pl.pallas_call(
        _memkv_kernel,
        grid=(b,),
        in_specs=[pl.BlockSpec((m, d), lambda i: (i, 0)), _const_spec((1, d)),
                  _const_spec((d, 2 * MEM_W)), _const_spec((depth, 1, MEM_HEAD_DIM))],
        out_specs=[pl.BlockSpec((depth, m, MEM_W), lambda i: (0, i, 0)),
                   pl.BlockSpec((m, MEM_W), lambda i: (i, 0))],
        out_shape=[jax.ShapeDtypeStruct((depth, b * m, MEM_W), jnp.bfloat16),
                   jax.ShapeDtypeStruct((b * m, MEM_W), jnp.bfloat16)],
        compiler_params=_params(("arbitrary",)),
        name="mem_kv",
    )(mem.reshape(b * m, d), row(mem_norm_g), _bf(w_mem_kv), mem_kn_g.reshape(depth, 1, MEM_HEAD_DIM))

    tok = lambda width: pl.BlockSpec((tt, width), lambda i: (i, 0))
    kn_spec = lambda layer: pl.BlockSpec((None, m, MEM_W), lambda i: (layer, i // tiles_per_batch, 0))
    mv_spec = pl.BlockSpec((m, MEM_W), lambda i: (i // tiles_per_batch, 0))
    rt_ = ROUTE_TILE
    rtok = lambda width: pl.BlockSpec((rt_, width), lambda i: (i, 0))
    rkn_spec = lambda layer: pl.BlockSpec((None, m, MEM_W), lambda i: (layer, i // (s // rt_), 0))
    rmv_spec = pl.BlockSpec((m, MEM_W), lambda i: (i // (s // rt_), 0))
    route_out_specs = [rtok(d), rtok(d // 2), rtok(LANE), pl.BlockSpec((SUBLANE, rt_), lambda i: (0, i)),
                       pl.BlockSpec((N_EXPERTS, LANE), lambda i: (0, 0))]
    route_out_shape = [jax.ShapeDtypeStruct((n, d), f32), jax.ShapeDtypeStruct((n, d // 2), jnp.uint32),
                       jax.ShapeDtypeStruct((n, LANE), f32), jax.ShapeDtypeStruct((SUBLANE, n), f32),
                       jax.ShapeDtypeStruct((N_EXPERTS, LANE), f32)]
    route_scratch = [pltpu.VMEM((N_EXPERTS, 1), f32)]

    first_block = chunk * (n // tt)
    group_tok = pl.BlockSpec((tt, d), lambda i: (i + first_block, 0))
    first_rblock = chunk * (n // rt_)

    wr0, br0 = _router_weights(moe_w_group[0], moe_b_group[0], moe_w_expert[0], moe_b_expert[0])
    a_in = a_w_in.shape[-1]
    bias_s = jnp.repeat(a_b_s[0].T, LANE, axis=1).astype(f32)
    x2, xn, slab, slabt, counts = pl.pallas_call(
        _layer0_kernel,
        grid=(n // rt_,),
        in_specs=[pl.BlockSpec((rt_, d), lambda i: (i + first_rblock, 0)),
                  _const_spec((1, d)), _const_spec((d, a_in)), _const_spec((1, d)),
                  _const_spec((1, d)), _const_spec((A_GROUPS, CHUNK, CHUNK)), _const_spec((CHUNK, d)),
                  rkn_spec(0), rmv_spec, _const_spec((1, MEM_HEAD_DIM)),
                  _const_spec((d + MEM_W, d)), _const_spec((1, d)), _const_spec((d, LANE)),
                  _const_spec((1, LANE))],
        out_specs=route_out_specs,
        out_shape=route_out_shape,
        scratch_shapes=route_scratch,
        compiler_params=_params(("arbitrary",)),
        name="layer0_mixer",
    )(x_all, row(norm1_g[0]), _bf(a_w_in[0]), row(a_ln_g[0]), row(a_ln_b[0]), _bf(a_w_s[0]), bias_s,
      kn_all, mem_v, row(mem_qn_g[0]), _bf(a_w_out[0]), row(norm2_g[0]), wr0, br0)
    after = yield slab
    picked, ys = _moe(0, xn, slabt, counts, moe_w_gate, moe_w_up, moe_w_down, after)
    after = yield ys

    hq = MLA_HEADS
    o1, o2, o3 = Q_LORA, Q_LORA + KV_LORA, Q_LORA + KV_LORA + ROPE_DIM
    w_in = b_w_in[0]
    w_in_p = jnp.concatenate([w_in[:, :o2], w_in[:, o3:], _pad_rope_cols(w_in[:, o2:o3])], axis=1)
    wq = b_w_q_up[0].reshape(Q_LORA, hq, QK_DIM)
    wq_p = jnp.concatenate([wq[..., :NOPE_DIM], _pad_rope_cols(wq[..., NOPE_DIM:])], axis=-1)
    wq_p = wq_p.reshape(Q_LORA, hq * QK_PAD)
    wkv = b_w_kv_up[0].reshape(KV_LORA, hq, NOPE_DIM + V_DIM)
    wkv_p = jnp.concatenate([wkv[..., :NOPE_DIM].reshape(KV_LORA, hq * NOPE_DIM),
                             wkv[..., NOPE_DIM:].reshape(KV_LORA, hq * V_DIM)], axis=1)
    pad_gain = lambda g: jnp.concatenate([g[:NOPE_DIM], _pad_rope_cols(g[NOPE_DIM:])]).reshape(1, QK_PAD)
    half = jnp.arange(ROPE_HALF, dtype=f32)
    inv = ROPE_BASE ** (-(half * 2.0 / ROPE_DIM))
    inv_l = _rope_lanes(inv).reshape(1, LANE)
    sgn_l = jnp.concatenate([-jnp.ones((2 * ROPE_HALF,), f32), jnp.ones((2 * ROPE_HALF,), f32)]).reshape(1, LANE)

    in_w = w_in_p.shape[1]
    head_spec = lambda width: pl.BlockSpec((None, hq, tt, width),
                                           lambda i: (i // tiles_per_batch, 0, i % tiles_per_batch, 0))
    per_ta = ta // tt

    def vt_index(i):
        t = i % tiles_per_batch
        return (i // tiles_per_batch, 0, t // per_ta, 0, t % per_ta)

    vt_spec = pl.BlockSpec((None, hq, None, V_DIM, tt), vt_index)
    x2, q, k, vt, mem_o = pl.pallas_call(
        _layer1_proj_kernel,
        grid=(n // tt,),
        in_specs=[_ANY, tok(d), tok(LANE), *_picked_specs(n, d // 2),
                  tok(1), _const_spec((1, d)), _const_spec((d, in_w)), _const_spec((1, Q_LORA)),
                  _const_spec((1, KV_LORA)), _const_spec((Q_LORA, hq * QK_PAD)),
                  _const_spec((KV_LORA, hq * (NOPE_DIM + V_DIM))), _const_spec((1, QK_PAD)),
                  _const_spec((1, QK_PAD)), _const_spec((1, LANE)), _const_spec((1, LANE)),
                  kn_spec(1), mv_spec, _const_spec((1, MEM_HEAD_DIM))],
        out_specs=[tok(d), head_spec(QK_PAD), head_spec(QK_PAD), vt_spec, tok(MEM_W)],
        out_shape=[jax.ShapeDtypeStruct((n, d), f32),
                   jax.ShapeDtypeStruct((b, hq, s, QK_PAD), jnp.bfloat16),
                   jax.ShapeDtypeStruct((b, hq, s, QK_PAD), jnp.bfloat16),
                   jax.ShapeDtypeStruct((b, hq, s // ta, V_DIM, ta), jnp.bfloat16),
                   jax.ShapeDtypeStruct((n, MEM_W), jnp.bfloat16)],
        compiler_params=_params(("arbitrary",)),
        name="layer1_proj",
    )(after, x2, slab, picked, picked,
      positions.reshape(n, 1), row(norm1_g[1]), _bf(w_in_p), row(b_q_norm_g[0]), row(b_kv_norm_g[0]),
      _bf(wq_p), _bf(wkv_p), pad_gain(b_qn_g[0]).astype(f32), pad_gain(b_kn_g[0]).astype(f32),
      inv_l, sgn_l, kn_all, mem_v, row(mem_qn_g[1]))

    qb = s // ta
    hp = ATTN_HEADS_PER_STEP
    attn = pl.pallas_call(
        _attn_kernel,
        grid=(b, hq // hp, qb),
        in_specs=[pl.BlockSpec((None, hp, ta, QK_PAD), lambda bi, hi, i: (bi, hi, i, 0)),
                  pl.BlockSpec((None, hp, s, QK_PAD), lambda bi, hi, i: (bi, hi, 0, 0)),
                  pl.BlockSpec((None, hp, qb, V_DIM, ta), lambda bi, hi, i: (bi, hi, 0, 0, 0))],
        out_specs=pl.BlockSpec((ta, hp * V_DIM), lambda bi, hi, i: (bi * qb + i, hi)),
        out_shape=jax.ShapeDtypeStruct((n, hq * V_DIM), jnp.bfloat16),
        scratch_shapes=[pltpu.VMEM((hp, 1, ta), f32), pltpu.VMEM((hp, V_DIM + ATTN_SUM_ROWS, ta), f32)],
        compiler_params=_params(("arbitrary", "arbitrary", "arbitrary")),
        name="causal_attention",
    )(q, k, vt)

    wr1, br1 = _router_weights(moe_w_group[1], moe_b_group[1], moe_w_expert[1], moe_b_expert[1])
    x2, xn, slab, slabt, counts = pl.pallas_call(
        _layer1_out_kernel,
        grid=(n // rt_,),
        in_specs=[rtok(d), rtok(hq * V_DIM), rtok(MEM_W), _const_spec((hq * V_DIM + MEM_W, d)),
                  _const_spec((1, d)), _const_spec((d, LANE)), _const_spec((1, LANE))],
        out_specs=route_out_specs,
        out_shape=route_out_shape,
        scratch_shapes=route_scratch,
        compiler_params=_params(("arbitrary",)),
        name="layer1_out",
    )(x2, attn, mem_o, _bf(b_w_out[0]), row(norm2_g[1]), wr1, br1)
    after = yield slab
    picked, ys = _moe(1, xn, slabt, counts, moe_w_gate, moe_w_up, moe_w_down, after)
    after = yield ys
    yield pl.pallas_call(
        _combine_kernel,
        grid=(n // tt,),
        in_specs=[_ANY, tok(d), tok(LANE), *_picked_specs(n, d // 2)],
        out_specs=group_tok,
        out_shape=jax.ShapeDtypeStruct((n_total, d), f32),
        input_output_aliases={0: 0} if chunk else {},
        compiler_params=_params(("arbitrary",)),
        name="moe_combine",
    )(after, x2, slab, picked, picked)


def kernel(x, mem, positions, mem_norm_g, w_mem_kv, mem_qn_g, mem_kn_g, norm1_g, norm2_g, a_w_in, a_ln_g, a_ln_b, a_w_s, a_b_s, a_w_out, b_w_in, b_q_norm_g, b_kv_norm_g, b_w_q_up, b_w_kv_up, b_qn_g, b_kn_g, b_w_out, moe_w_group, moe_b_group, moe_w_expert, moe_b_expert, moe_w_gate, moe_w_up, moe_w_down):
    params = (mem_norm_g, w_mem_kv, mem_qn_g, mem_kn_g, norm1_g, norm2_g, a_w_in, a_ln_g, a_ln_b, a_w_s,
              a_b_s, a_w_out, b_w_in, b_q_norm_g, b_kv_norm_g, b_w_q_up, b_w_kv_up, b_qn_g, b_kn_g, b_w_out,
              moe_w_group, moe_b_group, moe_w_expert, moe_b_expert, moe_w_gate, moe_w_up, moe_w_down)
    b, s, d = x.shape
    assert BATCH_GROUPS == 2 and b % BATCH_GROUPS == 0
    g = b // BATCH_GROUPS
    x_all = x.reshape(b * s, d)
    groups = [_trunk(c, x_all, mem[c * g:(c + 1) * g], positions[c * g:(c + 1) * g], *params)
              for c in range(BATCH_GROUPS)]
    done = [next(t) for t in groups]
    for _ in range(TRUNK_PHASES - 1):
        prev, done = done, []
        for c, t in enumerate(groups):
            done.append(t.send(prev[c + 1] if c + 1 < BATCH_GROUPS else done[0]))
    return done[-1].reshape(b, s, d)
```

```python
import jax
import jax.numpy as jnp
from jax import lax
from jax.experimental import pallas as pl
from jax.experimental.pallas import tpu as pltpu
from jax.experimental.pallas import tpu_sc as plsc

EPS = 1e-6
LANE = 128
SUBLANE = 8
MEM_HEADS = 4
MEM_HEAD_DIM = 128
MEM_W = MEM_HEADS * MEM_HEAD_DIM
CHUNK = 128
A_GROUPS = 8
MLA_HEADS = 8
Q_LORA = 512
KV_LORA = 256
NOPE_DIM = 128
ROPE_DIM = 64
ROPE_HALF = ROPE_DIM // 2
V_DIM = 128
QK_DIM = NOPE_DIM + ROPE_DIM
QK_PAD = 2 * LANE
ROPE_BASE = 10000.0
N_GROUPS = 4
EXPERTS_PER_GROUP = 8
N_EXPERTS = N_GROUPS * EXPERTS_PER_GROUP
ROUTE_ROWS = 40
EXPERT_FF = 256
LOG2E = 1.4426950408889634

BATCH_GROUPS = 2
TOKEN_TILE = 512
ROUTE_TILE = 1024
SUB_TILE = 256
ROW_TILE = 512
ROW_TILES_PER_STEP = 2
ATTN_TILE = 512
ATTN_HEADS_PER_STEP = 2
ATTN_KEY_TILES_PER_TRIP = 4
ATTN_QUERY_SPLIT = 2
ATTN_LOOKAHEAD = 3
ATTN_SUM_ROWS = 16
VMEM_LIMIT = 56 * 1024 * 1024
NEG_BIG = -1e30

SC_CORES = 2
SC_SUBCORES = 16
SC_WORKERS = SC_CORES * SC_SUBCORES
SC_INDEX_GROUP = 128
SC_CHUNK = 64

_NT = (((1,), (1,)), ((), ()))
_ANY = pl.BlockSpec(memory_space=pl.ANY)


def _const_spec(shape):
    nd = len(shape)
    return pl.BlockSpec(shape, lambda *_: (0,) * nd, pipeline_mode=pl.Buffered(1))


def _params(sem):
    return pltpu.CompilerParams(dimension_semantics=sem, vmem_limit_bytes=VMEM_LIMIT)


def _run_staggered(gens):
    waiting = list(gens)
    active = []
    while waiting or active:
        if waiting:
            active.append(waiting.pop(0))
        for g in list(active):
            try:
                next(g)
            except StopIteration:
                active.remove(g)


def _sub_rows(t):
    return [pl.ds(k * SUB_TILE, SUB_TILE) for k in range(t // SUB_TILE)]


def _rms(x, g):
    return x * lax.rsqrt(jnp.mean(x * x, axis=-1, keepdims=True) + EPS) * g


def _gelu(x):
    return 0.5 * x * (1.0 + lax.erf(x * (2.0 ** -0.5)))


def _bf(x):
    return x.astype(jnp.bfloat16)


def _dot(a, b):
    return jnp.dot(a, b, preferred_element_type=jnp.float32)


def _memkv_kernel(mem_ref, g_ref, w_ref, kng_ref, kn_ref, v_ref):
    h = _bf(_rms(mem_ref[...], g_ref[...]))
    kv = _dot(h, w_ref[...])
    v_ref[...] = _bf(kv[:, MEM_W:])
    for layer in range(kn_ref.shape[0]):
        g = kng_ref[layer]
        for hh in range(MEM_HEADS):
            k = kv[:, hh * LANE:(hh + 1) * LANE]
            kn_ref[layer, :, hh * LANE:(hh + 1) * LANE] = _bf(_rms(k, g))


def _mem_attention(qm, kn_ref, v_ref, qg):
    outs = []
    for hh in range(MEM_HEADS):
        sl = slice(hh * LANE, (hh + 1) * LANE)
        q = _rms(qm[:, sl], qg) * (MEM_HEAD_DIM ** -0.5)
        s = lax.dot_general(_bf(q), kn_ref[:, sl], _NT, preferred_element_type=jnp.float32)
        p = jnp.exp(s - jnp.max(s, axis=-1, keepdims=True))
        l = jnp.sum(p, axis=-1, keepdims=True)
        outs.append(_dot(_bf(p), v_ref[:, sl]) / l)
    return jnp.concatenate(outs, axis=-1)


def _pack_rows(x):
    w = x.shape[1] // 2
    bits = lambda v: lax.bitcast_convert_type(_bf(v).astype(jnp.float32), jnp.uint32)
    return (bits(x[:, :w]) >> 16) | (bits(x[:, w:]) & jnp.uint32(0xFFFF0000))


def _unpack_rows(p):
    lo = lax.bitcast_convert_type(p << 16, jnp.float32)
    hi = lax.bitcast_convert_type(p & jnp.uint32(0xFFFF0000), jnp.float32)
    return jnp.concatenate([lo, hi], axis=-1)


def _route_stages(x, rows, g2_ref, wr_ref, br_ref, carry_ref, xn_ref, slab_ref, slabt_ref, cnt_ref):
    t = x.shape[0]
    xn = _rms(x, g2_ref[...])
    xn_ref[rows, :] = _pack_rows(xn)
    logits = _dot(_bf(xn), wr_ref[...]) + br_ref[...]
    yield
    lt = logits.T[:ROUTE_ROWS, :]
    row = lax.broadcasted_iota(jnp.int32, lt.shape, 0)

    def first_max(v):
        m = jnp.max(v, axis=0, keepdims=True)
        idx = jnp.min(jnp.where(v == m, row, ROUTE_ROWS), axis=0, keepdims=True)
        return m, idx

    lg = jnp.where(row < N_GROUPS, lt, NEG_BIG)
    gmax, gidx = first_max(lg)
    g_w = 1.0 / jnp.sum(jnp.exp(lg - gmax), axis=0, keepdims=True)

    eid = row - N_GROUPS
    in_grp = (eid >= 0) & (eid < N_EXPERTS) & ((eid >> 3) == gidx)
    le = jnp.where(in_grp, lt, NEG_BIG)
    m1, i1 = first_max(le)
    m2, i2 = first_max(jnp.where(row == i1, NEG_BIG, le))
    r = jnp.exp(m2 - m1)
    w1 = g_w / (1.0 + r)
    w2 = w1 * r
    e1 = i1 - N_GROUPS
    e2 = i2 - N_GROUPS
    yield

    expert = lax.broadcasted_iota(jnp.int32, (N_EXPERTS, t), 0)
    oh1 = expert == e1
    oh2 = expert == e2
    oh = jnp.where(oh1 | oh2, 1.0, 0.0)
    src = lax.broadcasted_iota(jnp.int32, (t, t), 0)
    dst = lax.broadcasted_iota(jnp.int32, (t, t), 1)
    earlier = jnp.where(src < dst, 1.0, 0.0).astype(jnp.bfloat16)
    before = _dot(_bf(oh), earlier) + carry_ref[...]
    r1 = jnp.sum(jnp.where(oh1, before, 0.0), axis=0, keepdims=True)
    r2 = jnp.sum(jnp.where(oh2, before, 0.0), axis=0, keepdims=True)
    carry_ref[...] += jnp.sum(oh, axis=1, keepdims=True)
    cnt_ref[...] = jnp.broadcast_to(carry_ref[...], cnt_ref.shape)

    table = jnp.concatenate([e1.astype(jnp.float32), e2.astype(jnp.float32), w1, w2, r1, r2,
                             jnp.zeros((LANE - 6, t), jnp.float32)], axis=0)
    slabt_ref[:, rows] = table[:SUBLANE, :]
    slab_ref[rows, :] = table.T


def _reset_carry(carry_ref):
    @pl.when(pl.program_id(0) == 0)
    def _():
        carry_ref[...] = jnp.zeros_like(carry_ref)


def _layer0_kernel(x_ref, g1_ref, win_ref, lng_ref, lnb_ref, ws_ref, bs_ref, kn_ref, v_ref, qg_ref,
                   wout_ref, g2_ref, wr_ref, br_ref,
                   xo_ref, xn_ref, slab_ref, slabt_ref, cnt_ref, carry_ref):
    d = x_ref.shape[1]
    _reset_carry(carry_ref)
    row = lax.broadcasted_iota(jnp.int32, (CHUNK, CHUNK), 0)
    col = lax.broadcasted_iota(jnp.int32, (CHUNK, CHUNK), 1)
    causal = row >= col

    def stages(rows):
        x = x_ref[rows, :]
        t = x.shape[0]
        h = _bf(_rms(x, g1_ref[...]))
        z = _dot(h, win_ref[...])
        yield
        u = _gelu(z[:, :d])
        v = _gelu(z[:, d:2 * d])
        mu = jnp.mean(v, axis=-1, keepdims=True)
        vc = v - mu
        var = jnp.mean(vc * vc, axis=-1, keepdims=True)
        v = _bf(vc * lax.rsqrt(var + EPS) * lng_ref[...] + lnb_ref[...])
        yield
        nc = t // CHUNK
        per_group = []
        for g in range(A_GROUPS):
            w = jnp.where(causal, ws_ref[g], jnp.zeros((), ws_ref.dtype))
            vg = jnp.concatenate([v[c * CHUNK:(c + 1) * CHUNK, g * LANE:(g + 1) * LANE] for c in range(nc)],
                                 axis=-1)
            per_group.append(_dot(w, vg))
        chunks = [jnp.concatenate([pg[:, c * LANE:(c + 1) * LANE] for pg in per_group], axis=-1) + bs_ref[...]
                  for c in range(nc)]
        mix = _bf(u * jnp.concatenate(chunks, axis=0))
        yield
        mem = _bf(_mem_attention(z[:, 2 * d:], kn_ref, v_ref, qg_ref[...]))
        yield
        xo = x + _dot(mix, wout_ref[:d, :]) + _dot(mem, wout_ref[d:, :])
        xo_ref[rows, :] = xo
        yield
        yield from _route_stages(xo, rows, g2_ref, wr_ref, br_ref, carry_ref,
                                 xn_ref, slab_ref, slabt_ref, cnt_ref)

    _run_staggered([stages(rows) for rows in _sub_rows(x_ref.shape[0])])


def _layer1_proj_kernel(after_ref, x_ref, slab_ref, y0_ref, y1_ref, pos_ref, g1_ref, win_ref, qng_ref, kvng_ref,
                        wq_ref, wkv_ref, qg_ref, kg_ref, inv_ref, sgn_ref, kn_ref, mv_ref, mqg_ref,
                        xo_ref, q_ref, k_ref, v_ref, mem_ref):
    o1 = Q_LORA
    o2 = o1 + KV_LORA
    o3 = o2 + MEM_W
    q_scale = (QK_DIM ** -0.5) * LOG2E

    def stages(rows):
        x = _combined(x_ref, slab_ref, y0_ref, y1_ref, rows)
        xo_ref[rows, :] = x
        h = _bf(_rms(x, g1_ref[...]))
        z = _dot(h, win_ref[...])
        yield
        cq = _bf(_rms(z[:, :o1], qng_ref[...]))
        ckv = _bf(_rms(z[:, o1:o2], kvng_ref[...]))
        k_rope = z[:, o3:o3 + LANE]
        ang = pos_ref[rows, :].astype(jnp.float32) * inv_ref[...]
        cos = jnp.cos(ang)
        sin = jnp.sin(ang) * sgn_ref[...]

        def rope(r):
            return r * cos + pltpu.roll(r, LANE // 2, 1) * sin

        q = _dot(cq, wq_ref[...])
        kv = _dot(ckv, wkv_ref[...])
        yield
        qg = qg_ref[...]
        kg = kg_ref[...]
        kr_ss = jnp.sum(k_rope * k_rope, axis=-1, keepdims=True)
        kr = rope(k_rope * kg[:, LANE:])
        for hh in range(MLA_HEADS):
            qh = q[:, hh * QK_PAD:(hh + 1) * QK_PAD]
            rq = lax.rsqrt(jnp.sum(qh * qh, axis=-1, keepdims=True) * (1.0 / QK_DIM) + EPS) * q_scale
            qh = qh * rq * qg
            q_ref[hh, rows, :LANE] = _bf(qh[:, :LANE])
            q_ref[hh, rows, LANE:] = _bf(rope(qh[:, LANE:]))
            kn = kv[:, hh * LANE:(hh + 1) * LANE]
            rk = lax.rsqrt((jnp.sum(kn * kn, axis=-1, keepdims=True) + kr_ss) * (1.0 / QK_DIM) + EPS)
            k_ref[hh, rows, :LANE] = _bf(kn * rk * kg[:, :LANE])
            k_ref[hh, rows, LANE:] = _bf(kr * rk)
            v_ref[hh, :, rows] = _bf(kv[:, (MLA_HEADS + hh) * LANE:(MLA_HEADS + hh + 1) * LANE].T)
            if hh % 4 == 3:
                yield
        mem_ref[rows, :] = _bf(_mem_attention(z[:, o2:o3], kn_ref, mv_ref, mqg_ref[...]))

    _run_staggered([stages(rows) for rows in _sub_rows(x_ref.shape[0])])


def _attn_kernel(q_ref, k_ref, vt_ref, o_ref, m_ref, acc_ref):
    i = pl.program_id(2)
    heads, tq = q_ref.shape[0], q_ref.shape[1]
    m_ref[...] = jnp.full_like(m_ref, NEG_BIG)
    acc_ref[...] = jnp.zeros_like(acc_ref)

    qw = tq // ATTN_QUERY_SPLIT

    def scores(hh, j, part, masked):
        keys = (part + 1) * qw if masked else tq
        start = pl.multiple_of(j * tq, tq)
        s = lax.dot_general(k_ref[hh, pl.ds(start, keys), :], q_ref[hh, part * qw:(part + 1) * qw, :], _NT,
                            preferred_element_type=jnp.float32)
        if masked:
            key = lax.broadcasted_iota(jnp.int32, (keys, qw), 0)
            qry = lax.broadcasted_iota(jnp.int32, (keys, qw), 1) + part * qw
            s = jnp.where(key <= qry, s, NEG_BIG)
        return s

    def update(hh, j, part, s):
        keys = s.shape[0]
        lanes = slice(part * qw, (part + 1) * qw)
        m = m_ref[hh, :, lanes]
        m_new = jnp.maximum(m, jnp.max(s, axis=0, keepdims=True))
        alpha = jnp.exp2(m - m_new)
        p = _bf(jnp.exp2(s - m_new))
        m_ref[hh, :, lanes] = m_new
        vt1 = jnp.concatenate([vt_ref[hh, j, :, :keys], jnp.ones((ATTN_SUM_ROWS, keys), jnp.bfloat16)], axis=0)
        acc_ref[hh, :, lanes] = alpha * acc_ref[hh, :, lanes] + _dot(vt1, p)

    def run(items):
        ss = {}
        ahead = ATTN_LOOKAHEAD
        for t in range(len(items) + ahead):
            if t < len(items):
                ss[t] = scores(*items[t])
            if t >= ahead:
                update(*items[t - ahead][:3], ss.pop(t - ahead))

    def tile_items(first, count, masked):
        return [(hh, first + u, part, masked) for u in range(count) for hh in range(heads)
                for part in range(ATTN_QUERY_SPLIT)]

    unroll = ATTN_KEY_TILES_PER_TRIP

    def body(jj, c):
        run(tile_items(unroll * jj, unroll, False))
        return c

    lax.fori_loop(0, i // unroll, body, 0)
    for left in range(unroll):
        @pl.when(i % unroll == left)
        def _():
            run(tile_items(i - left, left, False) + tile_items(i, 1, True))

    for hh in range(heads):
        acc = acc_ref[hh]
        o_ref[:, hh * V_DIM:(hh + 1) * V_DIM] = _bf((acc[:V_DIM] / acc[V_DIM:V_DIM + 1]).T)


def _layer1_out_kernel(x_ref, o_ref, mem_ref, wout_ref, g2_ref, wr_ref, br_ref,
                       xo_ref, xn_ref, slab_ref, slabt_ref, cnt_ref, carry_ref):
    d = o_ref.shape[1]
    _reset_carry(carry_ref)

    def stages(rows):
        xo = (x_ref[rows, :] + _dot(o_ref[rows, :], wout_ref[:d, :])
              + _dot(mem_ref[rows, :], wout_ref[d:, :]))
        xo_ref[rows, :] = xo
        yield
        yield from _route_stages(xo, rows, g2_ref, wr_ref, br_ref, carry_ref,
                                 xn_ref, slab_ref, slabt_ref, cnt_ref)

    _run_staggered([stages(rows) for rows in _sub_rows(x_ref.shape[0])])


def _sc_mesh():
    return plsc.VectorSubcoreMesh(core_axis_name="c", subcore_axis_name="s")


def _sc_worker_base(rows_per_worker):
    return (lax.axis_index("c") * SC_SUBCORES + lax.axis_index("s")) * rows_per_worker


def _sc_scatter_rows(x, idx0, idx1, p_rows):
    n, w = x.shape
    per = n // SC_WORKERS
    chunks = SC_INDEX_GROUP // SC_CHUNK
    assert n % SC_WORKERS == 0 and per % SC_INDEX_GROUP == 0

    @pl.kernel(out_type=jax.ShapeDtypeStruct((p_rows, w), x.dtype), mesh=_sc_mesh(),
               scratch_types=[pltpu.VMEM((1, SC_INDEX_GROUP), jnp.int32),
                              pltpu.VMEM((1, SC_INDEX_GROUP), jnp.int32),
                              pltpu.VMEM((SC_CHUNK, w), x.dtype), pltpu.VMEM((SC_CHUNK, w), x.dtype),
                              pltpu.SemaphoreType.DMA, pltpu.SemaphoreType.DMA],
               name="moe_dispatch_sc")
    def scatter(x_hbm, i0_hbm, i1_hbm, o_hbm, i0_v, i1_v, buf_a, buf_b, sem_a, sem_b):
        base = _sc_worker_base(per)

        @pl.loop(0, per // SC_INDEX_GROUP)
        def _(g):
            off = pl.multiple_of(base + g * SC_INDEX_GROUP, SC_INDEX_GROUP)
            pltpu.sync_copy(i0_hbm.at[:, pl.ds(off, SC_INDEX_GROUP)], i0_v)
            pltpu.sync_copy(i1_hbm.at[:, pl.ds(off, SC_INDEX_GROUP)], i1_v)
            pending = []
            for c in range(chunks):
                buf, sem = ((buf_a, sem_a), (buf_b, sem_b))[c % 2]
                if c >= 2:
                    for cp in pending[c - 2]:
                        cp.wait()
                pltpu.sync_copy(x_hbm.at[pl.ds(off + c * SC_CHUNK, SC_CHUNK)], buf)
                sl = pl.ds(c * SC_CHUNK, SC_CHUNK)
                pending.append((pltpu.async_copy(buf, o_hbm.at[i0_v.at[0, sl]], sem),
                                pltpu.async_copy(buf, o_hbm.at[i1_v.at[0, sl]], sem)))
            for cps in pending[max(chunks - 2, 0):]:
                for cp in cps:
                    cp.wait()

    return scatter(x, idx0.reshape(1, n), idx1.reshape(1, n))


def _sc_gather_rows(table, idx):
    m = idx.shape[0]
    w = table.shape[1]
    per = m // SC_WORKERS
    chunks = SC_INDEX_GROUP // SC_CHUNK
    assert m % SC_WORKERS == 0 and per % SC_INDEX_GROUP == 0

    @pl.kernel(out_type=jax.ShapeDtypeStruct((m, w), table.dtype), mesh=_sc_mesh(),
               scratch_types=[pltpu.VMEM((1, SC_INDEX_GROUP), jnp.int32),
                              pltpu.VMEM((SC_CHUNK, w), table.dtype), pltpu.VMEM((SC_CHUNK, w), table.dtype),
                              pltpu.SemaphoreType.DMA, pltpu.SemaphoreType.DMA],
               name="moe_combine_sc")
    def gather(t_hbm, i_hbm, o_hbm, i_v, buf_a, buf_b, sem_a, sem_b):
        base = _sc_worker_base(per)

        @pl.loop(0, per // SC_INDEX_GROUP)
        def _(g):
            off = pl.multiple_of(base + g * SC_INDEX_GROUP, SC_INDEX_GROUP)
            pltpu.sync_copy(i_hbm.at[:, pl.ds(off, SC_INDEX_GROUP)], i_v)
            pending = []
            for c in range(chunks):
                buf, sem = ((buf_a, sem_a), (buf_b, sem_b))[c % 2]
                if c >= 2:
                    pending[c - 2].wait()
                pltpu.sync_copy(t_hbm.at[i_v.at[0, pl.ds(c * SC_CHUNK, SC_CHUNK)]], buf)
                pending.append(pltpu.async_copy(buf, o_hbm.at[pl.ds(off + c * SC_CHUNK, SC_CHUNK)], sem))
            for cp in pending[max(chunks - 2, 0):]:
                cp.wait()

    return gather(table, idx.reshape(1, m))


def _ffn_kernel(exp_ref, rows_ref, after_ref, xs_ref, *refs):
    slots = ROW_TILES_PER_STEP
    w_refs = [refs[3 * s:3 * s + 3] for s in range(slots)]
    ys_ref = refs[3 * slots]
    w_bf = [refs[3 * slots + 1 + 3 * s:3 * slots + 4 + 3 * s] for s in range(slots)]
    j = pl.program_id(0)
    steps = pl.num_programs(0)

    for s in range(slots):
        tile = s * steps + j

        @pl.when((j == 0) | (exp_ref[tile] != exp_ref[jnp.maximum(tile - 1, 0)]))
        def _():
            for src, dst in zip(w_refs[s], w_bf[s]):
                dst[...] = _bf(src[...])

    def stages(s, k, rows):
        packed = xs_ref[s, rows, :]
        row_id = lax.broadcasted_iota(jnp.int32, packed.shape, 0) + k * SUB_TILE
        x = _bf(_unpack_rows(jnp.where(row_id < rows_ref[s * steps + j], packed, jnp.uint32(0))))
        wg, wu, wd = w_bf[s]
        g = _dot(x, wg[...])
        u = _dot(x, wu[...])
        yield
        act = _bf(g * jax.nn.sigmoid(g) * u)
        yield
        ys_ref[s, rows, :] = _pack_rows(_dot(act, wd[...]))

    _run_staggered([stages(s, k, rows) for k, rows in enumerate(_sub_rows(xs_ref.shape[1]))
                    for s in range(slots)])


def _combined(x_ref, slab_ref, y0_ref, y1_ref, rows):
    slab = slab_ref[rows, :]
    return (x_ref[rows, :] + slab[:, 2:3] * _unpack_rows(y0_ref[rows, :])
            + slab[:, 3:4] * _unpack_rows(y1_ref[rows, :]))


def _combine_kernel(after_ref, x_ref, slab_ref, y0_ref, y1_ref, out_ref):
    out_ref[...] = _combined(x_ref, slab_ref, y0_ref, y1_ref, slice(None))


def _moe(layer, xn, slabt, counts, w_gate, w_up, w_down, after):
    n, wp = xn.shape
    d = w_gate.shape[-2]
    rt = ROW_TILE
    p_max = 2 * n + N_EXPERTS * rt
    n_tiles = p_max // rt

    cnt = counts[:, 0].astype(jnp.int32)
    padded = ((cnt + rt - 1) // rt) * rt
    experts = jnp.arange(N_EXPERTS, dtype=jnp.int32)
    start = jnp.sum(jnp.where(experts[None, :] < experts[:, None], padded[None, :], 0), axis=1)
    end = start + padded

    def position(e_row, r_row):
        e = e_row.astype(jnp.int32)[None, :]
        return jnp.sum(jnp.where(e == experts[:, None], start[:, None], 0), axis=0) + r_row.astype(jnp.int32)

    pos0 = position(slabt[0], slabt[4])
    pos1 = position(slabt[1], slabt[5])
    tile_start = jnp.arange(n_tiles, dtype=jnp.int32) * rt
    owns = (start[None, :] <= tile_start[:, None]) & (tile_start[:, None] < end[None, :])
    past = tile_start >= end[N_EXPERTS - 1]
    tile_exp = jnp.where(past, N_EXPERTS - 1, jnp.sum(jnp.where(owns, experts[None, :], 0), axis=1))
    used = jnp.sum(jnp.where(owns, (start + cnt)[None, :], 0), axis=1)
    tile_rows = jnp.clip(used - tile_start, 0, rt).astype(jnp.int32)

    xs = _sc_scatter_rows(xn, pos0, pos1, p_max)

    f = w_gate.shape[-1]
    slots = ROW_TILES_PER_STEP
    assert n_tiles % slots == 0
    steps = n_tiles // slots
    w_specs = []
    for s in range(slots):
        pick = lambda j, ex, rw, s=s: (layer, ex[s * steps + j], 0, 0)
        w_specs += [pl.BlockSpec((None, None, d, f), pick), pl.BlockSpec((None, None, d, f), pick),
                    pl.BlockSpec((None, None, f, d), pick)]
    slot_rows = pl.BlockSpec((slots, rt, wp), lambda j, ex, rw: (0, j, 0))
    ys = pl.pallas_call(
        _ffn_kernel,
        grid_spec=pltpu.PrefetchScalarGridSpec(
            num_scalar_prefetch=2, grid=(steps,),
            in_specs=[_ANY, slot_rows] + w_specs,
            out_specs=slot_rows,
            scratch_shapes=[pltpu.VMEM((d, f), jnp.bfloat16), pltpu.VMEM((d, f), jnp.bfloat16),
                            pltpu.VMEM((f, d), jnp.bfloat16)] * slots),
        out_shape=jax.ShapeDtypeStruct((slots, steps * rt, wp), jnp.uint32),
        compiler_params=_params(("arbitrary",)),
        name="moe_ffn",
    )(tile_exp.astype(jnp.int32), tile_rows, after, xs.reshape(slots, steps * rt, wp),
      *([w_gate, w_up, w_down] * slots)).reshape(p_max, wp)

    return _sc_gather_rows(ys, jnp.concatenate([pos0, pos1])), ys


def _picked_specs(n, wp):
    nb = n // TOKEN_TILE
    return [pl.BlockSpec((TOKEN_TILE, wp), lambda i: (i, 0)),
            pl.BlockSpec((TOKEN_TILE, wp), lambda i: (i + nb, 0))]


def _router_weights(w_group, b_group, w_expert, b_expert):
    d = w_group.shape[0]
    pad = LANE - N_GROUPS - N_EXPERTS
    wr = jnp.concatenate([w_group, w_expert, jnp.zeros((d, pad), w_group.dtype)], axis=1)
    br = jnp.concatenate([b_group, b_expert, jnp.zeros((pad,), b_group.dtype)])
    return _bf(wr), br.reshape(1, LANE).astype(jnp.float32)


def _rope_lanes(vec_half):
    z = jnp.zeros_like(vec_half)
    return jnp.concatenate([vec_half, z, vec_half, z], axis=-1)


def _pad_rope_cols(w):
    z = jnp.zeros(w.shape[:-1] + (ROPE_HALF,), w.dtype)
    return jnp.concatenate([w[..., :ROPE_HALF], z, w[..., ROPE_HALF:], z], axis=-1)


TRUNK_PHASES = 5


def _trunk(chunk, x_all, mem, positions, mem_norm_g, w_mem_kv, mem_qn_g, mem_kn_g, norm1_g,
           norm2_g, a_w_in, a_ln_g, a_ln_b, a_w_s, a_b_s, a_w_out, b_w_in, b_q_norm_g, b_kv_norm_g, b_w_q_up,
           b_w_kv_up, b_qn_g, b_kn_g, b_w_out, moe_w_group, moe_b_group, moe_w_expert, moe_b_expert,
           moe_w_gate, moe_w_up, moe_w_down):
    n_total, d = x_all.shape
    b, s = positions.shape
    m = mem.shape[1]
    n = b * s
    depth = norm1_g.shape[0]
    tt = TOKEN_TILE
    ta = ATTN_TILE
    tiles_per_batch = s // tt
    assert depth == 2 and s % tt == 0 and tt % SUB_TILE == 0 and d == A_GROUPS * LANE
    assert s % ta == 0 and ta % tt == 0 and s % ROUTE_TILE == 0 and ROUTE_TILE % SUB_TILE == 0
    f32 = jnp.float32
    row = lambda v: v.reshape(1, -1).astype(f32)

    kn_all, mem_v = pl.pallas_call(
        _memkv_kernel,
        grid=(b,),
        in_specs=[pl.BlockSpec((m, d), lambda i: (i, 0)), _const_spec((1, d)),
                  _const_spec((d, 2 * MEM_W)), _const_spec((depth, 1, MEM_HEAD_DIM))],
        out_specs=[pl.BlockSpec((depth, m, MEM_W), lambda i: (0, i, 0)),
                   pl.BlockSpec((m, MEM_W), lambda i: (i, 0))],
        out_shape=[jax.ShapeDtypeStruct((depth, b * m, MEM_W), jnp.bfloat16),
                   jax.ShapeDtypeStruct((b * m, MEM_W), jnp.bfloat16)],
        compiler_params=_params(("arbitrary",)),
        name="mem_kv",
    )(mem.reshape(b * m, d), row(mem_norm_g), _bf(w_mem_kv), mem_kn_g.reshape(depth, 1, MEM_HEAD_DIM))

    tok = lambda width: pl.BlockSpec((tt, width), lambda i: (i, 0))
    kn_spec = lambda layer: pl.BlockSpec((None, m, MEM_W), lambda i: (layer, i // tiles_per_batch, 0))
    mv_spec = pl.BlockSpec((m, MEM_W), lambda i: (i // tiles_per_batch, 0))
    rt_ = ROUTE_TILE
    rtok = lambda width: pl.BlockSpec((rt_, width), lambda i: (i, 0))
    rkn_spec = lambda layer: pl.BlockSpec((None, m, MEM_W), lambda i: (layer, i // (s // rt_), 0))
    rmv_spec = pl.BlockSpec((m, MEM_W), lambda i: (i // (s // rt_), 0))
    route_out_specs = [rtok(d), rtok(d // 2), rtok(LANE), pl.BlockSpec((SUBLANE, rt_), lambda i: (0, i)),
                       pl.BlockSpec((N_EXPERTS, LANE), lambda i: (0, 0))]
    route_out_shape = [jax.ShapeDtypeStruct((n, d), f32), jax.ShapeDtypeStruct((n, d // 2), jnp.uint32),
                       jax.ShapeDtypeStruct((n, LANE), f32), jax.ShapeDtypeStruct((SUBLANE, n), f32),
                       jax.ShapeDtypeStruct((N_EXPERTS, LANE), f32)]
    route_scratch = [pltpu.VMEM((N_EXPERTS, 1), f32)]

    first_block = chunk * (n // tt)
    group_tok = pl.BlockSpec((tt, d), lambda i: (i + first_block, 0))
    first_rblock = chunk * (n // rt_)

    wr0, br0 = _router_weights(moe_w_group[0], moe_b_group[0], moe_w_expert[0], moe_b_expert[0])
    a_in = a_w_in.shape[-1]
    bias_s = jnp.repeat(a_b_s[0].T, LANE, axis=1).astype(f32)
    x2, xn, slab, slabt, counts = pl.pallas_call(
        _layer0_kernel,
        grid=(n // rt_,),
        in_specs=[pl.BlockSpec((rt_, d), lambda i: (i + first_rblock, 0)),
                  _const_spec((1, d)), _const_spec((d, a_in)), _const_spec((1, d)),
                  _const_spec((1, d)), _const_spec((A_GROUPS, CHUNK, CHUNK)), _const_spec((CHUNK, d)),
                  rkn_spec(0), rmv_spec, _const_spec((1, MEM_HEAD_DIM)),
                  _const_spec((d + MEM_W, d)), _const_spec((1, d)), _const_spec((d, LANE)),
                  _const_spec((1, LANE))],
        out_specs=route_out_specs,
        out_shape=route_out_shape,
        scratch_shapes=route_scratch,
        compiler_params=_params(("arbitrary",)),
        name="layer0_mixer",
    )(x_all, row(norm1_g[0]), _bf(a_w_in[0]), row(a_ln_g[0]), row(a_ln_b[0]), _bf(a_w_s[0]), bias_s,
      kn_all, mem_v, row(mem_qn_g[0]), _bf(a_w_out[0]), row(norm2_g[0]), wr0, br0)
    after = yield slab
    picked, ys = _moe(0, xn, slabt, counts, moe_w_gate, moe_w_up, moe_w_down, after)
    after = yield ys

    hq = MLA_HEADS
    o1, o2, o3 = Q_LORA, Q_LORA + KV_LORA, Q_LORA + KV_LORA + ROPE_DIM
    w_in = b_w_in[0]
    w_in_p = jnp.concatenate([w_in[:, :o2], w_in[:, o3:], _pad_rope_cols(w_in[:, o2:o3])], axis=1)
    wq = b_w_q_up[0].reshape(Q_LORA, hq, QK_DIM)
    wq_p = jnp.concatenate([wq[..., :NOPE_DIM], _pad_rope_cols(wq[..., NOPE_DIM:])], axis=-1)
    wq_p = wq_p.reshape(Q_LORA, hq * QK_PAD)
    wkv = b_w_kv_up[0].reshape(KV_LORA, hq, NOPE_DIM + V_DIM)
    wkv_p = jnp.concatenate([wkv[..., :NOPE_DIM].reshape(KV_LORA, hq * NOPE_DIM),
                             wkv[..., NOPE_DIM:].reshape(KV_LORA, hq * V_DIM)], axis=1)
    pad_gain = lambda g: jnp.concatenate([g[:NOPE_DIM], _pad_rope_cols(g[NOPE_DIM:])]).reshape(1, QK_PAD)
    half = jnp.arange(ROPE_HALF, dtype=f32)
    inv = ROPE_BASE ** (-(half * 2.0 / ROPE_DIM))
    inv_l = _rope_lanes(inv).reshape(1, LANE)
    sgn_l = jnp.concatenate([-jnp.ones((2 * ROPE_HALF,), f32), jnp.ones((2 * ROPE_HALF,), f32)]).reshape(1, LANE)

    in_w = w_in_p.shape[1]
    head_spec = lambda width: pl.BlockSpec((None, hq, tt, width),
                                           lambda i: (i // tiles_per_batch, 0, i % tiles_per_batch, 0))
    per_ta = ta // tt

    def vt_index(i):
        t = i % tiles_per_batch
        return (i // tiles_per_batch, 0, t // per_ta, 0, t % per_ta)

    vt_spec = pl.BlockSpec((None, hq, None, V_DIM, tt), vt_index)
    x2, q, k, vt, mem_o = pl.pallas_call(
        _layer1_proj_kernel,
        grid=(n // tt,),
        in_specs=[_ANY, tok(d), tok(LANE), *_picked_specs(n, d // 2),
                  tok(1), _const_spec((1, d)), _const_spec((d, in_w)), _const_spec((1, Q_LORA)),
                  _const_spec((1, KV_LORA)), _const_spec((Q_LORA, hq * QK_PAD)),
                  _const_spec((KV_LORA, hq * (NOPE_DIM + V_DIM))), _const_spec((1, QK_PAD)),
                  _const_spec((1, QK_PAD)), _const_spec((1, LANE)), _const_spec((1, LANE)),
                  kn_spec(1), mv_spec, _const_spec((1, MEM_HEAD_DIM))],
        out_specs=[tok(d), head_spec(QK_PAD), head_spec(QK_PAD), vt_spec, tok(MEM_W)],
        out_shape=[jax.ShapeDtypeStruct((n, d), f32),
                   jax.ShapeDtypeStruct((b, hq, s, QK_PAD), jnp.bfloat16),
                   jax.ShapeDtypeStruct((b, hq, s, QK_PAD), jnp.bfloat16),
                   jax.ShapeDtypeStruct((b, hq, s // ta, V_DIM, ta), jnp.bfloat16),
                   jax.ShapeDtypeStruct((n, MEM_W), jnp.bfloat16)],
        compiler_params=_params(("arbitrary",)),
        name="layer1_proj",
    )(after, x2, slab, picked, picked,
      positions.reshape(n, 1), row(norm1_g[1]), _bf(w_in_p), row(b_q_norm_g[0]), row(b_kv_norm_g[0]),
      _bf(wq_p), _bf(wkv_p), pad_gain(b_qn_g[0]).astype(f32), pad_gain(b_kn_g[0]).astype(f32),
      inv_l, sgn_l, kn_all, mem_v, row(mem_qn_g[1]))

    qb = s // ta
    hp = ATTN_HEADS_PER_STEP
    attn = pl.pallas_call(
        _attn_kernel,
        grid=(b, hq // hp, qb),
        in_specs=[pl.BlockSpec((None, hp, ta, QK_PAD), lambda bi, hi, i: (bi, hi, i, 0)),
                  pl.BlockSpec((None, hp, s, QK_PAD), lambda bi, hi, i: (bi, hi, 0, 0)),
                  pl.BlockSpec((None, hp, qb, V_DIM, ta), lambda bi, hi, i: (bi, hi, 0, 0, 0))],
        out_specs=pl.BlockSpec((ta, hp * V_DIM), lambda bi, hi, i: (bi * qb + i, hi)),
        out_shape=jax.ShapeDtypeStruct((n, hq * V_DIM), jnp.bfloat16),
        scratch_shapes=[pltpu.VMEM((hp, 1, ta), f32), pltpu.VMEM((hp, V_DIM + ATTN_SUM_ROWS, ta), f32)],
        compiler_params=_params(("arbitrary", "arbitrary", "arbitrary")),
        name="causal_attention",
    )(q, k, vt)

    wr1, br1 = _router_weights(moe_w_group[1], moe_b_group[1], moe_w_expert[1], moe_b_expert[1])
    x2, xn, slab, slabt, counts = pl.pallas_call(
        _layer1_out_kernel,
        grid=(n // rt_,),
        in_specs=[rtok(d), rtok(hq * V_DIM), rtok(MEM_W), _const_spec((hq * V_DIM + MEM_W, d)),
                  _const_spec((1, d)), _const_spec((d, LANE)), _const_spec((1, LANE))],
        out_specs=route_out_specs,
        out_shape=route_out_shape,
        scratch_shapes=route_scratch,
        compiler_params=_params(("arbitrary",)),
        name="layer1_out",
    )(x2, attn, mem_o, _bf(b_w_out[0]), row(norm2_g[1]), wr1, br1)
    after = yield slab
    picked, ys = _moe(1, xn, slabt, counts, moe_w_gate, moe_w_up, moe_w_down, after)
    after = yield ys
    yield pl.pallas_call(
        _combine_kernel,
        grid=(n // tt,),
        in_specs=[_ANY, tok(d), tok(LANE), *_picked_specs(n, d // 2)],
        out_specs=group_tok,
        out_shape=jax.ShapeDtypeStruct((n_total, d), f32),
        input_output_aliases={0: 0} if chunk else {},
        compiler_params=_params(("arbitrary",)),
        name="moe_combine",
    )(after, x2, slab, picked, picked)


def kernel(x, mem, positions, mem_norm_g, w_mem_kv, mem_qn_g, mem_kn_g, norm1_g, norm2_g, a_w_in, a_ln_g, a_ln_b, a_w_s, a_b_s, a_w_out, b_w_in, b_q_norm_g, b_kv_norm_g, b_w_q_up, b_w_kv_up, b_qn_g, b_kn_g, b_w_out, moe_w_group, moe_b_group, moe_w_expert, moe_b_expert, moe_w_gate, moe_w_up, moe_w_down):
    params = (mem_norm_g, w_mem_kv, mem_qn_g, mem_kn_g, norm1_g, norm2_g, a_w_in, a_ln_g, a_ln_b, a_w_s,
              a_b_s, a_w_out, b_w_in, b_q_norm_g, b_kv_norm_g, b_w_q_up, b_w_kv_up, b_qn_g, b_kn_g, b_w_out,
              moe_w_group, moe_b_group, moe_w_expert, moe_b_expert, moe_w_gate, moe_w_up, moe_w_down)
    b, s, d = x.shape
    assert BATCH_GROUPS == 2 and b % BATCH_GROUPS == 0
    g = b // BATCH_GROUPS
    x_all = x.reshape(b * s, d)
    groups = [_trunk(c, x_all, mem[c * g:(c + 1) * g], positions[c * g:(c + 1) * g], *params)
              for c in range(BATCH_GROUPS)]
    done = [next(t) for t in groups]
    for _ in range(TRUNK_PHASES - 1):
        prev, done = done, []
        for c, t in enumerate(groups):
            done.append(t.send(prev[c + 1] if c + 1 < BATCH_GROUPS else done[0]))
    return done[-1].reshape(b, s, d)
```

```python
import jax
import jax.numpy as jnp
from jax import lax
from jax.experimental import pallas as pl
from jax.experimental.pallas import tpu as pltpu
from jax.experimental.pallas import tpu_sc as plsc

EPS = 1e-6
LANE = 128
SUBLANE = 8
MEM_HEADS = 4
MEM_HEAD_DIM = 128
MEM_W = MEM_HEADS * MEM_HEAD_DIM
CHUNK = 128
A_GROUPS = 8
MLA_HEADS = 8
Q_LORA = 512
KV_LORA = 256
NOPE_DIM = 128
ROPE_DIM = 64
ROPE_HALF = ROPE_DIM // 2
V_DIM = 128
QK_DIM = NOPE_DIM + ROPE_DIM
QK_PAD = 2 * LANE
ROPE_BASE = 10000.0
N_GROUPS = 4
EXPERTS_PER_GROUP = 8
N_EXPERTS = N_GROUPS * EXPERTS_PER_GROUP
ROUTE_ROWS = 40
EXPERT_FF = 256
LOG2E = 1.4426950408889634

BATCH_GROUPS = 2
TOKEN_TILE = 512
ROUTE_TILE = 1024
SUB_TILE = 256
ROW_TILE = 512
ROW_TILES_PER_STEP = 4
ATTN_TILE = 512
ATTN_HEADS_PER_STEP = 2
ATTN_Q_TILES_PER_STEP = 2
ATTN_KEY_TILES_PER_TRIP = 4
ATTN_QUERY_SPLIT = 2
ATTN_LOOKAHEAD = 3
ATTN_SUM_ROWS = 16
VMEM_LIMIT = 56 * 1024 * 1024
NEG_BIG = -1e30

SC_CORES = 2
SC_SUBCORES = 16
SC_WORKERS = SC_CORES * SC_SUBCORES
SC_INDEX_GROUP = 128
SC_CHUNK = 64

_NT = (((1,), (1,)), ((), ()))
_ANY = pl.BlockSpec(memory_space=pl.ANY)


def _const_spec(shape):
    nd = len(shape)
    return pl.BlockSpec(shape, lambda *_: (0,) * nd, pipeline_mode=pl.Buffered(1))


def _params(sem):
    return pltpu.CompilerParams(dimension_semantics=sem, vmem_limit_bytes=VMEM_LIMIT)


def _run_staggered(gens):
    waiting = list(gens)
    active = []
    while waiting or active:
        if waiting:
            active.append(waiting.pop(0))
        for g in list(active):
            try:
                next(g)
            except StopIteration:
                active.remove(g)


def _sub_rows(t):
    return [pl.ds(k * SUB_TILE, SUB_TILE) for k in range(t // SUB_TILE)]


def _rms(x, g):
    return x * lax.rsqrt(jnp.mean(x * x, axis=-1, keepdims=True) + EPS) * g


def _gelu(x):
    return 0.5 * x * (1.0 + lax.erf(x * (2.0 ** -0.5)))


def _bf(x):
    return x.astype(jnp.bfloat16)


def _dot(a, b):
    return jnp.dot(a, b, preferred_element_type=jnp.float32)


def _memkv_kernel(mem_ref, g_ref, w_ref, kng_ref, kn_ref, v_ref):
    h = _bf(_rms(mem_ref[...], g_ref[...]))
    kv = _dot(h, w_ref[...])
    v_ref[...] = _bf(kv[:, MEM_W:])
    for layer in range(kn_ref.shape[0]):
        g = kng_ref[layer]
        for hh in range(MEM_HEADS):
            k = kv[:, hh * LANE:(hh + 1) * LANE]
            kn_ref[layer, :, hh * LANE:(hh + 1) * LANE] = _bf(_rms(k, g))


def _mem_attention(qm, kn_ref, v_ref, qg):
    outs = []
    for hh in range(MEM_HEADS):
        sl = slice(hh * LANE, (hh + 1) * LANE)
        q = _rms(qm[:, sl], qg) * (MEM_HEAD_DIM ** -0.5)
        s = lax.dot_general(_bf(q), kn_ref[:, sl], _NT, preferred_element_type=jnp.float32)
        p = jnp.exp(s - jnp.max(s, axis=-1, keepdims=True))
        l = jnp.sum(p, axis=-1, keepdims=True)
        outs.append(_dot(_bf(p), v_ref[:, sl]) / l)
    return jnp.concatenate(outs, axis=-1)


def _pack_rows(x):
    w = x.shape[1] // 2
    bits = lambda v: lax.bitcast_convert_type(_bf(v).astype(jnp.float32), jnp.uint32)
    return (bits(x[:, :w]) >> 16) | (bits(x[:, w:]) & jnp.uint32(0xFFFF0000))


def _unpack_rows(p):
    lo = lax.bitcast_convert_type(p << 16, jnp.float32)
    hi = lax.bitcast_convert_type(p & jnp.uint32(0xFFFF0000), jnp.float32)
    return jnp.concatenate([lo, hi], axis=-1)


def _route_stages(x, rows, g2_ref, wr_ref, br_ref, carry_ref, xn_ref, slab_ref, slabt_ref, cnt_ref):
    t = x.shape[0]
    xn = _rms(x, g2_ref[...])
    xn_ref[rows, :] = _pack_rows(xn)
    logits = _dot(_bf(xn), wr_ref[...]) + br_ref[...]
    yield
    lt = logits.T[:ROUTE_ROWS, :]
    row = lax.broadcasted_iota(jnp.int32, lt.shape, 0)

    def first_max(v):
        m = jnp.max(v, axis=0, keepdims=True)
        idx = jnp.min(jnp.where(v == m, row, ROUTE_ROWS), axis=0, keepdims=True)
        return m, idx

    lg = jnp.where(row < N_GROUPS, lt, NEG_BIG)
    gmax, gidx = first_max(lg)
    g_w = 1.0 / jnp.sum(jnp.exp(lg - gmax), axis=0, keepdims=True)

    eid = row - N_GROUPS
    in_grp = (eid >= 0) & (eid < N_EXPERTS) & ((eid >> 3) == gidx)
    le = jnp.where(in_grp, lt, NEG_BIG)
    m1, i1 = first_max(le)
    m2, i2 = first_max(jnp.where(row == i1, NEG_BIG, le))
    r = jnp.exp(m2 - m1)
    w1 = g_w / (1.0 + r)
    w2 = w1 * r
    e1 = i1 - N_GROUPS
    e2 = i2 - N_GROUPS
    yield

    expert = lax.broadcasted_iota(jnp.int32, (N_EXPERTS, t), 0)
    oh1 = expert == e1
    oh2 = expert == e2
    oh = jnp.where(oh1 | oh2, 1.0, 0.0)
    src = lax.broadcasted_iota(jnp.int32, (t, t), 0)
    dst = lax.broadcasted_iota(jnp.int32, (t, t), 1)
    earlier = jnp.where(src < dst, 1.0, 0.0).astype(jnp.bfloat16)
    before = _dot(_bf(oh), earlier) + carry_ref[...]
    r1 = jnp.sum(jnp.where(oh1, before, 0.0), axis=0, keepdims=True)
    r2 = jnp.sum(jnp.where(oh2, before, 0.0), axis=0, keepdims=True)
    carry_ref[...] += jnp.sum(oh, axis=1, keepdims=True)
    cnt_ref[...] = jnp.broadcast_to(carry_ref[...], cnt_ref.shape)

    table = jnp.concatenate([e1.astype(jnp.float32), e2.astype(jnp.float32), w1, w2, r1, r2,
                             jnp.zeros((LANE - 6, t), jnp.float32)], axis=0)
    slabt_ref[:, rows] = table[:SUBLANE, :]
    slab_ref[rows, :] = table.T


def _reset_carry(carry_ref):
    @pl.when(pl.program_id(0) == 0)
    def _():
        carry_ref[...] = jnp.zeros_like(carry_ref)


def _layer0_kernel(x_ref, g1_ref, win_ref, lng_ref, lnb_ref, ws_ref, bs_ref, kn_ref, v_ref, qg_ref,
                   wout_ref, g2_ref, wr_ref, br_ref,
                   xo_ref, xn_ref, slab_ref, slabt_ref, cnt_ref, carry_ref):
    d = x_ref.shape[1]
    _reset_carry(carry_ref)
    row = lax.broadcasted_iota(jnp.int32, (CHUNK, CHUNK), 0)
    col = lax.broadcasted_iota(jnp.int32, (CHUNK, CHUNK), 1)
    causal = row >= col

    def stages(rows):
        x = x_ref[rows, :]
        t = x.shape[0]
        h = _bf(_rms(x, g1_ref[...]))
        z = _dot(h, win_ref[...])
        yield
        u = _gelu(z[:, :d])
        v = _gelu(z[:, d:2 * d])
        mu = jnp.mean(v, axis=-1, keepdims=True)
        vc = v - mu
        var = jnp.mean(vc * vc, axis=-1, keepdims=True)
        v = _bf(vc * lax.rsqrt(var + EPS) * lng_ref[...] + lnb_ref[...])
        yield
        nc = t // CHUNK
        per_group = []
        for g in range(A_GROUPS):
            w = jnp.where(causal, ws_ref[g], jnp.zeros((), ws_ref.dtype))
            vg = jnp.concatenate([v[c * CHUNK:(c + 1) * CHUNK, g * LANE:(g + 1) * LANE] for c in range(nc)],
                                 axis=-1)
            per_group.append(_dot(w, vg))
        chunks = [jnp.concatenate([pg[:, c * LANE:(c + 1) * LANE] for pg in per_group], axis=-1) + bs_ref[...]
                  for c in range(nc)]
        mix = _bf(u * jnp.concatenate(chunks, axis=0))
        yield
        mem = _bf(_mem_attention(z[:, 2 * d:], kn_ref, v_ref, qg_ref[...]))
        yield
        xo = x + _dot(mix, wout_ref[:d, :]) + _dot(mem, wout_ref[d:, :])
        xo_ref[rows, :] = xo
        yield
        yield from _route_stages(xo, rows, g2_ref, wr_ref, br_ref, carry_ref,
                                 xn_ref, slab_ref, slabt_ref, cnt_ref)

    _run_staggered([stages(rows) for rows in _sub_rows(x_ref.shape[0])])


def _layer1_proj_kernel(after_ref, x_ref, slab_ref, y0_ref, y1_ref, pos_ref, g1_ref, win_ref, qng_ref, kvng_ref,
                        wq_ref, wkv_ref, qg_ref, kg_ref, inv_ref, sgn_ref, kn_ref, mv_ref, mqg_ref,
                        xo_ref, q_ref, k_ref, v_ref, mem_ref):
    o1 = Q_LORA
    o2 = o1 + KV_LORA
    o3 = o2 + MEM_W
    q_scale = (QK_DIM ** -0.5) * LOG2E

    def stages(rows):
        x = _combined(x_ref, slab_ref, y0_ref, y1_ref, rows)
        xo_ref[rows, :] = x
        h = _bf(_rms(x, g1_ref[...]))
        z = _dot(h, win_ref[...])
        yield
        cq = _bf(_rms(z[:, :o1], qng_ref[...]))
        ckv = _bf(_rms(z[:, o1:o2], kvng_ref[...]))
        k_rope = z[:, o3:o3 + LANE]
        ang = pos_ref[rows, :].astype(jnp.float32) * inv_ref[...]
        cos = jnp.cos(ang)
        sin = jnp.sin(ang) * sgn_ref[...]

        def rope(r):
            return r * cos + pltpu.roll(r, LANE // 2, 1) * sin

        q = _dot(cq, wq_ref[...])
        kv = _dot(ckv, wkv_ref[...])
        yield
        qg = qg_ref[...]
        kg = kg_ref[...]
        kr_ss = jnp.sum(k_rope * k_rope, axis=-1, keepdims=True)
        kr = rope(k_rope * kg[:, LANE:])
        for hh in range(MLA_HEADS):
            qh = q[:, hh * QK_PAD:(hh + 1) * QK_PAD]
            rq = lax.rsqrt(jnp.sum(qh * qh, axis=-1, keepdims=True) * (1.0 / QK_DIM) + EPS) * q_scale
            qh = qh * rq * qg
            q_ref[hh, rows, :LANE] = _bf(qh[:, :LANE])
            q_ref[hh, rows, LANE:] = _bf(rope(qh[:, LANE:]))
            kn = kv[:, hh * LANE:(hh + 1) * LANE]
            rk = lax.rsqrt((jnp.sum(kn * kn, axis=-1, keepdims=True) + kr_ss) * (1.0 / QK_DIM) + EPS)
            k_ref[hh, rows, :LANE] = _bf(kn * rk * kg[:, :LANE])
            k_ref[hh, rows, LANE:] = _bf(kr * rk)
            v_ref[hh, :, rows] = _bf(kv[:, (MLA_HEADS + hh) * LANE:(MLA_HEADS + hh + 1) * LANE].T)
            if hh % 4 == 3:
                yield
        mem_ref[rows, :] = _bf(_mem_attention(z[:, o2:o3], kn_ref, mv_ref, mqg_ref[...]))

    _run_staggered([stages(rows) for rows in _sub_rows(x_ref.shape[0])])


def _attn_kernel(q_ref, k_ref, vt_ref, o_ref, m_ref, acc_ref):
    per_step = q_ref.shape[1] // ATTN_TILE
    for u in range(per_step):
        _attn_query_tile(pl.program_id(2) * per_step + u, u * ATTN_TILE,
                         q_ref, k_ref, vt_ref, o_ref, m_ref, acc_ref)


def _attn_query_tile(i, q0, q_ref, k_ref, vt_ref, o_ref, m_ref, acc_ref):
    heads, tq = q_ref.shape[0], ATTN_TILE
    m_ref[...] = jnp.full_like(m_ref, NEG_BIG)
    acc_ref[...] = jnp.zeros_like(acc_ref)

    qw = tq // ATTN_QUERY_SPLIT

    def scores(hh, j, part, masked):
        keys = (part + 1) * qw if masked else tq
        start = pl.multiple_of(j * tq, tq)
        q = q_ref[hh, q0 + part * qw:q0 + (part + 1) * qw, :]
        s = lax.dot_general(k_ref[hh, pl.ds(start, keys), :], q, _NT, preferred_element_type=jnp.float32)
        if masked:
            key = lax.broadcasted_iota(jnp.int32, (keys, qw), 0)
            qry = lax.broadcasted_iota(jnp.int32, (keys, qw), 1) + part * qw
            s = jnp.where(key <= qry, s, NEG_BIG)
        return s

    def update(hh, j, part, s):
        keys = s.shape[0]
        lanes = slice(part * qw, (part + 1) * qw)
        m = m_ref[hh, :, lanes]
        m_new = jnp.maximum(m, jnp.max(s, axis=0, keepdims=True))
        alpha = jnp.exp2(m - m_new)
        p = _bf(jnp.exp2(s - m_new))
        m_ref[hh, :, lanes] = m_new
        vt1 = jnp.concatenate([vt_ref[hh, j, :, :keys], jnp.ones((ATTN_SUM_ROWS, keys), jnp.bfloat16)], axis=0)
        acc_ref[hh, :, lanes] = alpha * acc_ref[hh, :, lanes] + _dot(vt1, p)

    def run(items):
        ss = {}
        ahead = ATTN_LOOKAHEAD
        for t in range(len(items) + ahead):
            if t < len(items):
                ss[t] = scores(*items[t])
            if t >= ahead:
                update(*items[t - ahead][:3], ss.pop(t - ahead))

    def tile_items(first, count, masked):
        return [(hh, first + u, part, masked) for u in range(count) for hh in range(heads)
                for part in range(ATTN_QUERY_SPLIT)]

    unroll = ATTN_KEY_TILES_PER_TRIP

    def body(jj, c):
        run(tile_items(unroll * jj, unroll, False))
        return c

    lax.fori_loop(0, i // unroll, body, 0)
    for left in range(unroll):
        @pl.when(i % unroll == left)
        def _():
            run(tile_items(i - left, left, False) + tile_items(i, 1, True))

    for hh in range(heads):
        acc = acc_ref[hh]
        o_ref[q0:q0 + tq, hh * V_DIM:(hh + 1) * V_DIM] = _bf((acc[:V_DIM] / acc[V_DIM:V_DIM + 1]).T)


def _layer1_out_kernel(x_ref, o_ref, mem_ref, wout_ref, g2_ref, wr_ref, br_ref,
                       xo_ref, xn_ref, slab_ref, slabt_ref, cnt_ref, carry_ref):
    d = o_ref.shape[1]
    _reset_carry(carry_ref)

    def stages(rows):
        xo = (x_ref[rows, :] + _dot(o_ref[rows, :], wout_ref[:d, :])
              + _dot(mem_ref[rows, :], wout_ref[d:, :]))
        xo_ref[rows, :] = xo
        yield
        yield from _route_stages(xo, rows, g2_ref, wr_ref, br_ref, carry_ref,
                                 xn_ref, slab_ref, slabt_ref, cnt_ref)

    _run_staggered([stages(rows) for rows in _sub_rows(x_ref.shape[0])])


def _sc_mesh():
    return plsc.VectorSubcoreMesh(core_axis_name="c", subcore_axis_name="s")


def _sc_worker_base(rows_per_worker):
    return (lax.axis_index("c") * SC_SUBCORES + lax.axis_index("s")) * rows_per_worker


def _sc_scatter_rows(x, idx0, idx1, p_rows):
    n, w = x.shape
    per = n // SC_WORKERS
    chunks = SC_INDEX_GROUP // SC_CHUNK
    assert n % SC_WORKERS == 0 and per % SC_INDEX_GROUP == 0

    @pl.kernel(out_type=jax.ShapeDtypeStruct((p_rows, w), x.dtype), mesh=_sc_mesh(),
               scratch_types=[pltpu.VMEM((1, SC_INDEX_GROUP), jnp.int32),
                              pltpu.VMEM((1, SC_INDEX_GROUP), jnp.int32),
                              pltpu.VMEM((SC_CHUNK, w), x.dtype), pltpu.VMEM((SC_CHUNK, w), x.dtype),
                              pltpu.SemaphoreType.DMA, pltpu.SemaphoreType.DMA],
               name="moe_dispatch_sc")
    def scatter(x_hbm, i0_hbm, i1_hbm, o_hbm, i0_v, i1_v, buf_a, buf_b, sem_a, sem_b):
        base = _sc_worker_base(per)

        @pl.loop(0, per // SC_INDEX_GROUP)
        def _(g):
            off = pl.multiple_of(base + g * SC_INDEX_GROUP, SC_INDEX_GROUP)
            pltpu.sync_copy(i0_hbm.at[:, pl.ds(off, SC_INDEX_GROUP)], i0_v)
            pltpu.sync_copy(i1_hbm.at[:, pl.ds(off, SC_INDEX_GROUP)], i1_v)
            pending = []
            for c in range(chunks):
                buf, sem = ((buf_a, sem_a), (buf_b, sem_b))[c % 2]
                if c >= 2:
                    for cp in pending[c - 2]:
                        cp.wait()
                pltpu.sync_copy(x_hbm.at[pl.ds(off + c * SC_CHUNK, SC_CHUNK)], buf)
                sl = pl.ds(c * SC_CHUNK, SC_CHUNK)
                pending.append((pltpu.async_copy(buf, o_hbm.at[i0_v.at[0, sl]], sem),
                                pltpu.async_copy(buf, o_hbm.at[i1_v.at[0, sl]], sem)))
            for cps in pending[max(chunks - 2, 0):]:
                for cp in cps:
                    cp.wait()

    return scatter(x, idx0.reshape(1, n), idx1.reshape(1, n))


def _sc_gather_rows(table, idx):
    m = idx.shape[0]
    w = table.shape[1]
    per = m // SC_WORKERS
    chunks = SC_INDEX_GROUP // SC_CHUNK
    assert m % SC_WORKERS == 0 and per % SC_INDEX_GROUP == 0

    @pl.kernel(out_type=jax.ShapeDtypeStruct((m, w), table.dtype), mesh=_sc_mesh(),
               scratch_types=[pltpu.VMEM((1, SC_INDEX_GROUP), jnp.int32),
                              pltpu.VMEM((SC_CHUNK, w), table.dtype), pltpu.VMEM((SC_CHUNK, w), table.dtype),
                              pltpu.SemaphoreType.DMA, pltpu.SemaphoreType.DMA],
               name="moe_combine_sc")
    def gather(t_hbm, i_hbm, o_hbm, i_v, buf_a, buf_b, sem_a, sem_b):
        base = _sc_worker_base(per)

        @pl.loop(0, per // SC_INDEX_GROUP)
        def _(g):
            off = pl.multiple_of(base + g * SC_INDEX_GROUP, SC_INDEX_GROUP)
            pltpu.sync_copy(i_hbm.at[:, pl.ds(off, SC_INDEX_GROUP)], i_v)
            pending = []
            for c in range(chunks):
                buf, sem = ((buf_a, sem_a), (buf_b, sem_b))[c % 2]
                if c >= 2:
                    pending[c - 2].wait()
                pltpu.sync_copy(t_hbm.at[i_v.at[0, pl.ds(c * SC_CHUNK, SC_CHUNK)]], buf)
                pending.append(pltpu.async_copy(buf, o_hbm.at[pl.ds(off + c * SC_CHUNK, SC_CHUNK)], sem))
            for cp in pending[max(chunks - 2, 0):]:
                cp.wait()

    return gather(table, idx.reshape(1, m))


def _ffn_kernel(exp_ref, rows_ref, after_ref, xs_ref, *refs):
    slots = ROW_TILES_PER_STEP
    w_refs = [refs[3 * s:3 * s + 3] for s in range(slots)]
    ys_ref = refs[3 * slots]
    w_bf = [refs[3 * slots + 1 + 3 * s:3 * slots + 4 + 3 * s] for s in range(slots)]
    j = pl.program_id(0)
    steps = pl.num_programs(0)

    for s in range(slots):
        tile = s * steps + j

        @pl.when((j == 0) | (exp_ref[tile] != exp_ref[jnp.maximum(tile - 1, 0)]))
        def _():
            for src, dst in zip(w_refs[s], w_bf[s]):
                dst[...] = _bf(src[...])

    def stages(s, k, rows):
        packed = xs_ref[s, rows, :]
        row_id = lax.broadcasted_iota(jnp.int32, packed.shape, 0) + k * SUB_TILE
        x = _bf(_unpack_rows(jnp.where(row_id < rows_ref[s * steps + j], packed, jnp.uint32(0))))
        wg, wu, wd = w_bf[s]
        g = _dot(x, wg[...])
        u = _dot(x, wu[...])
        yield
        act = _bf(g * jax.nn.sigmoid(g) * u)
        yield
        ys_ref[s, rows, :] = _pack_rows(_dot(act, wd[...]))

    _run_staggered([stages(s, k, rows) for k, rows in enumerate(_sub_rows(xs_ref.shape[1]))
                    for s in range(slots)])


def _combined(x_ref, slab_ref, y0_ref, y1_ref, rows):
    slab = slab_ref[rows, :]
    return (x_ref[rows, :] + slab[:, 2:3] * _unpack_rows(y0_ref[rows, :])
            + slab[:, 3:4] * _unpack_rows(y1_ref[rows, :]))


def _combine_kernel(after_ref, x_ref, slab_ref, y0_ref, y1_ref, out_ref):
    out_ref[...] = _combined(x_ref, slab_ref, y0_ref, y1_ref, slice(None))


def _moe(layer, xn, slabt, counts, w_gate, w_up, w_down, after):
    n, wp = xn.shape
    d = w_gate.shape[-2]
    rt = ROW_TILE
    p_max = 2 * n + N_EXPERTS * rt
    n_tiles = p_max // rt

    cnt = counts[:, 0].astype(jnp.int32)
    padded = ((cnt + rt - 1) // rt) * rt
    experts = jnp.arange(N_EXPERTS, dtype=jnp.int32)
    start = jnp.sum(jnp.where(experts[None, :] < experts[:, None], padded[None, :], 0), axis=1)
    end = start + padded

    def position(e_row, r_row):
        e = e_row.astype(jnp.int32)[None, :]
        return jnp.sum(jnp.where(e == experts[:, None], start[:, None], 0), axis=0) + r_row.astype(jnp.int32)

    pos0 = position(slabt[0], slabt[4])
    pos1 = position(slabt[1], slabt[5])
    tile_start = jnp.arange(n_tiles, dtype=jnp.int32) * rt
    owns = (start[None, :] <= tile_start[:, None]) & (tile_start[:, None] < end[None, :])
    past = tile_start >= end[N_EXPERTS - 1]
    tile_exp = jnp.where(past, N_EXPERTS - 1, jnp.sum(jnp.where(owns, experts[None, :], 0), axis=1))
    used = jnp.sum(jnp.where(owns, (start + cnt)[None, :], 0), axis=1)
    tile_rows = jnp.clip(used - tile_start, 0, rt).astype(jnp.int32)

    xs = _sc_scatter_rows(xn, pos0, pos1, p_max)

    f = w_gate.shape[-1]
    slots = ROW_TILES_PER_STEP
    assert n_tiles % slots == 0
    steps = n_tiles // slots
    w_specs = []
    for s in range(slots):
        pick = lambda j, ex, rw, s=s: (layer, ex[s * steps + j], 0, 0)
        w_specs += [pl.BlockSpec((None, None, d, f), pick), pl.BlockSpec((None, None, d, f), pick),
                    pl.BlockSpec((None, None, f, d), pick)]
    slot_rows = pl.BlockSpec((slots, rt, wp), lambda j, ex, rw: (0, j, 0))
    ys = pl.pallas_call(
        _ffn_kernel,
        grid_spec=pltpu.PrefetchScalarGridSpec(
            num_scalar_prefetch=2, grid=(steps,),
            in_specs=[_ANY, slot_rows] + w_specs,
            out_specs=slot_rows,
            scratch_shapes=[pltpu.VMEM((d, f), jnp.bfloat16), pltpu.VMEM((d, f), jnp.bfloat16),
                            pltpu.VMEM((f, d), jnp.bfloat16)] * slots),
        out_shape=jax.ShapeDtypeStruct((slots, steps * rt, wp), jnp.uint32),
        compiler_params=_params(("arbitrary",)),
        name="moe_ffn",
    )(tile_exp.astype(jnp.int32), tile_rows, after, xs.reshape(slots, steps * rt, wp),
      *([w_gate, w_up, w_down] * slots)).reshape(p_max, wp)

    return _sc_gather_rows(ys, jnp.concatenate([pos0, pos1])), ys


def _picked_specs(n, wp, tile):
    nb = n // tile
    return [pl.BlockSpec((tile, wp), lambda i: (i, 0)),
            pl.BlockSpec((tile, wp), lambda i: (i + nb, 0))]


def _router_weights(w_group, b_group, w_expert, b_expert):
    d = w_group.shape[0]
    pad = LANE - N_GROUPS - N_EXPERTS
    wr = jnp.concatenate([w_group, w_expert, jnp.zeros((d, pad), w_group.dtype)], axis=1)
    br = jnp.concatenate([b_group, b_expert, jnp.zeros((pad,), b_group.dtype)])
    return _bf(wr), br.reshape(1, LANE).astype(jnp.float32)


def _rope_lanes(vec_half):
    z = jnp.zeros_like(vec_half)
    return jnp.concatenate([vec_half, z, vec_half, z], axis=-1)


def _pad_rope_cols(w):
    z = jnp.zeros(w.shape[:-1] + (ROPE_HALF,), w.dtype)
    return jnp.concatenate([w[..., :ROPE_HALF], z, w[..., ROPE_HALF:], z], axis=-1)


TRUNK_PHASES = 5


def _trunk(chunk, x_all, mem, positions, mem_norm_g, w_mem_kv, mem_qn_g, mem_kn_g, norm1_g,
           norm2_g, a_w_in, a_ln_g, a_ln_b, a_w_s, a_b_s, a_w_out, b_w_in, b_q_norm_g, b_kv_norm_g, b_w_q_up,
           b_w_kv_up, b_qn_g, b_kn_g, b_w_out, moe_w_group, moe_b_group, moe_w_expert, moe_b_expert,
           moe_w_gate, moe_w_up, moe_w_down):
    n_total, d = x_all.shape
    b, s = positions.shape
    m = mem.shape[1]
    n = b * s
    depth = norm1_g.shape[0]
    tt = TOKEN_TILE
    ta = ATTN_TILE
    tiles_per_batch = s // tt
    assert depth == 2 and s % tt == 0 and tt % SUB_TILE == 0 and d == A_GROUPS * LANE
    assert s % ta == 0 and ta % tt == 0 and s % ROUTE_TILE == 0 and ROUTE_TILE % SUB_TILE == 0
    f32 = jnp.float32
    row = lambda v: v.reshape(1, -1).astype(f32)

    kn_all, mem_v = pl.pallas_call(
        _memkv_kernel,
        grid=(b,),
        in_specs=[pl.BlockSpec((m, d), lambda i: (i, 0)), _const_spec((1, d)),
                  _const_spec((d, 2 * MEM_W)), _const_spec((depth, 1, MEM_HEAD_DIM))],
        out_specs=[pl.BlockSpec((depth, m, MEM_W), lambda i: (0, i, 0)),
                   pl.BlockSpec((m, MEM_W), lambda i: (i, 0))],
        out_shape=[jax.ShapeDtypeStruct((depth, b * m, MEM_W), jnp.bfloat16),
                   jax.ShapeDtypeStruct((b * m, MEM_W), jnp.bfloat16)],
        compiler_params=_params(("arbitrary",)),
        name="mem_kv",
    )(mem.reshape(b * m, d), row(mem_norm_g), _bf(w_mem_kv), mem_kn_g.reshape(depth, 1, MEM_HEAD_DIM))

    tok = lambda width: pl.BlockSpec((tt, width), lambda i: (i, 0))
    kn_spec = lambda layer: pl.BlockSpec((None, m, MEM_W), lambda i: (layer, i // tiles_per_batch, 0))
    mv_spec = pl.BlockSpec((m, MEM_W), lambda i: (i // tiles_per_batch, 0))
    rt_ = ROUTE_TILE
    rtok = lambda width: pl.BlockSpec((rt_, width), lambda i: (i, 0))
    rkn_spec = lambda layer: pl.BlockSpec((None, m, MEM_W), lambda i: (layer, i // (s // rt_), 0))
    rmv_spec = pl.BlockSpec((m, MEM_W), lambda i: (i // (s // rt_), 0))
    route_out_specs = [rtok(d), rtok(d // 2), rtok(LANE), pl.BlockSpec((SUBLANE, rt_), lambda i: (0, i)),
                       pl.BlockSpec((N_EXPERTS, LANE), lambda i: (0, 0))]
    route_out_shape = [jax.ShapeDtypeStruct((n, d), f32), jax.ShapeDtypeStruct((n, d // 2), jnp.uint32),
                       jax.ShapeDtypeStruct((n, LANE), f32), jax.ShapeDtypeStruct((SUBLANE, n), f32),
                       jax.ShapeDtypeStruct((N_EXPERTS, LANE), f32)]
    route_scratch = [pltpu.VMEM((N_EXPERTS, 1), f32)]

    first_block = chunk * (n // tt)
    group_tok = pl.BlockSpec((tt, d), lambda i: (i + first_block, 0))
    first_rblock = chunk * (n // rt_)

    wr0, br0 = _router_weights(moe_w_group[0], moe_b_group[0], moe_w_expert[0], moe_b_expert[0])
    a_in = a_w_in.shape[-1]
    bias_s = jnp.repeat(a_b_s[0].T, LANE, axis=1).astype(f32)
    x2, xn, slab, slabt, counts = pl.pallas_call(
        _layer0_kernel,
        grid=(n // rt_,),
        in_specs=[pl.BlockSpec((rt_, d), lambda i: (i + first_rblock, 0)),
                  _const_spec((1, d)), _const_spec((d, a_in)), _const_spec((1, d)),
                  _const_spec((1, d)), _const_spec((A_GROUPS, CHUNK, CHUNK)), _const_spec((CHUNK, d)),
                  rkn_spec(0), rmv_spec, _const_spec((1, MEM_HEAD_DIM)),
                  _const_spec((d + MEM_W, d)), _const_spec((1, d)), _const_spec((d, LANE)),
                  _const_spec((1, LANE))],
        out_specs=route_out_specs,
        out_shape=route_out_shape,
        scratch_shapes=route_scratch,
        compiler_params=_params(("arbitrary",)),
        name="layer0_mixer",
    )(x_all, row(norm1_g[0]), _bf(a_w_in[0]), row(a_ln_g[0]), row(a_ln_b[0]), _bf(a_w_s[0]), bias_s,
      kn_all, mem_v, row(mem_qn_g[0]), _bf(a_w_out[0]), row(norm2_g[0]), wr0, br0)
    after = yield slab
    picked, ys = _moe(0, xn, slabt, counts, moe_w_gate, moe_w_up, moe_w_down, after)
    after = yield ys

    hq = MLA_HEADS
    o1, o2, o3 = Q_LORA, Q_LORA + KV_LORA, Q_LORA + KV_LORA + ROPE_DIM
    w_in = b_w_in[0]
    w_in_p = jnp.concatenate([w_in[:, :o2], w_in[:, o3:], _pad_rope_cols(w_in[:, o2:o3])], axis=1)
    wq = b_w_q_up[0].reshape(Q_LORA, hq, QK_DIM)
    wq_p = jnp.concatenate([wq[..., :NOPE_DIM], _pad_rope_cols(wq[..., NOPE_DIM:])], axis=-1)
    wq_p = wq_p.reshape(Q_LORA, hq * QK_PAD)
    wkv = b_w_kv_up[0].reshape(KV_LORA, hq, NOPE_DIM + V_DIM)
    wkv_p = jnp.concatenate([wkv[..., :NOPE_DIM].reshape(KV_LORA, hq * NOPE_DIM),
                             wkv[..., NOPE_DIM:].reshape(KV_LORA, hq * V_DIM)], axis=1)
    pad_gain = lambda g: jnp.concatenate([g[:NOPE_DIM], _pad_rope_cols(g[NOPE_DIM:])]).reshape(1, QK_PAD)
    half = jnp.arange(ROPE_HALF, dtype=f32)
    inv = ROPE_BASE ** (-(half * 2.0 / ROPE_DIM))
    inv_l = _rope_lanes(inv).reshape(1, LANE)
    sgn_l = jnp.concatenate([-jnp.ones((2 * ROPE_HALF,), f32), jnp.ones((2 * ROPE_HALF,), f32)]).reshape(1, LANE)

    in_w = w_in_p.shape[1]
    head_spec = lambda width: pl.BlockSpec((None, hq, tt, width),
                                           lambda i: (i // tiles_per_batch, 0, i % tiles_per_batch, 0))
    per_ta = ta // tt

    def vt_index(i):
        t = i % tiles_per_batch
        return (i // tiles_per_batch, 0, t // per_ta, 0, t % per_ta)

    vt_spec = pl.BlockSpec((None, hq, None, V_DIM, tt), vt_index)
    x2, q, k, vt, mem_o = pl.pallas_call(
        _layer1_proj_kernel,
        grid=(n // tt,),
        in_specs=[_ANY, tok(d), tok(LANE), *_picked_specs(n, d // 2, tt),
                  tok(1), _const_spec((1, d)), _const_spec((d, in_w)), _const_spec((1, Q_LORA)),
                  _const_spec((1, KV_LORA)), _const_spec((Q_LORA, hq * QK_PAD)),
                  _const_spec((KV_LORA, hq * (NOPE_DIM + V_DIM))), _const_spec((1, QK_PAD)),
                  _const_spec((1, QK_PAD)), _const_spec((1, LANE)), _const_spec((1, LANE)),
                  kn_spec(1), mv_spec, _const_spec((1, MEM_HEAD_DIM))],
        out_specs=[tok(d), head_spec(QK_PAD), head_spec(QK_PAD), vt_spec, tok(MEM_W)],
        out_shape=[jax.ShapeDtypeStruct((n, d), f32),
                   jax.ShapeDtypeStruct((b, hq, s, QK_PAD), jnp.bfloat16),
                   jax.ShapeDtypeStruct((b, hq, s, QK_PAD), jnp.bfloat16),
                   jax.ShapeDtypeStruct((b, hq, s // ta, V_DIM, ta), jnp.bfloat16),
                   jax.ShapeDtypeStruct((n, MEM_W), jnp.bfloat16)],
        compiler_params=_params(("arbitrary",)),
        name="layer1_proj",
    )(after, x2, slab, picked, picked,
      positions.reshape(n, 1), row(norm1_g[1]), _bf(w_in_p), row(b_q_norm_g[0]), row(b_kv_norm_g[0]),
      _bf(wq_p), _bf(wkv_p), pad_gain(b_qn_g[0]).astype(f32), pad_gain(b_kn_g[0]).astype(f32),
      inv_l, sgn_l, kn_all, mem_v, row(mem_qn_g[1]))

    qb = s // ta
    hp = ATTN_HEADS_PER_STEP
    qt = ATTN_Q_TILES_PER_STEP
    assert qb % qt == 0
    attn = pl.pallas_call(
        _attn_kernel,
        grid=(b, hq // hp, qb // qt),
        in_specs=[pl.BlockSpec((None, hp, qt * ta, QK_PAD), lambda bi, hi, i: (bi, hi, i, 0)),
                  pl.BlockSpec((None, hp, s, QK_PAD), lambda bi, hi, i: (bi, hi, 0, 0)),
                  pl.BlockSpec((None, hp, qb, V_DIM, ta), lambda bi, hi, i: (bi, hi, 0, 0, 0))],
        out_specs=pl.BlockSpec((qt * ta, hp * V_DIM), lambda bi, hi, i: (bi * (qb // qt) + i, hi)),
        out_shape=jax.ShapeDtypeStruct((n, hq * V_DIM), jnp.bfloat16),
        scratch_shapes=[pltpu.VMEM((hp, 1, ta), f32), pltpu.VMEM((hp, V_DIM + ATTN_SUM_ROWS, ta), f32)],
        compiler_params=_params(("arbitrary", "arbitrary", "arbitrary")),
        name="causal_attention",
    )(q, k, vt)

    wr1, br1 = _router_weights(moe_w_group[1], moe_b_group[1], moe_w_expert[1], moe_b_expert[1])
    x2, xn, slab, slabt, counts = pl.pallas_call(
        _layer1_out_kernel,
        grid=(n // rt_,),
        in_specs=[rtok(d), rtok(hq * V_DIM), rtok(MEM_W), _const_spec((hq * V_DIM + MEM_W, d)),
                  _const_spec((1, d)), _const_spec((d, LANE)), _const_spec((1, LANE))],
        out_specs=route_out_specs,
        out_shape=route_out_shape,
        scratch_shapes=route_scratch,
        compiler_params=_params(("arbitrary",)),
        name="layer1_out",
    )(x2, attn, mem_o, _bf(b_w_out[0]), row(norm2_g[1]), wr1, br1)
    after = yield slab
    picked, ys = _moe(1, xn, slabt, counts, moe_w_gate, moe_w_up, moe_w_down, after)
    after = yield ys
    yield pl.pallas_call(
        _combine_kernel,
        grid=(n // rt_,),
        in_specs=[_ANY, rtok(d), rtok(LANE), *_picked_specs(n, d // 2, rt_)],
        out_specs=pl.BlockSpec((rt_, d), lambda i: (i + first_rblock, 0)),
        out_shape=jax.ShapeDtypeStruct((n_total, d), f32),
        input_output_aliases={0: 0} if chunk else {},
        compiler_params=_params(("arbitrary",)),
        name="moe_combine",
    )(after, x2, slab, picked, picked)


def kernel(x, mem, positions, mem_norm_g, w_mem_kv, mem_qn_g, mem_kn_g, norm1_g, norm2_g, a_w_in, a_ln_g, a_ln_b, a_w_s, a_b_s, a_w_out, b_w_in, b_q_norm_g, b_kv_norm_g, b_w_q_up, b_w_kv_up, b_qn_g, b_kn_g, b_w_out, moe_w_group, moe_b_group, moe_w_expert, moe_b_expert, moe_w_gate, moe_w_up, moe_w_down):
    params = (mem_norm_g, w_mem_kv, mem_qn_g, mem_kn_g, norm1_g, norm2_g, a_w_in, a_ln_g, a_ln_b, a_w_s,
              a_b_s, a_w_out, b_w_in, b_q_norm_g, b_kv_norm_g, b_w_q_up, b_w_kv_up, b_qn_g, b_kn_g, b_w_out,
              moe_w_group, moe_b_group, moe_w_expert, moe_b_expert, moe_w_gate, moe_w_up, moe_w_down)
    b, s, d = x.shape
    assert BATCH_GROUPS == 2 and b % BATCH_GROUPS == 0
    g = b // BATCH_GROUPS
    x_all = x.reshape(b * s, d)
    groups = [_trunk(c, x_all, mem[c * g:(c + 1) * g], positions[c * g:(c + 1) * g], *params)
              for c in range(BATCH_GROUPS)]
    done = [next(t) for t in groups]
    for _ in range(TRUNK_PHASES - 1):
        prev, done = done, []
        for c, t in enumerate(groups):
            done.append(t.send(prev[c + 1] if c + 1 < BATCH_GROUPS else done[0]))
    return done[-1].reshape(b, s, d)
```

```python
import jax
import jax.numpy as jnp
from jax import lax
from jax.experimental import pallas as pl
from jax.experimental.pallas import tpu as pltpu
from jax.experimental.pallas import tpu_sc as plsc

EPS = 1e-6
LANE = 128
SUBLANE = 8
MEM_HEADS = 4
MEM_HEAD_DIM = 128
MEM_W = MEM_HEADS * MEM_HEAD_DIM
CHUNK = 128
A_GROUPS = 8
MLA_HEADS = 8
Q_LORA = 512
KV_LORA = 256
NOPE_DIM = 128
ROPE_DIM = 64
ROPE_HALF = ROPE_DIM // 2
V_DIM = 128
QK_DIM = NOPE_DIM + ROPE_DIM
QK_PAD = 2 * LANE
ROPE_BASE = 10000.0
N_GROUPS = 4
EXPERTS_PER_GROUP = 8
N_EXPERTS = N_GROUPS * EXPERTS_PER_GROUP
ROUTE_ROWS = 40
EXPERT_FF = 256
LOG2E = 1.4426950408889634

BATCH_GROUPS = 2
TOKEN_TILE = 512
ROUTE_TILE = 1024
SUB_TILE = 256
ROW_TILE = 512
ROW_TILES_PER_STEP = 4
ATTN_TILE = 512
ATTN_HEADS_PER_STEP = 2
ATTN_Q_TILES_PER_STEP = 2
ATTN_KEY_TILES_PER_TRIP = 4
ATTN_QUERY_SPLIT = 2
ATTN_LOOKAHEAD = 3
ATTN_SUM_ROWS = 16
VMEM_LIMIT = 56 * 1024 * 1024
NEG_BIG = -1e30

SC_CORES = 2
SC_SUBCORES = 16
SC_WORKERS = SC_CORES * SC_SUBCORES
SC_INDEX_GROUP = 128
SC_CHUNK = 64

_NT = (((1,), (1,)), ((), ()))
_ANY = pl.BlockSpec(memory_space=pl.ANY)


def _const_spec(shape):
    nd = len(shape)
    return pl.BlockSpec(shape, lambda *_: (0,) * nd, pipeline_mode=pl.Buffered(1))


def _params(sem):
    return pltpu.CompilerParams(dimension_semantics=sem, vmem_limit_bytes=VMEM_LIMIT)


def _run_staggered(gens):
    waiting = list(gens)
    active = []
    while waiting or active:
        if waiting:
            active.append(waiting.pop(0))
        for g in list(active):
            try:
                next(g)
            except StopIteration:
                active.remove(g)


def _sub_rows(t):
    return [pl.ds(k * SUB_TILE, SUB_TILE) for k in range(t // SUB_TILE)]


def _rms(x, g):
    return x * lax.rsqrt(jnp.mean(x * x, axis=-1, keepdims=True) + EPS) * g


def _gelu(x):
    return 0.5 * x * (1.0 + lax.erf(x * (2.0 ** -0.5)))


def _bf(x):
    return x.astype(jnp.bfloat16)


def _dot(a, b):
    return jnp.dot(a, b, preferred_element_type=jnp.float32)


def _memkv_kernel(mem_ref, g_ref, w_ref, kng_ref, kn_ref, v_ref):
    h = _bf(_rms(mem_ref[...], g_ref[...]))
    kv = _dot(h, w_ref[...])
    v_ref[...] = _bf(kv[:, MEM_W:])
    for layer in range(kn_ref.shape[0]):
        g = kng_ref[layer]
        for hh in range(MEM_HEADS):
            k = kv[:, hh * LANE:(hh + 1) * LANE]
            kn_ref[layer, :, hh * LANE:(hh + 1) * LANE] = _bf(_rms(k, g))


def _mem_attention(qm, kn_ref, v_ref, qg):
    outs = []
    for hh in range(MEM_HEADS):
        sl = slice(hh * LANE, (hh + 1) * LANE)
        q = _rms(qm[:, sl], qg) * (MEM_HEAD_DIM ** -0.5)
        s = lax.dot_general(_bf(q), kn_ref[:, sl], _NT, preferred_element_type=jnp.float32)
        p = jnp.exp(s - jnp.max(s, axis=-1, keepdims=True))
        l = jnp.sum(p, axis=-1, keepdims=True)
        outs.append(_dot(_bf(p), v_ref[:, sl]) / l)
    return jnp.concatenate(outs, axis=-1)


def _pack_rows(x):
    w = x.shape[1] // 2
    bits = lambda v: lax.bitcast_convert_type(_bf(v).astype(jnp.float32), jnp.uint32)
    return (bits(x[:, :w]) >> 16) | (bits(x[:, w:]) & jnp.uint32(0xFFFF0000))


def _unpack_rows(p):
    lo = lax.bitcast_convert_type(p << 16, jnp.float32)
    hi = lax.bitcast_convert_type(p & jnp.uint32(0xFFFF0000), jnp.float32)
    return jnp.concatenate([lo, hi], axis=-1)


def _route_stages(x, rows, g2_ref, wr_ref, br_ref, carry_ref, xn_ref, slab_ref, slabt_ref, cnt_ref):
    t = x.shape[0]
    xn = _rms(x, g2_ref[...])
    xn_ref[rows, :] = _pack_rows(xn)
    logits = _dot(_bf(xn), wr_ref[...]) + br_ref[...]
    yield
    lt = logits.T[:ROUTE_ROWS, :]
    row = lax.broadcasted_iota(jnp.int32, lt.shape, 0)

    def first_max(v):
        m = jnp.max(v, axis=0, keepdims=True)
        idx = jnp.min(jnp.where(v == m, row, ROUTE_ROWS), axis=0, keepdims=True)
        return m, idx

    lg = jnp.where(row < N_GROUPS, lt, NEG_BIG)
    gmax, gidx = first_max(lg)
    g_w = 1.0 / jnp.sum(jnp.exp(lg - gmax), axis=0, keepdims=True)

    eid = row - N_GROUPS
    in_grp = (eid >= 0) & (eid < N_EXPERTS) & ((eid >> 3) == gidx)
    le = jnp.where(in_grp, lt, NEG_BIG)
    m1, i1 = first_max(le)
    m2, i2 = first_max(jnp.where(row == i1, NEG_BIG, le))
    r = jnp.exp(m2 - m1)
    w1 = g_w / (1.0 + r)
    w2 = w1 * r
    e1 = i1 - N_GROUPS
    e2 = i2 - N_GROUPS
    yield

    expert = lax.broadcasted_iota(jnp.int32, (N_EXPERTS, t), 0)
    oh1 = expert == e1
    oh2 = expert == e2
    oh = jnp.where(oh1 | oh2, 1.0, 0.0)
    src = lax.broadcasted_iota(jnp.int32, (t, t), 0)
    dst = lax.broadcasted_iota(jnp.int32, (t, t), 1)
    earlier = jnp.where(src < dst, 1.0, 0.0).astype(jnp.bfloat16)
    before = _dot(_bf(oh), earlier) + carry_ref[...]
    r1 = jnp.sum(jnp.where(oh1, before, 0.0), axis=0, keepdims=True)
    r2 = jnp.sum(jnp.where(oh2, before, 0.0), axis=0, keepdims=True)
    carry_ref[...] += jnp.sum(oh, axis=1, keepdims=True)
    cnt_ref[...] = jnp.broadcast_to(carry_ref[...], cnt_ref.shape)

    table = jnp.concatenate([e1.astype(jnp.float32), e2.astype(jnp.float32), w1, w2, r1, r2,
                             jnp.zeros((LANE - 6, t), jnp.float32)], axis=0)
    slabt_ref[:, rows] = table[:SUBLANE, :]
    slab_ref[rows, :] = table.T


def _reset_carry(carry_ref):
    @pl.when(pl.program_id(0) == 0)
    def _():
        carry_ref[...] = jnp.zeros_like(carry_ref)


def _layer0_kernel(x_ref, g1_ref, win_ref, lng_ref, lnb_ref, ws_ref, bs_ref, kn_ref, v_ref, qg_ref,
                   wout_ref, g2_ref, wr_ref, br_ref,
                   xo_ref, xn_ref, slab_ref, slabt_ref, cnt_ref, carry_ref):
    d = x_ref.shape[1]
    _reset_carry(carry_ref)
    row = lax.broadcasted_iota(jnp.int32, (CHUNK, CHUNK), 0)
    col = lax.broadcasted_iota(jnp.int32, (CHUNK, CHUNK), 1)
    causal = row >= col

    def stages(rows):
        x = x_ref[rows, :]
        t = x.shape[0]
        h = _bf(_rms(x, g1_ref[...]))
        z = _dot(h, win_ref[...])
        yield
        v = _gelu(z[:, d:2 * d])
        mu = jnp.mean(v, axis=-1, keepdims=True)
        vc = v - mu
        var = jnp.mean(vc * vc, axis=-1, keepdims=True)
        v = _bf(vc * lax.rsqrt(var + EPS) * lng_ref[...] + lnb_ref[...])
        yield
        nc = t // CHUNK
        per_group = []
        for g in range(A_GROUPS):
            w = jnp.where(causal, ws_ref[g], jnp.zeros((), ws_ref.dtype))
            vg = jnp.concatenate([v[c * CHUNK:(c + 1) * CHUNK, g * LANE:(g + 1) * LANE] for c in range(nc)],
                                 axis=-1)
            per_group.append(_dot(w, vg))
        chunks = [jnp.concatenate([pg[:, c * LANE:(c + 1) * LANE] for pg in per_group], axis=-1) + bs_ref[...]
                  for c in range(nc)]
        mix = _bf(_gelu(z[:, :d]) * jnp.concatenate(chunks, axis=0))
        yield
        mem = _bf(_mem_attention(z[:, 2 * d:], kn_ref, v_ref, qg_ref[...]))
        yield
        xo = x + _dot(mix, wout_ref[:d, :]) + _dot(mem, wout_ref[d:, :])
        xo_ref[rows, :] = xo
        yield
        yield from _route_stages(xo, rows, g2_ref, wr_ref, br_ref, carry_ref,
                                 xn_ref, slab_ref, slabt_ref, cnt_ref)

    _run_staggered([stages(rows) for rows in _sub_rows(x_ref.shape[0])])


def _layer1_proj_kernel(after_ref, x_ref, slab_ref, y0_ref, y1_ref, pos_ref, g1_ref, win_ref, qng_ref, kvng_ref,
                        wq_ref, wkv_ref, qg_ref, kg_ref, inv_ref, sgn_ref, kn_ref, mv_ref, mqg_ref,
                        xo_ref, q_ref, k_ref, v_ref, mem_ref):
    o1 = Q_LORA
    o2 = o1 + KV_LORA
    o3 = o2 + MEM_W
    q_scale = (QK_DIM ** -0.5) * LOG2E

    def stages(rows):
        x = _combined(x_ref, slab_ref, y0_ref, y1_ref, rows)
        xo_ref[rows, :] = x
        h = _bf(_rms(x, g1_ref[...]))
        z = _dot(h, win_ref[...])
        yield
        cq = _bf(_rms(z[:, :o1], qng_ref[...]))
        ckv = _bf(_rms(z[:, o1:o2], kvng_ref[...]))
        k_rope = z[:, o3:o3 + LANE]
        ang = pos_ref[rows, :].astype(jnp.float32) * inv_ref[...]
        cos = jnp.cos(ang)
        sin = jnp.sin(ang) * sgn_ref[...]

        def rope(r):
            return r * cos + pltpu.roll(r, LANE // 2, 1) * sin

        q = _dot(cq, wq_ref[...])
        kv = _dot(ckv, wkv_ref[...])
        yield
        qg = qg_ref[...]
        kg = kg_ref[...]
        kr_ss = jnp.sum(k_rope * k_rope, axis=-1, keepdims=True)
        kr = rope(k_rope * kg[:, LANE:])
        for hh in range(MLA_HEADS):
            qh = q[:, hh * QK_PAD:(hh + 1) * QK_PAD]
            rq = lax.rsqrt(jnp.sum(qh * qh, axis=-1, keepdims=True) * (1.0 / QK_DIM) + EPS) * q_scale
            qh = qh * rq * qg
            q_ref[hh, rows, :LANE] = _bf(qh[:, :LANE])
            q_ref[hh, rows, LANE:] = _bf(rope(qh[:, LANE:]))
            kn = kv[:, hh * LANE:(hh + 1) * LANE]
            rk = lax.rsqrt((jnp.sum(kn * kn, axis=-1, keepdims=True) + kr_ss) * (1.0 / QK_DIM) + EPS)
            k_ref[hh, rows, :LANE] = _bf(kn * rk * kg[:, :LANE])
            k_ref[hh, rows, LANE:] = _bf(kr * rk)
            v_ref[hh, :, rows] = _bf(kv[:, (MLA_HEADS + hh) * LANE:(MLA_HEADS + hh + 1) * LANE].T)
            if hh % 2 == 1:
                yield
        mem_ref[rows, :] = _bf(_mem_attention(z[:, o2:o3], kn_ref, mv_ref, mqg_ref[...]))

    _run_staggered([stages(rows) for rows in _sub_rows(x_ref.shape[0])])


def _attn_kernel(q_ref, k_ref, vt_ref, o_ref, m_ref, acc_ref):
    per_step = q_ref.shape[1] // ATTN_TILE
    for u in range(per_step):
        _attn_query_tile(pl.program_id(2) * per_step + u, u * ATTN_TILE,
                         q_ref, k_ref, vt_ref, o_ref, m_ref, acc_ref)


def _attn_query_tile(i, q0, q_ref, k_ref, vt_ref, o_ref, m_ref, acc_ref):
    heads, tq = q_ref.shape[0], ATTN_TILE
    m_ref[...] = jnp.full_like(m_ref, NEG_BIG)
    acc_ref[...] = jnp.zeros_like(acc_ref)

    qw = tq // ATTN_QUERY_SPLIT

    def scores(hh, j, part, masked):
        keys = (part + 1) * qw if masked else tq
        start = pl.multiple_of(j * tq, tq)
        q = q_ref[hh, q0 + part * qw:q0 + (part + 1) * qw, :]
        s = lax.dot_general(k_ref[hh, pl.ds(start, keys), :], q, _NT, preferred_element_type=jnp.float32)
        if masked:
            key = lax.broadcasted_iota(jnp.int32, (keys, qw), 0)
            qry = lax.broadcasted_iota(jnp.int32, (keys, qw), 1) + part * qw
            s = jnp.where(key <= qry, s, NEG_BIG)
        return s

    def update(hh, j, part, s):
        keys = s.shape[0]
        lanes = slice(part * qw, (part + 1) * qw)
        m = m_ref[hh, :, lanes]
        m_new = jnp.maximum(m, jnp.max(s, axis=0, keepdims=True))
        alpha = jnp.exp2(m - m_new)
        p = _bf(jnp.exp2(s - m_new))
        m_ref[hh, :, lanes] = m_new
        vt1 = jnp.concatenate([vt_ref[hh, j, :, :keys], jnp.ones((ATTN_SUM_ROWS, keys), jnp.bfloat16)], axis=0)
        acc_ref[hh, :, lanes] = alpha * acc_ref[hh, :, lanes] + _dot(vt1, p)

    def run(items):
        ss = {}
        ahead = ATTN_LOOKAHEAD
        for t in range(len(items) + ahead):
            if t < len(items):
                ss[t] = scores(*items[t])
            if t >= ahead:
                update(*items[t - ahead][:3], ss.pop(t - ahead))

    def tile_items(first, count, masked):
        return [(hh, first + u, part, masked) for u in range(count) for hh in range(heads)
                for part in range(ATTN_QUERY_SPLIT)]

    unroll = ATTN_KEY_TILES_PER_TRIP

    def body(jj, c):
        run(tile_items(unroll * jj, unroll, False))
        return c

    lax.fori_loop(0, i // unroll, body, 0)
    for left in range(unroll):
        @pl.when(i % unroll == left)
        def _():
            run(tile_items(i - left, left, False) + tile_items(i, 1, True))

    for hh in range(heads):
        acc = acc_ref[hh]
        o_ref[q0:q0 + tq, hh * V_DIM:(hh + 1) * V_DIM] = _bf((acc[:V_DIM] / acc[V_DIM:V_DIM + 1]).T)


def _layer1_out_kernel(x_ref, o_ref, mem_ref, wout_ref, g2_ref, wr_ref, br_ref,
                       xo_ref, xn_ref, slab_ref, slabt_ref, cnt_ref, carry_ref):
    d = o_ref.shape[1]
    _reset_carry(carry_ref)

    def stages(rows):
        xo = (x_ref[rows, :] + _dot(o_ref[rows, :], wout_ref[:d, :])
              + _dot(mem_ref[rows, :], wout_ref[d:, :]))
        xo_ref[rows, :] = xo
        yield
        yield from _route_stages(xo, rows, g2_ref, wr_ref, br_ref, carry_ref,
                                 xn_ref, slab_ref, slabt_ref, cnt_ref)

    _run_staggered([stages(rows) for rows in _sub_rows(x_ref.shape[0])])


def _sc_mesh():
    return plsc.VectorSubcoreMesh(core_axis_name="c", subcore_axis_name="s")


def _sc_worker_base(rows_per_worker):
    return (lax.axis_index("c") * SC_SUBCORES + lax.axis_index("s")) * rows_per_worker


def _sc_scatter_rows(x, idx0, idx1, p_rows):
    n, w = x.shape
    per = n // SC_WORKERS
    chunks = SC_INDEX_GROUP // SC_CHUNK
    assert n % SC_WORKERS == 0 and per % SC_INDEX_GROUP == 0

    @pl.kernel(out_type=jax.ShapeDtypeStruct((p_rows, w), x.dtype), mesh=_sc_mesh(),
               scratch_types=[pltpu.VMEM((1, SC_INDEX_GROUP), jnp.int32),
                              pltpu.VMEM((1, SC_INDEX_GROUP), jnp.int32),
                              pltpu.VMEM((SC_CHUNK, w), x.dtype), pltpu.VMEM((SC_CHUNK, w), x.dtype),
                              pltpu.SemaphoreType.DMA, pltpu.SemaphoreType.DMA],
               name="moe_dispatch_sc")
    def scatter(x_hbm, i0_hbm, i1_hbm, o_hbm, i0_v, i1_v, buf_a, buf_b, sem_a, sem_b):
        base = _sc_worker_base(per)

        @pl.loop(0, per // SC_INDEX_GROUP)
        def _(g):
            off = pl.multiple_of(base + g * SC_INDEX_GROUP, SC_INDEX_GROUP)
            pltpu.sync_copy(i0_hbm.at[:, pl.ds(off, SC_INDEX_GROUP)], i0_v)
            pltpu.sync_copy(i1_hbm.at[:, pl.ds(off, SC_INDEX_GROUP)], i1_v)
            pending = []
            for c in range(chunks):
                buf, sem = ((buf_a, sem_a), (buf_b, sem_b))[c % 2]
                if c >= 2:
                    for cp in pending[c - 2]:
                        cp.wait()
                pltpu.sync_copy(x_hbm.at[pl.ds(off + c * SC_CHUNK, SC_CHUNK)], buf)
                sl = pl.ds(c * SC_CHUNK, SC_CHUNK)
                pending.append((pltpu.async_copy(buf, o_hbm.at[i0_v.at[0, sl]], sem),
                                pltpu.async_copy(buf, o_hbm.at[i1_v.at[0, sl]], sem)))
            for cps in pending[max(chunks - 2, 0):]:
                for cp in cps:
                    cp.wait()

    return scatter(x, idx0.reshape(1, n), idx1.reshape(1, n))


def _sc_gather_rows(table, idx):
    m = idx.shape[0]
    w = table.shape[1]
    per = m // SC_WORKERS
    chunks = SC_INDEX_GROUP // SC_CHUNK
    assert m % SC_WORKERS == 0 and per % SC_INDEX_GROUP == 0

    @pl.kernel(out_type=jax.ShapeDtypeStruct((m, w), table.dtype), mesh=_sc_mesh(),
               scratch_types=[pltpu.VMEM((1, SC_INDEX_GROUP), jnp.int32),
                              pltpu.VMEM((SC_CHUNK, w), table.dtype), pltpu.VMEM((SC_CHUNK, w), table.dtype),
                              pltpu.SemaphoreType.DMA, pltpu.SemaphoreType.DMA],
               name="moe_combine_sc")
    def gather(t_hbm, i_hbm, o_hbm, i_v, buf_a, buf_b, sem_a, sem_b):
        base = _sc_worker_base(per)

        @pl.loop(0, per // SC_INDEX_GROUP)
        def _(g):
            off = pl.multiple_of(base + g * SC_INDEX_GROUP, SC_INDEX_GROUP)
            pltpu.sync_copy(i_hbm.at[:, pl.ds(off, SC_INDEX_GROUP)], i_v)
            pending = []
            for c in range(chunks):
                buf, sem = ((buf_a, sem_a), (buf_b, sem_b))[c % 2]
                if c >= 2:
                    pending[c - 2].wait()
                pltpu.sync_copy(t_hbm.at[i_v.at[0, pl.ds(c * SC_CHUNK, SC_CHUNK)]], buf)
                pending.append(pltpu.async_copy(buf, o_hbm.at[pl.ds(off + c * SC_CHUNK, SC_CHUNK)], sem))
            for cp in pending[max(chunks - 2, 0):]:
                cp.wait()

    return gather(table, idx.reshape(1, m))


def _ffn_kernel(exp_ref, rows_ref, after_ref, xs_ref, *refs):
    slots = ROW_TILES_PER_STEP
    w_refs = [refs[3 * s:3 * s + 3] for s in range(slots)]
    ys_ref = refs[3 * slots]
    w_bf = [refs[3 * slots + 1 + 3 * s:3 * slots + 4 + 3 * s] for s in range(slots)]
    j = pl.program_id(0)
    steps = pl.num_programs(0)

    for s in range(slots):
        tile = s * steps + j

        @pl.when((j == 0) | (exp_ref[tile] != exp_ref[jnp.maximum(tile - 1, 0)]))
        def _():
            for src, dst in zip(w_refs[s], w_bf[s]):
                dst[...] = _bf(src[...])

    def stages(s, k, rows):
        packed = xs_ref[s, rows, :]
        row_id = lax.broadcasted_iota(jnp.int32, packed.shape, 0) + k * SUB_TILE
        x = _bf(_unpack_rows(jnp.where(row_id < rows_ref[s * steps + j], packed, jnp.uint32(0))))
        wg, wu, wd = w_bf[s]
        g = _dot(x, wg[...])
        u = _dot(x, wu[...])
        yield
        act = _bf(g * jax.nn.sigmoid(g) * u)
        yield
        ys_ref[s, rows, :] = _pack_rows(_dot(act, wd[...]))

    def run_slots(which):
        _run_staggered([stages(s, k, rows) for k, rows in enumerate(_sub_rows(xs_ref.shape[1]))
                        for s in which])

    last = slots - 1
    run_slots(range(last))

    @pl.when(rows_ref[last * steps + j] > 0)
    def _():
        run_slots([last])

    @pl.when(rows_ref[last * steps + j] == 0)
    def _():
        ys_ref[last] = jnp.zeros_like(ys_ref[last])


def _combined(x_ref, slab_ref, y0_ref, y1_ref, rows):
    slab = slab_ref[rows, :]
    return (x_ref[rows, :] + slab[:, 2:3] * _unpack_rows(y0_ref[rows, :])
            + slab[:, 3:4] * _unpack_rows(y1_ref[rows, :]))


def _combine_kernel(after_ref, x_ref, slab_ref, y0_ref, y1_ref, out_ref):
    out_ref[...] = _combined(x_ref, slab_ref, y0_ref, y1_ref, slice(None))


def _moe(layer, xn, slabt, counts, w_gate, w_up, w_down, after):
    n, wp = xn.shape
    d = w_gate.shape[-2]
    rt = ROW_TILE
    p_max = 2 * n + N_EXPERTS * rt
    n_tiles = p_max // rt

    cnt = counts[:, 0].astype(jnp.int32)
    padded = ((cnt + rt - 1) // rt) * rt
    experts = jnp.arange(N_EXPERTS, dtype=jnp.int32)
    start = jnp.sum(jnp.where(experts[None, :] < experts[:, None], padded[None, :], 0), axis=1)
    end = start + padded

    def position(e_row, r_row):
        e = e_row.astype(jnp.int32)[None, :]
        return jnp.sum(jnp.where(e == experts[:, None], start[:, None], 0), axis=0) + r_row.astype(jnp.int32)

    pos0 = position(slabt[0], slabt[4])
    pos1 = position(slabt[1], slabt[5])
    tile_start = jnp.arange(n_tiles, dtype=jnp.int32) * rt
    owns = (start[None, :] <= tile_start[:, None]) & (tile_start[:, None] < end[None, :])
    past = tile_start >= end[N_EXPERTS - 1]
    tile_exp = jnp.where(past, N_EXPERTS - 1, jnp.sum(jnp.where(owns, experts[None, :], 0), axis=1))
    used = jnp.sum(jnp.where(owns, (start + cnt)[None, :], 0), axis=1)
    tile_rows = jnp.clip(used - tile_start, 0, rt).astype(jnp.int32)

    xs = _sc_scatter_rows(xn, pos0, pos1, p_max)

    f = w_gate.shape[-1]
    slots = ROW_TILES_PER_STEP
    assert n_tiles % slots == 0
    steps = n_tiles // slots
    w_specs = []
    for s in range(slots):
        pick = lambda j, ex, rw, s=s: (layer, ex[s * steps + j], 0, 0)
        w_specs += [pl.BlockSpec((None, None, d, f), pick), pl.BlockSpec((None, None, d, f), pick),
                    pl.BlockSpec((None, None, f, d), pick)]
    slot_rows = pl.BlockSpec((slots, rt, wp), lambda j, ex, rw: (0, j, 0))
    ys = pl.pallas_call(
        _ffn_kernel,
        grid_spec=pltpu.PrefetchScalarGridSpec(
            num_scalar_prefetch=2, grid=(steps,),
            in_specs=[_ANY, slot_rows] + w_specs,
            out_specs=slot_rows,
            scratch_shapes=[pltpu.VMEM((d, f), jnp.bfloat16), pltpu.VMEM((d, f), jnp.bfloat16),
                            pltpu.VMEM((f, d), jnp.bfloat16)] * slots),
        out_shape=jax.ShapeDtypeStruct((slots, steps * rt, wp), jnp.uint32),
        compiler_params=_params(("arbitrary",)),
        name="moe_ffn",
    )(tile_exp.astype(jnp.int32), tile_rows, after, xs.reshape(slots, steps * rt, wp),
      *([w_gate, w_up, w_down] * slots)).reshape(p_max, wp)

    return _sc_gather_rows(ys, jnp.concatenate([pos0, pos1])), ys


def _picked_specs(n, wp, tile):
    nb = n // tile
    return [pl.BlockSpec((tile, wp), lambda i: (i, 0)),
            pl.BlockSpec((tile, wp), lambda i: (i + nb, 0))]


def _router_weights(w_group, b_group, w_expert, b_expert):
    d = w_group.shape[0]
    pad = LANE - N_GROUPS - N_EXPERTS
    wr = jnp.concatenate([w_group, w_expert, jnp.zeros((d, pad), w_group.dtype)], axis=1)
    br = jnp.concatenate([b_group, b_expert, jnp.zeros((pad,), b_group.dtype)])
    return _bf(wr), br.reshape(1, LANE).astype(jnp.float32)


def _rope_lanes(vec_half):
    z = jnp.zeros_like(vec_half)
    return jnp.concatenate([vec_half, z, vec_half, z], axis=-1)


def _pad_rope_cols(w):
    z = jnp.zeros(w.shape[:-1] + (ROPE_HALF,), w.dtype)
    return jnp.concatenate([w[..., :ROPE_HALF], z, w[..., ROPE_HALF:], z], axis=-1)


TRUNK_PHASES = 5


def _trunk(chunk, x_all, mem, positions, mem_norm_g, w_mem_kv, mem_qn_g, mem_kn_g, norm1_g,
           norm2_g, a_w_in, a_ln_g, a_ln_b, a_w_s, a_b_s, a_w_out, b_w_in, b_q_norm_g, b_kv_norm_g, b_w_q_up,
           b_w_kv_up, b_qn_g, b_kn_g, b_w_out, moe_w_group, moe_b_group, moe_w_expert, moe_b_expert,
           moe_w_gate, moe_w_up, moe_w_down):
    n_total, d = x_all.shape
    b, s = positions.shape
    m = mem.shape[1]
    n = b * s
    depth = norm1_g.shape[0]
    tt = TOKEN_TILE
    ta = ATTN_TILE
    tiles_per_batch = s // tt
    assert depth == 2 and s % tt == 0 and tt % SUB_TILE == 0 and d == A_GROUPS * LANE
    assert s % ta == 0 and ta % tt == 0 and s % ROUTE_TILE == 0 and ROUTE_TILE % SUB_TILE == 0
    f32 = jnp.float32
    row = lambda v: v.reshape(1, -1).astype(f32)

    kn_all, mem_v = pl.pallas_call(
        _memkv_kernel,
        grid=(b,),
        in_specs=[pl.BlockSpec((m, d), lambda i: (i, 0)), _const_spec((1, d)),
                  _const_spec((d, 2 * MEM_W)), _const_spec((depth, 1, MEM_HEAD_DIM))],
        out_specs=[pl.BlockSpec((depth, m, MEM_W), lambda i: (0, i, 0)),
                   pl.BlockSpec((m, MEM_W), lambda i: (i, 0))],
        out_shape=[jax.ShapeDtypeStruct((depth, b * m, MEM_W), jnp.bfloat16),
                   jax.ShapeDtypeStruct((b * m, MEM_W), jnp.bfloat16)],
        compiler_params=_params(("arbitrary",)),
        name="mem_kv",
    )(mem.reshape(b * m, d), row(mem_norm_g), _bf(w_mem_kv), mem_kn_g.reshape(depth, 1, MEM_HEAD_DIM))

    tok = lambda width: pl.BlockSpec((tt, width), lambda i: (i, 0))
    kn_spec = lambda layer: pl.BlockSpec((None, m, MEM_W), lambda i: (layer, i // tiles_per_batch, 0))
    mv_spec = pl.BlockSpec((m, MEM_W), lambda i: (i // tiles_per_batch, 0))
    rt_ = ROUTE_TILE
    rtok = lambda width: pl.BlockSpec((rt_, width), lambda i: (i, 0))
    rkn_spec = lambda layer: pl.BlockSpec((None, m, MEM_W), lambda i: (layer, i // (s // rt_), 0))
    rmv_spec = pl.BlockSpec((m, MEM_W), lambda i: (i // (s // rt_), 0))
    route_out_specs = [rtok(d), rtok(d // 2), rtok(LANE), pl.BlockSpec((SUBLANE, rt_), lambda i: (0, i)),
                       pl.BlockSpec((N_EXPERTS, LANE), lambda i: (0, 0))]
    route_out_shape = [jax.ShapeDtypeStruct((n, d), f32), jax.ShapeDtypeStruct((n, d // 2), jnp.uint32),
                       jax.ShapeDtypeStruct((n, LANE), f32), jax.ShapeDtypeStruct((SUBLANE, n), f32),
                       jax.ShapeDtypeStruct((N_EXPERTS, LANE), f32)]
    route_scratch = [pltpu.VMEM((N_EXPERTS, 1), f32)]

    first_block = chunk * (n // tt)
    group_tok = pl.BlockSpec((tt, d), lambda i: (i + first_block, 0))
    first_rblock = chunk * (n // rt_)

    wr0, br0 = _router_weights(moe_w_group[0], moe_b_group[0], moe_w_expert[0], moe_b_expert[0])
    a_in = a_w_in.shape[-1]
    bias_s = jnp.repeat(a_b_s[0].T, LANE, axis=1).astype(f32)
    x2, xn, slab, slabt, counts = pl.pallas_call(
        _layer0_kernel,
        grid=(n // rt_,),
        in_specs=[pl.BlockSpec((rt_, d), lambda i: (i + first_rblock, 0)),
                  _const_spec((1, d)), _const_spec((d, a_in)), _const_spec((1, d)),
                  _const_spec((1, d)), _const_spec((A_GROUPS, CHUNK, CHUNK)), _const_spec((CHUNK, d)),
                  rkn_spec(0), rmv_spec, _const_spec((1, MEM_HEAD_DIM)),
                  _const_spec((d + MEM_W, d)), _const_spec((1, d)), _const_spec((d, LANE)),
                  _const_spec((1, LANE))],
        out_specs=route_out_specs,
        out_shape=route_out_shape,
        scratch_shapes=route_scratch,
        compiler_params=_params(("arbitrary",)),
        name="layer0_mixer",
    )(x_all, row(norm1_g[0]), _bf(a_w_in[0]), row(a_ln_g[0]), row(a_ln_b[0]), _bf(a_w_s[0]), bias_s,
      kn_all, mem_v, row(mem_qn_g[0]), _bf(a_w_out[0]), row(norm2_g[0]), wr0, br0)
    after = yield slab
    picked, ys = _moe(0, xn, slabt, counts, moe_w_gate, moe_w_up, moe_w_down, after)
    after = yield ys

    hq = MLA_HEADS
    o1, o2, o3 = Q_LORA, Q_LORA + KV_LORA, Q_LORA + KV_LORA + ROPE_DIM
    w_in = b_w_in[0]
    w_in_p = jnp.concatenate([w_in[:, :o2], w_in[:, o3:], _pad_rope_cols(w_in[:, o2:o3])], axis=1)
    wq = b_w_q_up[0].reshape(Q_LORA, hq, QK_DIM)
    wq_p = jnp.concatenate([wq[..., :NOPE_DIM], _pad_rope_cols(wq[..., NOPE_DIM:])], axis=-1)
    wq_p = wq_p.reshape(Q_LORA, hq * QK_PAD)
    wkv = b_w_kv_up[0].reshape(KV_LORA, hq, NOPE_DIM + V_DIM)
    wkv_p = jnp.concatenate([wkv[..., :NOPE_DIM].reshape(KV_LORA, hq * NOPE_DIM),
                             wkv[..., NOPE_DIM:].reshape(KV_LORA, hq * V_DIM)], axis=1)
    pad_gain = lambda g: jnp.concatenate([g[:NOPE_DIM], _pad_rope_cols(g[NOPE_DIM:])]).reshape(1, QK_PAD)
    half = jnp.arange(ROPE_HALF, dtype=f32)
    inv = ROPE_BASE ** (-(half * 2.0 / ROPE_DIM))
    inv_l = _rope_lanes(inv).reshape(1, LANE)
    sgn_l = jnp.concatenate([-jnp.ones((2 * ROPE_HALF,), f32), jnp.ones((2 * ROPE_HALF,), f32)]).reshape(1, LANE)

    in_w = w_in_p.shape[1]
    head_spec = lambda width: pl.BlockSpec((None, hq, tt, width),
                                           lambda i: (i // tiles_per_batch, 0, i % tiles_per_batch, 0))
    per_ta = ta // tt

    def vt_index(i):
        t = i % tiles_per_batch
        return (i // tiles_per_batch, 0, t // per_ta, 0, t % per_ta)

    vt_spec = pl.BlockSpec((None, hq, None, V_DIM, tt), vt_index)
    x2, q, k, vt, mem_o = pl.pallas_call(
        _layer1_proj_kernel,
        grid=(n // tt,),
        in_specs=[_ANY, tok(d), tok(LANE), *_picked_specs(n, d // 2, tt),
                  tok(1), _const_spec((1, d)), _const_spec((d, in_w)), _const_spec((1, Q_LORA)),
                  _const_spec((1, KV_LORA)), _const_spec((Q_LORA, hq * QK_PAD)),
                  _const_spec((KV_LORA, hq * (NOPE_DIM + V_DIM))), _const_spec((1, QK_PAD)),
                  _const_spec((1, QK_PAD)), _const_spec((1, LANE)), _const_spec((1, LANE)),
                  kn_spec(1), mv_spec, _const_spec((1, MEM_HEAD_DIM))],
        out_specs=[tok(d), head_spec(QK_PAD), head_spec(QK_PAD), vt_spec, tok(MEM_W)],
        out_shape=[jax.ShapeDtypeStruct((n, d), f32),
                   jax.ShapeDtypeStruct((b, hq, s, QK_PAD), jnp.bfloat16),
                   jax.ShapeDtypeStruct((b, hq, s, QK_PAD), jnp.bfloat16),
                   jax.ShapeDtypeStruct((b, hq, s // ta, V_DIM, ta), jnp.bfloat16),
                   jax.ShapeDtypeStruct((n, MEM_W), jnp.bfloat16)],
        compiler_params=_params(("arbitrary",)),
        name="layer1_proj",
    )(after, x2, slab, picked, picked,
      positions.reshape(n, 1), row(norm1_g[1]), _bf(w_in_p), row(b_q_norm_g[0]), row(b_kv_norm_g[0]),
      _bf(wq_p), _bf(wkv_p), pad_gain(b_qn_g[0]).astype(f32), pad_gain(b_kn_g[0]).astype(f32),
      inv_l, sgn_l, kn_all, mem_v, row(mem_qn_g[1]))

    qb = s // ta
    hp = ATTN_HEADS_PER_STEP
    qt = ATTN_Q_TILES_PER_STEP
    assert qb % qt == 0
    attn = pl.pallas_call(
        _attn_kernel,
        grid=(b, hq // hp, qb // qt),
        in_specs=[pl.BlockSpec((None, hp, qt * ta, QK_PAD), lambda bi, hi, i: (bi, hi, i, 0)),
                  pl.BlockSpec((None, hp, s, QK_PAD), lambda bi, hi, i: (bi, hi, 0, 0)),
                  pl.BlockSpec((None, hp, qb, V_DIM, ta), lambda bi, hi, i: (bi, hi, 0, 0, 0))],
        out_specs=pl.BlockSpec((qt * ta, hp * V_DIM), lambda bi, hi, i: (bi * (qb // qt) + i, hi)),
        out_shape=jax.ShapeDtypeStruct((n, hq * V_DIM), jnp.bfloat16),
        scratch_shapes=[pltpu.VMEM((hp, 1, ta), f32), pltpu.VMEM((hp, V_DIM + ATTN_SUM_ROWS, ta), f32)],
        compiler_params=_params(("arbitrary", "arbitrary", "arbitrary")),
        name="causal_attention",
    )(q, k, vt)

    wr1, br1 = _router_weights(moe_w_group[1], moe_b_group[1], moe_w_expert[1], moe_b_expert[1])
    x2, xn, slab, slabt, counts = pl.pallas_call(
        _layer1_out_kernel,
        grid=(n // rt_,),
        in_specs=[rtok(d), rtok(hq * V_DIM), rtok(MEM_W), _const_spec((hq * V_DIM + MEM_W, d)),
                  _const_spec((1, d)), _const_spec((d, LANE)), _const_spec((1, LANE))],
        out_specs=route_out_specs,
        out_shape=route_out_shape,
        scratch_shapes=route_scratch,
        compiler_params=_params(("arbitrary",)),
        name="layer1_out",
    )(x2, attn, mem_o, _bf(b_w_out[0]), row(norm2_g[1]), wr1, br1)
    after = yield slab
    picked, ys = _moe(1, xn, slabt, counts, moe_w_gate, moe_w_up, moe_w_down, after)
    after = yield ys
    yield pl.pallas_call(
        _combine_kernel,
        grid=(n // rt_,),
        in_specs=[_ANY, rtok(d), rtok(LANE), *_picked_specs(n, d // 2, rt_)],
        out_specs=pl.BlockSpec((rt_, d), lambda i: (i + first_rblock, 0)),
        out_shape=jax.ShapeDtypeStruct((n_total, d), f32),
        input_output_aliases={0: 0} if chunk else {},
        compiler_params=_params(("arbitrary",)),
        name="moe_combine",
    )(after, x2, slab, picked, picked)


def kernel(x, mem, positions, mem_norm_g, w_mem_kv, mem_qn_g, mem_kn_g, norm1_g, norm2_g, a_w_in, a_ln_g, a_ln_b, a_w_s, a_b_s, a_w_out, b_w_in, b_q_norm_g, b_kv_norm_g, b_w_q_up, b_w_kv_up, b_qn_g, b_kn_g, b_w_out, moe_w_group, moe_b_group, moe_w_expert, moe_b_expert, moe_w_gate, moe_w_up, moe_w_down):
    params = (mem_norm_g, w_mem_kv, mem_qn_g, mem_kn_g, norm1_g, norm2_g, a_w_in, a_ln_g, a_ln_b, a_w_s,
              a_b_s, a_w_out, b_w_in, b_q_norm_g, b_kv_norm_g, b_w_q_up, b_w_kv_up, b_qn_g, b_kn_g, b_w_out,
              moe_w_group, moe_b_group, moe_w_expert, moe_b_expert, moe_w_gate, moe_w_up, moe_w_down)
    b, s, d = x.shape
    assert BATCH_GROUPS == 2 and b % BATCH_GROUPS == 0
    g = b // BATCH_GROUPS
    x_all = x.reshape(b * s, d)
    groups = [_trunk(c, x_all, mem[c * g:(c + 1) * g], positions[c * g:(c + 1) * g], *params)
              for c in range(BATCH_GROUPS)]
    done = [next(t) for t in groups]
    for _ in range(TRUNK_PHASES - 1):
        prev, done = done, []
        for c, t in enumerate(groups):
            done.append(t.send(prev[c + 1] if c + 1 < BATCH_GROUPS else done[0]))
    return done[-1].reshape(b, s, d)
```

```python
import jax
import jax.numpy as jnp
from jax import lax
from jax.experimental import pallas as pl
from jax.experimental.pallas import tpu as pltpu
from jax.experimental.pallas import tpu_sc as plsc

EPS = 1e-6
LANE = 128
SUBLANE = 8
MEM_HEADS = 4
MEM_HEAD_DIM = 128
MEM_W = MEM_HEADS * MEM_HEAD_DIM
CHUNK = 128
A_GROUPS = 8
MLA_HEADS = 8
Q_LORA = 512
KV_LORA = 256
NOPE_DIM = 128
ROPE_DIM = 64
ROPE_HALF = ROPE_DIM // 2
V_DIM = 128
QK_DIM = NOPE_DIM + ROPE_DIM
QK_PAD = 2 * LANE
ROPE_BASE = 10000.0
N_GROUPS = 4
EXPERTS_PER_GROUP = 8
N_EXPERTS = N_GROUPS * EXPERTS_PER_GROUP
ROUTE_ROWS = 40
EXPERT_FF = 256
LOG2E = 1.4426950408889634

BATCH_GROUPS = 2
TOKEN_TILE = 512
ROUTE_TILE = 1024
SUB_TILE = 256
ROW_TILE = 512
ROW_TILES_PER_STEP = 4
ATTN_TILE = 1024
ATTN_HEADS_PER_STEP = 2
ATTN_Q_TILES_PER_STEP = 1
ATTN_KEY_TILES_PER_TRIP = 2
ATTN_QUERY_SPLIT = 4
ATTN_LOOKAHEAD = 3
ATTN_SUM_ROWS = 16
VMEM_LIMIT = 56 * 1024 * 1024
NEG_BIG = -1e30

SC_CORES = 2
SC_SUBCORES = 16
SC_WORKERS = SC_CORES * SC_SUBCORES
SC_INDEX_GROUP = 128
SC_CHUNK = 64

_NT = (((1,), (1,)), ((), ()))
_ANY = pl.BlockSpec(memory_space=pl.ANY)


def _const_spec(shape):
    nd = len(shape)
    return pl.BlockSpec(shape, lambda *_: (0,) * nd, pipeline_mode=pl.Buffered(1))


def _params(sem):
    return pltpu.CompilerParams(dimension_semantics=sem, vmem_limit_bytes=VMEM_LIMIT)


def _run_staggered(gens):
    waiting = list(gens)
    active = []
    while waiting or active:
        if waiting:
            active.append(waiting.pop(0))
        for g in list(active):
            try:
                next(g)
            except StopIteration:
                active.remove(g)


def _sub_rows(t):
    return [pl.ds(k * SUB_TILE, SUB_TILE) for k in range(t // SUB_TILE)]


def _rms(x, g):
    return x * lax.rsqrt(jnp.mean(x * x, axis=-1, keepdims=True) + EPS) * g


def _gelu(x):
    return 0.5 * x * (1.0 + lax.erf(x * (2.0 ** -0.5)))


def _bf(x):
    return x.astype(jnp.bfloat16)


def _dot(a, b):
    return jnp.dot(a, b, preferred_element_type=jnp.float32)


def _memkv_kernel(mem_ref, g_ref, w_ref, kng_ref, kn_ref, v_ref):
    h = _bf(_rms(mem_ref[...], g_ref[...]))
    kv = _dot(h, w_ref[...])
    v_ref[...] = _bf(kv[:, MEM_W:])
    for layer in range(kn_ref.shape[0]):
        g = kng_ref[layer]
        for hh in range(MEM_HEADS):
            k = kv[:, hh * LANE:(hh + 1) * LANE]
            kn_ref[layer, :, hh * LANE:(hh + 1) * LANE] = _bf(_rms(k, g))


def _mem_attention(qm, kn_ref, v_ref, qg):
    outs = []
    for hh in range(MEM_HEADS):
        sl = slice(hh * LANE, (hh + 1) * LANE)
        q = _rms(qm[:, sl], qg) * (MEM_HEAD_DIM ** -0.5)
        s = lax.dot_general(_bf(q), kn_ref[:, sl], _NT, preferred_element_type=jnp.float32)
        p = jnp.exp(s - jnp.max(s, axis=-1, keepdims=True))
        l = jnp.sum(p, axis=-1, keepdims=True)
        outs.append(_dot(_bf(p), v_ref[:, sl]) / l)
    return jnp.concatenate(outs, axis=-1)


def _pack_rows(x):
    w = x.shape[1] // 2
    bits = lambda v: lax.bitcast_convert_type(_bf(v).astype(jnp.float32), jnp.uint32)
    return (bits(x[:, :w]) >> 16) | (bits(x[:, w:]) & jnp.uint32(0xFFFF0000))


def _unpack_rows(p):
    lo = lax.bitcast_convert_type(p << 16, jnp.float32)
    hi = lax.bitcast_convert_type(p & jnp.uint32(0xFFFF0000), jnp.float32)
    return jnp.concatenate([lo, hi], axis=-1)


def _route_stages(x, rows, g2_ref, wr_ref, br_ref, carry_ref, xn_ref, slab_ref, slabt_ref, cnt_ref):
    t = x.shape[0]
    xn = _rms(x, g2_ref[...])
    xn_ref[rows, :] = _pack_rows(xn)
    logits = _dot(_bf(xn), wr_ref[...]) + br_ref[...]
    yield
    lt = logits.T[:ROUTE_ROWS, :]
    row = lax.broadcasted_iota(jnp.int32, lt.shape, 0)

    def first_max(v):
        m = jnp.max(v, axis=0, keepdims=True)
        idx = jnp.min(jnp.where(v == m, row, ROUTE_ROWS), axis=0, keepdims=True)
        return m, idx

    lg = jnp.where(row < N_GROUPS, lt, NEG_BIG)
    gmax, gidx = first_max(lg)
    g_w = 1.0 / jnp.sum(jnp.exp(lg - gmax), axis=0, keepdims=True)

    eid = row - N_GROUPS
    in_grp = (eid >= 0) & (eid < N_EXPERTS) & ((eid >> 3) == gidx)
    le = jnp.where(in_grp, lt, NEG_BIG)
    m1, i1 = first_max(le)
    m2, i2 = first_max(jnp.where(row == i1, NEG_BIG, le))
    r = jnp.exp(m2 - m1)
    w1 = g_w / (1.0 + r)
    w2 = w1 * r
    e1 = i1 - N_GROUPS
    e2 = i2 - N_GROUPS
    yield

    expert = lax.broadcasted_iota(jnp.int32, (N_EXPERTS, t), 0)
    oh1 = expert == e1
    oh2 = expert == e2
    oh = jnp.where(oh1 | oh2, 1.0, 0.0)
    src = lax.broadcasted_iota(jnp.int32, (t, t), 0)
    dst = lax.broadcasted_iota(jnp.int32, (t, t), 1)
    earlier = jnp.where(src < dst, 1.0, 0.0).astype(jnp.bfloat16)
    before = _dot(_bf(oh), earlier) + carry_ref[...]
    r1 = jnp.sum(jnp.where(oh1, before, 0.0), axis=0, keepdims=True)
    r2 = jnp.sum(jnp.where(oh2, before, 0.0), axis=0, keepdims=True)
    carry_ref[...] += jnp.sum(oh, axis=1, keepdims=True)
    cnt_ref[...] = jnp.broadcast_to(carry_ref[...], cnt_ref.shape)

    table = jnp.concatenate([e1.astype(jnp.float32), e2.astype(jnp.float32), w1, w2, r1, r2,
                             jnp.zeros((LANE - 6, t), jnp.float32)], axis=0)
    slabt_ref[:, rows] = table[:SUBLANE, :]
    slab_ref[rows, :] = table.T


def _reset_carry(carry_ref):
    @pl.when(pl.program_id(0) == 0)
    def _():
        carry_ref[...] = jnp.zeros_like(carry_ref)


def _layer0_kernel(x_ref, g1_ref, win_ref, lng_ref, lnb_ref, ws_ref, bs_ref, kn_ref, v_ref, qg_ref,
                   wout_ref, g2_ref, wr_ref, br_ref,
                   xo_ref, xn_ref, slab_ref, slabt_ref, cnt_ref, carry_ref):
    d = x_ref.shape[1]
    _reset_carry(carry_ref)
    row = lax.broadcasted_iota(jnp.int32, (CHUNK, CHUNK), 0)
    col = lax.broadcasted_iota(jnp.int32, (CHUNK, CHUNK), 1)
    causal = row >= col

    def stages(rows):
        x = x_ref[rows, :]
        t = x.shape[0]
        h = _bf(_rms(x, g1_ref[...]))
        z = _dot(h, win_ref[...])
        yield
        v = _gelu(z[:, d:2 * d])
        mu = jnp.mean(v, axis=-1, keepdims=True)
        vc = v - mu
        var = jnp.mean(vc * vc, axis=-1, keepdims=True)
        v = _bf(vc * lax.rsqrt(var + EPS) * lng_ref[...] + lnb_ref[...])
        yield
        nc = t // CHUNK
        per_group = []
        for g in range(A_GROUPS):
            w = jnp.where(causal, ws_ref[g], jnp.zeros((), ws_ref.dtype))
            vg = jnp.concatenate([v[c * CHUNK:(c + 1) * CHUNK, g * LANE:(g + 1) * LANE] for c in range(nc)],
                                 axis=-1)
            per_group.append(_dot(w, vg))
        chunks = [jnp.concatenate([pg[:, c * LANE:(c + 1) * LANE] for pg in per_group], axis=-1) + bs_ref[...]
                  for c in range(nc)]
        mix = _bf(_gelu(z[:, :d]) * jnp.concatenate(chunks, axis=0))
        yield
        mem = _bf(_mem_attention(z[:, 2 * d:], kn_ref, v_ref, qg_ref[...]))
        yield
        xo = x + _dot(mix, wout_ref[:d, :]) + _dot(mem, wout_ref[d:, :])
        xo_ref[rows, :] = xo
        yield
        yield from _route_stages(xo, rows, g2_ref, wr_ref, br_ref, carry_ref,
                                 xn_ref, slab_ref, slabt_ref, cnt_ref)

    _run_staggered([stages(rows) for rows in _sub_rows(x_ref.shape[0])])


def _layer1_proj_kernel(after_ref, x_ref, slab_ref, y0_ref, y1_ref, pos_ref, g1_ref, win_ref, qng_ref, kvng_ref,
                        wq_ref, wkv_ref, qg_ref, kg_ref, inv_ref, sgn_ref, kn_ref, mv_ref, mqg_ref,
                        xo_ref, q_ref, k_ref, v_ref, mem_ref):
    o1 = Q_LORA
    o2 = o1 + KV_LORA
    o3 = o2 + MEM_W
    q_scale = (QK_DIM ** -0.5) * LOG2E

    def stages(rows):
        x = _combined(x_ref, slab_ref, y0_ref, y1_ref, rows)
        xo_ref[rows, :] = x
        h = _bf(_rms(x, g1_ref[...]))
        z = _dot(h, win_ref[...])
        yield
        cq = _bf(_rms(z[:, :o1], qng_ref[...]))
        ckv = _bf(_rms(z[:, o1:o2], kvng_ref[...]))
        k_rope = z[:, o3:o3 + LANE]
        ang = pos_ref[rows, :].astype(jnp.float32) * inv_ref[...]
        cos = jnp.cos(ang)
        sin = jnp.sin(ang) * sgn_ref[...]

        def rope(r):
            return r * cos + pltpu.roll(r, LANE // 2, 1) * sin

        q = _dot(cq, wq_ref[...])
        kv = _dot(ckv, wkv_ref[...])
        yield
        qg = qg_ref[...]
        kg = kg_ref[...]
        kr_ss = jnp.sum(k_rope * k_rope, axis=-1, keepdims=True)
        kr = rope(k_rope * kg[:, LANE:])
        for hh in range(MLA_HEADS):
            qh = q[:, hh * QK_PAD:(hh + 1) * QK_PAD]
            rq = lax.rsqrt(jnp.sum(qh * qh, axis=-1, keepdims=True) * (1.0 / QK_DIM) + EPS) * q_scale
            qh = qh * rq * qg
            q_ref[hh, rows, :LANE] = _bf(qh[:, :LANE])
            q_ref[hh, rows, LANE:] = _bf(rope(qh[:, LANE:]))
            kn = kv[:, hh * LANE:(hh + 1) * LANE]
            rk = lax.rsqrt((jnp.sum(kn * kn, axis=-1, keepdims=True) + kr_ss) * (1.0 / QK_DIM) + EPS)
            k_ref[hh, rows, :LANE] = _bf(kn * rk * kg[:, :LANE])
            k_ref[hh, rows, LANE:] = _bf(kr * rk)
            v_ref[hh, :, rows] = _bf(kv[:, (MLA_HEADS + hh) * LANE:(MLA_HEADS + hh + 1) * LANE].T)
            if hh % 2 == 1:
                yield
        mem_ref[rows, :] = _bf(_mem_attention(z[:, o2:o3], kn_ref, mv_ref, mqg_ref[...]))

    _run_staggered([stages(rows) for rows in _sub_rows(x_ref.shape[0])])


def _attn_kernel(q_ref, k_ref, vt_ref, o_ref, m_ref, acc_ref):
    per_step = q_ref.shape[1] // ATTN_TILE
    for u in range(per_step):
        _attn_query_tile(pl.program_id(2) * per_step + u, u * ATTN_TILE,
                         q_ref, k_ref, vt_ref, o_ref, m_ref, acc_ref)


def _attn_query_tile(i, q0, q_ref, k_ref, vt_ref, o_ref, m_ref, acc_ref):
    heads, tq = q_ref.shape[0], ATTN_TILE
    m_ref[...] = jnp.full_like(m_ref, NEG_BIG)
    acc_ref[...] = jnp.zeros_like(acc_ref)

    qw = tq // ATTN_QUERY_SPLIT

    def scores(hh, j, part, masked):
        keys = (part + 1) * qw if masked else tq
        start = pl.multiple_of(j * tq, tq)
        q = q_ref[hh, q0 + part * qw:q0 + (part + 1) * qw, :]
        s = lax.dot_general(k_ref[hh, pl.ds(start, keys), :], q, _NT, preferred_element_type=jnp.float32)
        if masked:
            key = lax.broadcasted_iota(jnp.int32, (keys, qw), 0)
            qry = lax.broadcasted_iota(jnp.int32, (keys, qw), 1) + part * qw
            s = jnp.where(key <= qry, s, NEG_BIG)
        return s

    def update(hh, j, part, s):
        keys = s.shape[0]
        lanes = slice(part * qw, (part + 1) * qw)
        m = m_ref[hh, :, lanes]
        m_new = jnp.maximum(m, jnp.max(s, axis=0, keepdims=True))
        alpha = jnp.exp2(m - m_new)
        p = _bf(jnp.exp2(s - m_new))
        m_ref[hh, :, lanes] = m_new
        vt1 = jnp.concatenate([vt_ref[hh, j, :, :keys], jnp.ones((ATTN_SUM_ROWS, keys), jnp.bfloat16)], axis=0)
        acc_ref[hh, :, lanes] = alpha * acc_ref[hh, :, lanes] + _dot(vt1, p)

    def run(items):
        ss = {}
        ahead = ATTN_LOOKAHEAD
        for t in range(len(items) + ahead):
            if t < len(items):
                ss[t] = scores(*items[t])
            if t >= ahead:
                update(*items[t - ahead][:3], ss.pop(t - ahead))

    def tile_items(first, count, masked):
        return [(hh, first + u, part, masked) for u in range(count) for hh in range(heads)
                for part in range(ATTN_QUERY_SPLIT)]

    unroll = ATTN_KEY_TILES_PER_TRIP

    def body(jj, c):
        run(tile_items(unroll * jj, unroll, False))
        return c

    lax.fori_loop(0, i // unroll, body, 0)
    for left in range(unroll):
        @pl.when(i % unroll == left)
        def _():
            run(tile_items(i - left, left, False) + tile_items(i, 1, True))

    for hh in range(heads):
        acc = acc_ref[hh]
        o_ref[q0:q0 + tq, hh * V_DIM:(hh + 1) * V_DIM] = _bf((acc[:V_DIM] / acc[V_DIM:V_DIM + 1]).T)


def _layer1_out_kernel(x_ref, o_ref, mem_ref, wout_ref, g2_ref, wr_ref, br_ref,
                       xo_ref, xn_ref, slab_ref, slabt_ref, cnt_ref, carry_ref):
    d = o_ref.shape[1]
    _reset_carry(carry_ref)

    def stages(rows):
        xo = (x_ref[rows, :] + _dot(o_ref[rows, :], wout_ref[:d, :])
              + _dot(mem_ref[rows, :], wout_ref[d:, :]))
        xo_ref[rows, :] = xo
        yield
        yield from _route_stages(xo, rows, g2_ref, wr_ref, br_ref, carry_ref,
                                 xn_ref, slab_ref, slabt_ref, cnt_ref)

    _run_staggered([stages(rows) for rows in _sub_rows(x_ref.shape[0])])


def _sc_mesh():
    return plsc.VectorSubcoreMesh(core_axis_name="c", subcore_axis_name="s")


def _sc_worker_base(rows_per_worker):
    return (lax.axis_index("c") * SC_SUBCORES + lax.axis_index("s")) * rows_per_worker


def _sc_scatter_rows(x, idx0, idx1, p_rows):
    n, w = x.shape
    per = n // SC_WORKERS
    chunks = SC_INDEX_GROUP // SC_CHUNK
    assert n % SC_WORKERS == 0 and per % SC_INDEX_GROUP == 0

    @pl.kernel(out_type=jax.ShapeDtypeStruct((p_rows, w), x.dtype), mesh=_sc_mesh(),
               scratch_types=[pltpu.VMEM((1, SC_INDEX_GROUP), jnp.int32),
                              pltpu.VMEM((1, SC_INDEX_GROUP), jnp.int32),
                              pltpu.VMEM((SC_CHUNK, w), x.dtype), pltpu.VMEM((SC_CHUNK, w), x.dtype),
                              pltpu.SemaphoreType.DMA, pltpu.SemaphoreType.DMA],
               name="moe_dispatch_sc")
    def scatter(x_hbm, i0_hbm, i1_hbm, o_hbm, i0_v, i1_v, buf_a, buf_b, sem_a, sem_b):
        base = _sc_worker_base(per)

        @pl.loop(0, per // SC_INDEX_GROUP)
        def _(g):
            off = pl.multiple_of(base + g * SC_INDEX_GROUP, SC_INDEX_GROUP)
            pltpu.sync_copy(i0_hbm.at[:, pl.ds(off, SC_INDEX_GROUP)], i0_v)
            pltpu.sync_copy(i1_hbm.at[:, pl.ds(off, SC_INDEX_GROUP)], i1_v)
            pending = []
            for c in range(chunks):
                buf, sem = ((buf_a, sem_a), (buf_b, sem_b))[c % 2]
                if c >= 2:
                    for cp in pending[c - 2]:
                        cp.wait()
                pltpu.sync_copy(x_hbm.at[pl.ds(off + c * SC_CHUNK, SC_CHUNK)], buf)
                sl = pl.ds(c * SC_CHUNK, SC_CHUNK)
                pending.append((pltpu.async_copy(buf, o_hbm.at[i0_v.at[0, sl]], sem),
                                pltpu.async_copy(buf, o_hbm.at[i1_v.at[0, sl]], sem)))
            for cps in pending[max(chunks - 2, 0):]:
                for cp in cps:
                    cp.wait()

    return scatter(x, idx0.reshape(1, n), idx1.reshape(1, n))


def _sc_gather_rows(table, idx):
    m = idx.shape[0]
    w = table.shape[1]
    per = m // SC_WORKERS
    chunks = SC_INDEX_GROUP // SC_CHUNK
    assert m % SC_WORKERS == 0 and per % SC_INDEX_GROUP == 0

    @pl.kernel(out_type=jax.ShapeDtypeStruct((m, w), table.dtype), mesh=_sc_mesh(),
               scratch_types=[pltpu.VMEM((1, SC_INDEX_GROUP), jnp.int32),
                              pltpu.VMEM((SC_CHUNK, w), table.dtype), pltpu.VMEM((SC_CHUNK, w), table.dtype),
                              pltpu.SemaphoreType.DMA, pltpu.SemaphoreType.DMA],
               name="moe_combine_sc")
    def gather(t_hbm, i_hbm, o_hbm, i_v, buf_a, buf_b, sem_a, sem_b):
        base = _sc_worker_base(per)

        @pl.loop(0, per // SC_INDEX_GROUP)
        def _(g):
            off = pl.multiple_of(base + g * SC_INDEX_GROUP, SC_INDEX_GROUP)
            pltpu.sync_copy(i_hbm.at[:, pl.ds(off, SC_INDEX_GROUP)], i_v)
            pending = []
            for c in range(chunks):
                buf, sem = ((buf_a, sem_a), (buf_b, sem_b))[c % 2]
                if c >= 2:
                    pending[c - 2].wait()
                pltpu.sync_copy(t_hbm.at[i_v.at[0, pl.ds(c * SC_CHUNK, SC_CHUNK)]], buf)
                pending.append(pltpu.async_copy(buf, o_hbm.at[pl.ds(off + c * SC_CHUNK, SC_CHUNK)], sem))
            for cp in pending[max(chunks - 2, 0):]:
                cp.wait()

    return gather(table, idx.reshape(1, m))


def _ffn_kernel(exp_ref, rows_ref, after_ref, xs_ref, *refs):
    slots = ROW_TILES_PER_STEP
    w_refs = [refs[3 * s:3 * s + 3] for s in range(slots)]
    ys_ref = refs[3 * slots]
    w_bf = [refs[3 * slots + 1 + 3 * s:3 * slots + 4 + 3 * s] for s in range(slots)]
    j = pl.program_id(0)
    steps = pl.num_programs(0)

    for s in range(slots):
        tile = s * steps + j

        @pl.when((j == 0) | (exp_ref[tile] != exp_ref[jnp.maximum(tile - 1, 0)]))
        def _():
            for src, dst in zip(w_refs[s], w_bf[s]):
                dst[...] = _bf(src[...])

    def stages(s, k, rows):
        packed = xs_ref[s, rows, :]
        row_id = lax.broadcasted_iota(jnp.int32, packed.shape, 0) + k * SUB_TILE
        x = _bf(_unpack_rows(jnp.where(row_id < rows_ref[s * steps + j], packed, jnp.uint32(0))))
        wg, wu, wd = w_bf[s]
        g = _dot(x, wg[...])
        u = _dot(x, wu[...])
        yield
        act = _bf(g * jax.nn.sigmoid(g) * u)
        yield
        ys_ref[s, rows, :] = _pack_rows(_dot(act, wd[...]))

    def run_slots(which):
        _run_staggered([stages(s, k, rows) for k, rows in enumerate(_sub_rows(xs_ref.shape[1]))
                        for s in which])

    last = slots - 1
    run_slots(range(last))

    @pl.when(rows_ref[last * steps + j] > 0)
    def _():
        run_slots([last])

    @pl.when(rows_ref[last * steps + j] == 0)
    def _():
        ys_ref[last] = jnp.zeros_like(ys_ref[last])


def _combined(x_ref, slab_ref, y0_ref, y1_ref, rows):
    slab = slab_ref[rows, :]
    return (x_ref[rows, :] + slab[:, 2:3] * _unpack_rows(y0_ref[rows, :])
            + slab[:, 3:4] * _unpack_rows(y1_ref[rows, :]))


def _combine_kernel(after_ref, x_ref, slab_ref, y0_ref, y1_ref, out_ref):
    out_ref[...] = _combined(x_ref, slab_ref, y0_ref, y1_ref, slice(None))


def _moe(layer, xn, slabt, counts, w_gate, w_up, w_down, after):
    n, wp = xn.shape
    d = w_gate.shape[-2]
    rt = ROW_TILE
    p_max = 2 * n + N_EXPERTS * rt
    n_tiles = p_max // rt

    cnt = counts[:, 0].astype(jnp.int32)
    padded = ((cnt + rt - 1) // rt) * rt
    experts = jnp.arange(N_EXPERTS, dtype=jnp.int32)
    start = jnp.sum(jnp.where(experts[None, :] < experts[:, None], padded[None, :], 0), axis=1)
    end = start + padded

    def position(e_row, r_row):
        e = e_row.astype(jnp.int32)[None, :]
        return jnp.sum(jnp.where(e == experts[:, None], start[:, None], 0), axis=0) + r_row.astype(jnp.int32)

    pos0 = position(slabt[0], slabt[4])
    pos1 = position(slabt[1], slabt[5])
    tile_start = jnp.arange(n_tiles, dtype=jnp.int32) * rt
    owns = (start[None, :] <= tile_start[:, None]) & (tile_start[:, None] < end[None, :])
    past = tile_start >= end[N_EXPERTS - 1]
    tile_exp = jnp.where(past, N_EXPERTS - 1, jnp.sum(jnp.where(owns, experts[None, :], 0), axis=1))
    used = jnp.sum(jnp.where(owns, (start + cnt)[None, :], 0), axis=1)
    tile_rows = jnp.clip(used - tile_start, 0, rt).astype(jnp.int32)

    xs = _sc_scatter_rows(xn, pos0, pos1, p_max)

    f = w_gate.shape[-1]
    slots = ROW_TILES_PER_STEP
    assert n_tiles % slots == 0
    steps = n_tiles // slots
    w_specs = []
    for s in range(slots):
        pick = lambda j, ex, rw, s=s: (layer, ex[s * steps + j], 0, 0)
        w_specs += [pl.BlockSpec((None, None, d, f), pick), pl.BlockSpec((None, None, d, f), pick),
                    pl.BlockSpec((None, None, f, d), pick)]
    slot_rows = pl.BlockSpec((slots, rt, wp), lambda j, ex, rw: (0, j, 0))
    ys = pl.pallas_call(
        _ffn_kernel,
        grid_spec=pltpu.PrefetchScalarGridSpec(
            num_scalar_prefetch=2, grid=(steps,),
            in_specs=[_ANY, slot_rows] + w_specs,
            out_specs=slot_rows,
            scratch_shapes=[pltpu.VMEM((d, f), jnp.bfloat16), pltpu.VMEM((d, f), jnp.bfloat16),
                            pltpu.VMEM((f, d), jnp.bfloat16)] * slots),
        out_shape=jax.ShapeDtypeStruct((slots, steps * rt, wp), jnp.uint32),
        compiler_params=_params(("arbitrary",)),
        name="moe_ffn",
    )(tile_exp.astype(jnp.int32), tile_rows, after, xs.reshape(slots, steps * rt, wp),
      *([w_gate, w_up, w_down] * slots)).reshape(p_max, wp)

    return _sc_gather_rows(ys, jnp.concatenate([pos0, pos1])), ys


def _picked_specs(n, wp, tile):
    nb = n // tile
    return [pl.BlockSpec((tile, wp), lambda i: (i, 0)),
            pl.BlockSpec((tile, wp), lambda i: (i + nb, 0))]


def _router_weights(w_group, b_group, w_expert, b_expert):
    d = w_group.shape[0]
    pad = LANE - N_GROUPS - N_EXPERTS
    wr = jnp.concatenate([w_group, w_expert, jnp.zeros((d, pad), w_group.dtype)], axis=1)
    br = jnp.concatenate([b_group, b_expert, jnp.zeros((pad,), b_group.dtype)])
    return _bf(wr), br.reshape(1, LANE).astype(jnp.float32)


def _rope_lanes(vec_half):
    z = jnp.zeros_like(vec_half)
    return jnp.concatenate([vec_half, z, vec_half, z], axis=-1)


def _pad_rope_cols(w):
    z = jnp.zeros(w.shape[:-1] + (ROPE_HALF,), w.dtype)
    return jnp.concatenate([w[..., :ROPE_HALF], z, w[..., ROPE_HALF:], z], axis=-1)


TRUNK_PHASES = 5


def _trunk(chunk, x_all, mem, positions, mem_norm_g, w_mem_kv, mem_qn_g, mem_kn_g, norm1_g,
           norm2_g, a_w_in, a_ln_g, a_ln_b, a_w_s, a_b_s, a_w_out, b_w_in, b_q_norm_g, b_kv_norm_g, b_w_q_up,
           b_w_kv_up, b_qn_g, b_kn_g, b_w_out, moe_w_group, moe_b_group, moe_w_expert, moe_b_expert,
           moe_w_gate, moe_w_up, moe_w_down):
    n_total, d = x_all.shape
    b, s = positions.shape
    m = mem.shape[1]
    n = b * s
    depth = norm1_g.shape[0]
    tt = TOKEN_TILE
    ta = ATTN_TILE
    tiles_per_batch = s // tt
    assert depth == 2 and s % tt == 0 and tt % SUB_TILE == 0 and d == A_GROUPS * LANE
    assert s % ta == 0 and ta % tt == 0 and s % ROUTE_TILE == 0 and ROUTE_TILE % SUB_TILE == 0
    f32 = jnp.float32
    row = lambda v: v.reshape(1, -1).astype(f32)

    kn_all, mem_v = pl.pallas_call(
        _memkv_kernel,
        grid=(b,),
        in_specs=[pl.BlockSpec((m, d), lambda i: (i, 0)), _const_spec((1, d)),
                  _const_spec((d, 2 * MEM_W)), _const_spec((depth, 1, MEM_HEAD_DIM))],
        out_specs=[pl.BlockSpec((depth, m, MEM_W), lambda i: (0, i, 0)),
                   pl.BlockSpec((m, MEM_W), lambda i: (i, 0))],
        out_shape=[jax.ShapeDtypeStruct((depth, b * m, MEM_W), jnp.bfloat16),
                   jax.ShapeDtypeStruct((b * m, MEM_W), jnp.bfloat16)],
        compiler_params=_params(("arbitrary",)),
        name="mem_kv",
    )(mem.reshape(b * m, d), row(mem_norm_g), _bf(w_mem_kv), mem_kn_g.reshape(depth, 1, MEM_HEAD_DIM))

    tok = lambda width: pl.BlockSpec((tt, width), lambda i: (i, 0))
    kn_spec = lambda layer: pl.BlockSpec((None, m, MEM_W), lambda i: (layer, i // tiles_per_batch, 0))
    mv_spec = pl.BlockSpec((m, MEM_W), lambda i: (i // tiles_per_batch, 0))
    rt_ = ROUTE_TILE
    rtok = lambda width: pl.BlockSpec((rt_, width), lambda i: (i, 0))
    rkn_spec = lambda layer: pl.BlockSpec((None, m, MEM_W), lambda i: (layer, i // (s // rt_), 0))
    rmv_spec = pl.BlockSpec((m, MEM_W), lambda i: (i // (s // rt_), 0))
    route_out_specs = [rtok(d), rtok(d // 2), rtok(LANE), pl.BlockSpec((SUBLANE, rt_), lambda i: (0, i)),
                       pl.BlockSpec((N_EXPERTS, LANE), lambda i: (0, 0))]
    route_out_shape = [jax.ShapeDtypeStruct((n, d), f32), jax.ShapeDtypeStruct((n, d // 2), jnp.uint32),
                       jax.ShapeDtypeStruct((n, LANE), f32), jax.ShapeDtypeStruct((SUBLANE, n), f32),
                       jax.ShapeDtypeStruct((N_EXPERTS, LANE), f32)]
    route_scratch = [pltpu.VMEM((N_EXPERTS, 1), f32)]

    first_block = chunk * (n // tt)
    group_tok = pl.BlockSpec((tt, d), lambda i: (i + first_block, 0))
    first_rblock = chunk * (n // rt_)

    wr0, br0 = _router_weights(moe_w_group[0], moe_b_group[0], moe_w_expert[0], moe_b_expert[0])
    a_in = a_w_in.shape[-1]
    bias_s = jnp.repeat(a_b_s[0].T, LANE, axis=1).astype(f32)
    x2, xn, slab, slabt, counts = pl.pallas_call(
        _layer0_kernel,
        grid=(n // rt_,),
        in_specs=[pl.BlockSpec((rt_, d), lambda i: (i + first_rblock, 0)),
                  _const_spec((1, d)), _const_spec((d, a_in)), _const_spec((1, d)),
                  _const_spec((1, d)), _const_spec((A_GROUPS, CHUNK, CHUNK)), _const_spec((CHUNK, d)),
                  rkn_spec(0), rmv_spec, _const_spec((1, MEM_HEAD_DIM)),
                  _const_spec((d + MEM_W, d)), _const_spec((1, d)), _const_spec((d, LANE)),
                  _const_spec((1, LANE))],
        out_specs=route_out_specs,
        out_shape=route_out_shape,
        scratch_shapes=route_scratch,
        compiler_params=_params(("arbitrary",)),
        name="layer0_mixer",
    )(x_all, row(norm1_g[0]), _bf(a_w_in[0]), row(a_ln_g[0]), row(a_ln_b[0]), _bf(a_w_s[0]), bias_s,
      kn_all, mem_v, row(mem_qn_g[0]), _bf(a_w_out[0]), row(norm2_g[0]), wr0, br0)
    after = yield slab
    picked, ys = _moe(0, xn, slabt, counts, moe_w_gate, moe_w_up, moe_w_down, after)
    after = yield ys

    hq = MLA_HEADS
    o1, o2, o3 = Q_LORA, Q_LORA + KV_LORA, Q_LORA + KV_LORA + ROPE_DIM
    w_in = b_w_in[0]
    w_in_p = jnp.concatenate([w_in[:, :o2], w_in[:, o3:], _pad_rope_cols(w_in[:, o2:o3])], axis=1)
    wq = b_w_q_up[0].reshape(Q_LORA, hq, QK_DIM)
    wq_p = jnp.concatenate([wq[..., :NOPE_DIM], _pad_rope_cols(wq[..., NOPE_DIM:])], axis=-1)
    wq_p = wq_p.reshape(Q_LORA, hq * QK_PAD)
    wkv = b_w_kv_up[0].reshape(KV_LORA, hq, NOPE_DIM + V_DIM)
    wkv_p = jnp.concatenate([wkv[..., :NOPE_DIM].reshape(KV_LORA, hq * NOPE_DIM),
                             wkv[..., NOPE_DIM:].reshape(KV_LORA, hq * V_DIM)], axis=1)
    pad_gain = lambda g: jnp.concatenate([g[:NOPE_DIM], _pad_rope_cols(g[NOPE_DIM:])]).reshape(1, QK_PAD)
    half = jnp.arange(ROPE_HALF, dtype=f32)
    inv = ROPE_BASE ** (-(half * 2.0 / ROPE_DIM))
    inv_l = _rope_lanes(inv).reshape(1, LANE)
    sgn_l = jnp.concatenate([-jnp.ones((2 * ROPE_HALF,), f32), jnp.ones((2 * ROPE_HALF,), f32)]).reshape(1, LANE)

    in_w = w_in_p.shape[1]
    head_spec = lambda width: pl.BlockSpec((None, hq, tt, width),
                                           lambda i: (i // tiles_per_batch, 0, i % tiles_per_batch, 0))
    per_ta = ta // tt

    def vt_index(i):
        t = i % tiles_per_batch
        return (i // tiles_per_batch, 0, t // per_ta, 0, t % per_ta)

    vt_spec = pl.BlockSpec((None, hq, None, V_DIM, tt), vt_index)
    x2, q, k, vt, mem_o = pl.pallas_call(
        _layer1_proj_kernel,
        grid=(n // tt,),
        in_specs=[_ANY, tok(d), tok(LANE), *_picked_specs(n, d // 2, tt),
                  tok(1), _const_spec((1, d)), _const_spec((d, in_w)), _const_spec((1, Q_LORA)),
                  _const_spec((1, KV_LORA)), _const_spec((Q_LORA, hq * QK_PAD)),
                  _const_spec((KV_LORA, hq * (NOPE_DIM + V_DIM))), _const_spec((1, QK_PAD)),
                  _const_spec((1, QK_PAD)), _const_spec((1, LANE)), _const_spec((1, LANE)),
                  kn_spec(1), mv_spec, _const_spec((1, MEM_HEAD_DIM))],
        out_specs=[tok(d), head_spec(QK_PAD), head_spec(QK_PAD), vt_spec, tok(MEM_W)],
        out_shape=[jax.ShapeDtypeStruct((n, d), f32),
                   jax.ShapeDtypeStruct((b, hq, s, QK_PAD), jnp.bfloat16),
                   jax.ShapeDtypeStruct((b, hq, s, QK_PAD), jnp.bfloat16),
                   jax.ShapeDtypeStruct((b, hq, s // ta, V_DIM, ta), jnp.bfloat16),
                   jax.ShapeDtypeStruct((n, MEM_W), jnp.bfloat16)],
        compiler_params=_params(("arbitrary",)),
        name="layer1_proj",
    )(after, x2, slab, picked, picked,
      positions.reshape(n, 1), row(norm1_g[1]), _bf(w_in_p), row(b_q_norm_g[0]), row(b_kv_norm_g[0]),
      _bf(wq_p), _bf(wkv_p), pad_gain(b_qn_g[0]).astype(f32), pad_gain(b_kn_g[0]).astype(f32),
      inv_l, sgn_l, kn_all, mem_v, row(mem_qn_g[1]))

    qb = s // ta
    hp = ATTN_HEADS_PER_STEP
    qt = ATTN_Q_TILES_PER_STEP
    assert qb % qt == 0
    attn = pl.pallas_call(
        _attn_kernel,
        grid=(b, hq // hp, qb // qt),
        in_specs=[pl.BlockSpec((None, hp, qt * ta, QK_PAD), lambda bi, hi, i: (bi, hi, i, 0)),
                  pl.BlockSpec((None, hp, s, QK_PAD), lambda bi, hi, i: (bi, hi, 0, 0)),
                  pl.BlockSpec((None, hp, qb, V_DIM, ta), lambda bi, hi, i: (bi, hi, 0, 0, 0))],
        out_specs=pl.BlockSpec((qt * ta, hp * V_DIM), lambda bi, hi, i: (bi * (qb // qt) + i, hi)),
        out_shape=jax.ShapeDtypeStruct((n, hq * V_DIM), jnp.bfloat16),
        scratch_shapes=[pltpu.VMEM((hp, 1, ta), f32), pltpu.VMEM((hp, V_DIM + ATTN_SUM_ROWS, ta), f32)],
        compiler_params=_params(("arbitrary", "arbitrary", "arbitrary")),
        name="causal_attention",
    )(q, k, vt)

    wr1, br1 = _router_weights(moe_w_group[1], moe_b_group[1], moe_w_expert[1], moe_b_expert[1])
    x2, xn, slab, slabt, counts = pl.pallas_call(
        _layer1_out_kernel,
        grid=(n // rt_,),
        in_specs=[rtok(d), rtok(hq * V_DIM), rtok(MEM_W), _const_spec((hq * V_DIM + MEM_W, d)),
                  _const_spec((1, d)), _const_spec((d, LANE)), _const_spec((1, LANE))],
        out_specs=route_out_specs,
        out_shape=route_out_shape,
        scratch_shapes=route_scratch,
        compiler_params=_params(("arbitrary",)),
        name="layer1_out",
    )(x2, attn, mem_o, _bf(b_w_out[0]), row(norm2_g[1]), wr1, br1)
    after = yield slab
    picked, ys = _moe(1, xn, slabt, counts, moe_w_gate, moe_w_up, moe_w_down, after)
    after = yield ys
    yield pl.pallas_call(
        _combine_kernel,
        grid=(n // rt_,),
        in_specs=[_ANY, rtok(d), rtok(LANE), *_picked_specs(n, d // 2, rt_)],
        out_specs=pl.BlockSpec((rt_, d), lambda i: (i + first_rblock, 0)),
        out_shape=jax.ShapeDtypeStruct((n_total, d), f32),
        input_output_aliases={0: 0} if chunk else {},
        compiler_params=_params(("arbitrary",)),
        name="moe_combine",
    )(after, x2, slab, picked, picked)


def kernel(x, mem, positions, mem_norm_g, w_mem_kv, mem_qn_g, mem_kn_g, norm1_g, norm2_g, a_w_in, a_ln_g, a_ln_b, a_w_s, a_b_s, a_w_out, b_w_in, b_q_norm_g, b_kv_norm_g, b_w_q_up, b_w_kv_up, b_qn_g, b_kn_g, b_w_out, moe_w_group, moe_b_group, moe_w_expert, moe_b_expert, moe_w_gate, moe_w_up, moe_w_down):
    params = (mem_norm_g, w_mem_kv, mem_qn_g, mem_kn_g, norm1_g, norm2_g, a_w_in, a_ln_g, a_ln_b, a_w_s,
              a_b_s, a_w_out, b_w_in, b_q_norm_g, b_kv_norm_g, b_w_q_up, b_w_kv_up, b_qn_g, b_kn_g, b_w_out,
              moe_w_group, moe_b_group, moe_w_expert, moe_b_expert, moe_w_gate, moe_w_up, moe_w_down)
    b, s, d = x.shape
    assert BATCH_GROUPS == 2 and b % BATCH_GROUPS == 0
    g = b // BATCH_GROUPS
    x_all = x.reshape(b * s, d)
    groups = [_trunk(c, x_all, mem[c * g:(c + 1) * g], positions[c * g:(c + 1) * g], *params)
              for c in range(BATCH_GROUPS)]
    done = [next(t) for t in groups]
    for _ in range(TRUNK_PHASES - 1):
        prev, done = done, []
        for c, t in enumerate(groups):
            done.append(t.send(prev[c + 1] if c + 1 < BATCH_GROUPS else done[0]))
    return done[-1].reshape(b, s, d)
```

```python
import jax
import jax.numpy as jnp
from jax import lax
from jax.experimental import pallas as pl
from jax.experimental.pallas import tpu as pltpu
from jax.experimental.pallas import tpu_sc as plsc

EPS = 1e-6
LANE = 128
SUBLANE = 8
MEM_HEADS = 4
MEM_HEAD_DIM = 128
MEM_W = MEM_HEADS * MEM_HEAD_DIM
CHUNK = 128
A_GROUPS = 8
MLA_HEADS = 8
Q_LORA = 512
KV_LORA = 256
NOPE_DIM = 128
ROPE_DIM = 64
ROPE_HALF = ROPE_DIM // 2
V_DIM = 128
QK_DIM = NOPE_DIM + ROPE_DIM
QK_PAD = 2 * LANE
ROPE_BASE = 10000.0
N_GROUPS = 4
EXPERTS_PER_GROUP = 8
N_EXPERTS = N_GROUPS * EXPERTS_PER_GROUP
ROUTE_ROWS = 40
EXPERT_FF = 256
LOG2E = 1.4426950408889634

BATCH_GROUPS = 2
TOKEN_TILE = 512
ROUTE_TILE = 1024
SUB_TILE = 256
ROW_TILE = 512
ROW_TILES_PER_STEP = 4
ATTN_TILE = 1024
ATTN_HEADS_PER_STEP = 2
ATTN_Q_TILES_PER_STEP = 1
ATTN_KEY_TILES_PER_TRIP = 2
ATTN_QUERY_SPLIT = 4
ATTN_LOOKAHEAD = 6
ATTN_SUM_ROWS = 16
VMEM_LIMIT = 56 * 1024 * 1024
NEG_BIG = -1e30

SC_CORES = 2
SC_SUBCORES = 16
SC_WORKERS = SC_CORES * SC_SUBCORES
SC_INDEX_GROUP = 128
SC_CHUNK = 64

_NT = (((1,), (1,)), ((), ()))
_ANY = pl.BlockSpec(memory_space=pl.ANY)


def _const_spec(shape):
    nd = len(shape)
    return pl.BlockSpec(shape, lambda *_: (0,) * nd, pipeline_mode=pl.Buffered(1))


def _params(sem):
    return pltpu.CompilerParams(dimension_semantics=sem, vmem_limit_bytes=VMEM_LIMIT)


def _run_staggered(gens):
    waiting = list(gens)
    active = []
    while waiting or active:
        if waiting:
            active.append(waiting.pop(0))
        for g in list(active):
            try:
                next(g)
            except StopIteration:
                active.remove(g)


def _sub_rows(t):
    return [pl.ds(k * SUB_TILE, SUB_TILE) for k in range(t // SUB_TILE)]


def _rms(x, g):
    return x * lax.rsqrt(jnp.mean(x * x, axis=-1, keepdims=True) + EPS) * g


def _gelu(x):
    return 0.5 * x * (1.0 + lax.erf(x * (2.0 ** -0.5)))


def _bf(x):
    return x.astype(jnp.bfloat16)


def _dot(a, b):
    return jnp.dot(a, b, preferred_element_type=jnp.float32)


def _memkv_kernel(mem_ref, g_ref, w_ref, kng_ref, kn_ref, v_ref):
    h = _bf(_rms(mem_ref[...], g_ref[...]))
    kv = _dot(h, w_ref[...])
    v_ref[...] = _bf(kv[:, MEM_W:])
    for layer in range(kn_ref.shape[0]):
        g = kng_ref[layer]
        for hh in range(MEM_HEADS):
            k = kv[:, hh * LANE:(hh + 1) * LANE]
            kn_ref[layer, :, hh * LANE:(hh + 1) * LANE] = _bf(_rms(k, g))


def _mem_attention(qm, kn_ref, v_ref, qg):
    outs = []
    for hh in range(MEM_HEADS):
        sl = slice(hh * LANE, (hh + 1) * LANE)
        q = _rms(qm[:, sl], qg) * (MEM_HEAD_DIM ** -0.5)
        s = lax.dot_general(_bf(q), kn_ref[:, sl], _NT, preferred_element_type=jnp.float32)
        p = jnp.exp(s - jnp.max(s, axis=-1, keepdims=True))
        l = jnp.sum(p, axis=-1, keepdims=True)
        outs.append(_dot(_bf(p), v_ref[:, sl]) / l)
    return jnp.concatenate(outs, axis=-1)


def _pack_rows(x):
    w = x.shape[1] // 2
    bits = lambda v: lax.bitcast_convert_type(_bf(v).astype(jnp.float32), jnp.uint32)
    return (bits(x[:, :w]) >> 16) | (bits(x[:, w:]) & jnp.uint32(0xFFFF0000))


def _unpack_rows(p):
    lo = lax.bitcast_convert_type(p << 16, jnp.float32)
    hi = lax.bitcast_convert_type(p & jnp.uint32(0xFFFF0000), jnp.float32)
    return jnp.concatenate([lo, hi], axis=-1)


def _route_stages(x, rows, g2_ref, wr_ref, br_ref, carry_ref, xn_ref, slab_ref, slabt_ref, cnt_ref):
    t = x.shape[0]
    xn = _rms(x, g2_ref[...])
    xn_ref[rows, :] = _pack_rows(xn)
    logits = _dot(_bf(xn), wr_ref[...]) + br_ref[...]
    yield
    lt = logits.T[:ROUTE_ROWS, :]
    row = lax.broadcasted_iota(jnp.int32, lt.shape, 0)

    def first_max(v):
        m = jnp.max(v, axis=0, keepdims=True)
        idx = jnp.min(jnp.where(v == m, row, ROUTE_ROWS), axis=0, keepdims=True)
        return m, idx

    lg = jnp.where(row < N_GROUPS, lt, NEG_BIG)
    gmax, gidx = first_max(lg)
    g_w = 1.0 / jnp.sum(jnp.exp(lg - gmax), axis=0, keepdims=True)

    eid = row - N_GROUPS
    in_grp = (eid >= 0) & (eid < N_EXPERTS) & ((eid >> 3) == gidx)
    le = jnp.where(in_grp, lt, NEG_BIG)
    m1, i1 = first_max(le)
    m2, i2 = first_max(jnp.where(row == i1, NEG_BIG, le))
    r = jnp.exp(m2 - m1)
    w1 = g_w / (1.0 + r)
    w2 = w1 * r
    e1 = i1 - N_GROUPS
    e2 = i2 - N_GROUPS
    yield

    expert = lax.broadcasted_iota(jnp.int32, (N_EXPERTS, t), 0)
    oh1 = expert == e1
    oh2 = expert == e2
    oh = jnp.where(oh1 | oh2, 1.0, 0.0)
    src = lax.broadcasted_iota(jnp.int32, (t, t), 0)
    dst = lax.broadcasted_iota(jnp.int32, (t, t), 1)
    earlier = jnp.where(src < dst, 1.0, 0.0).astype(jnp.bfloat16)
    before = _dot(_bf(oh), earlier) + carry_ref[...]
    r1 = jnp.sum(jnp.where(oh1, before, 0.0), axis=0, keepdims=True)
    r2 = jnp.sum(jnp.where(oh2, before, 0.0), axis=0, keepdims=True)
    carry_ref[...] += jnp.sum(oh, axis=1, keepdims=True)
    cnt_ref[...] = jnp.broadcast_to(carry_ref[...], cnt_ref.shape)

    table = jnp.concatenate([e1.astype(jnp.float32), e2.astype(jnp.float32), w1, w2, r1, r2,
                             jnp.zeros((LANE - 6, t), jnp.float32)], axis=0)
    slabt_ref[:, rows] = table[:SUBLANE, :]
    slab_ref[rows, :] = table.T


def _reset_carry(carry_ref):
    @pl.when(pl.program_id(0) == 0)
    def _():
        carry_ref[...] = jnp.zeros_like(carry_ref)


def _layer0_kernel(x_ref, g1_ref, win_ref, lng_ref, lnb_ref, ws_ref, bs_ref, kn_ref, v_ref, qg_ref,
                   wout_ref, g2_ref, wr_ref, br_ref,
                   xo_ref, xn_ref, slab_ref, slabt_ref, cnt_ref, carry_ref):
    d = x_ref.shape[1]
    _reset_carry(carry_ref)
    row = lax.broadcasted_iota(jnp.int32, (CHUNK, CHUNK), 0)
    col = lax.broadcasted_iota(jnp.int32, (CHUNK, CHUNK), 1)
    causal = row >= col

    def stages(rows):
        x = x_ref[rows, :]
        t = x.shape[0]
        h = _bf(_rms(x, g1_ref[...]))
        z = _dot(h, win_ref[...])
        yield
        v = _gelu(z[:, d:2 * d])
        mu = jnp.mean(v, axis=-1, keepdims=True)
        vc = v - mu
        var = jnp.mean(vc * vc, axis=-1, keepdims=True)
        v = _bf(vc * lax.rsqrt(var + EPS) * lng_ref[...] + lnb_ref[...])
        yield
        nc = t // CHUNK
        per_group = []
        for g in range(A_GROUPS):
            w = jnp.where(causal, ws_ref[g], jnp.zeros((), ws_ref.dtype))
            vg = jnp.concatenate([v[c * CHUNK:(c + 1) * CHUNK, g * LANE:(g + 1) * LANE] for c in range(nc)],
                                 axis=-1)
            per_group.append(_dot(w, vg))
        chunks = [jnp.concatenate([pg[:, c * LANE:(c + 1) * LANE] for pg in per_group], axis=-1) + bs_ref[...]
                  for c in range(nc)]
        mix = _bf(_gelu(z[:, :d]) * jnp.concatenate(chunks, axis=0))
        yield
        mem = _bf(_mem_attention(z[:, 2 * d:], kn_ref, v_ref, qg_ref[...]))
        yield
        xo = x + _dot(mix, wout_ref[:d, :]) + _dot(mem, wout_ref[d:, :])
        xo_ref[rows, :] = xo
        yield
        yield from _route_stages(xo, rows, g2_ref, wr_ref, br_ref, carry_ref,
                                 xn_ref, slab_ref, slabt_ref, cnt_ref)

    _run_staggered([stages(rows) for rows in _sub_rows(x_ref.shape[0])])


def _layer1_proj_kernel(after_ref, x_ref, slab_ref, y0_ref, y1_ref, pos_ref, g1_ref, win_ref, qng_ref, kvng_ref,
                        wq_ref, wkv_ref, qg_ref, kg_ref, inv_ref, sgn_ref, kn_ref, mv_ref, mqg_ref,
                        xo_ref, q_ref, k_ref, v_ref, mem_ref):
    o1 = Q_LORA
    o2 = o1 + KV_LORA
    o3 = o2 + MEM_W
    q_scale = (QK_DIM ** -0.5) * LOG2E

    def stages(rows):
        x = _combined(x_ref, slab_ref, y0_ref, y1_ref, rows)
        xo_ref[rows, :] = x
        h = _bf(_rms(x, g1_ref[...]))
        z = _dot(h, win_ref[...])
        yield
        cq = _bf(_rms(z[:, :o1], qng_ref[...]))
        ckv = _bf(_rms(z[:, o1:o2], kvng_ref[...]))
        k_rope = z[:, o3:o3 + LANE]
        ang = pos_ref[rows, :].astype(jnp.float32) * inv_ref[...]
        cos = jnp.cos(ang)
        sin = jnp.sin(ang) * sgn_ref[...]

        def rope(r):
            return r * cos + pltpu.roll(r, LANE // 2, 1) * sin

        q = _dot(cq, wq_ref[...])
        kv = _dot(ckv, wkv_ref[...])
        yield
        qg = qg_ref[...]
        kg = kg_ref[...]
        kr_ss = jnp.sum(k_rope * k_rope, axis=-1, keepdims=True)
        kr = rope(k_rope * kg[:, LANE:])
        for hh in range(MLA_HEADS):
            qh = q[:, hh * QK_PAD:(hh + 1) * QK_PAD]
            rq = lax.rsqrt(jnp.sum(qh * qh, axis=-1, keepdims=True) * (1.0 / QK_DIM) + EPS) * q_scale
            qh = qh * rq * qg
            q_ref[hh, rows, :LANE] = _bf(qh[:, :LANE])
            q_ref[hh, rows, LANE:] = _bf(rope(qh[:, LANE:]))
            kn = kv[:, hh * LANE:(hh + 1) * LANE]
            rk = lax.rsqrt((jnp.sum(kn * kn, axis=-1, keepdims=True) + kr_ss) * (1.0 / QK_DIM) + EPS)
            k_ref[hh, rows, :LANE] = _bf(kn * rk * kg[:, :LANE])
            k_ref[hh, rows, LANE:] = _bf(kr * rk)
            v_ref[hh, :, rows] = _bf(kv[:, (MLA_HEADS + hh) * LANE:(MLA_HEADS + hh + 1) * LANE].T)
            if hh % 2 == 1:
                yield
        mem_ref[rows, :] = _bf(_mem_attention(z[:, o2:o3], kn_ref, mv_ref, mqg_ref[...]))

    _run_staggered([stages(rows) for rows in _sub_rows(x_ref.shape[0])])


def _attn_kernel(q_ref, k_ref, vt_ref, o_ref, m_ref, acc_ref):
    per_step = q_ref.shape[1] // ATTN_TILE
    for u in range(per_step):
        _attn_query_tile(pl.program_id(2) * per_step + u, u * ATTN_TILE,
                         q_ref, k_ref, vt_ref, o_ref, m_ref, acc_ref)


def _attn_query_tile(i, q0, q_ref, k_ref, vt_ref, o_ref, m_ref, acc_ref):
    heads, tq = q_ref.shape[0], ATTN_TILE
    m_ref[...] = jnp.full_like(m_ref, NEG_BIG)
    acc_ref[...] = jnp.zeros_like(acc_ref)

    qw = tq // ATTN_QUERY_SPLIT

    def scores(hh, j, part, masked):
        keys = (part + 1) * qw if masked else tq
        start = pl.multiple_of(j * tq, tq)
        q = q_ref[hh, q0 + part * qw:q0 + (part + 1) * qw, :]
        s = lax.dot_general(k_ref[hh, pl.ds(start, keys), :], q, _NT, preferred_element_type=jnp.float32)
        if masked:
            key = lax.broadcasted_iota(jnp.int32, (keys, qw), 0)
            qry = lax.broadcasted_iota(jnp.int32, (keys, qw), 1) + part * qw
            s = jnp.where(key <= qry, s, NEG_BIG)
        return s

    def update(hh, j, part, s):
        keys = s.shape[0]
        lanes = slice(part * qw, (part + 1) * qw)
        m = m_ref[hh, :, lanes]
        m_new = jnp.maximum(m, jnp.max(s, axis=0, keepdims=True))
        alpha = jnp.exp2(m - m_new)
        p = _bf(jnp.exp2(s - m_new))
        m_ref[hh, :, lanes] = m_new
        vt1 = jnp.concatenate([vt_ref[hh, j, :, :keys], jnp.ones((ATTN_SUM_ROWS, keys), jnp.bfloat16)], axis=0)
        acc_ref[hh, :, lanes] = alpha * acc_ref[hh, :, lanes] + _dot(vt1, p)

    def run(items):
        ss = {}
        ahead = ATTN_LOOKAHEAD
        for t in range(len(items) + ahead):
            if t < len(items):
                ss[t] = scores(*items[t])
            if t >= ahead:
                update(*items[t - ahead][:3], ss.pop(t - ahead))

    def tile_items(first, count, masked):
        return [(hh, first + u, part, masked) for u in range(count) for hh in range(heads)
                for part in range(ATTN_QUERY_SPLIT)]

    unroll = ATTN_KEY_TILES_PER_TRIP

    def body(jj, c):
        run(tile_items(unroll * jj, unroll, False))
        return c

    lax.fori_loop(0, i // unroll, body, 0)
    for left in range(unroll):
        @pl.when(i % unroll == left)
        def _():
            run(tile_items(i - left, left, False) + tile_items(i, 1, True))

    for hh in range(heads):
        acc = acc_ref[hh]
        o_ref[q0:q0 + tq, hh * V_DIM:(hh + 1) * V_DIM] = _bf((acc[:V_DIM] / acc[V_DIM:V_DIM + 1]).T)


def _layer1_out_kernel(x_ref, o_ref, mem_ref, wout_ref, g2_ref, wr_ref, br_ref,
                       xo_ref, xn_ref, slab_ref, slabt_ref, cnt_ref, carry_ref):
    d = o_ref.shape[1]
    _reset_carry(carry_ref)

    def stages(rows):
        xo = (x_ref[rows, :] + _dot(o_ref[rows, :], wout_ref[:d, :])
              + _dot(mem_ref[rows, :], wout_ref[d:, :]))
        xo_ref[rows, :] = xo
        yield
        yield from _route_stages(xo, rows, g2_ref, wr_ref, br_ref, carry_ref,
                                 xn_ref, slab_ref, slabt_ref, cnt_ref)

    _run_staggered([stages(rows) for rows in _sub_rows(x_ref.shape[0])])


def _sc_mesh():
    return plsc.VectorSubcoreMesh(core_axis_name="c", subcore_axis_name="s")


def _sc_worker_base(rows_per_worker):
    return (lax.axis_index("c") * SC_SUBCORES + lax.axis_index("s")) * rows_per_worker


def _sc_scatter_rows(x, idx0, idx1, p_rows):
    n, w = x.shape
    per = n // SC_WORKERS
    chunks = SC_INDEX_GROUP // SC_CHUNK
    assert n % SC_WORKERS == 0 and per % SC_INDEX_GROUP == 0

    @pl.kernel(out_type=jax.ShapeDtypeStruct((p_rows, w), x.dtype), mesh=_sc_mesh(),
               scratch_types=[pltpu.VMEM((1, SC_INDEX_GROUP), jnp.int32),
                              pltpu.VMEM((1, SC_INDEX_GROUP), jnp.int32),
                              pltpu.VMEM((SC_CHUNK, w), x.dtype), pltpu.VMEM((SC_CHUNK, w), x.dtype),
                              pltpu.SemaphoreType.DMA, pltpu.SemaphoreType.DMA],
               name="moe_dispatch_sc")
    def scatter(x_hbm, i0_hbm, i1_hbm, o_hbm, i0_v, i1_v, buf_a, buf_b, sem_a, sem_b):
        base = _sc_worker_base(per)

        @pl.loop(0, per // SC_INDEX_GROUP)
        def _(g):
            off = pl.multiple_of(base + g * SC_INDEX_GROUP, SC_INDEX_GROUP)
            pltpu.sync_copy(i0_hbm.at[:, pl.ds(off, SC_INDEX_GROUP)], i0_v)
            pltpu.sync_copy(i1_hbm.at[:, pl.ds(off, SC_INDEX_GROUP)], i1_v)
            pending = []
            for c in range(chunks):
                buf, sem = ((buf_a, sem_a), (buf_b, sem_b))[c % 2]
                if c >= 2:
                    for cp in pending[c - 2]:
                        cp.wait()
                pltpu.sync_copy(x_hbm.at[pl.ds(off + c * SC_CHUNK, SC_CHUNK)], buf)
                sl = pl.ds(c * SC_CHUNK, SC_CHUNK)
                pending.append((pltpu.async_copy(buf, o_hbm.at[i0_v.at[0, sl]], sem),
                                pltpu.async_copy(buf, o_hbm.at[i1_v.at[0, sl]], sem)))
            for cps in pending[max(chunks - 2, 0):]:
                for cp in cps:
                    cp.wait()

    return scatter(x, idx0.reshape(1, n), idx1.reshape(1, n))


def _sc_gather_rows(table, idx):
    m = idx.shape[0]
    w = table.shape[1]
    per = m // SC_WORKERS
    chunks = SC_INDEX_GROUP // SC_CHUNK
    assert m % SC_WORKERS == 0 and per % SC_INDEX_GROUP == 0

    @pl.kernel(out_type=jax.ShapeDtypeStruct((m, w), table.dtype), mesh=_sc_mesh(),
               scratch_types=[pltpu.VMEM((1, SC_INDEX_GROUP), jnp.int32),
                              pltpu.VMEM((SC_CHUNK, w), table.dtype), pltpu.VMEM((SC_CHUNK, w), table.dtype),
                              pltpu.SemaphoreType.DMA, pltpu.SemaphoreType.DMA],
               name="moe_combine_sc")
    def gather(t_hbm, i_hbm, o_hbm, i_v, buf_a, buf_b, sem_a, sem_b):
        base = _sc_worker_base(per)

        @pl.loop(0, per // SC_INDEX_GROUP)
        def _(g):
            off = pl.multiple_of(base + g * SC_INDEX_GROUP, SC_INDEX_GROUP)
            pltpu.sync_copy(i_hbm.at[:, pl.ds(off, SC_INDEX_GROUP)], i_v)
            pending = []
            for c in range(chunks):
                buf, sem = ((buf_a, sem_a), (buf_b, sem_b))[c % 2]
                if c >= 2:
                    pending[c - 2].wait()
                pltpu.sync_copy(t_hbm.at[i_v.at[0, pl.ds(c * SC_CHUNK, SC_CHUNK)]], buf)
                pending.append(pltpu.async_copy(buf, o_hbm.at[pl.ds(off + c * SC_CHUNK, SC_CHUNK)], sem))
            for cp in pending[max(chunks - 2, 0):]:
                cp.wait()

    return gather(table, idx.reshape(1, m))


def _ffn_kernel(exp_ref, rows_ref, after_ref, xs_ref, *refs):
    slots = ROW_TILES_PER_STEP
    w_refs = [refs[3 * s:3 * s + 3] for s in range(slots)]
    ys_ref = refs[3 * slots]
    w_bf = [refs[3 * slots + 1 + 3 * s:3 * slots + 4 + 3 * s] for s in range(slots)]
    j = pl.program_id(0)
    steps = pl.num_programs(0)

    for s in range(slots):
        tile = s * steps + j

        @pl.when((j == 0) | (exp_ref[tile] != exp_ref[jnp.maximum(tile - 1, 0)]))
        def _():
            for src, dst in zip(w_refs[s], w_bf[s]):
                dst[...] = _bf(src[...])

    def stages(s, k, rows):
        packed = xs_ref[s, rows, :]
        row_id = lax.broadcasted_iota(jnp.int32, packed.shape, 0) + k * SUB_TILE
        x = _bf(_unpack_rows(jnp.where(row_id < rows_ref[s * steps + j], packed, jnp.uint32(0))))
        wg, wu, wd = w_bf[s]
        g = _dot(x, wg[...])
        u = _dot(x, wu[...])
        yield
        act = _bf(g * jax.nn.sigmoid(g) * u)
        yield
        ys_ref[s, rows, :] = _pack_rows(_dot(act, wd[...]))

    def run_slots(which):
        _run_staggered([stages(s, k, rows) for k, rows in enumerate(_sub_rows(xs_ref.shape[1]))
                        for s in which])

    last = slots - 1
    run_slots(range(last))

    @pl.when(rows_ref[last * steps + j] > 0)
    def _():
        run_slots([last])

    @pl.when(rows_ref[last * steps + j] == 0)
    def _():
        ys_ref[last] = jnp.zeros_like(ys_ref[last])


def _combined(x_ref, slab_ref, y0_ref, y1_ref, rows):
    slab = slab_ref[rows, :]
    return (x_ref[rows, :] + slab[:, 2:3] * _unpack_rows(y0_ref[rows, :])
            + slab[:, 3:4] * _unpack_rows(y1_ref[rows, :]))


def _combine_kernel(after_ref, x_ref, slab_ref, y0_ref, y1_ref, out_ref):
    out_ref[...] = _combined(x_ref, slab_ref, y0_ref, y1_ref, slice(None))


def _moe(layer, xn, slabt, counts, w_gate, w_up, w_down, after):
    n, wp = xn.shape
    d = w_gate.shape[-2]
    rt = ROW_TILE
    p_max = 2 * n + N_EXPERTS * rt
    n_tiles = p_max // rt

    cnt = counts[:, 0].astype(jnp.int32)
    padded = ((cnt + rt - 1) // rt) * rt
    experts = jnp.arange(N_EXPERTS, dtype=jnp.int32)
    start = jnp.sum(jnp.where(experts[None, :] < experts[:, None], padded[None, :], 0), axis=1)
    end = start + padded

    def position(e_row, r_row):
        e = e_row.astype(jnp.int32)[None, :]
        return jnp.sum(jnp.where(e == experts[:, None], start[:, None], 0), axis=0) + r_row.astype(jnp.int32)

    pos0 = position(slabt[0], slabt[4])
    pos1 = position(slabt[1], slabt[5])
    tile_start = jnp.arange(n_tiles, dtype=jnp.int32) * rt
    owns = (start[None, :] <= tile_start[:, None]) & (tile_start[:, None] < end[None, :])
    past = tile_start >= end[N_EXPERTS - 1]
    tile_exp = jnp.where(past, N_EXPERTS - 1, jnp.sum(jnp.where(owns, experts[None, :], 0), axis=1))
    used = jnp.sum(jnp.where(owns, (start + cnt)[None, :], 0), axis=1)
    tile_rows = jnp.clip(used - tile_start, 0, rt).astype(jnp.int32)

    xs = _sc_scatter_rows(xn, pos0, pos1, p_max)

    f = w_gate.shape[-1]
    slots = ROW_TILES_PER_STEP
    assert n_tiles % slots == 0
    steps = n_tiles // slots
    w_specs = []
    for s in range(slots):
        pick = lambda j, ex, rw, s=s: (layer, ex[s * steps + j], 0, 0)
        w_specs += [pl.BlockSpec((None, None, d, f), pick), pl.BlockSpec((None, None, d, f), pick),
                    pl.BlockSpec((None, None, f, d), pick)]
    slot_rows = pl.BlockSpec((slots, rt, wp), lambda j, ex, rw: (0, j, 0))
    ys = pl.pallas_call(
        _ffn_kernel,
        grid_spec=pltpu.PrefetchScalarGridSpec(
            num_scalar_prefetch=2, grid=(steps,),
            in_specs=[_ANY, slot_rows] + w_specs,
            out_specs=slot_rows,
            scratch_shapes=[pltpu.VMEM((d, f), jnp.bfloat16), pltpu.VMEM((d, f), jnp.bfloat16),
                            pltpu.VMEM((f, d), jnp.bfloat16)] * slots),
        out_shape=jax.ShapeDtypeStruct((slots, steps * rt, wp), jnp.uint32),
        compiler_params=_params(("arbitrary",)),
        name="moe_ffn",
    )(tile_exp.astype(jnp.int32), tile_rows, after, xs.reshape(slots, steps * rt, wp),
      *([w_gate, w_up, w_down] * slots)).reshape(p_max, wp)

    return _sc_gather_rows(ys, jnp.concatenate([pos0, pos1])), ys


def _picked_specs(n, wp, tile):
    nb = n // tile
    return [pl.BlockSpec((tile, wp), lambda i: (i, 0)),
            pl.BlockSpec((tile, wp), lambda i: (i + nb, 0))]


def _router_weights(w_group, b_group, w_expert, b_expert):
    d = w_group.shape[0]
    pad = LANE - N_GROUPS - N_EXPERTS
    wr = jnp.concatenate([w_group, w_expert, jnp.zeros((d, pad), w_group.dtype)], axis=1)
    br = jnp.concatenate([b_group, b_expert, jnp.zeros((pad,), b_group.dtype)])
    return _bf(wr), br.reshape(1, LANE).astype(jnp.float32)


def _rope_lanes(vec_half):
    z = jnp.zeros_like(vec_half)
    return jnp.concatenate([vec_half, z, vec_half, z], axis=-1)


def _pad_rope_cols(w):
    z = jnp.zeros(w.shape[:-1] + (ROPE_HALF,), w.dtype)
    return jnp.concatenate([w[..., :ROPE_HALF], z, w[..., ROPE_HALF:], z], axis=-1)


TRUNK_PHASES = 5


def _trunk(chunk, x_all, mem, positions, mem_norm_g, w_mem_kv, mem_qn_g, mem_kn_g, norm1_g,
           norm2_g, a_w_in, a_ln_g, a_ln_b, a_w_s, a_b_s, a_w_out, b_w_in, b_q_norm_g, b_kv_norm_g, b_w_q_up,
           b_w_kv_up, b_qn_g, b_kn_g, b_w_out, moe_w_group, moe_b_group, moe_w_expert, moe_b_expert,
           moe_w_gate, moe_w_up, moe_w_down):
    n_total, d = x_all.shape
    b, s = positions.shape
    m = mem.shape[1]
    n = b * s
    depth = norm1_g.shape[0]
    tt = TOKEN_TILE
    ta = ATTN_TILE
    tiles_per_batch = s // tt
    assert depth == 2 and s % tt == 0 and tt % SUB_TILE == 0 and d == A_GROUPS * LANE
    assert s % ta == 0 and ta % tt == 0 and s % ROUTE_TILE == 0 and ROUTE_TILE % SUB_TILE == 0
    f32 = jnp.float32
    row = lambda v: v.reshape(1, -1).astype(f32)

    kn_all, mem_v = pl.pallas_call(
        _memkv_kernel,
        grid=(b,),
        in_specs=[pl.BlockSpec((m, d), lambda i: (i, 0)), _const_spec((1, d)),
                  _const_spec((d, 2 * MEM_W)), _const_spec((depth, 1, MEM_HEAD_DIM))],
        out_specs=[pl.BlockSpec((depth, m, MEM_W), lambda i: (0, i, 0)),
                   pl.BlockSpec((m, MEM_W), lambda i: (i, 0))],
        out_shape=[jax.ShapeDtypeStruct((depth, b * m, MEM_W), jnp.bfloat16),
                   jax.ShapeDtypeStruct((b * m, MEM_W), jnp.bfloat16)],
        compiler_params=_params(("arbitrary",)),
        name="mem_kv",
    )(mem.reshape(b * m, d), row(mem_norm_g), _bf(w_mem_kv), mem_kn_g.reshape(depth, 1, MEM_HEAD_DIM))

    tok = lambda width: pl.BlockSpec((tt, width), lambda i: (i, 0))
    kn_spec = lambda layer: pl.BlockSpec((None, m, MEM_W), lambda i: (layer, i // tiles_per_batch, 0))
    mv_spec = pl.BlockSpec((m, MEM_W), lambda i: (i // tiles_per_batch, 0))
    rt_ = ROUTE_TILE
    rtok = lambda width: pl.BlockSpec((rt_, width), lambda i: (i, 0))
    rkn_spec = lambda layer: pl.BlockSpec((None, m, MEM_W), lambda i: (layer, i // (s // rt_), 0))
    rmv_spec = pl.BlockSpec((m, MEM_W), lambda i: (i // (s // rt_), 0))
    route_out_specs = [rtok(d), rtok(d // 2), rtok(LANE), pl.BlockSpec((SUBLANE, rt_), lambda i: (0, i)),
                       pl.BlockSpec((N_EXPERTS, LANE), lambda i: (0, 0))]
    route_out_shape = [jax.ShapeDtypeStruct((n, d), f32), jax.ShapeDtypeStruct((n, d // 2), jnp.uint32),
                       jax.ShapeDtypeStruct((n, LANE), f32), jax.ShapeDtypeStruct((SUBLANE, n), f32),
                       jax.ShapeDtypeStruct((N_EXPERTS, LANE), f32)]
    route_scratch = [pltpu.VMEM((N_EXPERTS, 1), f32)]

    first_block = chunk * (n // tt)
    group_tok = pl.BlockSpec((tt, d), lambda i: (i + first_block, 0))
    first_rblock = chunk * (n // rt_)

    wr0, br0 = _router_weights(moe_w_group[0], moe_b_group[0], moe_w_expert[0], moe_b_expert[0])
    a_in = a_w_in.shape[-1]
    bias_s = jnp.repeat(a_b_s[0].T, LANE, axis=1).astype(f32)
    x2, xn, slab, slabt, counts = pl.pallas_call(
        _layer0_kernel,
        grid=(n // rt_,),
        in_specs=[pl.BlockSpec((rt_, d), lambda i: (i + first_rblock, 0)),
                  _const_spec((1, d)), _const_spec((d, a_in)), _const_spec((1, d)),
                  _const_spec((1, d)), _const_spec((A_GROUPS, CHUNK, CHUNK)), _const_spec((CHUNK, d)),
                  rkn_spec(0), rmv_spec, _const_spec((1, MEM_HEAD_DIM)),
                  _const_spec((d + MEM_W, d)), _const_spec((1, d)), _const_spec((d, LANE)),
                  _const_spec((1, LANE))],
        out_specs=route_out_specs,
        out_shape=route_out_shape,
        scratch_shapes=route_scratch,
        compiler_params=_params(("arbitrary",)),
        name="layer0_mixer",
    )(x_all, row(norm1_g[0]), _bf(a_w_in[0]), row(a_ln_g[0]), row(a_ln_b[0]), _bf(a_w_s[0]), bias_s,
      kn_all, mem_v, row(mem_qn_g[0]), _bf(a_w_out[0]), row(norm2_g[0]), wr0, br0)
    after = yield slab
    picked, ys = _moe(0, xn, slabt, counts, moe_w_gate, moe_w_up, moe_w_down, after)
    after = yield ys

    hq = MLA_HEADS
    o1, o2, o3 = Q_LORA, Q_LORA + KV_LORA, Q_LORA + KV_LORA + ROPE_DIM
    w_in = b_w_in[0]
    w_in_p = jnp.concatenate([w_in[:, :o2], w_in[:, o3:], _pad_rope_cols(w_in[:, o2:o3])], axis=1)
    wq = b_w_q_up[0].reshape(Q_LORA, hq, QK_DIM)
    wq_p = jnp.concatenate([wq[..., :NOPE_DIM], _pad_rope_cols(wq[..., NOPE_DIM:])], axis=-1)
    wq_p = wq_p.reshape(Q_LORA, hq * QK_PAD)
    wkv = b_w_kv_up[0].reshape(KV_LORA, hq, NOPE_DIM + V_DIM)
    wkv_p = jnp.concatenate([wkv[..., :NOPE_DIM].reshape(KV_LORA, hq * NOPE_DIM),
                             wkv[..., NOPE_DIM:].reshape(KV_LORA, hq * V_DIM)], axis=1)
    pad_gain = lambda g: jnp.concatenate([g[:NOPE_DIM], _pad_rope_cols(g[NOPE_DIM:])]).reshape(1, QK_PAD)
    half = jnp.arange(ROPE_HALF, dtype=f32)
    inv = ROPE_BASE ** (-(half * 2.0 / ROPE_DIM))
    inv_l = _rope_lanes(inv).reshape(1, LANE)
    sgn_l = jnp.concatenate([-jnp.ones((2 * ROPE_HALF,), f32), jnp.ones((2 * ROPE_HALF,), f32)]).reshape(1, LANE)

    in_w = w_in_p.shape[1]
    head_spec = lambda width: pl.BlockSpec((None, hq, tt, width),
                                           lambda i: (i // tiles_per_batch, 0, i % tiles_per_batch, 0))
    per_ta = ta // tt

    def vt_index(i):
        t = i % tiles_per_batch
        return (i // tiles_per_batch, 0, t // per_ta, 0, t % per_ta)

    vt_spec = pl.BlockSpec((None, hq, None, V_DIM, tt), vt_index)
    x2, q, k, vt, mem_o = pl.pallas_call(
        _layer1_proj_kernel,
        grid=(n // tt,),
        in_specs=[_ANY, tok(d), tok(LANE), *_picked_specs(n, d // 2, tt),
                  tok(1), _const_spec((1, d)), _const_spec((d, in_w)), _const_spec((1, Q_LORA)),
                  _const_spec((1, KV_LORA)), _const_spec((Q_LORA, hq * QK_PAD)),
                  _const_spec((KV_LORA, hq * (NOPE_DIM + V_DIM))), _const_spec((1, QK_PAD)),
                  _const_spec((1, QK_PAD)), _const_spec((1, LANE)), _const_spec((1, LANE)),
                  kn_spec(1), mv_spec, _const_spec((1, MEM_HEAD_DIM))],
        out_specs=[tok(d), head_spec(QK_PAD), head_spec(QK_PAD), vt_spec, tok(MEM_W)],
        out_shape=[jax.ShapeDtypeStruct((n, d), f32),
                   jax.ShapeDtypeStruct((b, hq, s, QK_PAD), jnp.bfloat16),
                   jax.ShapeDtypeStruct((b, hq, s, QK_PAD), jnp.bfloat16),
                   jax.ShapeDtypeStruct((b, hq, s // ta, V_DIM, ta), jnp.bfloat16),
                   jax.ShapeDtypeStruct((n, MEM_W), jnp.bfloat16)],
        compiler_params=_params(("arbitrary",)),
        name="layer1_proj",
    )(after, x2, slab, picked, picked,
      positions.reshape(n, 1), row(norm1_g[1]), _bf(w_in_p), row(b_q_norm_g[0]), row(b_kv_norm_g[0]),
      _bf(wq_p), _bf(wkv_p), pad_gain(b_qn_g[0]).astype(f32), pad_gain(b_kn_g[0]).astype(f32),
      inv_l, sgn_l, kn_all, mem_v, row(mem_qn_g[1]))

    qb = s // ta
    hp = ATTN_HEADS_PER_STEP
    qt = ATTN_Q_TILES_PER_STEP
    assert qb % qt == 0
    attn = pl.pallas_call(
        _attn_kernel,
        grid=(b, hq // hp, qb // qt),
        in_specs=[pl.BlockSpec((None, hp, qt * ta, QK_PAD), lambda bi, hi, i: (bi, hi, i, 0)),
                  pl.BlockSpec((None, hp, s, QK_PAD), lambda bi, hi, i: (bi, hi, 0, 0)),
                  pl.BlockSpec((None, hp, qb, V_DIM, ta), lambda bi, hi, i: (bi, hi, 0, 0, 0))],
        out_specs=pl.BlockSpec((qt * ta, hp * V_DIM), lambda bi, hi, i: (bi * (qb // qt) + i, hi)),
        out_shape=jax.ShapeDtypeStruct((n, hq * V_DIM), jnp.bfloat16),
        scratch_shapes=[pltpu.VMEM((hp, 1, ta), f32), pltpu.VMEM((hp, V_DIM + ATTN_SUM_ROWS, ta), f32)],
        compiler_params=_params(("arbitrary", "arbitrary", "arbitrary")),
        name="causal_attention",
    )(q, k, vt)

    wr1, br1 = _router_weights(moe_w_group[1], moe_b_group[1], moe_w_expert[1], moe_b_expert[1])
    x2, xn, slab, slabt, counts = pl.pallas_call(
        _layer1_out_kernel,
        grid=(n // rt_,),
        in_specs=[rtok(d), rtok(hq * V_DIM), rtok(MEM_W), _const_spec((hq * V_DIM + MEM_W, d)),
                  _const_spec((1, d)), _const_spec((d, LANE)), _const_spec((1, LANE))],
        out_specs=route_out_specs,
        out_shape=route_out_shape,
        scratch_shapes=route_scratch,
        compiler_params=_params(("arbitrary",)),
        name="layer1_out",
    )(x2, attn, mem_o, _bf(b_w_out[0]), row(norm2_g[1]), wr1, br1)
    after = yield slab
    picked, ys = _moe(1, xn, slabt, counts, moe_w_gate, moe_w_up, moe_w_down, after)
    after = yield ys
    yield pl.pallas_call(
        _combine_kernel,
        grid=(n // rt_,),
        in_specs=[_ANY, rtok(d), rtok(LANE), *_picked_specs(n, d // 2, rt_)],
        out_specs=pl.BlockSpec((rt_, d), lambda i: (i + first_rblock, 0)),
        out_shape=jax.ShapeDtypeStruct((n_total, d), f32),
        input_output_aliases={0: 0} if chunk else {},
        compiler_params=_params(("arbitrary",)),
        name="moe_combine",
    )(after, x2, slab, picked, picked)


def kernel(x, mem, positions, mem_norm_g, w_mem_kv, mem_qn_g, mem_kn_g, norm1_g, norm2_g, a_w_in, a_ln_g, a_ln_b, a_w_s, a_b_s, a_w_out, b_w_in, b_q_norm_g, b_kv_norm_g, b_w_q_up, b_w_kv_up, b_qn_g, b_kn_g, b_w_out, moe_w_group, moe_b_group, moe_w_expert, moe_b_expert, moe_w_gate, moe_w_up, moe_w_down):
    params = (mem_norm_g, w_mem_kv, mem_qn_g, mem_kn_g, norm1_g, norm2_g, a_w_in, a_ln_g, a_ln_b, a_w_s,
              a_b_s, a_w_out, b_w_in, b_q_norm_g, b_kv_norm_g, b_w_q_up, b_w_kv_up, b_qn_g, b_kn_g, b_w_out,
              moe_w_group, moe_b_group, moe_w_expert, moe_b_expert, moe_w_gate, moe_w_up, moe_w_down)
    b, s, d = x.shape
    assert BATCH_GROUPS == 2 and b % BATCH_GROUPS == 0
    g = b // BATCH_GROUPS
    x_all = x.reshape(b * s, d)
    groups = [_trunk(c, x_all, mem[c * g:(c + 1) * g], positions[c * g:(c + 1) * g], *params)
              for c in range(BATCH_GROUPS)]
    done = [next(t) for t in groups]
    for _ in range(TRUNK_PHASES - 1):
        prev, done = done, []
        for c, t in enumerate(groups):
            done.append(t.send(prev[c + 1] if c + 1 < BATCH_GROUPS else done[0]))
    return done[-1].reshape(b, s, d)
```
